```python
import jax
import jax.numpy as jnp
from jax import lax
import numpy as np


D_MODEL = 2048
BATCH = 2
SEQ = 4096
DEPTH = 2

GRID_W = 64
CTX_LEN = 256
N_MIXERS = 4
GROUP_WIDTH = D_MODEL // N_MIXERS
HEAD_DIM = 64
N_HEADS = GROUP_WIDTH // HEAD_DIM
N_KV_HEADS = 2
GQA_GROUP = N_HEADS // N_KV_HEADS
Q_DIM = N_HEADS * HEAD_DIM
KV_DIM = N_KV_HEADS * HEAD_DIM
WINDOW = 128
BLOCK = 128
ROPE_BASE = 10000.0
ROPE_AXIS_DIM = HEAD_DIM // 2
FNET_DIM = GROUP_WIDTH
FNET_GROUPS = 4
FNET_GROUP_DIM = FNET_DIM // FNET_GROUPS
SCONV_DIM = GROUP_WIDTH
SCONV_WIDTH = 3
CONF_DIM = GROUP_WIDTH
CONF_WIDTH = 31
IN_DIM = Q_DIM + 2 * KV_DIM + FNET_DIM + 3 * SCONV_DIM + 2 * CONF_DIM
MIX_DIM = Q_DIM + FNET_DIM + SCONV_DIM + CONF_DIM
N_EXPERT_GROUPS = 4
EXPERTS_PER_GROUP = 4
N_EXPERTS = N_EXPERT_GROUPS * EXPERTS_PER_GROUP
TOP_K = 2
D_EXPERT = D_MODEL // 4
N_MOD = 6
NORM_EPS = 1e-6
NEG_INF = -1e30

kernel_name = 'hybrid_parallel_mixer_dit_block'


def rmsnorm(x, g):
    xf = x.astype(jnp.float32)
    y = xf * lax.rsqrt(jnp.mean(xf * xf, axis=-1, keepdims=True) + NORM_EPS)
    return (y * g.astype(jnp.float32)).astype(x.dtype)


def layernorm(x, g, b):
    xf = x.astype(jnp.float32)
    mu = jnp.mean(xf, axis=-1, keepdims=True)
    var = jnp.mean(jnp.square(xf - mu), axis=-1, keepdims=True)
    y = (xf - mu) * lax.rsqrt(var + NORM_EPS) * g.astype(jnp.float32) + b.astype(jnp.float32)
    return y.astype(x.dtype)


def modulate(h, shift, scale):
    return h * (1 + scale) + shift


def split_in(u):
    sizes = (Q_DIM, KV_DIM, KV_DIM, FNET_DIM, SCONV_DIM, SCONV_DIM, SCONV_DIM, CONF_DIM, CONF_DIM)
    offs = []
    acc = 0
    for s in sizes[:-1]:
        acc += s
        offs.append(acc)
    return jnp.split(u, offs, axis=-1)


def rope_tables(rows, dtype):
    row = jnp.repeat(jnp.arange(rows, dtype=jnp.float32), GRID_W)
    col = jnp.tile(jnp.arange(GRID_W, dtype=jnp.float32), rows)
    freqs = ROPE_BASE ** (-jnp.arange(0, ROPE_AXIS_DIM, 2, dtype=jnp.float32) / ROPE_AXIS_DIM)
    ang_r = row[:, None] * freqs
    ang_c = col[:, None] * freqs
    return (jnp.cos(ang_r).astype(dtype), jnp.sin(ang_r).astype(dtype),
            jnp.cos(ang_c).astype(dtype), jnp.sin(ang_c).astype(dtype))


def rotate_axis(z, cos, sin):
    half = ROPE_AXIS_DIM // 2
    z1, z2 = z[..., :half], z[..., half:]
    cos = cos[None, :, None, :]
    sin = sin[None, :, None, :]
    return jnp.concatenate([z1 * cos - z2 * sin, z1 * sin + z2 * cos], axis=-1)


def apply_rope_2d(x, tables):
    cos_r, sin_r, cos_c, sin_c = tables
    return jnp.concatenate([rotate_axis(x[..., :ROPE_AXIS_DIM], cos_r, sin_r),
                            rotate_axis(x[..., ROPE_AXIS_DIM:], cos_c, sin_c)], axis=-1)


def depthwise_conv(x, w):
    return lax.conv_general_dilated(x, w[:, None, :].astype(x.dtype), window_strides=(1,), padding='SAME',
                                    dimension_numbers=('NWC', 'WIO', 'NWC'), feature_group_count=x.shape[-1])


def latent_window_attention(q, k, v, kc, vc, sink):
    B, S = q.shape[0], q.shape[1]
    nb = S // BLOCK
    scale = HEAD_DIM ** -0.5
    qb = q.reshape(B, nb, BLOCK, N_KV_HEADS, GQA_GROUP, HEAD_DIM)

    def band(t):
        tp = jnp.pad(t, ((0, 0), (BLOCK, BLOCK), (0, 0), (0, 0))).reshape(B, nb + 2, BLOCK, N_KV_HEADS, HEAD_DIM)
        return jnp.concatenate([tp[:, :-2], tp[:, 1:-1], tp[:, 2:]], axis=2)

    kw, vw = band(k), band(v)
    s_loc = jnp.einsum('bnqhgd,bnjhd->bnhgqj', qb, kw).astype(jnp.float32) * scale
    s_ctx = jnp.einsum('bnqhgd,blhd->bnhgql', qb, kc).astype(jnp.float32) * scale
    qi = jnp.arange(BLOCK)[:, None]
    kj = jnp.arange(3 * BLOCK)[None, :]
    rel = kj - BLOCK - qi
    kpos = (jnp.arange(nb) * BLOCK - BLOCK)[:, None, None] + kj[None]
    valid = (jnp.abs(rel) <= WINDOW)[None] & (kpos >= 0) & (kpos < S)
    s_loc = jnp.where(valid[None, :, None, None], s_loc, NEG_INF)
    sink_l = jnp.broadcast_to(sink.astype(jnp.float32).reshape(1, 1, N_KV_HEADS, GQA_GROUP, 1, 1),
                              s_loc.shape[:-1] + (1,))
    p = jax.nn.softmax(jnp.concatenate([s_loc, s_ctx, sink_l], axis=-1), axis=-1).astype(v.dtype)
    n_loc = 3 * BLOCK
    n_ctx = kc.shape[1]
    o = (jnp.einsum('bnhgqj,bnjhd->bnqhgd', p[..., :n_loc], vw)
         + jnp.einsum('bnhgql,blhd->bnqhgd', p[..., n_loc:n_loc + n_ctx], vc))
    return o.reshape(B, S, Q_DIM)


def context_attention(qc, kc, vc, sink):
    B, L = qc.shape[0], qc.shape[1]
    s = jnp.einsum('blhgd,bmhd->bhglm', qc, kc).astype(jnp.float32) * (HEAD_DIM ** -0.5)
    sink_c = jnp.broadcast_to(sink.astype(jnp.float32).reshape(1, N_KV_HEADS, GQA_GROUP, 1, 1), s.shape[:-1] + (1,))
    p = jax.nn.softmax(jnp.concatenate([s, sink_c], axis=-1), axis=-1)[..., :L].astype(vc.dtype)
    return jnp.einsum('bhglm,bmhd->blhgd', p, vc).reshape(B, L, Q_DIM)


def fourier_mix(f):
    B, T, _ = f.shape
    fg = f.reshape(B, T, FNET_GROUPS, FNET_GROUP_DIM).astype(jnp.float32)
    out = jnp.fft.fftn(fg, axes=(1, 3), norm='ortho').real
    return out.reshape(B, T, FNET_DIM).astype(f.dtype)


def local_mixers(f, sb, sc, sv, da, dg, sconv_w, cconv_w, cconv_b, cnorm_g, cnorm_b):
    o_b = fourier_mix(f)
    o_c = sb * depthwise_conv(sc * sv, sconv_w)
    z = depthwise_conv(da * jax.nn.sigmoid(dg), cconv_w) + cconv_b
    o_d = jax.nn.silu(layernorm(z, cnorm_g, cnorm_b))
    return o_b, o_c, o_d


def hier_moe(h, wg, bg, we, be, w1, w3, w2):
    gl = (h @ wg).astype(jnp.float32) + bg.astype(jnp.float32)
    pg_sel = jnp.max(jax.nn.softmax(gl, axis=-1), axis=-1, keepdims=True)
    gsel = jnp.argmax(gl, axis=-1)
    el = ((h @ we).astype(jnp.float32) + be.astype(jnp.float32)).reshape(h.shape[0], h.shape[1], N_EXPERT_GROUPS, EXPERTS_PER_GROUP)
    el_g = jnp.einsum('btge,btg->bte', el, jax.nn.one_hot(gsel, N_EXPERT_GROUPS, dtype=jnp.float32))
    top_v, top_i = lax.top_k(el_g, TOP_K)
    wk = jax.nn.softmax(top_v, axis=-1) * pg_sel
    eid = gsel[..., None] * EXPERTS_PER_GROUP + top_i
    gates = jnp.sum(jax.nn.one_hot(eid, N_EXPERTS, dtype=jnp.float32) * wk[..., None], axis=-2)
    hid = jax.nn.silu(jnp.einsum('btd,edf->btef', h, w1)) * jnp.einsum('btd,edf->btef', h, w3)
    hid = hid * gates.astype(h.dtype)[..., None]
    return jnp.einsum('btef,efd->btd', hid, w2)


def setup_inputs(seed: int = 0) -> dict:
    key = jax.random.key(seed)
    ks = jax.random.split(key, 24)
    D = D_MODEL

    def nrm(k, shape, s):
        return jax.random.normal(k, shape, jnp.float32) * s

    return {
        'x': nrm(ks[0], (BATCH, SEQ, D), 1.0),
        'c': nrm(ks[1], (BATCH, D), 1.0),
        'ctx': nrm(ks[2], (BATCH, CTX_LEN, D), 1.0),
        'c_ctx': nrm(ks[3], (D,), 1.0),
        'w_mod': nrm(ks[4], (DEPTH, D, N_MOD * D), 0.5 * D ** -0.5),
        'b_mod': nrm(ks[5], (DEPTH, N_MOD * D), 0.02),
        'g_norm1': 1.0 + nrm(ks[6], (DEPTH, D), 0.02),
        'g_norm2': 1.0 + nrm(ks[7], (DEPTH, D), 0.02),
        'w_in': nrm(ks[8], (DEPTH, D, IN_DIM), D ** -0.5),
        'w_out': nrm(ks[9], (DEPTH, MIX_DIM, D), MIX_DIM ** -0.5),
        'attn_sink': nrm(ks[10], (DEPTH, N_HEADS), 0.5),
        'sconv_w': nrm(ks[11], (DEPTH, SCONV_WIDTH, SCONV_DIM), SCONV_WIDTH ** -0.5),
        'cconv_w': nrm(ks[12], (DEPTH, CONF_WIDTH, CONF_DIM), CONF_WIDTH ** -0.5),
        'cconv_b': nrm(ks[13], (DEPTH, CONF_DIM), 0.02),
        'cnorm_g': 1.0 + nrm(ks[14], (DEPTH, CONF_DIM), 0.02),
        'cnorm_b': nrm(ks[15], (DEPTH, CONF_DIM), 0.02),
        'w_route_group': nrm(ks[16], (DEPTH, D, N_EXPERT_GROUPS), D ** -0.5),
        'b_route_group': nrm(ks[17], (DEPTH, N_EXPERT_GROUPS), 0.01),
        'w_route_expert': nrm(ks[18], (DEPTH, D, N_EXPERTS), D ** -0.5),
        'b_route_expert': nrm(ks[19], (DEPTH, N_EXPERTS), 0.01),
        'w_gate_e': nrm(ks[20], (DEPTH, N_EXPERTS, D, D_EXPERT), D ** -0.5),
        'w_up_e': nrm(ks[21], (DEPTH, N_EXPERTS, D, D_EXPERT), D ** -0.5),
        'w_down_e': nrm(ks[22], (DEPTH, N_EXPERTS, D_EXPERT, D), D_EXPERT ** -0.5),
        'g_final': 1.0 + nrm(ks[23], (D,), 0.02),
    }


def reference(x, c, ctx, c_ctx, w_mod, b_mod, g_norm1, g_norm2, w_in, w_out, attn_sink, sconv_w, cconv_w,
              cconv_b, cnorm_g, cnorm_b, w_route_group, b_route_group, w_route_expert, b_route_expert,
              w_gate_e, w_up_e, w_down_e, g_final):
    B, S, _ = x.shape
    L = ctx.shape[1]
    rows = S // GRID_W
    tables = rope_tables(rows, x.dtype)
    xc = ctx
    for l in range(DEPTH):
        last = l == DEPTH - 1
        mod = jax.nn.silu(c) @ w_mod[l] + b_mod[l]
        sh1, sc1, ga1, sh2, sc2, ga2 = [m[:, None, :] for m in jnp.split(mod, N_MOD, axis=-1)]
        modc = jax.nn.silu(c_ctx) @ w_mod[l] + b_mod[l]
        csh1, csc1, cga1, csh2, csc2, cga2 = jnp.split(modc, N_MOD, axis=-1)

        h = modulate(rmsnorm(x, g_norm1[l]), sh1, sc1)
        hc = modulate(rmsnorm(xc, g_norm1[l]), csh1, csc1)
        q, k, v, f, sb, sc, sv, da, dg = split_in(h @ w_in[l])
        if last:
            kc_, vc_ = jnp.split(hc @ w_in[l][:, Q_DIM:Q_DIM + 2 * KV_DIM], 2, axis=-1)
        else:
            qc_, kc_, vc_, fc, sbc, scc, svc, dac, dgc = split_in(hc @ w_in[l])
        kc4 = kc_.reshape(B, L, N_KV_HEADS, HEAD_DIM)
        vc4 = vc_.reshape(B, L, N_KV_HEADS, HEAD_DIM)
        q4 = apply_rope_2d(q.reshape(B, S, N_HEADS, HEAD_DIM), tables).reshape(B, S, N_KV_HEADS, GQA_GROUP, HEAD_DIM)
        k4 = apply_rope_2d(k.reshape(B, S, N_KV_HEADS, HEAD_DIM), tables)
        v4 = v.reshape(B, S, N_KV_HEADS, HEAD_DIM)
        o_a = latent_window_attention(q4, k4, v4, kc4, vc4, attn_sink[l])
        o_b, o_c, o_d = local_mixers(f, sb, sc, sv, da, dg, sconv_w[l], cconv_w[l], cconv_b[l], cnorm_g[l], cnorm_b[l])
        x = x + ga1 * (jnp.concatenate([o_a, o_b, o_c, o_d], axis=-1) @ w_out[l])
        if not last:
            o_ac = context_attention(qc_.reshape(B, L, N_KV_HEADS, GQA_GROUP, HEAD_DIM), kc4, vc4, attn_sink[l])
            o_bc, o_cc, o_dc = local_mixers(fc, sbc, scc, svc, dac, dgc, sconv_w[l], cconv_w[l], cconv_b[l], cnorm_g[l], cnorm_b[l])
            xc = xc + cga1 * (jnp.concatenate([o_ac, o_bc, o_cc, o_dc], axis=-1) @ w_out[l])

        h2 = modulate(rmsnorm(x, g_norm2[l]), sh2, sc2)
        x = x + ga2 * hier_moe(h2, w_route_group[l], b_route_group[l], w_route_expert[l], b_route_expert[l],
                               w_gate_e[l], w_up_e[l], w_down_e[l])
        if not last:
            h2c = modulate(rmsnorm(xc, g_norm2[l]), csh2, csc2)
            xc = xc + cga2 * hier_moe(h2c, w_route_group[l], b_route_group[l], w_route_expert[l], b_route_expert[l],
                                      w_gate_e[l], w_up_e[l], w_down_e[l])
    return rmsnorm(x, g_final)
```

```python
import functools

import numpy as np
import jax
import jax.numpy as jnp
from jax import lax
from jax.experimental import pallas as pl
from jax.experimental.pallas import tpu as pltpu

F32 = jnp.float32
BF16 = jnp.bfloat16
I32 = jnp.int32
HIGHEST = lax.Precision.HIGHEST

D = 2048
BATCH = 2
SEQ = 4096
CTX = 256
DEPTH = 2
GRID_W = 64
N_LAT = BATCH * SEQ
N_CTX = BATCH * CTX
NT = N_LAT + N_CTX
GW = 512
HEAD_DIM = 64
N_HEADS = 8
WINDOW = 128
ROPE_BASE = 10000.0
N_MOD = 6
EPS = 1e-6
NEG_INF = -1e30
N_GROUPS = 4
PER_GROUP = 4
N_EXPERTS = 16
D_EXPERT = 512
N_CLASSES = N_GROUPS * 6
CONF_W = 31
CONF_PAD = 15

TM = 256
N_TILES = NT // TM
LAT_TILES = N_LAT // TM
SEQ_TILES = SEQ // TM
TQ = 128
HALO = 16
TME = 128
P_ROWS = NT + N_CLASSES * TME
P_TILES = P_ROWS // TME
VMEM_LIMIT = 56 * 1024 * 1024


def _cparams(n_axes=1, **kw):
    return pltpu.CompilerParams(dimension_semantics=("arbitrary",) * n_axes,
                                vmem_limit_bytes=VMEM_LIMIT, **kw)


def _mod_row(i):
    return jnp.minimum(i // SEQ_TILES, 2)


MOD_TN = 1536


def _mod_kernel(c_ref, w_ref, b_ref, o_ref):
    c = c_ref[...]
    s = c * jax.nn.sigmoid(c)
    o_ref[...] = jnp.dot(s, w_ref[...], precision=HIGHEST, preferred_element_type=F32) + b_ref[...]


def _mod_call(c8, w_mod, b_mod):
    n = N_MOD * D
    return pl.pallas_call(
        _mod_kernel,
        out_shape=jax.ShapeDtypeStruct((DEPTH, 8, n), F32),
        grid=(DEPTH, n // MOD_TN),
        in_specs=[
            pl.BlockSpec((8, D), lambda l, j: (0, 0)),
            pl.BlockSpec((None, D, MOD_TN), lambda l, j: (l, 0, j)),
            pl.BlockSpec((None, 1, MOD_TN), lambda l, j: (l, 0, j)),
        ],
        out_specs=pl.BlockSpec((None, 8, MOD_TN), lambda l, j: (l, 0, j)),
        compiler_params=_cparams(2),
        name="adaln_mod",
    )(c8, w_mod, b_mod.reshape(DEPTH, 1, n))


QKV_W = 1024
CV_IN = 5 * GW
CV_OUT = 3 * GW


def _rmsnorm_mod(x, g, shift, scale):
    ms = jnp.mean(x * x, axis=-1, keepdims=True)
    y = x * lax.rsqrt(ms + EPS) * g
    return y * (1.0 + scale) + shift


def _in_kernel(x_ref, mod_ref, g_ref, wqkv_ref, wf_ref, wcv_ref, cos_ref, sin_ref,
               qkv_ref, f_ref, cv_ref):
    h = _rmsnorm_mod(x_ref[...], g_ref[...], mod_ref[0:1, :], mod_ref[1:2, :])
    hb = h.astype(BF16)
    qkv = jnp.dot(hb, wqkv_ref[...], preferred_element_type=F32)
    cos = cos_ref[...]
    sin = sin_ref[...]
    lane = lax.broadcasted_iota(I32, (TM, 128), 1)
    first_half = (lane % 32) < 16

    def rope(t):
        rot = jnp.where(first_half, pltpu.roll(t, 112, 1), pltpu.roll(t, 16, 1))
        return t * cos + rot * sin

    scale = HEAD_DIM ** -0.5
    for c in range(4):
        qkv_ref[:, c * 128:(c + 1) * 128] = (rope(qkv[:, c * 128:(c + 1) * 128]) * scale).astype(BF16)
    qkv_ref[:, 512:640] = rope(qkv[:, 512:640]).astype(BF16)
    qkv_ref[:, 640:768] = qkv[:, 640:768].astype(BF16)
    qkv_ref[:, 768:896] = rope(qkv[:, 768:896]).astype(BF16)
    qkv_ref[:, 896:1024] = qkv[:, 896:1024].astype(BF16)

    f_ref[...] = jnp.dot(hb, wf_ref[...], preferred_element_type=F32).astype(BF16)

    cv = jnp.dot(hb, wcv_ref[...], preferred_element_type=F32)
    cv_ref[:, 0:GW] = cv[:, 0:GW]
    cv_ref[:, GW:2 * GW] = cv[:, GW:2 * GW] * cv[:, 2 * GW:3 * GW]
    cv_ref[:, 2 * GW:3 * GW] = cv[:, 3 * GW:4 * GW] * jax.nn.sigmoid(cv[:, 4 * GW:5 * GW])


def _in_call(l, x, mod, g_norm1, wqkv, wf, wcv, cos_t, sin_t):
    const = dict(pipeline_mode=pl.Buffered(1))
    return pl.pallas_call(
        _in_kernel,
        out_shape=(jax.ShapeDtypeStruct((NT, QKV_W), BF16),
                   jax.ShapeDtypeStruct((NT, GW), BF16),
                   jax.ShapeDtypeStruct((NT, CV_OUT), F32)),
        grid=(N_TILES,),
        in_specs=[
            pl.BlockSpec((TM, D), lambda i: (i, 0)),
            pl.BlockSpec((None, None, 8, D), lambda i: (l, _mod_row(i), 0, 0)),
            pl.BlockSpec((None, 1, D), lambda i: (l, 0, 0)),
            pl.BlockSpec((None, D, QKV_W), lambda i: (l, 0, 0), **const),
            pl.BlockSpec((None, D, GW), lambda i: (l, 0, 0), **const),
            pl.BlockSpec((None, D, CV_IN), lambda i: (l, 0, 0), **const),
            pl.BlockSpec((TM, 128), lambda i: (jnp.where(i < LAT_TILES, i % SEQ_TILES, SEQ_TILES), 0)),
            pl.BlockSpec((TM, 128), lambda i: (jnp.where(i < LAT_TILES, i % SEQ_TILES, SEQ_TILES), 0)),
        ],
        out_specs=(pl.BlockSpec((TM, QKV_W), lambda i: (i, 0)),
                   pl.BlockSpec((TM, GW), lambda i: (i, 0)),
                   pl.BlockSpec((TM, CV_OUT), lambda i: (i, 0))),
        compiler_params=_cparams(1),
        name="norm_in_proj",
    )(x, mod, g_norm1.reshape(DEPTH, 1, D), wqkv, wf, wcv, cos_t, sin_t)


N_QBLK_LAT = N_LAT // TQ
N_QBLK = NT // TQ
QBLK_PER_SEQ = SEQ // TQ
KWIN = 3 * TQ


def _attn_heads(sink_ref, l, q, keys, vals, mask, o_ref):
    lane = lax.broadcasted_iota(I32, (TQ, 128), 1)
    row2 = lax.broadcasted_iota(I32, (2 * TQ, 1), 0)
    zero = jnp.zeros((TQ, 128), BF16)
    out = [jnp.zeros((TQ, 128), F32) for _ in range(4)]
    for hk in range(2):
        c0, c1 = 2 * hk, 2 * hk + 1
        for p in range(2):
            kx = keys[0] if hk == p else keys[1]
            vx = vals[0] if hk == p else vals[1]
            in_half = (lane // HEAD_DIM) == p
            qs = jnp.concatenate([jnp.where(in_half, q[:, c0 * 128:(c0 + 1) * 128], zero),
                                  jnp.where(in_half, q[:, c1 * 128:(c1 + 1) * 128], zero)], axis=0)
            s = lax.dot_general(qs, kx, (((1,), (1,)), ((), ())), preferred_element_type=F32)
            if mask is not None:
                s = jnp.where(mask, s, NEG_INF)
            sink = jnp.where(row2 < TQ, sink_ref[l, 2 * c0 + p], sink_ref[l, 2 * c1 + p])
            m = jnp.maximum(jnp.max(s, axis=-1, keepdims=True), sink)
            e = jnp.exp(s - m)
            den = jnp.sum(e, axis=-1, keepdims=True) + jnp.exp(sink - m)
            o = jnp.dot(e.astype(BF16), vx, preferred_element_type=F32) / den
            out[c0] = out[c0] + jnp.where(in_half, o[:TQ], 0.0)
            out[c1] = out[c1] + jnp.where(in_half, o[TQ:], 0.0)
    for c in range(4):
        o_ref[:, c * 128:(c + 1) * 128] = out[c].astype(BF16)


def _attn_kernel(sink_ref, q_ref, k_ref, v_ref, ks_ref, vs_ref, kc_ref, vc_ref, kcs_ref, vcs_ref, o_ref, *, l):
    i = pl.program_id(0)
    q = q_ref[...]

    @pl.when(i < N_QBLK_LAT)
    def _latent():
        n = i % QBLK_PER_SEQ
        ws = pl.multiple_of(jnp.clip((n - 1) * TQ, 0, SEQ - KWIN), TQ)
        keys = [jnp.concatenate([r[pl.ds(ws, KWIN), :], c[...]], axis=0)
                for r, c in ((k_ref, kc_ref), (ks_ref, kcs_ref))]
        vals = [jnp.concatenate([r[pl.ds(ws, KWIN), :], c[...]], axis=0)
                for r, c in ((v_ref, vc_ref), (vs_ref, vcs_ref))]
        nk = KWIN + CTX
        qpos = n * TQ + lax.broadcasted_iota(I32, (2 * TQ, nk), 0) % TQ
        col = lax.broadcasted_iota(I32, (2 * TQ, nk), 1)
        mask = (jnp.abs(ws + col - qpos) <= WINDOW) | (col >= KWIN)
        _attn_heads(sink_ref, l, q, keys, vals, mask, o_ref)

    @pl.when(i >= N_QBLK_LAT)
    def _context():
        _attn_heads(sink_ref, l, q, [kc_ref[...], kcs_ref[...]], [vc_ref[...], vcs_ref[...]], None, o_ref)


def _attn_call(l, attn_sink, qkv):
    def seq_blk(i):
        return jnp.minimum(i // QBLK_PER_SEQ, 1)

    def ctx_blk(i):
        b = jnp.where(i < N_QBLK_LAT, i // QBLK_PER_SEQ, (i - N_QBLK_LAT) // (CTX // TQ))
        return N_LAT // CTX + b

    seq_specs = [pl.BlockSpec((SEQ, 128), functools.partial(lambda i, col: (seq_blk(i), col), col=col))
                 for col in (4, 5, 6, 7)]
    ctx_specs = [pl.BlockSpec((CTX, 128), functools.partial(lambda i, col: (ctx_blk(i), col), col=col))
                 for col in (4, 5, 6, 7)]
    return pl.pallas_call(
        functools.partial(_attn_kernel, l=l),
        out_shape=jax.ShapeDtypeStruct((NT, GW), BF16),
        grid=(N_QBLK,),
        in_specs=[pl.BlockSpec(memory_space=pltpu.SMEM),
                  pl.BlockSpec((TQ, GW), lambda i: (i, 0))] + seq_specs + ctx_specs,
        out_specs=pl.BlockSpec((TQ, GW), lambda i: (i, 0)),
        compiler_params=_cparams(1),
        name="window_attention",
    )(attn_sink, qkv, *([qkv] * 8))


FC = 256


def _dft_tables(n):
    t = n * n
    j = np.arange(n)
    ang = 2.0 * np.pi * np.outer(j, j) / n
    cn, sn = np.cos(ang), np.sin(ang)
    m1 = np.block([[cn, sn], [-sn, cn]])
    m2 = np.concatenate([cn, sn], axis=1)
    angc = 2.0 * np.pi * np.outer(np.arange(128), np.arange(128)) / 128.0
    wc = np.concatenate([np.cos(angc), -np.sin(angc)], axis=1)
    angt = 2.0 * np.pi * np.outer(j, j) / t
    tw_re = np.broadcast_to(np.cos(angt)[:, :, None], (n, n, 128))
    tw_im = np.broadcast_to(-np.sin(angt)[:, :, None], (n, n, 128))
    return (jnp.asarray(m1, F32), jnp.asarray(m2, F32), jnp.asarray(wc, F32),
            jnp.asarray(np.ascontiguousarray(tw_re), F32), jnp.asarray(np.ascontiguousarray(tw_im), F32))


def _fft_kernel(f_ref, m1_ref, m2_ref, wc_ref, twr_ref, twi_ref, o_ref, are, aim, hre, him, *, n):
    t = n * n
    ng = FC // 128
    wc = wc_ref[...]
    for g in range(ng):
        xg = f_ref[:, g * 128:(g + 1) * 128].astype(F32)
        a = jnp.dot(xg, wc, precision=HIGHEST, preferred_element_type=F32)
        are[g] = a[:, :128]
        aim[g] = a[:, 128:]
    m1 = m1_ref[...]
    m2 = m2_ref[...]

    def gather_rows(re, im, rows):
        return jnp.concatenate([jnp.concatenate([re[g, rows, :] for g in range(ng)], axis=1),
                                jnp.concatenate([im[g, rows, :] for g in range(ng)], axis=1)], axis=0)

    def stage1(t1, carry):
        x = gather_rows(are, aim, pl.ds(t1, n, stride=n))
        h = jnp.dot(m1, x, precision=HIGHEST, preferred_element_type=F32)
        h_re, h_im = h[:n], h[n:]
        twr = jnp.concatenate([twr_ref[t1]] * ng, axis=1)
        twi = jnp.concatenate([twi_ref[t1]] * ng, axis=1)
        dst = pl.ds(pl.multiple_of(t1 * n, n), n)
        o_re = h_re * twr - h_im * twi
        o_im = h_re * twi + h_im * twr
        for g in range(ng):
            hre[g, dst, :] = o_re[:, g * 128:(g + 1) * 128]
            him[g, dst, :] = o_im[:, g * 128:(g + 1) * 128]
        return carry

    lax.fori_loop(0, n, stage1, 0)

    def stage2(k2, carry):
        rows = pl.ds(k2, n, stride=n)
        y = gather_rows(hre, him, rows)
        z = jnp.dot(m2, y, precision=HIGHEST, preferred_element_type=F32)
        z = z * ((t * 128.0) ** -0.5)
        for g in range(ng):
            are[g, rows, :] = z[:, g * 128:(g + 1) * 128]
        return carry

    lax.fori_loop(0, n, stage2, 0)
    for g in range(ng):
        o_ref[:, g * 128:(g + 1) * 128] = are[g].astype(BF16)


def _fft_call(f, n, first_blk, n_seq):
    t = n * n
    m1, m2, wc, twr, twi = _dft_tables(n)
    full = lambda shape: pl.BlockSpec(shape, lambda b, h: (0,) * len(shape))
    return pl.pallas_call(
        functools.partial(_fft_kernel, n=n),
        out_shape=jax.ShapeDtypeStruct((n_seq * t, GW), BF16),
        grid=(n_seq, GW // FC),
        in_specs=[pl.BlockSpec((t, FC), lambda b, h: (first_blk + b, h)),
                  full((2 * n, 2 * n)), full((n, 2 * n)), full((128, 256)),
                  full((n, n, 128)), full((n, n, 128))],
        out_specs=pl.BlockSpec((t, FC), lambda b, h: (b, h)),
        scratch_shapes=[pltpu.VMEM((FC // 128, t, 128), F32)] * 4,
        compiler_params=_cparams(2),
        name="fourier_mix_%d" % t,
    )(f, m1, m2, wc, twr, twi)


CONV_ROWS = 32


def _conv_kernel(sb_ref, p_ref, pp_ref, pn_ref, g_ref, gp_ref, gn_ref, sw_ref, cw_ref, cb_ref, ng_ref, nb_ref,
                 oc_ref, od_ref, pbuf, gbuf):
    i = pl.program_id(0)
    seq_start = jnp.logical_or(i % SEQ_TILES == 0, i >= LAT_TILES)
    seq_end = jnp.logical_or(i % SEQ_TILES == SEQ_TILES - 1, i >= LAT_TILES)
    for buf, main, prev, nxt in ((pbuf, p_ref, pp_ref, pn_ref), (gbuf, g_ref, gp_ref, gn_ref)):
        buf[0:HALO, :] = jnp.where(seq_start, 0.0, prev[...])
        buf[HALO:HALO + TM, :] = main[...]
        buf[HALO + TM:HALO + TM + HALO, :] = jnp.where(seq_end, 0.0, nxt[...])
    sw = sw_ref[...]
    cw = cw_ref[...]
    for r0 in range(0, TM, CONV_ROWS):
        base = HALO + r0
        acc = sw[0:1, :] * pbuf[base - 1:base - 1 + CONV_ROWS, :]
        acc = acc + sw[1:2, :] * pbuf[base:base + CONV_ROWS, :]
        acc = acc + sw[2:3, :] * pbuf[base + 1:base + 1 + CONV_ROWS, :]
        oc_ref[r0:r0 + CONV_ROWS, :] = (sb_ref[r0:r0 + CONV_ROWS, :] * acc).astype(BF16)

        z = cw[0:1, :] * gbuf[base - CONF_PAD:base - CONF_PAD + CONV_ROWS, :]
        for j in range(1, CONF_W):
            o = base - CONF_PAD + j
            z = z + cw[j:j + 1, :] * gbuf[o:o + CONV_ROWS, :]
        z = z + cb_ref[...]
        mu = jnp.mean(z, axis=-1, keepdims=True)
        zc = z - mu
        var = jnp.mean(zc * zc, axis=-1, keepdims=True)
        y = zc * lax.rsqrt(var + EPS) * ng_ref[...] + nb_ref[...]
        od_ref[r0:r0 + CONV_ROWS, :] = (y * jax.nn.sigmoid(y)).astype(BF16)


def _conv_call(l, cv, sconv_w, cconv_w, cconv_b, cnorm_g, cnorm_b):
    hb = TM // HALO
    last = NT // HALO - 1

    def halo_specs(col):
        return [pl.BlockSpec((TM, GW), lambda i: (i, col)),
                pl.BlockSpec((HALO, GW), lambda i: (jnp.maximum(i * hb - 1, 0), col)),
                pl.BlockSpec((HALO, GW), lambda i: (jnp.minimum((i + 1) * hb, last), col))]

    vec = lambda: pl.BlockSpec((None, 1, GW), lambda i: (l, 0, 0))
    cw = jnp.pad(cconv_w, ((0, 0), (0, 32 - CONF_W), (0, 0)))
    sw = jnp.pad(sconv_w, ((0, 0), (0, 8 - 3), (0, 0)))
    return pl.pallas_call(
        _conv_kernel,
        out_shape=(jax.ShapeDtypeStruct((NT, GW), BF16), jax.ShapeDtypeStruct((NT, GW), BF16)),
        grid=(N_TILES,),
        in_specs=[pl.BlockSpec((TM, GW), lambda i: (i, 0))] + halo_specs(1) + halo_specs(2) + [
            pl.BlockSpec((None, 8, GW), lambda i: (l, 0, 0)),
            pl.BlockSpec((None, 32, GW), lambda i: (l, 0, 0)),
            vec(), vec(), vec()],
        out_specs=(pl.BlockSpec((TM, GW), lambda i: (i, 0)), pl.BlockSpec((TM, GW), lambda i: (i, 0))),
        scratch_shapes=[pltpu.VMEM((TM + 2 * HALO, GW), F32)] * 2,
        compiler_params=_cparams(1),
        name="conv_mixers",
    )(cv, cv, cv, cv, cv, cv, cv, sw, cw, cconv_b.reshape(DEPTH, 1, GW), cnorm_g.reshape(DEPTH, 1, GW),
      cnorm_b.reshape(DEPTH, 1, GW))


R_ROWS = 32


def _route(lg):
    gl = [lg[N_EXPERTS + g:N_EXPERTS + g + 1, :] for g in range(N_GROUPS)]
    gmax = functools.reduce(jnp.maximum, gl)
    pg = 1.0 / functools.reduce(jnp.add, [jnp.exp(v - gmax) for v in gl])
    gsel = jnp.full_like(gmax, N_GROUPS - 1).astype(I32)
    for g in range(N_GROUPS - 2, -1, -1):
        gsel = jnp.where(gl[g] == gmax, g, gsel)
    el = []
    for j in range(PER_GROUP):
        v = lg[j:j + 1, :]
        for g in range(1, N_GROUPS):
            v = jnp.where(gsel == g, lg[g * PER_GROUP + j:g * PER_GROUP + j + 1, :], v)
        el.append(v)
    v1 = functools.reduce(jnp.maximum, el)
    i1 = jnp.full_like(gsel, PER_GROUP - 1)
    for j in range(PER_GROUP - 2, -1, -1):
        i1 = jnp.where(el[j] == v1, j, i1)
    rest = [jnp.where(i1 == j, -jnp.inf, el[j]) for j in range(PER_GROUP)]
    v2 = functools.reduce(jnp.maximum, rest)
    i2 = jnp.full_like(gsel, PER_GROUP - 1)
    for j in range(PER_GROUP - 2, -1, -1):
        i2 = jnp.where(jnp.logical_and(rest[j] == v2, i1 != j), j, i2)
    e2 = jnp.exp(v2 - v1)
    w1 = pg / (1.0 + e2)
    w2 = pg * e2 / (1.0 + e2)
    lo = jnp.minimum(i1, i2)
    hi = jnp.maximum(i1, i2)
    pair = jnp.where(lo == 0, hi - 1, jnp.where(lo == 1, hi + 1, 5))
    cls = gsel * 6 + pair
    first_is_lo = i1 < i2
    return cls, jnp.where(first_is_lo, w1, w2), jnp.where(first_is_lo, w2, w1)


def _out_kernel(oa_ref, obl_ref, obc_ref, oc_ref, od_ref, w_ref, x_ref, mod_ref, g_ref, wr_ref, br_ref,
                x1_ref, h2_ref, cls_ref, gw_ref):
    i = pl.program_id(0)
    ob = jnp.where(i < LAT_TILES, obl_ref[...], obc_ref[...])
    y = jnp.dot(oa_ref[...], w_ref[0:GW, :], preferred_element_type=F32)
    y = y + jnp.dot(ob, w_ref[GW:2 * GW, :], preferred_element_type=F32)
    y = y + jnp.dot(oc_ref[...], w_ref[2 * GW:3 * GW, :], preferred_element_type=F32)
    y = y + jnp.dot(od_ref[...], w_ref[3 * GW:4 * GW, :], preferred_element_type=F32)
    x1 = x_ref[...] + mod_ref[2:3, :] * y
    x1_ref[...] = x1
    h2 = _rmsnorm_mod(x1, g_ref[...], mod_ref[3:4, :], mod_ref[4:5, :])
    h2_ref[...] = h2
    lg = lax.dot_general(wr_ref[...], h2, (((1,), (1,)), ((), ())), precision=HIGHEST,
                         preferred_element_type=F32) + br_ref[:, 0:1]
    cls, w_lo, w_hi = _route(lg)
    cls_ref[...] = jnp.broadcast_to(cls, (8, TM))
    gw_ref[...] = jnp.concatenate([w_lo, w_hi, jnp.zeros((6, TM), F32)], axis=0)


def _out_call(l, o_a, o_b_lat, o_b_ctx, o_c, o_d, w_out, x, mod, g_norm2, w_route, b_route):
    row = lambda: pl.BlockSpec((TM, GW), lambda i: (i, 0))
    return pl.pallas_call(
        _out_kernel,
        out_shape=(jax.ShapeDtypeStruct((NT, D), F32), jax.ShapeDtypeStruct((NT, D), F32),
                   jax.ShapeDtypeStruct((8, NT), I32), jax.ShapeDtypeStruct((8, NT), F32)),
        grid=(N_TILES,),
        in_specs=[row(),
                  pl.BlockSpec((TM, GW), lambda i: (jnp.minimum(i, LAT_TILES - 1), 0)),
                  pl.BlockSpec((TM, GW), lambda i: (jnp.maximum(i - LAT_TILES, 0), 0)),
                  row(), row(),
                  pl.BlockSpec((None, D, D), lambda i: (l, 0, 0), pipeline_mode=pl.Buffered(1)),
                  pl.BlockSpec((TM, D), lambda i: (i, 0)),
                  pl.BlockSpec((None, None, 8, D), lambda i: (l, _mod_row(i), 0, 0)),
                  pl.BlockSpec((None, 1, D), lambda i: (l, 0, 0)),
                  pl.BlockSpec((None, R_ROWS, D), lambda i: (l, 0, 0)),
                  pl.BlockSpec((None, R_ROWS, 128), lambda i: (l, 0, 0))],
        out_specs=(pl.BlockSpec((TM, D), lambda i: (i, 0)), pl.BlockSpec((TM, D), lambda i: (i, 0)),
                   pl.BlockSpec((8, TM), lambda i: (0, i)), pl.BlockSpec((8, TM), lambda i: (0, i))),
        compiler_params=_cparams(1),
        name="out_proj_route",
    )(o_a, o_b_lat, o_b_ctx, o_c, o_d, w_out, x, mod, g_norm2.reshape(DEPTH, 1, D), w_route, b_route)


def _row_gather(idx_ref, base, src_hbm, dst_ref, sem, n_rows):
    def start(r, c):
        pltpu.make_async_copy(src_hbm.at[pl.ds(idx_ref[base + r], 1), :], dst_ref.at[pl.ds(r, 1), :], sem).start()
        return c

    lax.fori_loop(0, n_rows, start, 0)

    def wait(r, c):
        pltpu.make_async_copy(src_hbm.at[pl.ds(0, 1), :], dst_ref.at[pl.ds(r, 1), :], sem).wait()
        return c

    lax.fori_loop(0, n_rows, wait, 0)


def _gather_kernel(src_ref, nvalid_ref, h_hbm, o_ref, sem):
    i = pl.program_id(0)

    @pl.when(i < nvalid_ref[0])
    def _():
        _row_gather(src_ref, i * TME, h_hbm, o_ref, sem, TME)

    @pl.when(i >= nvalid_ref[0])
    def _():
        o_ref[...] = jnp.zeros_like(o_ref)


def _gather_call(src, nvalid, h2):
    return pl.pallas_call(
        _gather_kernel,
        out_shape=jax.ShapeDtypeStruct((P_ROWS, D), F32),
        grid_spec=pltpu.PrefetchScalarGridSpec(
            num_scalar_prefetch=2,
            grid=(P_TILES,),
            in_specs=[pl.BlockSpec(memory_space=pl.ANY)],
            out_specs=pl.BlockSpec((TME, D), lambda i, src, nv: (i, 0)),
            scratch_shapes=[pltpu.SemaphoreType.DMA]),
        compiler_params=_cparams(1),
        name="moe_gather",
    )(src, nvalid, h2)


def _expert_kernel(ea_ref, eb_ref, nvalid_ref, h_ref, gw_ref, w13a_ref, w2a_ref, w13b_ref, w2b_ref, y_ref):
    i = pl.program_id(0)

    @pl.when(i < nvalid_ref[0])
    def _():
        h = h_ref[...].astype(BF16)
        gw = gw_ref[...]
        y = None
        for k, (w13_ref, w2_ref) in enumerate(((w13a_ref, w2a_ref), (w13b_ref, w2b_ref))):
            gu = jnp.dot(h, w13_ref[...], preferred_element_type=F32)
            gate = gu[:, :D_EXPERT]
            hid = gate * jax.nn.sigmoid(gate) * gu[:, D_EXPERT:] * gw[:, k:k + 1]
            yk = jnp.dot(hid.astype(BF16), w2_ref[...], preferred_element_type=F32)
            y = yk if y is None else y + yk
        y_ref[...] = y

    @pl.when(i >= nvalid_ref[0])
    def _():
        y_ref[...] = jnp.zeros_like(y_ref)


def _expert_call(l, tile_ea, tile_eb, nvalid, hs, gws, w13, w2):
    return pl.pallas_call(
        _expert_kernel,
        out_shape=jax.ShapeDtypeStruct((P_ROWS, D), F32),
        grid_spec=pltpu.PrefetchScalarGridSpec(
            num_scalar_prefetch=3,
            grid=(P_TILES,),
            in_specs=[pl.BlockSpec((TME, D), lambda i, ea, eb, nv: (i, 0)),
                      pl.BlockSpec((TME, 2), lambda i, ea, eb, nv: (i, 0)),
                      pl.BlockSpec((None, None, D, 2 * D_EXPERT), lambda i, ea, eb, nv: (l, ea[i], 0, 0)),
                      pl.BlockSpec((None, None, D_EXPERT, D), lambda i, ea, eb, nv: (l, ea[i], 0, 0)),
                      pl.BlockSpec((None, None, D, 2 * D_EXPERT), lambda i, ea, eb, nv: (l, eb[i], 0, 0)),
                      pl.BlockSpec((None, None, D_EXPERT, D), lambda i, ea, eb, nv: (l, eb[i], 0, 0))],
            out_specs=pl.BlockSpec((TME, D), lambda i, ea, eb, nv: (i, 0))),
        compiler_params=_cparams(1),
        name="moe_experts",
    )(tile_ea, tile_eb, nvalid, hs, gws, w13, w2, w13, w2)


def _combine_kernel(pos_ref, x1_ref, mod_ref, gf_ref, y_hbm, o_ref, ybuf, sem, *, final):
    i = pl.program_id(0)
    _row_gather(pos_ref, i * TM, y_hbm, ybuf, sem, TM)
    x2 = x1_ref[...] + mod_ref[5:6, :] * ybuf[...]
    if final:
        ms = jnp.mean(x2 * x2, axis=-1, keepdims=True)
        x2 = x2 * lax.rsqrt(ms + EPS) * gf_ref[...]
    o_ref[...] = x2


def _combine_call(l, pos, x1, mod, g_final, ys, final):
    n_tiles = LAT_TILES if final else N_TILES
    return pl.pallas_call(
        functools.partial(_combine_kernel, final=final),
        out_shape=jax.ShapeDtypeStruct((n_tiles * TM, D), F32),
        grid_spec=pltpu.PrefetchScalarGridSpec(
            num_scalar_prefetch=1,
            grid=(n_tiles,),
            in_specs=[pl.BlockSpec((TM, D), lambda i, pos: (i, 0)),
                      pl.BlockSpec((None, None, 8, D), lambda i, pos: (l, _mod_row(i), 0, 0)),
                      pl.BlockSpec((1, D), lambda i, pos: (0, 0)),
                      pl.BlockSpec(memory_space=pl.ANY)],
            out_specs=pl.BlockSpec((TM, D), lambda i, pos: (i, 0)),
            scratch_shapes=[pltpu.VMEM((TM, D), F32), pltpu.SemaphoreType.DMA]),
        compiler_params=_cparams(1),
        name="moe_combine",
    )(pos, x1, mod, g_final.reshape(1, D), ys)


_PAIR_LO = np.array([0, 0, 0, 1, 1, 2], np.int32)
_PAIR_HI = np.array([1, 2, 3, 2, 3, 3], np.int32)


def _sort_plan(cls):
    onehot = (cls[:, None] == jnp.arange(N_CLASSES, dtype=I32)[None, :]).astype(I32)
    counts = jnp.sum(onehot, axis=0)
    rank = jnp.sum((jnp.cumsum(onehot, axis=0) - 1) * onehot, axis=1)
    padded = ((counts + TME - 1) // TME) * TME
    ends = jnp.cumsum(padded)
    starts = ends - padded
    pos = starts[cls] + rank
    src = jnp.zeros((P_ROWS,), I32).at[pos].set(jnp.arange(NT, dtype=I32))
    tile_start = jnp.arange(P_TILES, dtype=I32) * TME
    tile_cls = jnp.minimum(jnp.sum((tile_start[:, None] >= ends[None, :]).astype(I32), axis=1), N_CLASSES - 1)
    nvalid = (ends[-1] // TME).reshape(1)
    grp = tile_cls // 6
    tile_ea = grp * PER_GROUP + jnp.asarray(_PAIR_LO)[tile_cls % 6]
    tile_eb = grp * PER_GROUP + jnp.asarray(_PAIR_HI)[tile_cls % 6]
    return pos, src, tile_ea, tile_eb, nvalid


def _rope_tables():
    t = jnp.arange(SEQ, dtype=I32)
    row = (t // GRID_W).astype(F32)
    col = (t % GRID_W).astype(F32)
    half = HEAD_DIM // 4
    freqs = ROPE_BASE ** (-jnp.arange(0, 2 * half, 2, dtype=F32) / (2 * half))
    ang_r = row[:, None] * freqs
    ang_c = col[:, None] * freqs
    cos_h = jnp.concatenate([jnp.cos(ang_r), jnp.cos(ang_r), jnp.cos(ang_c), jnp.cos(ang_c)], axis=1)
    sin_h = jnp.concatenate([-jnp.sin(ang_r), jnp.sin(ang_r), -jnp.sin(ang_c), jnp.sin(ang_c)], axis=1)
    cos_t = jnp.concatenate([jnp.tile(cos_h, (1, 2)), jnp.ones((TM, 128), F32)], axis=0)
    sin_t = jnp.concatenate([jnp.tile(sin_h, (1, 2)), jnp.zeros((TM, 128), F32)], axis=0)
    return cos_t, sin_t


def kernel(x, c, ctx, c_ctx, w_mod, b_mod, g_norm1, g_norm2, w_in, w_out, attn_sink, sconv_w, cconv_w, cconv_b,
           cnorm_g, cnorm_b, w_route_group, b_route_group, w_route_expert, b_route_expert, w_gate_e, w_up_e,
           w_down_e, g_final):
    xs = jnp.concatenate([x.reshape(N_LAT, D), ctx.reshape(N_CTX, D)], axis=0)
    c8 = jnp.zeros((8, D), F32).at[0:BATCH].set(c).at[BATCH].set(c_ctx)
    q_w, k_w, v_w = w_in[:, :, 0:512], w_in[:, :, 512:640], w_in[:, :, 640:768]
    swap = lambda w: jnp.concatenate([w[:, :, HEAD_DIM:], w[:, :, :HEAD_DIM]], axis=-1)
    wqkv = jnp.concatenate([q_w, k_w, v_w, swap(k_w), swap(v_w)], axis=-1).astype(BF16)
    wf = w_in[:, :, 768:1280].astype(BF16)
    wcv = w_in[:, :, 1280:].astype(BF16)
    w_out_b = w_out.astype(BF16)
    w13 = jnp.concatenate([w_gate_e, w_up_e], axis=-1).astype(BF16)
    w2 = w_down_e.astype(BF16)
    w_route = jnp.concatenate([jnp.swapaxes(w_route_expert, 1, 2), jnp.swapaxes(w_route_group, 1, 2),
                               jnp.zeros((DEPTH, R_ROWS - N_EXPERTS - N_GROUPS, D), F32)], axis=1)
    b_route = jnp.concatenate([b_route_expert, b_route_group,
                               jnp.zeros((DEPTH, R_ROWS - N_EXPERTS - N_GROUPS), F32)], axis=1)
    b_route = jnp.broadcast_to(b_route[:, :, None], (DEPTH, R_ROWS, 128))
    cos_t, sin_t = _rope_tables()

    mod = _mod_call(c8, w_mod, b_mod)
    mod = mod.reshape(DEPTH, 8, N_MOD, D)[:, :3]
    mod = jnp.pad(mod, ((0, 0), (0, 0), (0, 8 - N_MOD), (0, 0)))

    out = None
    for l in range(DEPTH):
        qkv, f, cv = _in_call(l, xs, mod, g_norm1, wqkv, wf, wcv, cos_t, sin_t)
        o_a = _attn_call(l, attn_sink, qkv)
        o_b_lat = _fft_call(f, 64, 0, BATCH)
        o_b_ctx = _fft_call(f, 16, N_LAT // CTX, BATCH)
        o_c, o_d = _conv_call(l, cv, sconv_w, cconv_w, cconv_b, cnorm_g, cnorm_b)
        x1, h2, cls8, gw8 = _out_call(l, o_a, o_b_lat, o_b_ctx, o_c, o_d, w_out_b, xs, mod, g_norm2,
                                      w_route, b_route)
        pos, src, tile_ea, tile_eb, nvalid = _sort_plan(cls8[0])
        hs = _gather_call(src, nvalid, h2)
        gws = jnp.transpose(gw8[0:2])[src]
        ys = _expert_call(l, tile_ea, tile_eb, nvalid, hs, gws, w13, w2)
        final = l == DEPTH - 1
        res = _combine_call(l, pos, x1, mod, g_final, ys, final)
        if final:
            out = res
        else:
            xs = res
    return out.reshape(BATCH, SEQ, D)
```

```python
import functools

import numpy as np
import jax
import jax.numpy as jnp
from jax import lax
from jax.experimental import pallas as pl
from jax.experimental.pallas import tpu as pltpu

F32 = jnp.float32
BF16 = jnp.bfloat16
I32 = jnp.int32

D = 2048
BATCH = 2
SEQ = 4096
CTX = 256
DEPTH = 2
GRID_W = 64
N_LAT = BATCH * SEQ
N_CTX = BATCH * CTX
NT = N_LAT + N_CTX
GW = 512
HEAD_DIM = 64
N_HEADS = 8
WINDOW = 128
ROPE_BASE = 10000.0
N_MOD = 6
EPS = 1e-6
NEG_INF = -1e30
N_GROUPS = 4
PER_GROUP = 4
N_EXPERTS = 16
D_EXPERT = 512
N_CLASSES = N_GROUPS * 6
CONF_W = 31
CONF_PAD = 15

TM = 256
N_TILES = NT // TM
LAT_TILES = N_LAT // TM
SEQ_TILES = SEQ // TM
TQ = 128
HALO = 16
TME = 128
P_ROWS = NT + N_CLASSES * TME
P_TILES = P_ROWS // TME
Y_SUB = D // 128
HP_SUB = Y_SUB
VMEM_LIMIT = 56 * 1024 * 1024


def _cparams(n_axes=1, **kw):
    return pltpu.CompilerParams(dimension_semantics=("arbitrary",) * n_axes,
                                vmem_limit_bytes=VMEM_LIMIT, **kw)


def _split_bf16(v):
    hi = v.astype(BF16)
    lo = (v - hi.astype(F32)).astype(BF16)
    return hi, lo


def _mod_row(i):
    return jnp.minimum(i // SEQ_TILES, 2)


MOD_TN = 1536


def _mod_kernel(c_ref, w_ref, b_ref, o_ref):
    c = c_ref[...]
    s = c * jax.nn.sigmoid(c)
    s_hi, s_lo = _split_bf16(s)
    w_hi, w_lo = _split_bf16(w_ref[...])
    acc = jnp.dot(s_hi, w_hi, preferred_element_type=F32)
    acc = acc + jnp.dot(s_lo, w_hi, preferred_element_type=F32)
    acc = acc + jnp.dot(s_hi, w_lo, preferred_element_type=F32)
    o_ref[...] = acc + b_ref[...]


def _mod_call(c8, w_mod, b_mod):
    n = N_MOD * D
    return pl.pallas_call(
        _mod_kernel,
        out_shape=jax.ShapeDtypeStruct((DEPTH, 8, n), F32),
        grid=(DEPTH, n // MOD_TN),
        in_specs=[
            pl.BlockSpec((8, D), lambda l, j: (0, 0)),
            pl.BlockSpec((None, D, MOD_TN), lambda l, j: (l, 0, j)),
            pl.BlockSpec((None, 1, MOD_TN), lambda l, j: (l, 0, j)),
        ],
        out_specs=pl.BlockSpec((None, 8, MOD_TN), lambda l, j: (l, 0, j)),
        compiler_params=_cparams(2),
        name="adaln_mod",
    )(c8, w_mod, b_mod.reshape(DEPTH, 1, n))


QKV_W = 1024
CV_IN = 5 * GW
CV_OUT = 3 * GW


def _rmsnorm_mod(x, g, shift, scale):
    ms = jnp.mean(x * x, axis=-1, keepdims=True)
    y = x * lax.rsqrt(ms + EPS) * g
    return y * (1.0 + scale) + shift


def _in_kernel(x_ref, mod_ref, g_ref, wqkv_ref, wf_ref, wcv_ref, cos_ref, sin_ref,
               qkv_ref, f_ref, cv_ref):
    h = _rmsnorm_mod(x_ref[...], g_ref[...], mod_ref[0:1, :], mod_ref[1:2, :])
    hb = h.astype(BF16)
    qkv = jnp.dot(hb, wqkv_ref[...], preferred_element_type=F32)
    cos = cos_ref[...]
    sin = sin_ref[...]
    lane = lax.broadcasted_iota(I32, (TM, 128), 1)
    first_half = (lane % 32) < 16

    def rope(t):
        rot = jnp.where(first_half, pltpu.roll(t, 112, 1), pltpu.roll(t, 16, 1))
        return t * cos + rot * sin

    scale = HEAD_DIM ** -0.5
    for c in range(4):
        qkv_ref[:, c * 128:(c + 1) * 128] = (rope(qkv[:, c * 128:(c + 1) * 128]) * scale).astype(BF16)
    qkv_ref[:, 512:640] = rope(qkv[:, 512:640]).astype(BF16)
    qkv_ref[:, 640:768] = qkv[:, 640:768].astype(BF16)
    qkv_ref[:, 768:896] = rope(qkv[:, 768:896]).astype(BF16)
    qkv_ref[:, 896:1024] = qkv[:, 896:1024].astype(BF16)

    f_ref[...] = jnp.dot(hb, wf_ref[...], preferred_element_type=F32).astype(BF16)

    cv = jnp.dot(hb, wcv_ref[...], preferred_element_type=F32)
    cv_ref[:, 0:GW] = cv[:, 0:GW]
    cv_ref[:, GW:2 * GW] = cv[:, GW:2 * GW] * cv[:, 2 * GW:3 * GW]
    cv_ref[:, 2 * GW:3 * GW] = cv[:, 3 * GW:4 * GW] * jax.nn.sigmoid(cv[:, 4 * GW:5 * GW])


def _in_call(l, x, mod, g_norm1, wqkv, wf, wcv, cos_t, sin_t):
    const = dict(pipeline_mode=pl.Buffered(1))
    return pl.pallas_call(
        _in_kernel,
        out_shape=(jax.ShapeDtypeStruct((NT, QKV_W), BF16),
                   jax.ShapeDtypeStruct((NT, GW), BF16),
                   jax.ShapeDtypeStruct((NT, CV_OUT), F32)),
        grid=(N_TILES,),
        in_specs=[
            pl.BlockSpec((TM, D), lambda i: (i, 0)),
            pl.BlockSpec((None, None, 8, D), lambda i: (l, _mod_row(i), 0, 0)),
            pl.BlockSpec((None, 1, D), lambda i: (l, 0, 0)),
            pl.BlockSpec((None, D, QKV_W), lambda i: (l, 0, 0), **const),
            pl.BlockSpec((None, D, GW), lambda i: (l, 0, 0), **const),
            pl.BlockSpec((None, D, CV_IN), lambda i: (l, 0, 0), **const),
            pl.BlockSpec((TM, 128), lambda i: (jnp.where(i < LAT_TILES, i % SEQ_TILES, SEQ_TILES), 0)),
            pl.BlockSpec((TM, 128), lambda i: (jnp.where(i < LAT_TILES, i % SEQ_TILES, SEQ_TILES), 0)),
        ],
        out_specs=(pl.BlockSpec((TM, QKV_W), lambda i: (i, 0)),
                   pl.BlockSpec((TM, GW), lambda i: (i, 0)),
                   pl.BlockSpec((TM, CV_OUT), lambda i: (i, 0))),
        compiler_params=_cparams(1),
        name="norm_in_proj",
    )(x, mod, g_norm1.reshape(DEPTH, 1, D), wqkv, wf, wcv, cos_t, sin_t)


N_QBLK_LAT = N_LAT // TQ
N_QBLK = NT // TQ
QBLK_PER_SEQ = SEQ // TQ
KWIN = 3 * TQ


def _attn_heads(sink_ref, l, q, keys, vals, mask, o_ref):
    lane = lax.broadcasted_iota(I32, (TQ, 128), 1)
    row2 = lax.broadcasted_iota(I32, (2 * TQ, 1), 0)
    zero = jnp.zeros((TQ, 128), BF16)
    out = [jnp.zeros((TQ, 128), F32) for _ in range(4)]
    for hk in range(2):
        c0, c1 = 2 * hk, 2 * hk + 1
        for p in range(2):
            kx = keys[0] if hk == p else keys[1]
            vx = vals[0] if hk == p else vals[1]
            in_half = (lane // HEAD_DIM) == p
            qs = jnp.concatenate([jnp.where(in_half, q[:, c0 * 128:(c0 + 1) * 128], zero),
                                  jnp.where(in_half, q[:, c1 * 128:(c1 + 1) * 128], zero)], axis=0)
            s = lax.dot_general(qs, kx, (((1,), (1,)), ((), ())), preferred_element_type=F32)
            if mask is not None:
                s = jnp.where(mask, s, NEG_INF)
            sink = jnp.where(row2 < TQ, sink_ref[l, 2 * c0 + p], sink_ref[l, 2 * c1 + p])
            m = jnp.maximum(jnp.max(s, axis=-1, keepdims=True), sink)
            e = jnp.exp(s - m)
            den = jnp.sum(e, axis=-1, keepdims=True) + jnp.exp(sink - m)
            o = jnp.dot(e.astype(BF16), vx, preferred_element_type=F32) / den
            out[c0] = out[c0] + jnp.where(in_half, o[:TQ], 0.0)
            out[c1] = out[c1] + jnp.where(in_half, o[TQ:], 0.0)
    for c in range(4):
        o_ref[:, c * 128:(c + 1) * 128] = out[c].astype(BF16)


def _attn_kernel(sink_ref, q_ref, k_ref, v_ref, ks_ref, vs_ref, kc_ref, vc_ref, kcs_ref, vcs_ref, o_ref, *, l):
    i = pl.program_id(0)
    q = q_ref[...]

    @pl.when(i < N_QBLK_LAT)
    def _latent():
        n = i % QBLK_PER_SEQ
        ws = pl.multiple_of(jnp.clip((n - 1) * TQ, 0, SEQ - KWIN), TQ)
        keys = [jnp.concatenate([r[pl.ds(ws, KWIN), :], c[...]], axis=0)
                for r, c in ((k_ref, kc_ref), (ks_ref, kcs_ref))]
        vals = [jnp.concatenate([r[pl.ds(ws, KWIN), :], c[...]], axis=0)
                for r, c in ((v_ref, vc_ref), (vs_ref, vcs_ref))]
        nk = KWIN + CTX
        qpos = n * TQ + lax.broadcasted_iota(I32, (2 * TQ, nk), 0) % TQ
        col = lax.broadcasted_iota(I32, (2 * TQ, nk), 1)
        mask = (jnp.abs(ws + col - qpos) <= WINDOW) | (col >= KWIN)
        _attn_heads(sink_ref, l, q, keys, vals, mask, o_ref)

    @pl.when(i >= N_QBLK_LAT)
    def _context():
        _attn_heads(sink_ref, l, q, [kc_ref[...], kcs_ref[...]], [vc_ref[...], vcs_ref[...]], None, o_ref)


def _attn_call(l, attn_sink, qkv):
    def seq_blk(i):
        return jnp.minimum(i // QBLK_PER_SEQ, 1)

    def ctx_blk(i):
        b = jnp.where(i < N_QBLK_LAT, i // QBLK_PER_SEQ, (i - N_QBLK_LAT) // (CTX // TQ))
        return N_LAT // CTX + b

    seq_specs = [pl.BlockSpec((SEQ, 128), functools.partial(lambda i, col: (seq_blk(i), col), col=col))
                 for col in (4, 5, 6, 7)]
    ctx_specs = [pl.BlockSpec((CTX, 128), functools.partial(lambda i, col: (ctx_blk(i), col), col=col))
                 for col in (4, 5, 6, 7)]
    return pl.pallas_call(
        functools.partial(_attn_kernel, l=l),
        out_shape=jax.ShapeDtypeStruct((NT, GW), BF16),
        grid=(N_QBLK,),
        in_specs=[pl.BlockSpec(memory_space=pltpu.SMEM),
                  pl.BlockSpec((TQ, GW), lambda i: (i, 0))] + seq_specs + ctx_specs,
        out_specs=pl.BlockSpec((TQ, GW), lambda i: (i, 0)),
        compiler_params=_cparams(1),
        name="window_attention",
    )(attn_sink, qkv, *([qkv] * 8))


FC = 256
FFT_UNROLL = 4


def _dft_tables(n):
    t = n * n
    j = np.arange(n)
    ang = 2.0 * np.pi * np.outer(j, j) / n
    cn, sn = np.cos(ang), np.sin(ang)
    m1 = np.block([[cn, sn], [-sn, cn]])
    m2 = np.concatenate([cn, sn], axis=1)
    angc = 2.0 * np.pi * np.outer(np.arange(128), np.arange(128)) / 128.0
    wc = np.concatenate([np.cos(angc), -np.sin(angc)], axis=1)
    angt = 2.0 * np.pi * np.outer(j, j) / t
    tw_re = np.broadcast_to(np.cos(angt)[:, :, None], (n, n, 128))
    tw_im = np.broadcast_to(-np.sin(angt)[:, :, None], (n, n, 128))
    def hi_lo(m):
        return jnp.stack(_split_bf16(jnp.asarray(m, F32)))

    return (hi_lo(m1), hi_lo(m2), hi_lo(wc),
            jnp.asarray(np.ascontiguousarray(tw_re), F32), jnp.asarray(np.ascontiguousarray(tw_im), F32))


def _fft_kernel(f_ref, m1_ref, m2_ref, wc_ref, twr_ref, twi_ref, o_ref, are, aim, hre, him, *, n):
    t = n * n
    ng = FC // 128
    for g in range(ng):
        fg = f_ref[:, g * 128:(g + 1) * 128]
        a = (jnp.dot(fg, wc_ref[0], preferred_element_type=F32)
             + jnp.dot(fg, wc_ref[1], preferred_element_type=F32))
        are[g] = a[:, :128]
        aim[g] = a[:, 128:]

    def table_dot(m_ref, v):
        return (jnp.dot(m_ref[0], v, preferred_element_type=F32)
                + jnp.dot(m_ref[1], v, preferred_element_type=F32))

    def gather_rows(re, im, first):
        rows = [pl.ds(first + u, n, stride=n) for u in range(FFT_UNROLL)]
        top = jnp.concatenate([re[g, r, :] for r in rows for g in range(ng)], axis=1)
        bot = jnp.concatenate([im[g, r, :] for r in rows for g in range(ng)], axis=1)
        return jnp.concatenate([top, bot], axis=0).astype(BF16)

    def stage1(it, carry):
        t1 = it * FFT_UNROLL
        h = table_dot(m1_ref, gather_rows(are, aim, t1))
        for u in range(FFT_UNROLL):
            dst = pl.ds(pl.multiple_of((t1 + u) * n, n), n)
            twr = twr_ref[t1 + u]
            twi = twi_ref[t1 + u]
            for g in range(ng):
                lanes = slice((u * ng + g) * 128, (u * ng + g + 1) * 128)
                h_re, h_im = h[:n, lanes], h[n:, lanes]
                hre[g, dst, :] = h_re * twr - h_im * twi
                him[g, dst, :] = h_re * twi + h_im * twr
        return carry

    lax.fori_loop(0, n // FFT_UNROLL, stage1, 0)

    def stage2(it, carry):
        k2 = it * FFT_UNROLL
        z = table_dot(m2_ref, gather_rows(hre, him, k2))
        z = z * ((t * 128.0) ** -0.5)
        for u in range(FFT_UNROLL):
            for g in range(ng):
                are[g, pl.ds(k2 + u, n, stride=n), :] = z[:, (u * ng + g) * 128:(u * ng + g + 1) * 128]
        return carry

    lax.fori_loop(0, n // FFT_UNROLL, stage2, 0)
    for g in range(ng):
        o_ref[:, g * 128:(g + 1) * 128] = are[g].astype(BF16)


def _fft_call(f, n, first_blk, n_seq):
    t = n * n
    m1, m2, wc, twr, twi = _dft_tables(n)
    full = lambda shape: pl.BlockSpec(shape, lambda b, h: (0,) * len(shape))
    return pl.pallas_call(
        functools.partial(_fft_kernel, n=n),
        out_shape=jax.ShapeDtypeStruct((n_seq * t, GW), BF16),
        grid=(n_seq, GW // FC),
        in_specs=[pl.BlockSpec((t, FC), lambda b, h: (first_blk + b, h)),
                  full((2, 2 * n, 2 * n)), full((2, n, 2 * n)), full((2, 128, 256)),
                  full((n, n, 128)), full((n, n, 128))],
        out_specs=pl.BlockSpec((t, FC), lambda b, h: (b, h)),
        scratch_shapes=[pltpu.VMEM((FC // 128, t, 128), F32)] * 4,
        compiler_params=_cparams(2),
        name="fourier_mix_%d" % t,
    )(f, m1, m2, wc, twr, twi)


CONV_ROWS = 32


def _conv_kernel(sb_ref, p_ref, pp_ref, pn_ref, g_ref, gp_ref, gn_ref, sw_ref, cw_ref, cb_ref, ng_ref, nb_ref,
                 oc_ref, od_ref, pbuf, gbuf):
    i = pl.program_id(0)
    seq_start = jnp.logical_or(i % SEQ_TILES == 0, i >= LAT_TILES)
    seq_end = jnp.logical_or(i % SEQ_TILES == SEQ_TILES - 1, i >= LAT_TILES)
    for buf, main, prev, nxt in ((pbuf, p_ref, pp_ref, pn_ref), (gbuf, g_ref, gp_ref, gn_ref)):
        buf[0:HALO, :] = jnp.where(seq_start, 0.0, prev[...])
        buf[HALO:HALO + TM, :] = main[...]
        buf[HALO + TM:HALO + TM + HALO, :] = jnp.where(seq_end, 0.0, nxt[...])
    sw = sw_ref[...]
    cw = cw_ref[...]
    for r0 in range(0, TM, CONV_ROWS):
        base = HALO + r0
        acc = sw[0:1, :] * pbuf[base - 1:base - 1 + CONV_ROWS, :]
        acc = acc + sw[1:2, :] * pbuf[base:base + CONV_ROWS, :]
        acc = acc + sw[2:3, :] * pbuf[base + 1:base + 1 + CONV_ROWS, :]
        oc_ref[r0:r0 + CONV_ROWS, :] = (sb_ref[r0:r0 + CONV_ROWS, :] * acc).astype(BF16)

        z = cw[0:1, :] * gbuf[base - CONF_PAD:base - CONF_PAD + CONV_ROWS, :]
        for j in range(1, CONF_W):
            o = base - CONF_PAD + j
            z = z + cw[j:j + 1, :] * gbuf[o:o + CONV_ROWS, :]
        z = z + cb_ref[...]
        mu = jnp.mean(z, axis=-1, keepdims=True)
        zc = z - mu
        var = jnp.mean(zc * zc, axis=-1, keepdims=True)
        y = zc * lax.rsqrt(var + EPS) * ng_ref[...] + nb_ref[...]
        od_ref[r0:r0 + CONV_ROWS, :] = (y * jax.nn.sigmoid(y)).astype(BF16)


def _conv_call(l, cv, sconv_w, cconv_w, cconv_b, cnorm_g, cnorm_b):
    hb = TM // HALO
    last = NT // HALO - 1

    def halo_specs(col):
        return [pl.BlockSpec((TM, GW), lambda i: (i, col)),
                pl.BlockSpec((HALO, GW), lambda i: (jnp.maximum(i * hb - 1, 0), col)),
                pl.BlockSpec((HALO, GW), lambda i: (jnp.minimum((i + 1) * hb, last), col))]

    vec = lambda: pl.BlockSpec((None, 1, GW), lambda i: (l, 0, 0))
    cw = jnp.pad(cconv_w, ((0, 0), (0, 32 - CONF_W), (0, 0)))
    sw = jnp.pad(sconv_w, ((0, 0), (0, 8 - 3), (0, 0)))
    return pl.pallas_call(
        _conv_kernel,
        out_shape=(jax.ShapeDtypeStruct((NT, GW), BF16), jax.ShapeDtypeStruct((NT, GW), BF16)),
        grid=(N_TILES,),
        in_specs=[pl.BlockSpec((TM, GW), lambda i: (i, 0))] + halo_specs(1) + halo_specs(2) + [
            pl.BlockSpec((None, 8, GW), lambda i: (l, 0, 0)),
            pl.BlockSpec((None, 32, GW), lambda i: (l, 0, 0)),
            vec(), vec(), vec()],
        out_specs=(pl.BlockSpec((TM, GW), lambda i: (i, 0)), pl.BlockSpec((TM, GW), lambda i: (i, 0))),
        scratch_shapes=[pltpu.VMEM((TM + 2 * HALO, GW), F32)] * 2,
        compiler_params=_cparams(1),
        name="conv_mixers",
    )(cv, cv, cv, cv, cv, cv, cv, sw, cw, cconv_b.reshape(DEPTH, 1, GW), cnorm_g.reshape(DEPTH, 1, GW),
      cnorm_b.reshape(DEPTH, 1, GW))


R_ROWS = 32
_SLOT_A = np.array([0, 2, 2, 0, 0, 1], np.int32)
_SLOT_B = np.array([1, 1, 3, 3, 2, 3], np.int32)


def _route(lg):
    gl = [lg[N_EXPERTS + g:N_EXPERTS + g + 1, :] for g in range(N_GROUPS)]
    gmax = functools.reduce(jnp.maximum, gl)
    pg = 1.0 / functools.reduce(jnp.add, [jnp.exp(v - gmax) for v in gl])
    gsel = jnp.full_like(gmax, N_GROUPS - 1).astype(I32)
    for g in range(N_GROUPS - 2, -1, -1):
        gsel = jnp.where(gl[g] == gmax, g, gsel)
    el = []
    for j in range(PER_GROUP):
        v = lg[j:j + 1, :]
        for g in range(1, N_GROUPS):
            v = jnp.where(gsel == g, lg[g * PER_GROUP + j:g * PER_GROUP + j + 1, :], v)
        el.append(v)
    v1 = functools.reduce(jnp.maximum, el)
    i1 = jnp.full_like(gsel, PER_GROUP - 1)
    for j in range(PER_GROUP - 2, -1, -1):
        i1 = jnp.where(el[j] == v1, j, i1)
    rest = [jnp.where(i1 == j, -jnp.inf, el[j]) for j in range(PER_GROUP)]
    v2 = functools.reduce(jnp.maximum, rest)
    i2 = jnp.full_like(gsel, PER_GROUP - 1)
    for j in range(PER_GROUP - 2, -1, -1):
        i2 = jnp.where(jnp.logical_and(rest[j] == v2, i1 != j), j, i2)
    e2 = jnp.exp(v2 - v1)
    w1 = pg / (1.0 + e2)
    w2 = pg * e2 / (1.0 + e2)
    lo = jnp.minimum(i1, i2)
    hi = jnp.maximum(i1, i2)
    w_lo = jnp.where(i1 < i2, w1, w2)
    w_hi = jnp.where(i1 < i2, w2, w1)
    pair = jnp.where(lo == 0, jnp.where(hi == 1, 0, jnp.where(hi == 2, 4, 3)),
                     jnp.where(lo == 1, jnp.where(hi == 2, 1, 5), 2))
    cls = gsel * 6 + pair
    a_is_hi = pair == 1
    return cls, jnp.where(a_is_hi, w_hi, w_lo), jnp.where(a_is_hi, w_lo, w_hi)


def _out_kernel(oa_ref, obl_ref, obc_ref, oc_ref, od_ref, w_ref, x_ref, mod_ref, g_ref, wrh_ref, wrl_ref, br_ref,
                tri_ref, x1_ref, hp_ref, cls_ref, gw_ref, cnt_ref, carry):
    i = pl.program_id(0)

    @pl.when(i == 0)
    def _():
        carry[...] = jnp.zeros_like(carry)

    ob = jnp.where(i < LAT_TILES, obl_ref[...], obc_ref[...])
    y = jnp.dot(oa_ref[...], w_ref[0:GW, :], preferred_element_type=F32)
    y = y + jnp.dot(ob, w_ref[GW:2 * GW, :], preferred_element_type=F32)
    y = y + jnp.dot(oc_ref[...], w_ref[2 * GW:3 * GW, :], preferred_element_type=F32)
    y = y + jnp.dot(od_ref[...], w_ref[3 * GW:4 * GW, :], preferred_element_type=F32)
    x1 = x_ref[...] + mod_ref[2:3, :] * y
    x1_ref[...] = x1
    h2 = _rmsnorm_mod(x1, g_ref[...], mod_ref[3:4, :], mod_ref[4:5, :])

    for c in range(HP_SUB):
        hp_ref[pl.ds(c, TM, stride=HP_SUB), :] = h2[:, c * 128:(c + 1) * 128]

    h_hi, h_lo = _split_bf16(h2)
    lg = jnp.dot(h_hi, wrh_ref[...], preferred_element_type=F32)
    lg = lg + jnp.dot(h_lo, wrh_ref[...], preferred_element_type=F32)
    lg = lg + jnp.dot(h_hi, wrl_ref[...], preferred_element_type=F32)
    lg = jnp.transpose(lg + br_ref[...])[0:R_ROWS, :]
    cls, w_a, w_b = _route(lg)

    onehot = jnp.where(lax.broadcasted_iota(I32, (R_ROWS, TM), 0) == cls, 1.0, 0.0)
    prefix = jnp.dot(onehot.astype(BF16), tri_ref[...], preferred_element_type=F32)
    rank = jnp.sum(onehot * (prefix - 1.0 + carry[:, 0:1]), axis=0, keepdims=True)
    total = carry[...] + jnp.sum(onehot, axis=1, keepdims=True)
    carry[...] = total
    cnt_ref[...] = total.astype(I32)
    cls_ref[...] = jnp.concatenate([cls, rank.astype(I32), jnp.zeros((6, TM), I32)], axis=0)
    gw_ref[...] = jnp.concatenate([w_a, w_b, jnp.zeros((6, TM), F32)], axis=0)


def _out_call(l, o_a, o_b_lat, o_b_ctx, o_c, o_d, w_out, x, mod, g_norm2, wr_hi, wr_lo, b_route):
    row = lambda: pl.BlockSpec((TM, GW), lambda i: (i, 0))
    tri = jnp.asarray(np.triu(np.ones((TM, TM), np.float32)), BF16)
    return pl.pallas_call(
        _out_kernel,
        out_shape=(jax.ShapeDtypeStruct((NT, D), F32), jax.ShapeDtypeStruct((NT * HP_SUB, 128), F32),
                   jax.ShapeDtypeStruct((8, NT), I32), jax.ShapeDtypeStruct((8, NT), F32),
                   jax.ShapeDtypeStruct((R_ROWS, 128), I32)),
        grid=(N_TILES,),
        in_specs=[row(),
                  pl.BlockSpec((TM, GW), lambda i: (jnp.minimum(i, LAT_TILES - 1), 0)),
                  pl.BlockSpec((TM, GW), lambda i: (jnp.maximum(i - LAT_TILES, 0), 0)),
                  row(), row(),
                  pl.BlockSpec((None, D, D), lambda i: (l, 0, 0), pipeline_mode=pl.Buffered(1)),
                  pl.BlockSpec((TM, D), lambda i: (i, 0)),
                  pl.BlockSpec((None, None, 8, D), lambda i: (l, _mod_row(i), 0, 0)),
                  pl.BlockSpec((None, 1, D), lambda i: (l, 0, 0)),
                  pl.BlockSpec((None, D, 128), lambda i: (l, 0, 0)),
                  pl.BlockSpec((None, D, 128), lambda i: (l, 0, 0)),
                  pl.BlockSpec((None, 1, 128), lambda i: (l, 0, 0)),
                  pl.BlockSpec((TM, TM), lambda i: (0, 0))],
        out_specs=(pl.BlockSpec((TM, D), lambda i: (i, 0)),
                   pl.BlockSpec((TM * HP_SUB, 128), lambda i: (i, 0)),
                   pl.BlockSpec((8, TM), lambda i: (0, i)), pl.BlockSpec((8, TM), lambda i: (0, i)),
                   pl.BlockSpec((R_ROWS, 128), lambda i: (0, 0))),
        scratch_shapes=[pltpu.VMEM((R_ROWS, 128), F32)],
        compiler_params=_cparams(1),
        name="out_proj_route",
    )(o_a, o_b_lat, o_b_ctx, o_c, o_d, w_out, x, mod, g_norm2.reshape(DEPTH, 1, D), wr_hi, wr_lo, b_route, tri)


ROW_UNROLL = 8
FILL_BITS = (64, 32, 16, 8, 4, 2, 1)


def _dispatch_kernel(pos_ref, tail_start_ref, tail_n_ref, nvalid_ref, hp_hbm, hs_hbm, src_ref, zbuf, sem, fill_sem):
    i = pl.program_id(0)

    def row_copy(t, p):
        return pltpu.make_async_copy(hp_hbm.at[pl.ds(pl.multiple_of(t * HP_SUB, HP_SUB), HP_SUB), :],
                                     hs_hbm.at[pl.ds(pl.multiple_of(p * HP_SUB, HP_SUB), HP_SUB), :], sem)

    def fill_copies(action):
        def per_class(c, carry):
            n = tail_n_ref[c]
            s = tail_start_ref[c]
            for bit in FILL_BITS:
                @pl.when((n & bit) != 0)
                def _(s=s, bit=bit):
                    action(pltpu.make_async_copy(
                        zbuf.at[pl.ds(0, bit * HP_SUB), :],
                        hs_hbm.at[pl.ds(pl.multiple_of(s * HP_SUB, HP_SUB), bit * HP_SUB), :], fill_sem))
                s = s + (n & bit)
            return carry

        lax.fori_loop(0, N_CLASSES, per_class, 0)

        def per_tile(j, carry):
            action(pltpu.make_async_copy(
                zbuf, hs_hbm.at[pl.ds(pl.multiple_of(j * (TME * HP_SUB), TME * HP_SUB), TME * HP_SUB), :], fill_sem))
            return carry

        lax.fori_loop(nvalid_ref[0], P_TILES, per_tile, 0)

    @pl.when(i == 0)
    def _():
        zbuf[...] = jnp.zeros_like(zbuf)

        def init(p, carry):
            src_ref[p] = 0
            return carry

        lax.fori_loop(0, P_ROWS, init, 0)
        fill_copies(lambda cp: cp.start())

    def start(r, carry):
        t = i * TM + r
        p = pos_ref[t]
        src_ref[p] = t
        row_copy(t, p).start()
        return carry

    lax.fori_loop(0, TM, start, 0, unroll=ROW_UNROLL)

    def wait(r, carry):
        row_copy(0, 0).wait()
        return carry

    lax.fori_loop(0, TM, wait, 0, unroll=ROW_UNROLL)

    @pl.when(i == N_TILES - 1)
    def _():
        fill_copies(lambda cp: cp.wait())


def _dispatch_call(pos, tail_start, tail_n, nvalid, hp):
    return pl.pallas_call(
        _dispatch_kernel,
        out_shape=(jax.ShapeDtypeStruct((P_ROWS * HP_SUB, 128), F32), jax.ShapeDtypeStruct((P_ROWS,), I32)),
        grid_spec=pltpu.PrefetchScalarGridSpec(
            num_scalar_prefetch=4,
            grid=(N_TILES,),
            in_specs=[pl.BlockSpec(memory_space=pl.ANY)],
            out_specs=(pl.BlockSpec(memory_space=pl.ANY), pl.BlockSpec(memory_space=pltpu.SMEM)),
            scratch_shapes=[pltpu.VMEM((TME * HP_SUB, 128), F32), pltpu.SemaphoreType.DMA,
                            pltpu.SemaphoreType.DMA]),
        compiler_params=_cparams(1),
        name="moe_dispatch",
    )(pos, tail_start, tail_n, nvalid, hp)


def _expert_kernel(ea_ref, eb_ref, nvalid_ref, h_ref, gw_ref, w13a_ref, w2a_ref, w13b_ref, w2b_ref, y_ref):
    i = pl.program_id(0)

    @pl.when(i < nvalid_ref[0])
    def _():
        h = jnp.concatenate([h_ref[pl.ds(c, TME, stride=HP_SUB), :].astype(BF16) for c in range(HP_SUB)], axis=1)
        gw = gw_ref[...]
        y = None
        for k, (w13_ref, w2_ref) in enumerate(((w13a_ref, w2a_ref), (w13b_ref, w2b_ref))):
            gu = jnp.dot(h, w13_ref[...], preferred_element_type=F32)
            gate = gu[:, :D_EXPERT]
            hid = gate * jax.nn.sigmoid(gate) * gu[:, D_EXPERT:] * gw[:, k:k + 1]
            yk = jnp.dot(hid.astype(BF16), w2_ref[...], preferred_element_type=F32)
            y = yk if y is None else y + yk
        for c in range(Y_SUB):
            y_ref[pl.ds(c, TME, stride=Y_SUB), :] = y[:, c * 128:(c + 1) * 128]

    @pl.when(i >= nvalid_ref[0])
    def _():
        y_ref[...] = jnp.zeros_like(y_ref)


def _expert_call(l, tile_ea, tile_eb, nvalid, hs, gws, w13, w2):
    return pl.pallas_call(
        _expert_kernel,
        out_shape=jax.ShapeDtypeStruct((P_ROWS * Y_SUB, 128), F32),
        grid_spec=pltpu.PrefetchScalarGridSpec(
            num_scalar_prefetch=3,
            grid=(P_TILES,),
            in_specs=[pl.BlockSpec((TME * HP_SUB, 128), lambda i, ea, eb, nv: (i, 0)),
                      pl.BlockSpec((TME, 2), lambda i, ea, eb, nv: (i, 0)),
                      pl.BlockSpec((None, None, D, 2 * D_EXPERT), lambda i, ea, eb, nv: (l, ea[i], 0, 0)),
                      pl.BlockSpec((None, None, D_EXPERT, D), lambda i, ea, eb, nv: (l, ea[i], 0, 0)),
                      pl.BlockSpec((None, None, D, 2 * D_EXPERT), lambda i, ea, eb, nv: (l, eb[i], 0, 0)),
                      pl.BlockSpec((None, None, D_EXPERT, D), lambda i, ea, eb, nv: (l, eb[i], 0, 0))],
            out_specs=pl.BlockSpec((TME * Y_SUB, 128), lambda i, ea, eb, nv: (i, 0))),
        compiler_params=_cparams(1),
        name="moe_experts",
    )(tile_ea, tile_eb, nvalid, hs, gws, w13, w2, w13, w2)


def _combine_kernel(pos_ref, x1_ref, mod_ref, gf_ref, y_hbm, o_ref, ybuf, sem, *, final):
    i = pl.program_id(0)

    def row_copy(p, r):
        return pltpu.make_async_copy(y_hbm.at[pl.ds(pl.multiple_of(p * Y_SUB, Y_SUB), Y_SUB), :],
                                     ybuf.at[pl.ds(pl.multiple_of(r * Y_SUB, Y_SUB), Y_SUB), :], sem)

    def start(r, carry):
        row_copy(pos_ref[i * TM + r], r).start()
        return carry

    lax.fori_loop(0, TM, start, 0, unroll=ROW_UNROLL)

    def wait(r, carry):
        row_copy(0, r).wait()
        return carry

    lax.fori_loop(0, TM, wait, 0, unroll=ROW_UNROLL)
    y = jnp.concatenate([ybuf[pl.ds(c, TM, stride=Y_SUB), :] for c in range(Y_SUB)], axis=1)
    x2 = x1_ref[...] + mod_ref[5:6, :] * y
    if final:
        ms = jnp.mean(x2 * x2, axis=-1, keepdims=True)
        x2 = x2 * lax.rsqrt(ms + EPS) * gf_ref[...]
    o_ref[...] = x2


def _combine_call(l, pos, x1, mod, g_final, ys, final):
    n_tiles = LAT_TILES if final else N_TILES
    return pl.pallas_call(
        functools.partial(_combine_kernel, final=final),
        out_shape=jax.ShapeDtypeStruct((n_tiles * TM, D), F32),
        grid_spec=pltpu.PrefetchScalarGridSpec(
            num_scalar_prefetch=1,
            grid=(n_tiles,),
            in_specs=[pl.BlockSpec((TM, D), lambda i, pos: (i, 0)),
                      pl.BlockSpec((None, None, 8, D), lambda i, pos: (l, _mod_row(i), 0, 0)),
                      pl.BlockSpec((1, D), lambda i, pos: (0, 0)),
                      pl.BlockSpec(memory_space=pl.ANY)],
            out_specs=pl.BlockSpec((TM, D), lambda i, pos: (i, 0)),
            scratch_shapes=[pltpu.VMEM((TM * Y_SUB, 128), F32), pltpu.SemaphoreType.DMA]),
        compiler_params=_cparams(1),
        name="moe_combine",
    )(pos, x1, mod, g_final.reshape(1, D), ys)


def _slot_plan(cls, rank, counts):
    padded = ((counts + TME - 1) // TME) * TME
    ends = jnp.cumsum(padded)
    starts = ends - padded
    pos = starts[cls] + rank
    tile_start = jnp.arange(P_TILES, dtype=I32) * TME
    tile_cls = jnp.minimum(jnp.sum((tile_start[:, None] >= ends[None, :]).astype(I32), axis=1), N_CLASSES - 1)
    nvalid = (ends[-1] // TME).reshape(1)
    grp = tile_cls // 6
    tile_ea = grp * PER_GROUP + jnp.asarray(_SLOT_A)[tile_cls % 6]
    tile_eb = grp * PER_GROUP + jnp.asarray(_SLOT_B)[tile_cls % 6]
    return pos, starts + counts, padded - counts, tile_ea, tile_eb, nvalid


def _rope_tables():
    t = jnp.arange(SEQ, dtype=I32)
    row = (t // GRID_W).astype(F32)
    col = (t % GRID_W).astype(F32)
    half = HEAD_DIM // 4
    freqs = ROPE_BASE ** (-jnp.arange(0, 2 * half, 2, dtype=F32) / (2 * half))
    ang_r = row[:, None] * freqs
    ang_c = col[:, None] * freqs
    cos_h = jnp.concatenate([jnp.cos(ang_r), jnp.cos(ang_r), jnp.cos(ang_c), jnp.cos(ang_c)], axis=1)
    sin_h = jnp.concatenate([-jnp.sin(ang_r), jnp.sin(ang_r), -jnp.sin(ang_c), jnp.sin(ang_c)], axis=1)
    cos_t = jnp.concatenate([jnp.tile(cos_h, (1, 2)), jnp.ones((TM, 128), F32)], axis=0)
    sin_t = jnp.concatenate([jnp.tile(sin_h, (1, 2)), jnp.zeros((TM, 128), F32)], axis=0)
    return cos_t, sin_t


def kernel(x, c, ctx, c_ctx, w_mod, b_mod, g_norm1, g_norm2, w_in, w_out, attn_sink, sconv_w, cconv_w, cconv_b,
           cnorm_g, cnorm_b, w_route_group, b_route_group, w_route_expert, b_route_expert, w_gate_e, w_up_e,
           w_down_e, g_final):
    xs = jnp.concatenate([x.reshape(N_LAT, D), ctx.reshape(N_CTX, D)], axis=0)
    c8 = jnp.zeros((8, D), F32).at[0:BATCH].set(c).at[BATCH].set(c_ctx)
    q_w, k_w, v_w = w_in[:, :, 0:512], w_in[:, :, 512:640], w_in[:, :, 640:768]
    swap = lambda w: jnp.concatenate([w[:, :, HEAD_DIM:], w[:, :, :HEAD_DIM]], axis=-1)
    wqkv = jnp.concatenate([q_w, k_w, v_w, swap(k_w), swap(v_w)], axis=-1).astype(BF16)
    wf = w_in[:, :, 768:1280].astype(BF16)
    wcv = w_in[:, :, 1280:].astype(BF16)
    w_out_b = w_out.astype(BF16)
    w13 = jnp.concatenate([w_gate_e, w_up_e], axis=-1).astype(BF16)
    w2 = w_down_e.astype(BF16)
    w_route = jnp.concatenate([w_route_expert, w_route_group,
                               jnp.zeros((DEPTH, D, 128 - N_EXPERTS - N_GROUPS), F32)], axis=2)
    wr_hi = w_route.astype(BF16)
    wr_lo = (w_route - wr_hi.astype(F32)).astype(BF16)
    b_route = jnp.concatenate([b_route_expert, b_route_group,
                               jnp.zeros((DEPTH, 128 - N_EXPERTS - N_GROUPS), F32)], axis=1).reshape(DEPTH, 1, 128)
    cos_t, sin_t = _rope_tables()

    mod = _mod_call(c8, w_mod, b_mod)
    mod = mod.reshape(DEPTH, 8, N_MOD, D)[:, :3]
    mod = jnp.pad(mod, ((0, 0), (0, 0), (0, 8 - N_MOD), (0, 0)))

    out = None
    for l in range(DEPTH):
        qkv, f, cv = _in_call(l, xs, mod, g_norm1, wqkv, wf, wcv, cos_t, sin_t)
        o_a = _attn_call(l, attn_sink, qkv)
        o_b_lat = _fft_call(f, 64, 0, BATCH)
        o_b_ctx = _fft_call(f, 16, N_LAT // CTX, BATCH)
        o_c, o_d = _conv_call(l, cv, sconv_w, cconv_w, cconv_b, cnorm_g, cnorm_b)
        x1, hp, cls8, gw8, cnt = _out_call(l, o_a, o_b_lat, o_b_ctx, o_c, o_d, w_out_b, xs, mod, g_norm2,
                                           wr_hi, wr_lo, b_route)
        pos, tail_start, tail_n, tile_ea, tile_eb, nvalid = _slot_plan(cls8[0], cls8[1], cnt[:N_CLASSES, 0])
        hs, src = _dispatch_call(pos, tail_start, tail_n, nvalid, hp)
        gws = jnp.transpose(gw8[0:2])[src]
        ys = _expert_call(l, tile_ea, tile_eb, nvalid, hs, gws, w13, w2)
        final = l == DEPTH - 1
        res = _combine_call(l, pos, x1, mod, g_final, ys, final)
        if final:
            out = res
        else:
            xs = res
    return out.reshape(BATCH, SEQ, D)
```

```python
import functools

import numpy as np
import jax
import jax.numpy as jnp
from jax import lax
from jax.experimental import pallas as pl
from jax.experimental.pallas import tpu as pltpu

F32 = jnp.float32
BF16 = jnp.bfloat16
I32 = jnp.int32

D = 2048
BATCH = 2
SEQ = 4096
CTX = 256
DEPTH = 2
GRID_W = 64
N_LAT = BATCH * SEQ
N_CTX = BATCH * CTX
NT = N_LAT + N_CTX
GW = 512
HEAD_DIM = 64
N_HEADS = 8
WINDOW = 128
ROPE_BASE = 10000.0
N_MOD = 6
EPS = 1e-6
NEG_INF = -1e30
N_GROUPS = 4
PER_GROUP = 4
N_EXPERTS = 16
D_EXPERT = 512
N_CLASSES = N_GROUPS * 6
CONF_W = 31
CONF_PAD = 15

TM = 256
N_TILES = NT // TM
LAT_TILES = N_LAT // TM
SEQ_TILES = SEQ // TM
TQ = 128
HALO = 16
TME = 128
P_ROWS = NT + N_CLASSES * TME
P_TILES = P_ROWS // TME
SLAB = D // 128
VMEM_LIMIT = 56 * 1024 * 1024


def _cparams(n_axes=1, **kw):
    return pltpu.CompilerParams(dimension_semantics=("arbitrary",) * n_axes,
                                vmem_limit_bytes=VMEM_LIMIT, **kw)


def _split_bf16(v):
    hi = v.astype(BF16)
    lo = (v - hi.astype(F32)).astype(BF16)
    return hi, lo


def _mod_row(i):
    return jnp.minimum(i // SEQ_TILES, 2)


MOD_TN = 1536


def _mod_kernel(c_ref, w_ref, b_ref, o_ref):
    c = c_ref[...]
    s = c * jax.nn.sigmoid(c)
    s_hi, s_lo = _split_bf16(s)
    w_hi, w_lo = _split_bf16(w_ref[...])
    acc = jnp.dot(s_hi, w_hi, preferred_element_type=F32)
    acc = acc + jnp.dot(s_lo, w_hi, preferred_element_type=F32)
    acc = acc + jnp.dot(s_hi, w_lo, preferred_element_type=F32)
    o_ref[...] = acc + b_ref[...]


def _mod_call(c8, w_mod, b_mod):
    n = N_MOD * D
    return pl.pallas_call(
        _mod_kernel,
        out_shape=jax.ShapeDtypeStruct((DEPTH, 8, n), F32),
        grid=(DEPTH, n // MOD_TN),
        in_specs=[
            pl.BlockSpec((8, D), lambda l, j: (0, 0)),
            pl.BlockSpec((None, D, MOD_TN), lambda l, j: (l, 0, j)),
            pl.BlockSpec((None, 1, MOD_TN), lambda l, j: (l, 0, j)),
        ],
        out_specs=pl.BlockSpec((None, 8, MOD_TN), lambda l, j: (l, 0, j)),
        compiler_params=_cparams(2),
        name="adaln_mod",
    )(c8, w_mod, b_mod.reshape(DEPTH, 1, n))


QKV_W = 1024
CV_IN = 5 * GW
CV_OUT = 3 * GW


def _rmsnorm_mod(x, g, shift, scale):
    ms = jnp.mean(x * x, axis=-1, keepdims=True)
    y = x * lax.rsqrt(ms + EPS) * g
    return y * (1.0 + scale) + shift


def _stream_specs(ctx_base):
    return [pl.BlockSpec((TM, D), lambda i, *_: (jnp.minimum(i, LAT_TILES - 1), 0)),
            pl.BlockSpec((TM, D), lambda i, *_: (ctx_base + jnp.maximum(i - LAT_TILES, 0), 0))]


def _stream_tile(xl_ref, xc_ref):
    return jnp.where(pl.program_id(0) < LAT_TILES, xl_ref[...], xc_ref[...])


def _in_kernel(xl_ref, xc_ref, mod_ref, g_ref, wqkv_ref, wf_ref, wcv_ref, cos_ref, sin_ref,
               qkv_ref, f_ref, cv_ref):
    h = _rmsnorm_mod(_stream_tile(xl_ref, xc_ref), g_ref[...], mod_ref[0:1, :], mod_ref[1:2, :])
    hb = h.astype(BF16)
    qkv = jnp.dot(hb, wqkv_ref[...], preferred_element_type=F32)
    cos = cos_ref[...]
    sin = sin_ref[...]
    lane = lax.broadcasted_iota(I32, (TM, 128), 1)
    first_half = (lane % 32) < 16

    def rope(t):
        rot = jnp.where(first_half, pltpu.roll(t, 112, 1), pltpu.roll(t, 16, 1))
        return t * cos + rot * sin

    scale = HEAD_DIM ** -0.5
    for c in range(4):
        qkv_ref[:, c * 128:(c + 1) * 128] = (rope(qkv[:, c * 128:(c + 1) * 128]) * scale).astype(BF16)
    qkv_ref[:, 512:640] = rope(qkv[:, 512:640]).astype(BF16)
    qkv_ref[:, 640:768] = qkv[:, 640:768].astype(BF16)
    qkv_ref[:, 768:896] = rope(qkv[:, 768:896]).astype(BF16)
    qkv_ref[:, 896:1024] = qkv[:, 896:1024].astype(BF16)

    f_ref[...] = jnp.dot(hb, wf_ref[...], preferred_element_type=F32).astype(BF16)

    cv = jnp.dot(hb, wcv_ref[...], preferred_element_type=F32)
    cv_ref[:, 0:GW] = cv[:, 0:GW]
    cv_ref[:, GW:2 * GW] = cv[:, GW:2 * GW] * cv[:, 2 * GW:3 * GW]
    cv_ref[:, 2 * GW:3 * GW] = cv[:, 3 * GW:4 * GW] * jax.nn.sigmoid(cv[:, 4 * GW:5 * GW])


def _in_call(l, x_lat, x_ctx, ctx_base, mod, g_norm1, wqkv, wf, wcv, cos_t, sin_t):
    const = dict(pipeline_mode=pl.Buffered(1))
    return pl.pallas_call(
        _in_kernel,
        out_shape=(jax.ShapeDtypeStruct((NT, QKV_W), BF16),
                   jax.ShapeDtypeStruct((NT, GW), BF16),
                   jax.ShapeDtypeStruct((NT, CV_OUT), F32)),
        grid=(N_TILES,),
        in_specs=_stream_specs(ctx_base) + [
            pl.BlockSpec((None, None, 8, D), lambda i: (l, _mod_row(i), 0, 0)),
            pl.BlockSpec((None, 1, D), lambda i: (l, 0, 0)),
            pl.BlockSpec((None, D, QKV_W), lambda i: (l, 0, 0), **const),
            pl.BlockSpec((None, D, GW), lambda i: (l, 0, 0), **const),
            pl.BlockSpec((None, D, CV_IN), lambda i: (l, 0, 0), **const),
            pl.BlockSpec((TM, 128), lambda i: (jnp.where(i < LAT_TILES, i % SEQ_TILES, SEQ_TILES), 0)),
            pl.BlockSpec((TM, 128), lambda i: (jnp.where(i < LAT_TILES, i % SEQ_TILES, SEQ_TILES), 0)),
        ],
        out_specs=(pl.BlockSpec((TM, QKV_W), lambda i: (i, 0)),
                   pl.BlockSpec((TM, GW), lambda i: (i, 0)),
                   pl.BlockSpec((TM, CV_OUT), lambda i: (i, 0))),
        compiler_params=_cparams(1),
        name="norm_in_proj",
    )(x_lat, x_ctx, mod, g_norm1.reshape(DEPTH, 1, D), wqkv, wf, wcv, cos_t, sin_t)


N_QBLK_LAT = N_LAT // TQ
N_QBLK = NT // TQ
QBLK_PER_SEQ = SEQ // TQ
KWIN = 3 * TQ


def _attn_heads(sink_ref, l, q, keys, vals, mask, o_ref):
    lane = lax.broadcasted_iota(I32, (TQ, 128), 1)
    row2 = lax.broadcasted_iota(I32, (2 * TQ, 1), 0)
    zero = jnp.zeros((TQ, 128), BF16)
    out = [jnp.zeros((TQ, 128), F32) for _ in range(4)]
    for hk in range(2):
        c0, c1 = 2 * hk, 2 * hk + 1
        for p in range(2):
            kx = keys[0] if hk == p else keys[1]
            vx = vals[0] if hk == p else vals[1]
            in_half = (lane // HEAD_DIM) == p
            qs = jnp.concatenate([jnp.where(in_half, q[:, c0 * 128:(c0 + 1) * 128], zero),
                                  jnp.where(in_half, q[:, c1 * 128:(c1 + 1) * 128], zero)], axis=0)
            s = lax.dot_general(qs, kx, (((1,), (1,)), ((), ())), preferred_element_type=F32)
            if mask is not None:
                s = jnp.where(mask, s, NEG_INF)
            sink = jnp.where(row2 < TQ, sink_ref[l, 2 * c0 + p], sink_ref[l, 2 * c1 + p])
            m = jnp.maximum(jnp.max(s, axis=-1, keepdims=True), sink)
            e = jnp.exp(s - m)
            den = jnp.sum(e, axis=-1, keepdims=True) + jnp.exp(sink - m)
            o = jnp.dot(e.astype(BF16), vx, preferred_element_type=F32) / den
            out[c0] = out[c0] + jnp.where(in_half, o[:TQ], 0.0)
            out[c1] = out[c1] + jnp.where(in_half, o[TQ:], 0.0)
    for c in range(4):
        o_ref[:, c * 128:(c + 1) * 128] = out[c].astype(BF16)


def _attn_kernel(sink_ref, q_ref, k_ref, v_ref, ks_ref, vs_ref, kc_ref, vc_ref, kcs_ref, vcs_ref, o_ref, *, l):
    i = pl.program_id(0)
    q = q_ref[...]

    @pl.when(i < N_QBLK_LAT)
    def _latent():
        n = i % QBLK_PER_SEQ
        ws = pl.multiple_of(jnp.clip((n - 1) * TQ, 0, SEQ - KWIN), TQ)
        keys = [jnp.concatenate([r[pl.ds(ws, KWIN), :], c[...]], axis=0)
                for r, c in ((k_ref, kc_ref), (ks_ref, kcs_ref))]
        vals = [jnp.concatenate([r[pl.ds(ws, KWIN), :], c[...]], axis=0)
                for r, c in ((v_ref, vc_ref), (vs_ref, vcs_ref))]
        nk = KWIN + CTX
        qpos = n * TQ + lax.broadcasted_iota(I32, (2 * TQ, nk), 0) % TQ
        col = lax.broadcasted_iota(I32, (2 * TQ, nk), 1)
        mask = (jnp.abs(ws + col - qpos) <= WINDOW) | (col >= KWIN)
        _attn_heads(sink_ref, l, q, keys, vals, mask, o_ref)

    @pl.when(i >= N_QBLK_LAT)
    def _context():
        _attn_heads(sink_ref, l, q, [kc_ref[...], kcs_ref[...]], [vc_ref[...], vcs_ref[...]], None, o_ref)


def _attn_call(l, attn_sink, qkv):
    def seq_blk(i):
        return jnp.minimum(i // QBLK_PER_SEQ, 1)

    def ctx_blk(i):
        b = jnp.where(i < N_QBLK_LAT, i // QBLK_PER_SEQ, (i - N_QBLK_LAT) // (CTX // TQ))
        return N_LAT // CTX + b

    seq_specs = [pl.BlockSpec((SEQ, 128), functools.partial(lambda i, col: (seq_blk(i), col), col=col))
                 for col in (4, 5, 6, 7)]
    ctx_specs = [pl.BlockSpec((CTX, 128), functools.partial(lambda i, col: (ctx_blk(i), col), col=col))
                 for col in (4, 5, 6, 7)]
    return pl.pallas_call(
        functools.partial(_attn_kernel, l=l),
        out_shape=jax.ShapeDtypeStruct((NT, GW), BF16),
        grid=(N_QBLK,),
        in_specs=[pl.BlockSpec(memory_space=pltpu.SMEM),
                  pl.BlockSpec((TQ, GW), lambda i: (i, 0))] + seq_specs + ctx_specs,
        out_specs=pl.BlockSpec((TQ, GW), lambda i: (i, 0)),
        compiler_params=_cparams(1),
        name="window_attention",
    )(attn_sink, qkv, *([qkv] * 8))


FC = 256
FFT_UNROLL = 4


def _dft_tables(n):
    t = n * n
    j = np.arange(n)
    ang = 2.0 * np.pi * np.outer(j, j) / n
    cn, sn = np.cos(ang), np.sin(ang)
    m1 = np.block([[cn, sn], [-sn, cn]])
    m2 = np.concatenate([cn, sn], axis=1)
    angc = 2.0 * np.pi * np.outer(np.arange(128), np.arange(128)) / 128.0
    wc = np.concatenate([np.cos(angc), -np.sin(angc)], axis=1)
    angt = 2.0 * np.pi * np.outer(j, j) / t
    tw_re = np.broadcast_to(np.cos(angt)[:, :, None], (n, n, 128))
    tw_im = np.broadcast_to(-np.sin(angt)[:, :, None], (n, n, 128))
    def hi_lo(m):
        return jnp.stack(_split_bf16(jnp.asarray(m, F32)))

    return (hi_lo(m1), hi_lo(m2), hi_lo(wc),
            jnp.asarray(np.ascontiguousarray(tw_re), F32), jnp.asarray(np.ascontiguousarray(tw_im), F32))


def _fft_kernel(f_ref, m1_ref, m2_ref, wc_ref, twr_ref, twi_ref, o_ref, are, aim, hre, him, *, n):
    t = n * n
    ng = FC // 128
    for g in range(ng):
        fg = f_ref[:, g * 128:(g + 1) * 128]
        a = (jnp.dot(fg, wc_ref[0], preferred_element_type=F32)
             + jnp.dot(fg, wc_ref[1], preferred_element_type=F32))
        are[g] = a[:, :128]
        aim[g] = a[:, 128:]

    def table_dot(m_ref, v):
        return (jnp.dot(m_ref[0], v, preferred_element_type=F32)
                + jnp.dot(m_ref[1], v, preferred_element_type=F32))

    def gather_rows(re, im, first):
        rows = [pl.ds(first + u, n, stride=n) for u in range(FFT_UNROLL)]
        top = jnp.concatenate([re[g, r, :] for r in rows for g in range(ng)], axis=1)
        bot = jnp.concatenate([im[g, r, :] for r in rows for g in range(ng)], axis=1)
        return jnp.concatenate([top, bot], axis=0).astype(BF16)

    def stage1(it, carry):
        t1 = it * FFT_UNROLL
        h = table_dot(m1_ref, gather_rows(are, aim, t1))
        for u in range(FFT_UNROLL):
            dst = pl.ds(pl.multiple_of((t1 + u) * n, n), n)
            twr = twr_ref[t1 + u]
            twi = twi_ref[t1 + u]
            for g in range(ng):
                lanes = slice((u * ng + g) * 128, (u * ng + g + 1) * 128)
                h_re, h_im = h[:n, lanes], h[n:, lanes]
                hre[g, dst, :] = h_re * twr - h_im * twi
                him[g, dst, :] = h_re * twi + h_im * twr
        return carry

    lax.fori_loop(0, n // FFT_UNROLL, stage1, 0)

    def stage2(it, carry):
        k2 = it * FFT_UNROLL
        z = table_dot(m2_ref, gather_rows(hre, him, k2))
        z = z * ((t * 128.0) ** -0.5)
        for u in range(FFT_UNROLL):
            for g in range(ng):
                are[g, pl.ds(k2 + u, n, stride=n), :] = z[:, (u * ng + g) * 128:(u * ng + g + 1) * 128]
        return carry

    lax.fori_loop(0, n // FFT_UNROLL, stage2, 0)
    for g in range(ng):
        o_ref[:, g * 128:(g + 1) * 128] = are[g].astype(BF16)


def _fft_call(f, n, first_blk, n_seq):
    t = n * n
    m1, m2, wc, twr, twi = _dft_tables(n)
    full = lambda shape: pl.BlockSpec(shape, lambda b, h: (0,) * len(shape))
    return pl.pallas_call(
        functools.partial(_fft_kernel, n=n),
        out_shape=jax.ShapeDtypeStruct((n_seq * t, GW), BF16),
        grid=(n_seq, GW // FC),
        in_specs=[pl.BlockSpec((t, FC), lambda b, h: (first_blk + b, h)),
                  full((2, 2 * n, 2 * n)), full((2, n, 2 * n)), full((2, 128, 256)),
                  full((n, n, 128)), full((n, n, 128))],
        out_specs=pl.BlockSpec((t, FC), lambda b, h: (b, h)),
        scratch_shapes=[pltpu.VMEM((FC // 128, t, 128), F32)] * 4,
        compiler_params=_cparams(2),
        name="fourier_mix_%d" % t,
    )(f, m1, m2, wc, twr, twi)


CONV_ROWS = 32


def _conv_kernel(sb_ref, p_ref, pp_ref, pn_ref, g_ref, gp_ref, gn_ref, sw_ref, cw_ref, cb_ref, ng_ref, nb_ref,
                 oc_ref, od_ref, pbuf, gbuf):
    i = pl.program_id(0)
    seq_start = jnp.logical_or(i % SEQ_TILES == 0, i >= LAT_TILES)
    seq_end = jnp.logical_or(i % SEQ_TILES == SEQ_TILES - 1, i >= LAT_TILES)
    for buf, main, prev, nxt in ((pbuf, p_ref, pp_ref, pn_ref), (gbuf, g_ref, gp_ref, gn_ref)):
        buf[0:HALO, :] = jnp.where(seq_start, 0.0, prev[...])
        buf[HALO:HALO + TM, :] = main[...]
        buf[HALO + TM:HALO + TM + HALO, :] = jnp.where(seq_end, 0.0, nxt[...])
    sw = sw_ref[...]
    cw = cw_ref[...]
    for r0 in range(0, TM, CONV_ROWS):
        base = HALO + r0
        acc = sw[0:1, :] * pbuf[base - 1:base - 1 + CONV_ROWS, :]
        acc = acc + sw[1:2, :] * pbuf[base:base + CONV_ROWS, :]
        acc = acc + sw[2:3, :] * pbuf[base + 1:base + 1 + CONV_ROWS, :]
        oc_ref[r0:r0 + CONV_ROWS, :] = (sb_ref[r0:r0 + CONV_ROWS, :] * acc).astype(BF16)

        z = cw[0:1, :] * gbuf[base - CONF_PAD:base - CONF_PAD + CONV_ROWS, :]
        for j in range(1, CONF_W):
            o = base - CONF_PAD + j
            z = z + cw[j:j + 1, :] * gbuf[o:o + CONV_ROWS, :]
        z = z + cb_ref[...]
        mu = jnp.mean(z, axis=-1, keepdims=True)
        zc = z - mu
        var = jnp.mean(zc * zc, axis=-1, keepdims=True)
        y = zc * lax.rsqrt(var + EPS) * ng_ref[...] + nb_ref[...]
        od_ref[r0:r0 + CONV_ROWS, :] = (y * jax.nn.sigmoid(y)).astype(BF16)


def _conv_call(l, cv, sconv_w, cconv_w, cconv_b, cnorm_g, cnorm_b):
    hb = TM // HALO
    last = NT // HALO - 1

    def halo_specs(col):
        return [pl.BlockSpec((TM, GW), lambda i: (i, col)),
                pl.BlockSpec((HALO, GW), lambda i: (jnp.maximum(i * hb - 1, 0), col)),
                pl.BlockSpec((HALO, GW), lambda i: (jnp.minimum((i + 1) * hb, last), col))]

    vec = lambda: pl.BlockSpec((None, 1, GW), lambda i: (l, 0, 0))
    cw = jnp.pad(cconv_w, ((0, 0), (0, 32 - CONF_W), (0, 0)))
    sw = jnp.pad(sconv_w, ((0, 0), (0, 8 - 3), (0, 0)))
    return pl.pallas_call(
        _conv_kernel,
        out_shape=(jax.ShapeDtypeStruct((NT, GW), BF16), jax.ShapeDtypeStruct((NT, GW), BF16)),
        grid=(N_TILES,),
        in_specs=[pl.BlockSpec((TM, GW), lambda i: (i, 0))] + halo_specs(1) + halo_specs(2) + [
            pl.BlockSpec((None, 8, GW), lambda i: (l, 0, 0)),
            pl.BlockSpec((None, 32, GW), lambda i: (l, 0, 0)),
            vec(), vec(), vec()],
        out_specs=(pl.BlockSpec((TM, GW), lambda i: (i, 0)), pl.BlockSpec((TM, GW), lambda i: (i, 0))),
        scratch_shapes=[pltpu.VMEM((TM + 2 * HALO, GW), F32)] * 2,
        compiler_params=_cparams(1),
        name="conv_mixers",
    )(cv, cv, cv, cv, cv, cv, cv, sw, cw, cconv_b.reshape(DEPTH, 1, GW), cnorm_g.reshape(DEPTH, 1, GW),
      cnorm_b.reshape(DEPTH, 1, GW))


R_ROWS = 32
_SLOT_A = np.array([0, 2, 2, 0, 0, 1], np.int32)
_SLOT_B = np.array([1, 1, 3, 3, 2, 3], np.int32)


def _route(lg):
    gl = [lg[N_EXPERTS + g:N_EXPERTS + g + 1, :] for g in range(N_GROUPS)]
    gmax = functools.reduce(jnp.maximum, gl)
    pg = 1.0 / functools.reduce(jnp.add, [jnp.exp(v - gmax) for v in gl])
    gsel = jnp.full_like(gmax, N_GROUPS - 1).astype(I32)
    for g in range(N_GROUPS - 2, -1, -1):
        gsel = jnp.where(gl[g] == gmax, g, gsel)
    el = []
    for j in range(PER_GROUP):
        v = lg[j:j + 1, :]
        for g in range(1, N_GROUPS):
            v = jnp.where(gsel == g, lg[g * PER_GROUP + j:g * PER_GROUP + j + 1, :], v)
        el.append(v)
    v1 = functools.reduce(jnp.maximum, el)
    i1 = jnp.full_like(gsel, PER_GROUP - 1)
    for j in range(PER_GROUP - 2, -1, -1):
        i1 = jnp.where(el[j] == v1, j, i1)
    rest = [jnp.where(i1 == j, -jnp.inf, el[j]) for j in range(PER_GROUP)]
    v2 = functools.reduce(jnp.maximum, rest)
    i2 = jnp.full_like(gsel, PER_GROUP - 1)
    for j in range(PER_GROUP - 2, -1, -1):
        i2 = jnp.where(jnp.logical_and(rest[j] == v2, i1 != j), j, i2)
    e2 = jnp.exp(v2 - v1)
    w1 = pg / (1.0 + e2)
    w2 = pg * e2 / (1.0 + e2)
    lo = jnp.minimum(i1, i2)
    hi = jnp.maximum(i1, i2)
    w_lo = jnp.where(i1 < i2, w1, w2)
    w_hi = jnp.where(i1 < i2, w2, w1)
    pair = jnp.where(lo == 0, jnp.where(hi == 1, 0, jnp.where(hi == 2, 4, 3)),
                     jnp.where(lo == 1, jnp.where(hi == 2, 1, 5), 2))
    cls = gsel * 6 + pair
    a_is_hi = pair == 1
    return cls, jnp.where(a_is_hi, w_hi, w_lo), jnp.where(a_is_hi, w_lo, w_hi)


def _rows_to_slabs(v, stage_ref, out_ref):
    n = v.shape[0]
    for c in range(SLAB):
        stage_ref[pl.ds(c, n, stride=SLAB), :] = v[:, c * 128:(c + 1) * 128]
    out_ref[...] = stage_ref[...].astype(out_ref.dtype)


def _slabs_to_rows(src_ref, stage_ref, n):
    stage_ref[...] = src_ref[...].astype(F32)
    return jnp.concatenate([stage_ref[pl.ds(c, n, stride=SLAB), :] for c in range(SLAB)], axis=1)


def _out_kernel(oa_ref, obl_ref, obc_ref, oc_ref, od_ref, w_ref, xl_ref, xc_ref, mod_ref, g_ref, wrh_ref, wrl_ref,
                br_ref, tri_ref, x1_ref, hp_ref, cls_ref, gw_ref, cnt_ref, carry, stage):
    i = pl.program_id(0)

    @pl.when(i == 0)
    def _():
        carry[...] = jnp.zeros_like(carry)

    ob = jnp.where(i < LAT_TILES, obl_ref[...], obc_ref[...])
    y = jnp.dot(oa_ref[...], w_ref[0:GW, :], preferred_element_type=F32)
    y = y + jnp.dot(ob, w_ref[GW:2 * GW, :], preferred_element_type=F32)
    y = y + jnp.dot(oc_ref[...], w_ref[2 * GW:3 * GW, :], preferred_element_type=F32)
    y = y + jnp.dot(od_ref[...], w_ref[3 * GW:4 * GW, :], preferred_element_type=F32)
    x1 = _stream_tile(xl_ref, xc_ref) + mod_ref[2:3, :] * y
    x1_ref[...] = x1
    h2 = _rmsnorm_mod(x1, g_ref[...], mod_ref[3:4, :], mod_ref[4:5, :])
    _rows_to_slabs(h2, stage, hp_ref)

    h_hi, h_lo = _split_bf16(h2)
    lg = jnp.dot(h_hi, wrh_ref[...], preferred_element_type=F32)
    lg = lg + jnp.dot(h_lo, wrh_ref[...], preferred_element_type=F32)
    lg = lg + jnp.dot(h_hi, wrl_ref[...], preferred_element_type=F32)
    lg = jnp.transpose(lg + br_ref[...])[0:R_ROWS, :]
    cls, w_a, w_b = _route(lg)

    onehot = jnp.where(lax.broadcasted_iota(I32, (R_ROWS, TM), 0) == cls, 1.0, 0.0)
    prefix = jnp.dot(onehot.astype(BF16), tri_ref[...], preferred_element_type=F32)
    rank = jnp.sum(onehot * (prefix - 1.0 + carry[:, 0:1]), axis=0, keepdims=True)
    total = carry[...] + jnp.sum(onehot, axis=1, keepdims=True)
    carry[...] = total
    cnt_ref[...] = total.astype(I32)
    cls_ref[...] = jnp.concatenate([cls, rank.astype(I32), jnp.zeros((6, TM), I32)], axis=0)
    gw_ref[...] = jnp.concatenate([w_a, w_b, jnp.zeros((6, TM), F32)], axis=0)


def _out_call(l, o_a, o_b_lat, o_b_ctx, o_c, o_d, w_out, x_lat, x_ctx, ctx_base, mod, g_norm2, wr_hi, wr_lo,
              b_route):
    row = lambda: pl.BlockSpec((TM, GW), lambda i: (i, 0))
    tri = jnp.asarray(np.triu(np.ones((TM, TM), np.float32)), BF16)
    return pl.pallas_call(
        _out_kernel,
        out_shape=(jax.ShapeDtypeStruct((NT, D), F32), jax.ShapeDtypeStruct((NT * SLAB, 128), BF16),
                   jax.ShapeDtypeStruct((8, NT), I32), jax.ShapeDtypeStruct((8, NT), F32),
                   jax.ShapeDtypeStruct((R_ROWS, 128), I32)),
        grid=(N_TILES,),
        in_specs=[row(),
                  pl.BlockSpec((TM, GW), lambda i: (jnp.minimum(i, LAT_TILES - 1), 0)),
                  pl.BlockSpec((TM, GW), lambda i: (jnp.maximum(i - LAT_TILES, 0), 0)),
                  row(), row(),
                  pl.BlockSpec((None, D, D), lambda i: (l, 0, 0), pipeline_mode=pl.Buffered(1))]
        + _stream_specs(ctx_base) + [
                  pl.BlockSpec((None, None, 8, D), lambda i: (l, _mod_row(i), 0, 0)),
                  pl.BlockSpec((None, 1, D), lambda i: (l, 0, 0)),
                  pl.BlockSpec((None, D, 128), lambda i: (l, 0, 0)),
                  pl.BlockSpec((None, D, 128), lambda i: (l, 0, 0)),
                  pl.BlockSpec((None, 1, 128), lambda i: (l, 0, 0)),
                  pl.BlockSpec((TM, TM), lambda i: (0, 0))],
        out_specs=(pl.BlockSpec((TM, D), lambda i: (i, 0)),
                   pl.BlockSpec((TM * SLAB, 128), lambda i: (i, 0)),
                   pl.BlockSpec((8, TM), lambda i: (0, i)), pl.BlockSpec((8, TM), lambda i: (0, i)),
                   pl.BlockSpec((R_ROWS, 128), lambda i: (0, 0))),
        scratch_shapes=[pltpu.VMEM((R_ROWS, 128), F32), pltpu.VMEM((TM * SLAB, 128), F32)],
        compiler_params=_cparams(1),
        name="out_proj_route",
    )(o_a, o_b_lat, o_b_ctx, o_c, o_d, w_out, x_lat, x_ctx, mod, g_norm2.reshape(DEPTH, 1, D), wr_hi, wr_lo,
      b_route, tri)


ROW_UNROLL = 8
FILL_BITS = (64, 32, 16, 8, 4, 2, 1)


def _dispatch_kernel(pos_ref, gwa_ref, gwb_ref, tail_start_ref, tail_n_ref, nvalid_ref, hp_ref, hs_hbm, gsa_ref,
                     gsb_ref, zbuf, sem, fill_sem):
    i = pl.program_id(0)

    def row_copy(r, p):
        return pltpu.make_async_copy(hp_ref.at[pl.ds(pl.multiple_of(r * SLAB, SLAB), SLAB), :],
                                     hs_hbm.at[pl.ds(pl.multiple_of(p * SLAB, SLAB), SLAB), :], sem)

    def fill_copies(action):
        def per_class(c, carry):
            n = tail_n_ref[c]
            s = tail_start_ref[c]
            for bit in FILL_BITS:
                @pl.when((n & bit) != 0)
                def _(s=s, bit=bit):
                    action(pltpu.make_async_copy(
                        zbuf.at[pl.ds(0, bit * SLAB), :],
                        hs_hbm.at[pl.ds(pl.multiple_of(s * SLAB, SLAB), bit * SLAB), :], fill_sem))
                s = s + (n & bit)
            return carry

        lax.fori_loop(0, N_CLASSES, per_class, 0)

        def per_tile(j, carry):
            action(pltpu.make_async_copy(
                zbuf, hs_hbm.at[pl.ds(pl.multiple_of(j * (TME * SLAB), TME * SLAB), TME * SLAB), :], fill_sem))
            return carry

        lax.fori_loop(nvalid_ref[0], P_TILES, per_tile, 0)

    @pl.when(i == 0)
    def _():
        zbuf[...] = jnp.zeros_like(zbuf)

        def init(p, carry):
            gsa_ref[p] = 0.0
            gsb_ref[p] = 0.0
            return carry

        lax.fori_loop(0, P_ROWS, init, 0)
        fill_copies(lambda cp: cp.start())

    def start(r, carry):
        t = i * TM + r
        p = pos_ref[t]
        gsa_ref[p] = gwa_ref[t]
        gsb_ref[p] = gwb_ref[t]
        row_copy(r, p).start()
        return carry

    lax.fori_loop(0, TM, start, 0, unroll=ROW_UNROLL)

    def wait(r, carry):
        row_copy(0, 0).wait()
        return carry

    lax.fori_loop(0, TM, wait, 0, unroll=ROW_UNROLL)

    @pl.when(i == N_TILES - 1)
    def _():
        fill_copies(lambda cp: cp.wait())


def _dispatch_call(pos, gwa, gwb, tail_start, tail_n, nvalid, hp):
    return pl.pallas_call(
        _dispatch_kernel,
        out_shape=(jax.ShapeDtypeStruct((P_ROWS * SLAB, 128), BF16), jax.ShapeDtypeStruct((P_ROWS,), F32),
                   jax.ShapeDtypeStruct((P_ROWS,), F32)),
        grid_spec=pltpu.PrefetchScalarGridSpec(
            num_scalar_prefetch=6,
            grid=(N_TILES,),
            in_specs=[pl.BlockSpec((TM * SLAB, 128), lambda i, *_: (i, 0))],
            out_specs=(pl.BlockSpec(memory_space=pl.ANY), pl.BlockSpec(memory_space=pltpu.SMEM),
                       pl.BlockSpec(memory_space=pltpu.SMEM)),
            scratch_shapes=[pltpu.VMEM((TME * SLAB, 128), BF16), pltpu.SemaphoreType.DMA,
                            pltpu.SemaphoreType.DMA]),
        compiler_params=_cparams(1),
        name="moe_dispatch",
    )(pos, gwa, gwb, tail_start, tail_n, nvalid, hp)


def _gate_column(g_ref, base):
    sub = lax.broadcasted_iota(I32, (8, 128), 0)
    pieces = []
    for g in range(TME // 8):
        v = jnp.zeros((8, 128), F32)
        for k in range(8):
            v = jnp.where(sub == k, g_ref[base + 8 * g + k], v)
        pieces.append(v)
    return jnp.concatenate(pieces, axis=0)


def _expert_kernel(ea_ref, eb_ref, nvalid_ref, gsa_ref, gsb_ref, h_ref, w13a_ref, w2a_ref, w13b_ref, w2b_ref, y_ref,
                   stage):
    i = pl.program_id(0)

    @pl.when(i < nvalid_ref[0])
    def _():
        h = _slabs_to_rows(h_ref, stage, TME).astype(BF16)
        y = None
        for g_ref, w13_ref, w2_ref in ((gsa_ref, w13a_ref, w2a_ref), (gsb_ref, w13b_ref, w2b_ref)):
            gu = jnp.dot(h, w13_ref[...], preferred_element_type=F32)
            gate = gu[:, :D_EXPERT]
            gcol = _gate_column(g_ref, i * TME)
            hid = gate * jax.nn.sigmoid(gate) * gu[:, D_EXPERT:] * jnp.concatenate([gcol] * (D_EXPERT // 128), axis=1)
            yk = jnp.dot(hid.astype(BF16), w2_ref[...], preferred_element_type=F32)
            y = yk if y is None else y + yk
        _rows_to_slabs(y, stage, y_ref)

    @pl.when(i >= nvalid_ref[0])
    def _():
        y_ref[...] = jnp.zeros_like(y_ref)


def _expert_call(l, tile_ea, tile_eb, nvalid, gsa, gsb, hs, w13, w2):
    return pl.pallas_call(
        _expert_kernel,
        out_shape=jax.ShapeDtypeStruct((P_ROWS * SLAB, 128), BF16),
        grid_spec=pltpu.PrefetchScalarGridSpec(
            num_scalar_prefetch=5,
            grid=(P_TILES,),
            in_specs=[pl.BlockSpec((TME * SLAB, 128), lambda i, *_: (i, 0)),
                      pl.BlockSpec((None, None, D, 2 * D_EXPERT), lambda i, ea, eb, *_: (l, ea[i], 0, 0)),
                      pl.BlockSpec((None, None, D_EXPERT, D), lambda i, ea, eb, *_: (l, ea[i], 0, 0)),
                      pl.BlockSpec((None, None, D, 2 * D_EXPERT), lambda i, ea, eb, *_: (l, eb[i], 0, 0)),
                      pl.BlockSpec((None, None, D_EXPERT, D), lambda i, ea, eb, *_: (l, eb[i], 0, 0))],
            out_specs=pl.BlockSpec((TME * SLAB, 128), lambda i, *_: (i, 0)),
            scratch_shapes=[pltpu.VMEM((TME * SLAB, 128), F32)]),
        compiler_params=_cparams(1),
        name="moe_experts",
    )(tile_ea, tile_eb, nvalid, gsa, gsb, hs, w13, w2, w13, w2)


def _combine_kernel(pos_ref, x1_ref, mod_ref, gf_ref, y_hbm, o_ref, ybuf, stage, sem, *, final):
    i = pl.program_id(0)

    def row_copy(p, r):
        return pltpu.make_async_copy(y_hbm.at[pl.ds(pl.multiple_of(p * SLAB, SLAB), SLAB), :],
                                     ybuf.at[pl.ds(pl.multiple_of(r * SLAB, SLAB), SLAB), :], sem)

    def start(r, carry):
        row_copy(pos_ref[i * TM + r], r).start()
        return carry

    lax.fori_loop(0, TM, start, 0, unroll=ROW_UNROLL)

    def wait(r, carry):
        row_copy(0, r).wait()
        return carry

    lax.fori_loop(0, TM, wait, 0, unroll=ROW_UNROLL)
    x2 = x1_ref[...] + mod_ref[5:6, :] * _slabs_to_rows(ybuf, stage, TM)
    if final:
        ms = jnp.mean(x2 * x2, axis=-1, keepdims=True)
        x2 = x2 * lax.rsqrt(ms + EPS) * gf_ref[...]
    o_ref[...] = x2


def _combine_call(l, pos, x1, mod, g_final, ys, final):
    n_tiles = LAT_TILES if final else N_TILES
    return pl.pallas_call(
        functools.partial(_combine_kernel, final=final),
        out_shape=jax.ShapeDtypeStruct((n_tiles * TM, D), F32),
        grid_spec=pltpu.PrefetchScalarGridSpec(
            num_scalar_prefetch=1,
            grid=(n_tiles,),
            in_specs=[pl.BlockSpec((TM, D), lambda i, pos: (i, 0)),
                      pl.BlockSpec((None, None, 8, D), lambda i, pos: (l, _mod_row(i), 0, 0)),
                      pl.BlockSpec((1, D), lambda i, pos: (0, 0)),
                      pl.BlockSpec(memory_space=pl.ANY)],
            out_specs=pl.BlockSpec((TM, D), lambda i, pos: (i, 0)),
            scratch_shapes=[pltpu.VMEM((TM * SLAB, 128), BF16), pltpu.VMEM((TM * SLAB, 128), F32),
                            pltpu.SemaphoreType.DMA]),
        compiler_params=_cparams(1),
        name="moe_combine",
    )(pos, x1, mod, g_final.reshape(1, D), ys)


def _slot_plan(cls, rank, counts):
    padded = ((counts + TME - 1) // TME) * TME
    ends = jnp.cumsum(padded)
    starts = ends - padded
    pos = starts[cls] + rank
    tile_start = jnp.arange(P_TILES, dtype=I32) * TME
    tile_cls = jnp.minimum(jnp.sum((tile_start[:, None] >= ends[None, :]).astype(I32), axis=1), N_CLASSES - 1)
    nvalid = (ends[-1] // TME).reshape(1)
    grp = tile_cls // 6
    tile_ea = grp * PER_GROUP + jnp.asarray(_SLOT_A)[tile_cls % 6]
    tile_eb = grp * PER_GROUP + jnp.asarray(_SLOT_B)[tile_cls % 6]
    return pos, starts + counts, padded - counts, tile_ea, tile_eb, nvalid


def _rope_tables():
    t = jnp.arange(SEQ, dtype=I32)
    row = (t // GRID_W).astype(F32)
    col = (t % GRID_W).astype(F32)
    half = HEAD_DIM // 4
    freqs = ROPE_BASE ** (-jnp.arange(0, 2 * half, 2, dtype=F32) / (2 * half))
    ang_r = row[:, None] * freqs
    ang_c = col[:, None] * freqs
    cos_h = jnp.concatenate([jnp.cos(ang_r), jnp.cos(ang_r), jnp.cos(ang_c), jnp.cos(ang_c)], axis=1)
    sin_h = jnp.concatenate([-jnp.sin(ang_r), jnp.sin(ang_r), -jnp.sin(ang_c), jnp.sin(ang_c)], axis=1)
    cos_t = jnp.concatenate([jnp.tile(cos_h, (1, 2)), jnp.ones((TM, 128), F32)], axis=0)
    sin_t = jnp.concatenate([jnp.tile(sin_h, (1, 2)), jnp.zeros((TM, 128), F32)], axis=0)
    return cos_t, sin_t


def kernel(x, c, ctx, c_ctx, w_mod, b_mod, g_norm1, g_norm2, w_in, w_out, attn_sink, sconv_w, cconv_w, cconv_b,
           cnorm_g, cnorm_b, w_route_group, b_route_group, w_route_expert, b_route_expert, w_gate_e, w_up_e,
           w_down_e, g_final):
    x_lat, x_ctx, ctx_base = x.reshape(N_LAT, D), ctx.reshape(N_CTX, D), 0
    c8 = jnp.zeros((8, D), F32).at[0:BATCH].set(c).at[BATCH].set(c_ctx)
    q_w, k_w, v_w = w_in[:, :, 0:512], w_in[:, :, 512:640], w_in[:, :, 640:768]
    swap = lambda w: jnp.concatenate([w[:, :, HEAD_DIM:], w[:, :, :HEAD_DIM]], axis=-1)
    wqkv = jnp.concatenate([q_w, k_w, v_w, swap(k_w), swap(v_w)], axis=-1).astype(BF16)
    wf = w_in[:, :, 768:1280].astype(BF16)
    wcv = w_in[:, :, 1280:].astype(BF16)
    w_out_b = w_out.astype(BF16)
    w13 = jnp.concatenate([w_gate_e, w_up_e], axis=-1).astype(BF16)
    w2 = w_down_e.astype(BF16)
    w_route = jnp.concatenate([w_route_expert, w_route_group,
                               jnp.zeros((DEPTH, D, 128 - N_EXPERTS - N_GROUPS), F32)], axis=2)
    wr_hi = w_route.astype(BF16)
    wr_lo = (w_route - wr_hi.astype(F32)).astype(BF16)
    b_route = jnp.concatenate([b_route_expert, b_route_group,
                               jnp.zeros((DEPTH, 128 - N_EXPERTS - N_GROUPS), F32)], axis=1).reshape(DEPTH, 1, 128)
    cos_t, sin_t = _rope_tables()

    mod = _mod_call(c8, w_mod, b_mod)
    mod = mod.reshape(DEPTH, 8, N_MOD, D)[:, :3]
    mod = jnp.pad(mod, ((0, 0), (0, 0), (0, 8 - N_MOD), (0, 0)))

    out = None
    for l in range(DEPTH):
        qkv, f, cv = _in_call(l, x_lat, x_ctx, ctx_base, mod, g_norm1, wqkv, wf, wcv, cos_t, sin_t)
        o_a = _attn_call(l, attn_sink, qkv)
        o_b_lat = _fft_call(f, 64, 0, BATCH)
        o_b_ctx = _fft_call(f, 16, N_LAT // CTX, BATCH)
        o_c, o_d = _conv_call(l, cv, sconv_w, cconv_w, cconv_b, cnorm_g, cnorm_b)
        x1, hp, cls8, gw8, cnt = _out_call(l, o_a, o_b_lat, o_b_ctx, o_c, o_d, w_out_b, x_lat, x_ctx, ctx_base, mod,
                                           g_norm2, wr_hi, wr_lo, b_route)
        pos, tail_start, tail_n, tile_ea, tile_eb, nvalid = _slot_plan(cls8[0], cls8[1], cnt[:N_CLASSES, 0])
        hs, gsa, gsb = _dispatch_call(pos, gw8[0], gw8[1], tail_start, tail_n, nvalid, hp)
        ys = _expert_call(l, tile_ea, tile_eb, nvalid, gsa, gsb, hs, w13, w2)
        final = l == DEPTH - 1
        res = _combine_call(l, pos, x1, mod, g_final, ys, final)
        if final:
            out = res
        else:
            x_lat, x_ctx, ctx_base = res, res, LAT_TILES
    return out.reshape(BATCH, SEQ, D)
```

```python
import functools

import numpy as np
import jax
import jax.numpy as jnp
from jax import lax
from jax.experimental import pallas as pl
from jax.experimental.pallas import tpu as pltpu

F32 = jnp.float32
BF16 = jnp.bfloat16
I32 = jnp.int32

D = 2048
BATCH = 2
SEQ = 4096
CTX = 256
DEPTH = 2
GRID_W = 64
N_LAT = BATCH * SEQ
N_CTX = BATCH * CTX
NT = N_LAT + N_CTX
GW = 512
HEAD_DIM = 64
N_HEADS = 8
WINDOW = 128
ROPE_BASE = 10000.0
N_MOD = 6
EPS = 1e-6
NEG_INF = -1e30
N_GROUPS = 4
PER_GROUP = 4
N_EXPERTS = 16
D_EXPERT = 512
N_CLASSES = N_GROUPS * 6
CONF_W = 31
CONF_PAD = 15

TM = 256
N_TILES = NT // TM
LAT_TILES = N_LAT // TM
SEQ_TILES = SEQ // TM
TQ = 128
HALO = 16
TME = 128
P_ROWS = NT + N_CLASSES * TME
P_TILES = P_ROWS // TME
SLAB = D // 128
VMEM_LIMIT = 56 * 1024 * 1024


def _cparams(n_axes=1, **kw):
    return pltpu.CompilerParams(dimension_semantics=("arbitrary",) * n_axes,
                                vmem_limit_bytes=VMEM_LIMIT, **kw)


def _split_bf16(v):
    hi = v.astype(BF16)
    lo = (v - hi.astype(F32)).astype(BF16)
    return hi, lo


def _mod_row(i):
    return jnp.minimum(i // SEQ_TILES, 2)


MOD_TN = 1536


def _mod_kernel(c_ref, w_ref, b_ref, o_ref):
    c = c_ref[...]
    s = c * jax.nn.sigmoid(c)
    s_hi, s_lo = _split_bf16(s)
    w_hi, w_lo = _split_bf16(w_ref[...])
    acc = jnp.dot(s_hi, w_hi, preferred_element_type=F32)
    acc = acc + jnp.dot(s_lo, w_hi, preferred_element_type=F32)
    acc = acc + jnp.dot(s_hi, w_lo, preferred_element_type=F32)
    o_ref[...] = acc + b_ref[...]


def _mod_call(c8, w_mod, b_mod):
    n = N_MOD * D
    return pl.pallas_call(
        _mod_kernel,
        out_shape=jax.ShapeDtypeStruct((DEPTH, 8, n), F32),
        grid=(DEPTH, n // MOD_TN),
        in_specs=[
            pl.BlockSpec((8, D), lambda l, j: (0, 0)),
            pl.BlockSpec((None, D, MOD_TN), lambda l, j: (l, 0, j)),
            pl.BlockSpec((None, 1, MOD_TN), lambda l, j: (l, 0, j)),
        ],
        out_specs=pl.BlockSpec((None, 8, MOD_TN), lambda l, j: (l, 0, j)),
        compiler_params=_cparams(2),
        name="adaln_mod",
    )(c8, w_mod, b_mod.reshape(DEPTH, 1, n))


QKV_W = 1024
CV_IN = 5 * GW
CV_OUT = 3 * GW


def _rmsnorm_mod(x, g, shift, scale):
    ms = jnp.mean(x * x, axis=-1, keepdims=True)
    y = x * lax.rsqrt(ms + EPS) * g
    return y * (1.0 + scale) + shift


def _stream_specs(ctx_base):
    return [pl.BlockSpec((TM, D), lambda i, *_: (jnp.minimum(i, LAT_TILES - 1), 0)),
            pl.BlockSpec((TM, D), lambda i, *_: (ctx_base + jnp.maximum(i - LAT_TILES, 0), 0))]


def _stream_tile(xl_ref, xc_ref):
    return jnp.where(pl.program_id(0) < LAT_TILES, xl_ref[...], xc_ref[...])


CAST_STEPS = 32


def _cast_spec(rows, cols, l):
    return pl.BlockSpec((rows, cols), lambda i, *_: (l * CAST_STEPS + jnp.minimum(i, CAST_STEPS - 1), 0))


def _cast_out_spec(rows, cols):
    return pl.BlockSpec((rows, cols), lambda i, *_: (jnp.minimum(i, CAST_STEPS - 1), 0))


def _in_kernel(xl_ref, xc_ref, mod_ref, g_ref, wqkv_ref, wf_ref, wcv_ref, cos_ref, sin_ref, w1_ref, w3_ref,
               qkv_ref, f_ref, cv_ref, w1b_ref, w3b_ref):
    w1b_ref[...] = w1_ref[...].astype(BF16)
    w3b_ref[...] = w3_ref[...].astype(BF16)
    h = _rmsnorm_mod(_stream_tile(xl_ref, xc_ref), g_ref[...], mod_ref[0:1, :], mod_ref[1:2, :])
    hb = h.astype(BF16)
    qkv = jnp.dot(hb, wqkv_ref[...], preferred_element_type=F32)
    cos = cos_ref[...]
    sin = sin_ref[...]
    lane = lax.broadcasted_iota(I32, (TM, 128), 1)
    first_half = (lane % 32) < 16

    def rope(t):
        rot = jnp.where(first_half, pltpu.roll(t, 112, 1), pltpu.roll(t, 16, 1))
        return t * cos + rot * sin

    scale = HEAD_DIM ** -0.5
    for c in range(4):
        qkv_ref[:, c * 128:(c + 1) * 128] = (rope(qkv[:, c * 128:(c + 1) * 128]) * scale).astype(BF16)
    qkv_ref[:, 512:640] = rope(qkv[:, 512:640]).astype(BF16)
    qkv_ref[:, 640:768] = qkv[:, 640:768].astype(BF16)
    qkv_ref[:, 768:896] = rope(qkv[:, 768:896]).astype(BF16)
    qkv_ref[:, 896:1024] = qkv[:, 896:1024].astype(BF16)

    f_ref[...] = jnp.dot(hb, wf_ref[...], preferred_element_type=F32).astype(BF16)

    cv = jnp.dot(hb, wcv_ref[...], preferred_element_type=F32)
    cv_ref[:, 0:GW] = cv[:, 0:GW]
    cv_ref[:, GW:2 * GW] = cv[:, GW:2 * GW] * cv[:, 2 * GW:3 * GW]
    cv_ref[:, 2 * GW:3 * GW] = cv[:, 3 * GW:4 * GW] * jax.nn.sigmoid(cv[:, 4 * GW:5 * GW])


W13_ROWS = N_EXPERTS * D // CAST_STEPS
W2_ROWS = N_EXPERTS * D_EXPERT // CAST_STEPS


def _in_call(l, x_lat, x_ctx, ctx_base, mod, g_norm1, wqkv, wf, wcv, cos_t, sin_t, w_gate_e, w_up_e):
    const = dict(pipeline_mode=pl.Buffered(1))
    flat = lambda w: w.reshape(DEPTH * N_EXPERTS * D, D_EXPERT)
    return pl.pallas_call(
        _in_kernel,
        out_shape=(jax.ShapeDtypeStruct((NT, QKV_W), BF16),
                   jax.ShapeDtypeStruct((NT, GW), BF16),
                   jax.ShapeDtypeStruct((NT, CV_OUT), F32),
                   jax.ShapeDtypeStruct((N_EXPERTS * D, D_EXPERT), BF16),
                   jax.ShapeDtypeStruct((N_EXPERTS * D, D_EXPERT), BF16)),
        grid=(N_TILES,),
        in_specs=_stream_specs(ctx_base) + [
            pl.BlockSpec((None, None, 8, D), lambda i: (l, _mod_row(i), 0, 0)),
            pl.BlockSpec((None, 1, D), lambda i: (l, 0, 0)),
            pl.BlockSpec((None, D, QKV_W), lambda i: (l, 0, 0), **const),
            pl.BlockSpec((None, D, GW), lambda i: (l, 0, 0), **const),
            pl.BlockSpec((None, D, CV_IN), lambda i: (l, 0, 0), **const),
            pl.BlockSpec((TM, 128), lambda i: (jnp.where(i < LAT_TILES, i % SEQ_TILES, SEQ_TILES), 0)),
            pl.BlockSpec((TM, 128), lambda i: (jnp.where(i < LAT_TILES, i % SEQ_TILES, SEQ_TILES), 0)),
            _cast_spec(W13_ROWS, D_EXPERT, l), _cast_spec(W13_ROWS, D_EXPERT, l),
        ],
        out_specs=(pl.BlockSpec((TM, QKV_W), lambda i: (i, 0)),
                   pl.BlockSpec((TM, GW), lambda i: (i, 0)),
                   pl.BlockSpec((TM, CV_OUT), lambda i: (i, 0)),
                   _cast_out_spec(W13_ROWS, D_EXPERT), _cast_out_spec(W13_ROWS, D_EXPERT)),
        compiler_params=_cparams(1),
        name="norm_in_proj",
    )(x_lat, x_ctx, mod, g_norm1.reshape(DEPTH, 1, D), wqkv, wf, wcv, cos_t, sin_t, flat(w_gate_e), flat(w_up_e))


N_QBLK_LAT = N_LAT // TQ
N_QBLK = NT // TQ
QBLK_PER_SEQ = SEQ // TQ
KWIN = 3 * TQ


def _attn_heads(sink_ref, l, q, keys, vals, mask, o_ref):
    lane = lax.broadcasted_iota(I32, (TQ, 128), 1)
    row2 = lax.broadcasted_iota(I32, (2 * TQ, 1), 0)
    zero = jnp.zeros((TQ, 128), BF16)
    out = [jnp.zeros((TQ, 128), F32) for _ in range(4)]
    for hk in range(2):
        c0, c1 = 2 * hk, 2 * hk + 1
        for p in range(2):
            kx = keys[0] if hk == p else keys[1]
            vx = vals[0] if hk == p else vals[1]
            in_half = (lane // HEAD_DIM) == p
            qs = jnp.concatenate([jnp.where(in_half, q[:, c0 * 128:(c0 + 1) * 128], zero),
                                  jnp.where(in_half, q[:, c1 * 128:(c1 + 1) * 128], zero)], axis=0)
            s = lax.dot_general(qs, kx, (((1,), (1,)), ((), ())), preferred_element_type=F32)
            if mask is not None:
                s = jnp.where(mask, s, NEG_INF)
            sink = jnp.where(row2 < TQ, sink_ref[l, 2 * c0 + p], sink_ref[l, 2 * c1 + p])
            m = jnp.maximum(jnp.max(s, axis=-1, keepdims=True), sink)
            e = jnp.exp(s - m)
            den = jnp.sum(e, axis=-1, keepdims=True) + jnp.exp(sink - m)
            o = jnp.dot(e.astype(BF16), vx, preferred_element_type=F32) / den
            out[c0] = out[c0] + jnp.where(in_half, o[:TQ], 0.0)
            out[c1] = out[c1] + jnp.where(in_half, o[TQ:], 0.0)
    for c in range(4):
        o_ref[:, c * 128:(c + 1) * 128] = out[c].astype(BF16)


def _attn_kernel(sink_ref, q_ref, k_ref, v_ref, ks_ref, vs_ref, kc_ref, vc_ref, kcs_ref, vcs_ref, o_ref, *, l):
    i = pl.program_id(0)
    q = q_ref[...]

    @pl.when(i < N_QBLK_LAT)
    def _latent():
        n = i % QBLK_PER_SEQ
        ws = pl.multiple_of(jnp.clip((n - 1) * TQ, 0, SEQ - KWIN), TQ)
        keys = [jnp.concatenate([r[pl.ds(ws, KWIN), :], c[...]], axis=0)
                for r, c in ((k_ref, kc_ref), (ks_ref, kcs_ref))]
        vals = [jnp.concatenate([r[pl.ds(ws, KWIN), :], c[...]], axis=0)
                for r, c in ((v_ref, vc_ref), (vs_ref, vcs_ref))]
        nk = KWIN + CTX
        qpos = n * TQ + lax.broadcasted_iota(I32, (2 * TQ, nk), 0) % TQ
        col = lax.broadcasted_iota(I32, (2 * TQ, nk), 1)
        mask = (jnp.abs(ws + col - qpos) <= WINDOW) | (col >= KWIN)
        _attn_heads(sink_ref, l, q, keys, vals, mask, o_ref)

    @pl.when(i >= N_QBLK_LAT)
    def _context():
        _attn_heads(sink_ref, l, q, [kc_ref[...], kcs_ref[...]], [vc_ref[...], vcs_ref[...]], None, o_ref)


def _attn_call(l, attn_sink, qkv):
    def seq_blk(i):
        return jnp.minimum(i // QBLK_PER_SEQ, 1)

    def ctx_blk(i):
        b = jnp.where(i < N_QBLK_LAT, i // QBLK_PER_SEQ, (i - N_QBLK_LAT) // (CTX // TQ))
        return N_LAT // CTX + b

    seq_specs = [pl.BlockSpec((SEQ, 128), functools.partial(lambda i, col: (seq_blk(i), col), col=col))
                 for col in (4, 5, 6, 7)]
    ctx_specs = [pl.BlockSpec((CTX, 128), functools.partial(lambda i, col: (ctx_blk(i), col), col=col))
                 for col in (4, 5, 6, 7)]
    return pl.pallas_call(
        functools.partial(_attn_kernel, l=l),
        out_shape=jax.ShapeDtypeStruct((NT, GW), BF16),
        grid=(N_QBLK,),
        in_specs=[pl.BlockSpec(memory_space=pltpu.SMEM),
                  pl.BlockSpec((TQ, GW), lambda i: (i, 0))] + seq_specs + ctx_specs,
        out_specs=pl.BlockSpec((TQ, GW), lambda i: (i, 0)),
        compiler_params=_cparams(1),
        name="window_attention",
    )(attn_sink, qkv, *([qkv] * 8))


FC = 256
FFT_UNROLL = 4


def _dft_tables(n):
    t = n * n
    j = np.arange(n)
    ang = 2.0 * np.pi * np.outer(j, j) / n
    cn, sn = np.cos(ang), np.sin(ang)
    m1 = np.block([[cn, sn], [-sn, cn]])
    m2 = np.concatenate([cn, sn], axis=1)
    angc = 2.0 * np.pi * np.outer(np.arange(128), np.arange(128)) / 128.0
    wc = np.concatenate([np.cos(angc), -np.sin(angc)], axis=1)
    angt = 2.0 * np.pi * np.outer(j, j) / t
    tw_re = np.broadcast_to(np.cos(angt)[:, :, None], (n, n, 128))
    tw_im = np.broadcast_to(-np.sin(angt)[:, :, None], (n, n, 128))
    def hi_lo(m):
        return jnp.stack(_split_bf16(jnp.asarray(m, F32)))

    return (hi_lo(m1), hi_lo(m2), hi_lo(wc),
            jnp.asarray(np.ascontiguousarray(tw_re), F32), jnp.asarray(np.ascontiguousarray(tw_im), F32))


def _fft_kernel(f_ref, m1_ref, m2_ref, wc_ref, twr_ref, twi_ref, o_ref, are, aim, hre, him, *, n):
    t = n * n
    ng = FC // 128
    for g in range(ng):
        fg = f_ref[:, g * 128:(g + 1) * 128]
        a = (jnp.dot(fg, wc_ref[0], preferred_element_type=F32)
             + jnp.dot(fg, wc_ref[1], preferred_element_type=F32))
        are[g] = a[:, :128]
        aim[g] = a[:, 128:]

    def table_dot(m_ref, v):
        return (jnp.dot(m_ref[0], v, preferred_element_type=F32)
                + jnp.dot(m_ref[1], v, preferred_element_type=F32))

    def gather_rows(re, im, first):
        rows = [pl.ds(first + u, n, stride=n) for u in range(FFT_UNROLL)]
        top = jnp.concatenate([re[g, r, :] for r in rows for g in range(ng)], axis=1)
        bot = jnp.concatenate([im[g, r, :] for r in rows for g in range(ng)], axis=1)
        return jnp.concatenate([top, bot], axis=0).astype(BF16)

    def stage1(it, carry):
        t1 = it * FFT_UNROLL
        h = table_dot(m1_ref, gather_rows(are, aim, t1))
        for u in range(FFT_UNROLL):
            dst = pl.ds(pl.multiple_of((t1 + u) * n, n), n)
            twr = twr_ref[t1 + u]
            twi = twi_ref[t1 + u]
            for g in range(ng):
                lanes = slice((u * ng + g) * 128, (u * ng + g + 1) * 128)
                h_re, h_im = h[:n, lanes], h[n:, lanes]
                hre[g, dst, :] = h_re * twr - h_im * twi
                him[g, dst, :] = h_re * twi + h_im * twr
        return carry

    lax.fori_loop(0, n // FFT_UNROLL, stage1, 0)

    def stage2(it, carry):
        k2 = it * FFT_UNROLL
        z = table_dot(m2_ref, gather_rows(hre, him, k2))
        z = z * ((t * 128.0) ** -0.5)
        for u in range(FFT_UNROLL):
            for g in range(ng):
                are[g, pl.ds(k2 + u, n, stride=n), :] = z[:, (u * ng + g) * 128:(u * ng + g + 1) * 128]
        return carry

    lax.fori_loop(0, n // FFT_UNROLL, stage2, 0)
    for g in range(ng):
        o_ref[:, g * 128:(g + 1) * 128] = are[g].astype(BF16)


def _fft_call(f, n, first_blk, n_seq):
    t = n * n
    m1, m2, wc, twr, twi = _dft_tables(n)
    full = lambda shape: pl.BlockSpec(shape, lambda b, h: (0,) * len(shape))
    return pl.pallas_call(
        functools.partial(_fft_kernel, n=n),
        out_shape=jax.ShapeDtypeStruct((n_seq * t, GW), BF16),
        grid=(n_seq, GW // FC),
        in_specs=[pl.BlockSpec((t, FC), lambda b, h: (first_blk + b, h)),
                  full((2, 2 * n, 2 * n)), full((2, n, 2 * n)), full((2, 128, 256)),
                  full((n, n, 128)), full((n, n, 128))],
        out_specs=pl.BlockSpec((t, FC), lambda b, h: (b, h)),
        scratch_shapes=[pltpu.VMEM((FC // 128, t, 128), F32)] * 4,
        compiler_params=_cparams(2),
        name="fourier_mix_%d" % t,
    )(f, m1, m2, wc, twr, twi)


CONV_ROWS = 32


def _conv_kernel(sb_ref, p_ref, pp_ref, pn_ref, g_ref, gp_ref, gn_ref, sw_ref, cw_ref, cb_ref, ng_ref, nb_ref,
                 oc_ref, od_ref, pbuf, gbuf):
    i = pl.program_id(0)
    seq_start = jnp.logical_or(i % SEQ_TILES == 0, i >= LAT_TILES)
    seq_end = jnp.logical_or(i % SEQ_TILES == SEQ_TILES - 1, i >= LAT_TILES)
    for buf, main, prev, nxt in ((pbuf, p_ref, pp_ref, pn_ref), (gbuf, g_ref, gp_ref, gn_ref)):
        buf[0:HALO, :] = jnp.where(seq_start, 0.0, prev[...])
        buf[HALO:HALO + TM, :] = main[...]
        buf[HALO + TM:HALO + TM + HALO, :] = jnp.where(seq_end, 0.0, nxt[...])
    sw = sw_ref[...]
    cw = cw_ref[...]
    for r0 in range(0, TM, CONV_ROWS):
        base = HALO + r0
        acc = sw[0:1, :] * pbuf[base - 1:base - 1 + CONV_ROWS, :]
        acc = acc + sw[1:2, :] * pbuf[base:base + CONV_ROWS, :]
        acc = acc + sw[2:3, :] * pbuf[base + 1:base + 1 + CONV_ROWS, :]
        oc_ref[r0:r0 + CONV_ROWS, :] = (sb_ref[r0:r0 + CONV_ROWS, :] * acc).astype(BF16)

        z = cw[0:1, :] * gbuf[base - CONF_PAD:base - CONF_PAD + CONV_ROWS, :]
        for j in range(1, CONF_W):
            o = base - CONF_PAD + j
            z = z + cw[j:j + 1, :] * gbuf[o:o + CONV_ROWS, :]
        z = z + cb_ref[...]
        mu = jnp.mean(z, axis=-1, keepdims=True)
        zc = z - mu
        var = jnp.mean(zc * zc, axis=-1, keepdims=True)
        y = zc * lax.rsqrt(var + EPS) * ng_ref[...] + nb_ref[...]
        od_ref[r0:r0 + CONV_ROWS, :] = (y * jax.nn.sigmoid(y)).astype(BF16)


def _conv_call(l, cv, sconv_w, cconv_w, cconv_b, cnorm_g, cnorm_b):
    hb = TM // HALO
    last = NT // HALO - 1

    def halo_specs(col):
        return [pl.BlockSpec((TM, GW), lambda i: (i, col)),
                pl.BlockSpec((HALO, GW), lambda i: (jnp.maximum(i * hb - 1, 0), col)),
                pl.BlockSpec((HALO, GW), lambda i: (jnp.minimum((i + 1) * hb, last), col))]

    vec = lambda: pl.BlockSpec((None, 1, GW), lambda i: (l, 0, 0))
    cw = jnp.pad(cconv_w, ((0, 0), (0, 32 - CONF_W), (0, 0)))
    sw = jnp.pad(sconv_w, ((0, 0), (0, 8 - 3), (0, 0)))
    return pl.pallas_call(
        _conv_kernel,
        out_shape=(jax.ShapeDtypeStruct((NT, GW), BF16), jax.ShapeDtypeStruct((NT, GW), BF16)),
        grid=(N_TILES,),
        in_specs=[pl.BlockSpec((TM, GW), lambda i: (i, 0))] + halo_specs(1) + halo_specs(2) + [
            pl.BlockSpec((None, 8, GW), lambda i: (l, 0, 0)),
            pl.BlockSpec((None, 32, GW), lambda i: (l, 0, 0)),
            vec(), vec(), vec()],
        out_specs=(pl.BlockSpec((TM, GW), lambda i: (i, 0)), pl.BlockSpec((TM, GW), lambda i: (i, 0))),
        scratch_shapes=[pltpu.VMEM((TM + 2 * HALO, GW), F32)] * 2,
        compiler_params=_cparams(1),
        name="conv_mixers",
    )(cv, cv, cv, cv, cv, cv, cv, sw, cw, cconv_b.reshape(DEPTH, 1, GW), cnorm_g.reshape(DEPTH, 1, GW),
      cnorm_b.reshape(DEPTH, 1, GW))


R_ROWS = 32
_SLOT_A = np.array([0, 2, 2, 0, 0, 1], np.int32)
_SLOT_B = np.array([1, 1, 3, 3, 2, 3], np.int32)


def _route(lg):
    gl = [lg[N_EXPERTS + g:N_EXPERTS + g + 1, :] for g in range(N_GROUPS)]
    gmax = functools.reduce(jnp.maximum, gl)
    pg = 1.0 / functools.reduce(jnp.add, [jnp.exp(v - gmax) for v in gl])
    gsel = jnp.full_like(gmax, N_GROUPS - 1).astype(I32)
    for g in range(N_GROUPS - 2, -1, -1):
        gsel = jnp.where(gl[g] == gmax, g, gsel)
    el = []
    for j in range(PER_GROUP):
        v = lg[j:j + 1, :]
        for g in range(1, N_GROUPS):
            v = jnp.where(gsel == g, lg[g * PER_GROUP + j:g * PER_GROUP + j + 1, :], v)
        el.append(v)
    v1 = functools.reduce(jnp.maximum, el)
    i1 = jnp.full_like(gsel, PER_GROUP - 1)
    for j in range(PER_GROUP - 2, -1, -1):
        i1 = jnp.where(el[j] == v1, j, i1)
    rest = [jnp.where(i1 == j, -jnp.inf, el[j]) for j in range(PER_GROUP)]
    v2 = functools.reduce(jnp.maximum, rest)
    i2 = jnp.full_like(gsel, PER_GROUP - 1)
    for j in range(PER_GROUP - 2, -1, -1):
        i2 = jnp.where(jnp.logical_and(rest[j] == v2, i1 != j), j, i2)
    e2 = jnp.exp(v2 - v1)
    w1 = pg / (1.0 + e2)
    w2 = pg * e2 / (1.0 + e2)
    lo = jnp.minimum(i1, i2)
    hi = jnp.maximum(i1, i2)
    w_lo = jnp.where(i1 < i2, w1, w2)
    w_hi = jnp.where(i1 < i2, w2, w1)
    pair = jnp.where(lo == 0, jnp.where(hi == 1, 0, jnp.where(hi == 2, 4, 3)),
                     jnp.where(lo == 1, jnp.where(hi == 2, 1, 5), 2))
    cls = gsel * 6 + pair
    a_is_hi = pair == 1
    return cls, jnp.where(a_is_hi, w_hi, w_lo), jnp.where(a_is_hi, w_lo, w_hi)


def _rows_to_slabs(v, stage_ref, out_ref):
    n = v.shape[0]
    for c in range(SLAB):
        stage_ref[pl.ds(c, n, stride=SLAB), :] = v[:, c * 128:(c + 1) * 128]
    out_ref[...] = stage_ref[...].astype(out_ref.dtype)


def _slabs_to_rows(src_ref, stage_ref, n):
    stage_ref[...] = src_ref[...].astype(F32)
    return jnp.concatenate([stage_ref[pl.ds(c, n, stride=SLAB), :] for c in range(SLAB)], axis=1)


def _out_kernel(oa_ref, obl_ref, obc_ref, oc_ref, od_ref, w_ref, xl_ref, xc_ref, mod_ref, g_ref, wrh_ref, wrl_ref,
                br_ref, tri_ref, w2_ref, x1_ref, hp_ref, cls_ref, gw_ref, cnt_ref, w2b_ref, carry, stage):
    i = pl.program_id(0)
    w2b_ref[...] = w2_ref[...].astype(BF16)

    @pl.when(i == 0)
    def _():
        carry[...] = jnp.zeros_like(carry)

    ob = jnp.where(i < LAT_TILES, obl_ref[...], obc_ref[...])
    y = jnp.dot(oa_ref[...], w_ref[0:GW, :], preferred_element_type=F32)
    y = y + jnp.dot(ob, w_ref[GW:2 * GW, :], preferred_element_type=F32)
    y = y + jnp.dot(oc_ref[...], w_ref[2 * GW:3 * GW, :], preferred_element_type=F32)
    y = y + jnp.dot(od_ref[...], w_ref[3 * GW:4 * GW, :], preferred_element_type=F32)
    x1 = _stream_tile(xl_ref, xc_ref) + mod_ref[2:3, :] * y
    x1_ref[...] = x1
    h2 = _rmsnorm_mod(x1, g_ref[...], mod_ref[3:4, :], mod_ref[4:5, :])
    _rows_to_slabs(h2, stage, hp_ref)

    h_hi, h_lo = _split_bf16(h2)
    lg = jnp.dot(h_hi, wrh_ref[...], preferred_element_type=F32)
    lg = lg + jnp.dot(h_lo, wrh_ref[...], preferred_element_type=F32)
    lg = lg + jnp.dot(h_hi, wrl_ref[...], preferred_element_type=F32)
    lg = jnp.transpose(lg + br_ref[...])[0:R_ROWS, :]
    cls, w_a, w_b = _route(lg)

    onehot = jnp.where(lax.broadcasted_iota(I32, (R_ROWS, TM), 0) == cls, 1.0, 0.0)
    prefix = jnp.dot(onehot.astype(BF16), tri_ref[...], preferred_element_type=F32)
    rank = jnp.sum(onehot * (prefix - 1.0 + carry[:, 0:1]), axis=0, keepdims=True)
    total = carry[...] + jnp.sum(onehot, axis=1, keepdims=True)
    carry[...] = total
    cnt_ref[...] = total.astype(I32)
    cls_ref[...] = jnp.concatenate([cls, rank.astype(I32), jnp.zeros((6, TM), I32)], axis=0)
    gw_ref[...] = jnp.concatenate([w_a, w_b, jnp.zeros((6, TM), F32)], axis=0)


def _out_call(l, o_a, o_b_lat, o_b_ctx, o_c, o_d, w_out, x_lat, x_ctx, ctx_base, mod, g_norm2, wr_hi, wr_lo,
              b_route, w_down_e):
    row = lambda: pl.BlockSpec((TM, GW), lambda i: (i, 0))
    tri = jnp.asarray(np.triu(np.ones((TM, TM), np.float32)), BF16)
    return pl.pallas_call(
        _out_kernel,
        out_shape=(jax.ShapeDtypeStruct((NT, D), F32), jax.ShapeDtypeStruct((NT * SLAB, 128), BF16),
                   jax.ShapeDtypeStruct((8, NT), I32), jax.ShapeDtypeStruct((8, NT), F32),
                   jax.ShapeDtypeStruct((R_ROWS, 128), I32),
                   jax.ShapeDtypeStruct((N_EXPERTS * D_EXPERT, D), BF16)),
        grid=(N_TILES,),
        in_specs=[row(),
                  pl.BlockSpec((TM, GW), lambda i: (jnp.minimum(i, LAT_TILES - 1), 0)),
                  pl.BlockSpec((TM, GW), lambda i: (jnp.maximum(i - LAT_TILES, 0), 0)),
                  row(), row(),
                  pl.BlockSpec((None, D, D), lambda i: (l, 0, 0), pipeline_mode=pl.Buffered(1))]
        + _stream_specs(ctx_base) + [
                  pl.BlockSpec((None, None, 8, D), lambda i: (l, _mod_row(i), 0, 0)),
                  pl.BlockSpec((None, 1, D), lambda i: (l, 0, 0)),
                  pl.BlockSpec((None, D, 128), lambda i: (l, 0, 0)),
                  pl.BlockSpec((None, D, 128), lambda i: (l, 0, 0)),
                  pl.BlockSpec((None, 1, 128), lambda i: (l, 0, 0)),
                  pl.BlockSpec((TM, TM), lambda i: (0, 0)),
                  _cast_spec(W2_ROWS, D, l)],
        out_specs=(pl.BlockSpec((TM, D), lambda i: (i, 0)),
                   pl.BlockSpec((TM * SLAB, 128), lambda i: (i, 0)),
                   pl.BlockSpec((8, TM), lambda i: (0, i)), pl.BlockSpec((8, TM), lambda i: (0, i)),
                   pl.BlockSpec((R_ROWS, 128), lambda i: (0, 0)),
                   _cast_out_spec(W2_ROWS, D)),
        scratch_shapes=[pltpu.VMEM((R_ROWS, 128), F32), pltpu.VMEM((TM * SLAB, 128), F32)],
        compiler_params=_cparams(1),
        name="out_proj_route",
    )(o_a, o_b_lat, o_b_ctx, o_c, o_d, w_out, x_lat, x_ctx, mod, g_norm2.reshape(DEPTH, 1, D), wr_hi, wr_lo,
      b_route, tri, w_down_e.reshape(DEPTH * N_EXPERTS * D_EXPERT, D))


ROW_UNROLL = 8
FILL_BITS = (64, 32, 16, 8, 4, 2, 1)


def _dispatch_kernel(pos_ref, gwa_ref, gwb_ref, tail_start_ref, tail_n_ref, nvalid_ref, hp_ref, hs_hbm, gsa_ref,
                     gsb_ref, zbuf, sem, fill_sem):
    i = pl.program_id(0)

    def row_copy(r, p):
        return pltpu.make_async_copy(hp_ref.at[pl.ds(pl.multiple_of(r * SLAB, SLAB), SLAB), :],
                                     hs_hbm.at[pl.ds(pl.multiple_of(p * SLAB, SLAB), SLAB), :], sem)

    def fill_copies(action):
        def per_class(c, carry):
            n = tail_n_ref[c]
            s = tail_start_ref[c]
            for bit in FILL_BITS:
                @pl.when((n & bit) != 0)
                def _(s=s, bit=bit):
                    action(pltpu.make_async_copy(
                        zbuf.at[pl.ds(0, bit * SLAB), :],
                        hs_hbm.at[pl.ds(pl.multiple_of(s * SLAB, SLAB), bit * SLAB), :], fill_sem))
                s = s + (n & bit)
            return carry

        lax.fori_loop(0, N_CLASSES, per_class, 0)

        def per_tile(j, carry):
            action(pltpu.make_async_copy(
                zbuf, hs_hbm.at[pl.ds(pl.multiple_of(j * (TME * SLAB), TME * SLAB), TME * SLAB), :], fill_sem))
            return carry

        lax.fori_loop(nvalid_ref[0], P_TILES, per_tile, 0)

    @pl.when(i == 0)
    def _():
        zbuf[...] = jnp.zeros_like(zbuf)

        def init(p, carry):
            gsa_ref[p] = 0.0
            gsb_ref[p] = 0.0
            return carry

        lax.fori_loop(0, P_ROWS, init, 0, unroll=ROW_UNROLL)
        fill_copies(lambda cp: cp.start())

    def start(r, carry):
        t = i * TM + r
        p = pos_ref[t]
        gsa_ref[p] = gwa_ref[t]
        gsb_ref[p] = gwb_ref[t]
        row_copy(r, p).start()
        return carry

    lax.fori_loop(0, TM, start, 0, unroll=ROW_UNROLL)

    def wait(r, carry):
        row_copy(0, 0).wait()
        return carry

    lax.fori_loop(0, TM, wait, 0, unroll=ROW_UNROLL)

    @pl.when(i == N_TILES - 1)
    def _():
        fill_copies(lambda cp: cp.wait())


def _dispatch_call(pos, gwa, gwb, tail_start, tail_n, nvalid, hp):
    return pl.pallas_call(
        _dispatch_kernel,
        out_shape=(jax.ShapeDtypeStruct((P_ROWS * SLAB, 128), BF16), jax.ShapeDtypeStruct((P_ROWS,), F32),
                   jax.ShapeDtypeStruct((P_ROWS,), F32)),
        grid_spec=pltpu.PrefetchScalarGridSpec(
            num_scalar_prefetch=6,
            grid=(N_TILES,),
            in_specs=[pl.BlockSpec((TM * SLAB, 128), lambda i, *_: (i, 0))],
            out_specs=(pl.BlockSpec(memory_space=pl.ANY), pl.BlockSpec(memory_space=pltpu.SMEM),
                       pl.BlockSpec(memory_space=pltpu.SMEM)),
            scratch_shapes=[pltpu.VMEM((TME * SLAB, 128), BF16), pltpu.SemaphoreType.DMA,
                            pltpu.SemaphoreType.DMA]),
        compiler_params=_cparams(1),
        name="moe_dispatch",
    )(pos, gwa, gwb, tail_start, tail_n, nvalid, hp)


def _gate_column(g_ref, base):
    sub = lax.broadcasted_iota(I32, (8, 128), 0)
    pieces = []
    for g in range(TME // 8):
        v = jnp.zeros((8, 128), F32)
        for k in range(8):
            v = jnp.where(sub == k, g_ref[base + 8 * g + k], v)
        pieces.append(v)
    return jnp.concatenate(pieces, axis=0)


def _expert_kernel(ea_ref, eb_ref, nvalid_ref, gsa_ref, gsb_ref, h_ref, w1a_ref, w3a_ref, w2a_ref, w1b_ref, w3b_ref,
                   w2b_ref, y_ref, stage):
    i = pl.program_id(0)

    @pl.when(i < nvalid_ref[0])
    def _():
        h = _slabs_to_rows(h_ref, stage, TME).astype(BF16)
        y = None
        for g_ref, w1_ref, w3_ref, w2_ref in ((gsa_ref, w1a_ref, w3a_ref, w2a_ref),
                                              (gsb_ref, w1b_ref, w3b_ref, w2b_ref)):
            gate = jnp.dot(h, w1_ref[...], preferred_element_type=F32)
            up = jnp.dot(h, w3_ref[...], preferred_element_type=F32)
            gcol = _gate_column(g_ref, i * TME)
            hid = gate * jax.nn.sigmoid(gate) * up * jnp.concatenate([gcol] * (D_EXPERT // 128), axis=1)
            yk = jnp.dot(hid.astype(BF16), w2_ref[...], preferred_element_type=F32)
            y = yk if y is None else y + yk
        _rows_to_slabs(y, stage, y_ref)

    @pl.when(i >= nvalid_ref[0])
    def _():
        y_ref[...] = jnp.zeros_like(y_ref)


def _expert_call(tile_ea, tile_eb, nvalid, gsa, gsb, hs, w1b, w3b, w2b):
    up_a = pl.BlockSpec((None, D, D_EXPERT), lambda i, ea, eb, *_: (ea[i], 0, 0))
    up_b = pl.BlockSpec((None, D, D_EXPERT), lambda i, ea, eb, *_: (eb[i], 0, 0))
    down_a = pl.BlockSpec((None, D_EXPERT, D), lambda i, ea, eb, *_: (ea[i], 0, 0))
    down_b = pl.BlockSpec((None, D_EXPERT, D), lambda i, ea, eb, *_: (eb[i], 0, 0))
    return pl.pallas_call(
        _expert_kernel,
        out_shape=jax.ShapeDtypeStruct((P_ROWS * SLAB, 128), BF16),
        grid_spec=pltpu.PrefetchScalarGridSpec(
            num_scalar_prefetch=5,
            grid=(P_TILES,),
            in_specs=[pl.BlockSpec((TME * SLAB, 128), lambda i, *_: (i, 0)),
                      up_a, up_a, down_a, up_b, up_b, down_b],
            out_specs=pl.BlockSpec((TME * SLAB, 128), lambda i, *_: (i, 0)),
            scratch_shapes=[pltpu.VMEM((TME * SLAB, 128), F32)]),
        compiler_params=_cparams(1),
        name="moe_experts",
    )(tile_ea, tile_eb, nvalid, gsa, gsb, hs, w1b, w3b, w2b, w1b, w3b, w2b)


def _combine_kernel(pos_ref, x1_ref, mod_ref, gf_ref, y_hbm, o_ref, ybuf, stage, sem, *, final):
    i = pl.program_id(0)

    def row_copy(p, r):
        return pltpu.make_async_copy(y_hbm.at[pl.ds(pl.multiple_of(p * SLAB, SLAB), SLAB), :],
                                     ybuf.at[pl.ds(pl.multiple_of(r * SLAB, SLAB), SLAB), :], sem)

    def start(r, carry):
        row_copy(pos_ref[i * TM + r], r).start()
        return carry

    lax.fori_loop(0, TM, start, 0, unroll=ROW_UNROLL)

    def wait(r, carry):
        row_copy(0, r).wait()
        return carry

    lax.fori_loop(0, TM, wait, 0, unroll=ROW_UNROLL)
    x2 = x1_ref[...] + mod_ref[5:6, :] * _slabs_to_rows(ybuf, stage, TM)
    if final:
        ms = jnp.mean(x2 * x2, axis=-1, keepdims=True)
        x2 = x2 * lax.rsqrt(ms + EPS) * gf_ref[...]
    o_ref[...] = x2


def _combine_call(l, pos, x1, mod, g_final, ys, final):
    n_tiles = LAT_TILES if final else N_TILES
    return pl.pallas_call(
        functools.partial(_combine_kernel, final=final),
        out_shape=jax.ShapeDtypeStruct((n_tiles * TM, D), F32),
        grid_spec=pltpu.PrefetchScalarGridSpec(
            num_scalar_prefetch=1,
            grid=(n_tiles,),
            in_specs=[pl.BlockSpec((TM, D), lambda i, pos: (i, 0)),
                      pl.BlockSpec((None, None, 8, D), lambda i, pos: (l, _mod_row(i), 0, 0)),
                      pl.BlockSpec((1, D), lambda i, pos: (0, 0)),
                      pl.BlockSpec(memory_space=pl.ANY)],
            out_specs=pl.BlockSpec((TM, D), lambda i, pos: (i, 0)),
            scratch_shapes=[pltpu.VMEM((TM * SLAB, 128), BF16), pltpu.VMEM((TM * SLAB, 128), F32),
                            pltpu.SemaphoreType.DMA]),
        compiler_params=_cparams(1),
        name="moe_combine",
    )(pos, x1, mod, g_final.reshape(1, D), ys)


def _slot_plan(cls, rank, counts):
    padded = ((counts + TME - 1) // TME) * TME
    ends = jnp.cumsum(padded)
    starts = ends - padded
    pos = starts[cls] + rank
    tile_start = jnp.arange(P_TILES, dtype=I32) * TME
    tile_cls = jnp.minimum(jnp.sum((tile_start[:, None] >= ends[None, :]).astype(I32), axis=1), N_CLASSES - 1)
    nvalid = (ends[-1] // TME).reshape(1)
    grp = tile_cls // 6
    tile_ea = grp * PER_GROUP + jnp.asarray(_SLOT_A)[tile_cls % 6]
    tile_eb = grp * PER_GROUP + jnp.asarray(_SLOT_B)[tile_cls % 6]
    return pos, starts + counts, padded - counts, tile_ea, tile_eb, nvalid


def _rope_tables():
    t = jnp.arange(SEQ, dtype=I32)
    row = (t // GRID_W).astype(F32)
    col = (t % GRID_W).astype(F32)
    half = HEAD_DIM // 4
    freqs = ROPE_BASE ** (-jnp.arange(0, 2 * half, 2, dtype=F32) / (2 * half))
    ang_r = row[:, None] * freqs
    ang_c = col[:, None] * freqs
    cos_h = jnp.concatenate([jnp.cos(ang_r), jnp.cos(ang_r), jnp.cos(ang_c), jnp.cos(ang_c)], axis=1)
    sin_h = jnp.concatenate([-jnp.sin(ang_r), jnp.sin(ang_r), -jnp.sin(ang_c), jnp.sin(ang_c)], axis=1)
    cos_t = jnp.concatenate([jnp.tile(cos_h, (1, 2)), jnp.ones((TM, 128), F32)], axis=0)
    sin_t = jnp.concatenate([jnp.tile(sin_h, (1, 2)), jnp.zeros((TM, 128), F32)], axis=0)
    return cos_t, sin_t


def kernel(x, c, ctx, c_ctx, w_mod, b_mod, g_norm1, g_norm2, w_in, w_out, attn_sink, sconv_w, cconv_w, cconv_b,
           cnorm_g, cnorm_b, w_route_group, b_route_group, w_route_expert, b_route_expert, w_gate_e, w_up_e,
           w_down_e, g_final):
    x_lat, x_ctx, ctx_base = x.reshape(N_LAT, D), ctx.reshape(N_CTX, D), 0
    c8 = jnp.zeros((8, D), F32).at[0:BATCH].set(c).at[BATCH].set(c_ctx)
    q_w, k_w, v_w = w_in[:, :, 0:512], w_in[:, :, 512:640], w_in[:, :, 640:768]
    swap = lambda w: jnp.concatenate([w[:, :, HEAD_DIM:], w[:, :, :HEAD_DIM]], axis=-1)
    wqkv = jnp.concatenate([q_w, k_w, v_w, swap(k_w), swap(v_w)], axis=-1).astype(BF16)
    wf = w_in[:, :, 768:1280].astype(BF16)
    wcv = w_in[:, :, 1280:].astype(BF16)
    w_out_b = w_out.astype(BF16)
    w_route = jnp.concatenate([w_route_expert, w_route_group,
                               jnp.zeros((DEPTH, D, 128 - N_EXPERTS - N_GROUPS), F32)], axis=2)
    wr_hi = w_route.astype(BF16)
    wr_lo = (w_route - wr_hi.astype(F32)).astype(BF16)
    b_route = jnp.concatenate([b_route_expert, b_route_group,
                               jnp.zeros((DEPTH, 128 - N_EXPERTS - N_GROUPS), F32)], axis=1).reshape(DEPTH, 1, 128)
    cos_t, sin_t = _rope_tables()

    mod = _mod_call(c8, w_mod, b_mod)
    mod = mod.reshape(DEPTH, 8, N_MOD, D)[:, :3]
    mod = jnp.pad(mod, ((0, 0), (0, 0), (0, 8 - N_MOD), (0, 0)))

    out = None
    for l in range(DEPTH):
        qkv, f, cv, w1b, w3b = _in_call(l, x_lat, x_ctx, ctx_base, mod, g_norm1, wqkv, wf, wcv, cos_t, sin_t,
                                        w_gate_e, w_up_e)
        o_a = _attn_call(l, attn_sink, qkv)
        o_b_lat = _fft_call(f, 64, 0, BATCH)
        o_b_ctx = _fft_call(f, 16, N_LAT // CTX, BATCH)
        o_c, o_d = _conv_call(l, cv, sconv_w, cconv_w, cconv_b, cnorm_g, cnorm_b)
        x1, hp, cls8, gw8, cnt, w2b = _out_call(l, o_a, o_b_lat, o_b_ctx, o_c, o_d, w_out_b, x_lat, x_ctx, ctx_base,
                                                mod, g_norm2, wr_hi, wr_lo, b_route, w_down_e)
        pos, tail_start, tail_n, tile_ea, tile_eb, nvalid = _slot_plan(cls8[0], cls8[1], cnt[:N_CLASSES, 0])
        hs, gsa, gsb = _dispatch_call(pos, gw8[0], gw8[1], tail_start, tail_n, nvalid, hp)
        ys = _expert_call(tile_ea, tile_eb, nvalid, gsa, gsb, hs, w1b.reshape(N_EXPERTS, D, D_EXPERT),
                          w3b.reshape(N_EXPERTS, D, D_EXPERT), w2b.reshape(N_EXPERTS, D_EXPERT, D))
        final = l == DEPTH - 1
        res = _combine_call(l, pos, x1, mod, g_final, ys, final)
        if final:
            out = res
        else:
            x_lat, x_ctx, ctx_base = res, res, LAT_TILES
    return out.reshape(BATCH, SEQ, D)
```

```python
import functools

import numpy as np
import jax
import jax.numpy as jnp
from jax import lax
from jax.experimental import pallas as pl
from jax.experimental.pallas import tpu as pltpu

F32 = jnp.float32
BF16 = jnp.bfloat16
I32 = jnp.int32

D = 2048
BATCH = 2
SEQ = 4096
CTX = 256
DEPTH = 2
GRID_W = 64
N_LAT = BATCH * SEQ
N_CTX = BATCH * CTX
NT = N_LAT + N_CTX
GW = 512
HEAD_DIM = 64
N_HEADS = 8
WINDOW = 128
ROPE_BASE = 10000.0
N_MOD = 6
EPS = 1e-6
NEG_INF = -1e30
N_GROUPS = 4
PER_GROUP = 4
N_EXPERTS = 16
D_EXPERT = 512
N_CLASSES = N_GROUPS * 6
CONF_W = 31
CONF_PAD = 15

TM = 256
N_TILES = NT // TM
LAT_TILES = N_LAT // TM
SEQ_TILES = SEQ // TM
TQ = 128
HALO = 16
TME = 128
P_ROWS = NT + N_CLASSES * TME
P_TILES = P_ROWS // TME
SLAB = D // 128
VMEM_LIMIT = 56 * 1024 * 1024


def _cparams(n_axes=1, **kw):
    return pltpu.CompilerParams(dimension_semantics=("arbitrary",) * n_axes,
                                vmem_limit_bytes=VMEM_LIMIT, **kw)


def _split_bf16(v):
    hi = v.astype(BF16)
    lo = (v - hi.astype(F32)).astype(BF16)
    return hi, lo


def _mod_row(i):
    return jnp.minimum(i // SEQ_TILES, 2)


MOD_TN = 1536


def _mod_kernel(c_ref, w_ref, b_ref, o_ref):
    c = c_ref[...]
    s = c * jax.nn.sigmoid(c)
    s_hi, s_lo = _split_bf16(s)
    w_hi, w_lo = _split_bf16(w_ref[...])
    acc = jnp.dot(s_hi, w_hi, preferred_element_type=F32)
    acc = acc + jnp.dot(s_lo, w_hi, preferred_element_type=F32)
    acc = acc + jnp.dot(s_hi, w_lo, preferred_element_type=F32)
    o_ref[...] = acc + b_ref[...]


def _mod_call(c8, w_mod, b_mod):
    n = N_MOD * D
    return pl.pallas_call(
        _mod_kernel,
        out_shape=jax.ShapeDtypeStruct((DEPTH, 8, n), F32),
        grid=(DEPTH, n // MOD_TN),
        in_specs=[
            pl.BlockSpec((8, D), lambda l, j: (0, 0)),
            pl.BlockSpec((None, D, MOD_TN), lambda l, j: (l, 0, j)),
            pl.BlockSpec((None, 1, MOD_TN), lambda l, j: (l, 0, j)),
        ],
        out_specs=pl.BlockSpec((None, 8, MOD_TN), lambda l, j: (l, 0, j)),
        compiler_params=_cparams(2),
        name="adaln_mod",
    )(c8, w_mod, b_mod.reshape(DEPTH, 1, n))


QKV_W = 1024
CV_IN = 5 * GW
CV_OUT = 3 * GW


def _rmsnorm_mod(x, g, shift, scale):
    ms = jnp.mean(x * x, axis=-1, keepdims=True)
    y = x * lax.rsqrt(ms + EPS) * g
    return y * (1.0 + scale) + shift


def _stream_specs(ctx_base):
    return [pl.BlockSpec((TM, D), lambda i, *_: (jnp.minimum(i, LAT_TILES - 1), 0)),
            pl.BlockSpec((TM, D), lambda i, *_: (ctx_base + jnp.maximum(i - LAT_TILES, 0), 0))]


def _stream_tile(xl_ref, xc_ref):
    return jnp.where(pl.program_id(0) < LAT_TILES, xl_ref[...], xc_ref[...])


CAST_STEPS = 32


def _cast_spec(rows, cols, l):
    return pl.BlockSpec((rows, cols), lambda i, *_: (l * CAST_STEPS + jnp.minimum(i, CAST_STEPS - 1), 0))


def _cast_out_spec(rows, cols):
    return pl.BlockSpec((rows, cols), lambda i, *_: (jnp.minimum(i, CAST_STEPS - 1), 0))


def _in_kernel(xl_ref, xc_ref, mod_ref, g_ref, wqkv_ref, wf_ref, wcv_ref, cos_ref, sin_ref, w1_ref, w3_ref,
               qkv_ref, f_ref, cv_ref, w1b_ref, w3b_ref):
    w1b_ref[...] = w1_ref[...].astype(BF16)
    w3b_ref[...] = w3_ref[...].astype(BF16)
    h = _rmsnorm_mod(_stream_tile(xl_ref, xc_ref), g_ref[...], mod_ref[0:1, :], mod_ref[1:2, :])
    hb = h.astype(BF16)
    qkv = jnp.dot(hb, wqkv_ref[...], preferred_element_type=F32)
    cos = cos_ref[...]
    sin = sin_ref[...]
    lane = lax.broadcasted_iota(I32, (TM, 128), 1)
    first_half = (lane % 32) < 16

    def rope(t):
        rot = jnp.where(first_half, pltpu.roll(t, 112, 1), pltpu.roll(t, 16, 1))
        return t * cos + rot * sin

    scale = HEAD_DIM ** -0.5
    for c in range(4):
        qkv_ref[:, c * 128:(c + 1) * 128] = (rope(qkv[:, c * 128:(c + 1) * 128]) * scale).astype(BF16)
    qkv_ref[:, 512:640] = rope(qkv[:, 512:640]).astype(BF16)
    qkv_ref[:, 640:768] = qkv[:, 640:768].astype(BF16)
    qkv_ref[:, 768:896] = rope(qkv[:, 768:896]).astype(BF16)
    qkv_ref[:, 896:1024] = qkv[:, 896:1024].astype(BF16)

    f_ref[...] = jnp.dot(hb, wf_ref[...], preferred_element_type=F32).astype(BF16)

    cv = jnp.dot(hb, wcv_ref[...], preferred_element_type=F32)
    cv_ref[:, 0:GW] = cv[:, 0:GW]
    cv_ref[:, GW:2 * GW] = cv[:, GW:2 * GW] * cv[:, 2 * GW:3 * GW]
    cv_ref[:, 2 * GW:3 * GW] = cv[:, 3 * GW:4 * GW] * jax.nn.sigmoid(cv[:, 4 * GW:5 * GW])


W13_ROWS = N_EXPERTS * D // CAST_STEPS
W2_ROWS = N_EXPERTS * D_EXPERT // CAST_STEPS


def _in_call(l, x_lat, x_ctx, ctx_base, mod, g_norm1, wqkv, wf, wcv, cos_t, sin_t, w_gate_e, w_up_e):
    const = dict(pipeline_mode=pl.Buffered(1))
    flat = lambda w: w.reshape(DEPTH * N_EXPERTS * D, D_EXPERT)
    return pl.pallas_call(
        _in_kernel,
        out_shape=(jax.ShapeDtypeStruct((NT, QKV_W), BF16),
                   jax.ShapeDtypeStruct((NT, GW), BF16),
                   jax.ShapeDtypeStruct((NT, CV_OUT), F32),
                   jax.ShapeDtypeStruct((N_EXPERTS * D, D_EXPERT), BF16),
                   jax.ShapeDtypeStruct((N_EXPERTS * D, D_EXPERT), BF16)),
        grid=(N_TILES,),
        in_specs=_stream_specs(ctx_base) + [
            pl.BlockSpec((None, None, 8, D), lambda i: (l, _mod_row(i), 0, 0)),
            pl.BlockSpec((None, 1, D), lambda i: (l, 0, 0)),
            pl.BlockSpec((None, D, QKV_W), lambda i: (l, 0, 0), **const),
            pl.BlockSpec((None, D, GW), lambda i: (l, 0, 0), **const),
            pl.BlockSpec((None, D, CV_IN), lambda i: (l, 0, 0), **const),
            pl.BlockSpec((TM, 128), lambda i: (jnp.where(i < LAT_TILES, i % SEQ_TILES, SEQ_TILES), 0)),
            pl.BlockSpec((TM, 128), lambda i: (jnp.where(i < LAT_TILES, i % SEQ_TILES, SEQ_TILES), 0)),
            _cast_spec(W13_ROWS, D_EXPERT, l), _cast_spec(W13_ROWS, D_EXPERT, l),
        ],
        out_specs=(pl.BlockSpec((TM, QKV_W), lambda i: (i, 0)),
                   pl.BlockSpec((TM, GW), lambda i: (i, 0)),
                   pl.BlockSpec((TM, CV_OUT), lambda i: (i, 0)),
                   _cast_out_spec(W13_ROWS, D_EXPERT), _cast_out_spec(W13_ROWS, D_EXPERT)),
        compiler_params=_cparams(1),
        name="norm_in_proj",
    )(x_lat, x_ctx, mod, g_norm1.reshape(DEPTH, 1, D), wqkv, wf, wcv, cos_t, sin_t, flat(w_gate_e), flat(w_up_e))


N_QBLK_LAT = N_LAT // TQ
N_QBLK = NT // TQ
QBLK_PER_SEQ = SEQ // TQ
KWIN = 3 * TQ


def _attn_heads(sink_ref, l, q, keys, vals, mask, o_ref):
    lane = lax.broadcasted_iota(I32, (TQ, 128), 1)
    row2 = lax.broadcasted_iota(I32, (2 * TQ, 1), 0)
    zero = jnp.zeros((TQ, 128), BF16)
    out = [jnp.zeros((TQ, 128), F32) for _ in range(4)]
    for hk in range(2):
        c0, c1 = 2 * hk, 2 * hk + 1
        for p in range(2):
            kx = keys[0] if hk == p else keys[1]
            vx = vals[0] if hk == p else vals[1]
            in_half = (lane // HEAD_DIM) == p
            qs = jnp.concatenate([jnp.where(in_half, q[:, c0 * 128:(c0 + 1) * 128], zero),
                                  jnp.where(in_half, q[:, c1 * 128:(c1 + 1) * 128], zero)], axis=0)
            s = lax.dot_general(qs, kx, (((1,), (1,)), ((), ())), preferred_element_type=F32)
            if mask is not None:
                s = jnp.where(mask, s, NEG_INF)
            sink = jnp.where(row2 < TQ, sink_ref[l, 2 * c0 + p], sink_ref[l, 2 * c1 + p])
            m = jnp.maximum(jnp.max(s, axis=-1, keepdims=True), sink)
            e = jnp.exp(s - m)
            den = jnp.sum(e, axis=-1, keepdims=True) + jnp.exp(sink - m)
            o = jnp.dot(e.astype(BF16), vx, preferred_element_type=F32) / den
            out[c0] = out[c0] + jnp.where(in_half, o[:TQ], 0.0)
            out[c1] = out[c1] + jnp.where(in_half, o[TQ:], 0.0)
    for c in range(4):
        o_ref[:, c * 128:(c + 1) * 128] = out[c].astype(BF16)


def _attn_kernel(sink_ref, q_ref, k_ref, v_ref, ks_ref, vs_ref, kc_ref, vc_ref, kcs_ref, vcs_ref, o_ref, *, l):
    i = pl.program_id(0)
    q = q_ref[...]

    @pl.when(i < N_QBLK_LAT)
    def _latent():
        n = i % QBLK_PER_SEQ
        ws = pl.multiple_of(jnp.clip((n - 1) * TQ, 0, SEQ - KWIN), TQ)
        keys = [jnp.concatenate([r[pl.ds(ws, KWIN), :], c[...]], axis=0)
                for r, c in ((k_ref, kc_ref), (ks_ref, kcs_ref))]
        vals = [jnp.concatenate([r[pl.ds(ws, KWIN), :], c[...]], axis=0)
                for r, c in ((v_ref, vc_ref), (vs_ref, vcs_ref))]
        nk = KWIN + CTX
        qpos = n * TQ + lax.broadcasted_iota(I32, (2 * TQ, nk), 0) % TQ
        col = lax.broadcasted_iota(I32, (2 * TQ, nk), 1)
        mask = (jnp.abs(ws + col - qpos) <= WINDOW) | (col >= KWIN)
        _attn_heads(sink_ref, l, q, keys, vals, mask, o_ref)

    @pl.when(i >= N_QBLK_LAT)
    def _context():
        _attn_heads(sink_ref, l, q, [kc_ref[...], kcs_ref[...]], [vc_ref[...], vcs_ref[...]], None, o_ref)


def _attn_call(l, attn_sink, qkv):
    def seq_blk(i):
        return jnp.minimum(i // QBLK_PER_SEQ, 1)

    def ctx_blk(i):
        b = jnp.where(i < N_QBLK_LAT, i // QBLK_PER_SEQ, (i - N_QBLK_LAT) // (CTX // TQ))
        return N_LAT // CTX + b

    seq_specs = [pl.BlockSpec((SEQ, 128), functools.partial(lambda i, col: (seq_blk(i), col), col=col))
                 for col in (4, 5, 6, 7)]
    ctx_specs = [pl.BlockSpec((CTX, 128), functools.partial(lambda i, col: (ctx_blk(i), col), col=col))
                 for col in (4, 5, 6, 7)]
    return pl.pallas_call(
        functools.partial(_attn_kernel, l=l),
        out_shape=jax.ShapeDtypeStruct((NT, GW), BF16),
        grid=(N_QBLK,),
        in_specs=[pl.BlockSpec(memory_space=pltpu.SMEM),
                  pl.BlockSpec((TQ, GW), lambda i: (i, 0))] + seq_specs + ctx_specs,
        out_specs=pl.BlockSpec((TQ, GW), lambda i: (i, 0)),
        compiler_params=_cparams(1),
        name="window_attention",
    )(attn_sink, qkv, *([qkv] * 8))


FC = 256
FFT_UNROLL = 4


def _dft_tables(n):
    t = n * n
    j = np.arange(n)
    ang = 2.0 * np.pi * np.outer(j, j) / n
    cn, sn = np.cos(ang), np.sin(ang)
    m1 = np.block([[cn, sn], [-sn, cn]])
    m2 = np.concatenate([cn, sn], axis=1)
    angc = 2.0 * np.pi * np.outer(np.arange(128), np.arange(128)) / 128.0
    wc = np.concatenate([np.cos(angc), -np.sin(angc)], axis=1)
    angt = 2.0 * np.pi * np.outer(j, j) / t
    tw_re = np.broadcast_to(np.cos(angt)[:, :, None], (n, n, 128))
    tw_im = np.broadcast_to(-np.sin(angt)[:, :, None], (n, n, 128))
    def hi_lo(m):
        return jnp.stack(_split_bf16(jnp.asarray(m, F32)))

    return (hi_lo(m1), hi_lo(m2), hi_lo(wc),
            jnp.asarray(np.ascontiguousarray(tw_re), F32), jnp.asarray(np.ascontiguousarray(tw_im), F32))


def _fft_kernel(f_ref, m1_ref, m2_ref, wc_ref, twr_ref, twi_ref, o_ref, are, aim, hre, him, *, n):
    t = n * n
    ng = FC // 128
    for g in range(ng):
        fg = f_ref[:, g * 128:(g + 1) * 128]
        a = (jnp.dot(fg, wc_ref[0], preferred_element_type=F32)
             + jnp.dot(fg, wc_ref[1], preferred_element_type=F32))
        are[g] = a[:, :128]
        aim[g] = a[:, 128:]

    def table_dot(m_ref, v):
        return (jnp.dot(m_ref[0], v, preferred_element_type=F32)
                + jnp.dot(m_ref[1], v, preferred_element_type=F32))

    def gather_rows(re, im, first):
        rows = [pl.ds(first + u, n, stride=n) for u in range(FFT_UNROLL)]
        top = jnp.concatenate([re[g, r, :] for r in rows for g in range(ng)], axis=1)
        bot = jnp.concatenate([im[g, r, :] for r in rows for g in range(ng)], axis=1)
        return jnp.concatenate([top, bot], axis=0).astype(BF16)

    def stage1(it, carry):
        t1 = it * FFT_UNROLL
        h = table_dot(m1_ref, gather_rows(are, aim, t1))
        for u in range(FFT_UNROLL):
            dst = pl.ds(pl.multiple_of((t1 + u) * n, n), n)
            twr = twr_ref[t1 + u]
            twi = twi_ref[t1 + u]
            for g in range(ng):
                lanes = slice((u * ng + g) * 128, (u * ng + g + 1) * 128)
                h_re, h_im = h[:n, lanes], h[n:, lanes]
                hre[g, dst, :] = h_re * twr - h_im * twi
                him[g, dst, :] = h_re * twi + h_im * twr
        return carry

    lax.fori_loop(0, n // FFT_UNROLL, stage1, 0)

    def stage2(it, carry):
        k2 = it * FFT_UNROLL
        z = table_dot(m2_ref, gather_rows(hre, him, k2))
        z = z * ((t * 128.0) ** -0.5)
        for u in range(FFT_UNROLL):
            for g in range(ng):
                are[g, pl.ds(k2 + u, n, stride=n), :] = z[:, (u * ng + g) * 128:(u * ng + g + 1) * 128]
        return carry

    lax.fori_loop(0, n // FFT_UNROLL, stage2, 0)
    for g in range(ng):
        o_ref[:, g * 128:(g + 1) * 128] = are[g].astype(BF16)


def _fft_call(f, n, first_blk, n_seq):
    t = n * n
    m1, m2, wc, twr, twi = _dft_tables(n)
    full = lambda shape: pl.BlockSpec(shape, lambda b, h: (0,) * len(shape))
    return pl.pallas_call(
        functools.partial(_fft_kernel, n=n),
        out_shape=jax.ShapeDtypeStruct((n_seq * t, GW), BF16),
        grid=(n_seq, GW // FC),
        in_specs=[pl.BlockSpec((t, FC), lambda b, h: (first_blk + b, h)),
                  full((2, 2 * n, 2 * n)), full((2, n, 2 * n)), full((2, 128, 256)),
                  full((n, n, 128)), full((n, n, 128))],
        out_specs=pl.BlockSpec((t, FC), lambda b, h: (b, h)),
        scratch_shapes=[pltpu.VMEM((FC // 128, t, 128), F32)] * 4,
        compiler_params=_cparams(2),
        name="fourier_mix_%d" % t,
    )(f, m1, m2, wc, twr, twi)


CONV_ROWS = 32


def _conv_kernel(sb_ref, p_ref, pp_ref, pn_ref, g_ref, gp_ref, gn_ref, sw_ref, cw_ref, cb_ref, ng_ref, nb_ref,
                 oc_ref, od_ref, pbuf, gbuf):
    i = pl.program_id(0)
    seq_start = jnp.logical_or(i % SEQ_TILES == 0, i >= LAT_TILES)
    seq_end = jnp.logical_or(i % SEQ_TILES == SEQ_TILES - 1, i >= LAT_TILES)
    for buf, main, prev, nxt in ((pbuf, p_ref, pp_ref, pn_ref), (gbuf, g_ref, gp_ref, gn_ref)):
        buf[0:HALO, :] = jnp.where(seq_start, 0.0, prev[...])
        buf[HALO:HALO + TM, :] = main[...]
        buf[HALO + TM:HALO + TM + HALO, :] = jnp.where(seq_end, 0.0, nxt[...])
    sw = sw_ref[...]
    cw = cw_ref[...]
    for r0 in range(0, TM, CONV_ROWS):
        base = HALO + r0
        acc = sw[0:1, :] * pbuf[base - 1:base - 1 + CONV_ROWS, :]
        acc = acc + sw[1:2, :] * pbuf[base:base + CONV_ROWS, :]
        acc = acc + sw[2:3, :] * pbuf[base + 1:base + 1 + CONV_ROWS, :]
        oc_ref[r0:r0 + CONV_ROWS, :] = (sb_ref[r0:r0 + CONV_ROWS, :] * acc).astype(BF16)

        z = cw[0:1, :] * gbuf[base - CONF_PAD:base - CONF_PAD + CONV_ROWS, :]
        for j in range(1, CONF_W):
            o = base - CONF_PAD + j
            z = z + cw[j:j + 1, :] * gbuf[o:o + CONV_ROWS, :]
        z = z + cb_ref[...]
        mu = jnp.mean(z, axis=-1, keepdims=True)
        zc = z - mu
        var = jnp.mean(zc * zc, axis=-1, keepdims=True)
        y = zc * lax.rsqrt(var + EPS) * ng_ref[...] + nb_ref[...]
        od_ref[r0:r0 + CONV_ROWS, :] = (y * jax.nn.sigmoid(y)).astype(BF16)


def _conv_call(l, cv, sconv_w, cconv_w, cconv_b, cnorm_g, cnorm_b):
    hb = TM // HALO
    last = NT // HALO - 1

    def halo_specs(col):
        return [pl.BlockSpec((TM, GW), lambda i: (i, col)),
                pl.BlockSpec((HALO, GW), lambda i: (jnp.maximum(i * hb - 1, 0), col)),
                pl.BlockSpec((HALO, GW), lambda i: (jnp.minimum((i + 1) * hb, last), col))]

    vec = lambda: pl.BlockSpec((None, 1, GW), lambda i: (l, 0, 0))
    cw = jnp.pad(cconv_w, ((0, 0), (0, 32 - CONF_W), (0, 0)))
    sw = jnp.pad(sconv_w, ((0, 0), (0, 8 - 3), (0, 0)))
    return pl.pallas_call(
        _conv_kernel,
        out_shape=(jax.ShapeDtypeStruct((NT, GW), BF16), jax.ShapeDtypeStruct((NT, GW), BF16)),
        grid=(N_TILES,),
        in_specs=[pl.BlockSpec((TM, GW), lambda i: (i, 0))] + halo_specs(1) + halo_specs(2) + [
            pl.BlockSpec((None, 8, GW), lambda i: (l, 0, 0)),
            pl.BlockSpec((None, 32, GW), lambda i: (l, 0, 0)),
            vec(), vec(), vec()],
        out_specs=(pl.BlockSpec((TM, GW), lambda i: (i, 0)), pl.BlockSpec((TM, GW), lambda i: (i, 0))),
        scratch_shapes=[pltpu.VMEM((TM + 2 * HALO, GW), F32)] * 2,
        compiler_params=_cparams(1),
        name="conv_mixers",
    )(cv, cv, cv, cv, cv, cv, cv, sw, cw, cconv_b.reshape(DEPTH, 1, GW), cnorm_g.reshape(DEPTH, 1, GW),
      cnorm_b.reshape(DEPTH, 1, GW))


R_ROWS = 32
_SLOT_A = np.array([0, 2, 2, 0, 0, 1], np.int32)
_SLOT_B = np.array([1, 1, 3, 3, 2, 3], np.int32)


def _route(lg):
    gl = [lg[N_EXPERTS + g:N_EXPERTS + g + 1, :] for g in range(N_GROUPS)]
    gmax = functools.reduce(jnp.maximum, gl)
    pg = 1.0 / functools.reduce(jnp.add, [jnp.exp(v - gmax) for v in gl])
    gsel = jnp.full_like(gmax, N_GROUPS - 1).astype(I32)
    for g in range(N_GROUPS - 2, -1, -1):
        gsel = jnp.where(gl[g] == gmax, g, gsel)
    el = []
    for j in range(PER_GROUP):
        v = lg[j:j + 1, :]
        for g in range(1, N_GROUPS):
            v = jnp.where(gsel == g, lg[g * PER_GROUP + j:g * PER_GROUP + j + 1, :], v)
        el.append(v)
    v1 = functools.reduce(jnp.maximum, el)
    i1 = jnp.full_like(gsel, PER_GROUP - 1)
    for j in range(PER_GROUP - 2, -1, -1):
        i1 = jnp.where(el[j] == v1, j, i1)
    rest = [jnp.where(i1 == j, -jnp.inf, el[j]) for j in range(PER_GROUP)]
    v2 = functools.reduce(jnp.maximum, rest)
    i2 = jnp.full_like(gsel, PER_GROUP - 1)
    for j in range(PER_GROUP - 2, -1, -1):
        i2 = jnp.where(jnp.logical_and(rest[j] == v2, i1 != j), j, i2)
    e2 = jnp.exp(v2 - v1)
    w1 = pg / (1.0 + e2)
    w2 = pg * e2 / (1.0 + e2)
    lo = jnp.minimum(i1, i2)
    hi = jnp.maximum(i1, i2)
    w_lo = jnp.where(i1 < i2, w1, w2)
    w_hi = jnp.where(i1 < i2, w2, w1)
    pair = jnp.where(lo == 0, jnp.where(hi == 1, 0, jnp.where(hi == 2, 4, 3)),
                     jnp.where(lo == 1, jnp.where(hi == 2, 1, 5), 2))
    cls = gsel * 6 + pair
    a_is_hi = pair == 1
    return cls, jnp.where(a_is_hi, w_hi, w_lo), jnp.where(a_is_hi, w_lo, w_hi)


def _rows_to_slabs(v, stage_ref, out_ref):
    n = v.shape[0]
    for c in range(SLAB):
        stage_ref[pl.ds(c, n, stride=SLAB), :] = v[:, c * 128:(c + 1) * 128]
    out_ref[...] = stage_ref[...].astype(out_ref.dtype)


def _slabs_to_rows(src_ref, stage_ref, n):
    stage_ref[...] = src_ref[...].astype(F32)
    return jnp.concatenate([stage_ref[pl.ds(c, n, stride=SLAB), :] for c in range(SLAB)], axis=1)


def _out_kernel(oa_ref, obl_ref, obc_ref, oc_ref, od_ref, w_ref, xl_ref, xc_ref, mod_ref, g_ref, wrh_ref, wrl_ref,
                br_ref, tri_ref, w2_ref, x1_ref, hp_ref, cls_ref, gw_ref, cnt_ref, w2b_ref, carry, stage):
    i = pl.program_id(0)
    w2b_ref[...] = w2_ref[...].astype(BF16)

    @pl.when(i == 0)
    def _():
        carry[...] = jnp.zeros_like(carry)

    ob = jnp.where(i < LAT_TILES, obl_ref[...], obc_ref[...])
    y = jnp.dot(oa_ref[...], w_ref[0:GW, :], preferred_element_type=F32)
    y = y + jnp.dot(ob, w_ref[GW:2 * GW, :], preferred_element_type=F32)
    y = y + jnp.dot(oc_ref[...], w_ref[2 * GW:3 * GW, :], preferred_element_type=F32)
    y = y + jnp.dot(od_ref[...], w_ref[3 * GW:4 * GW, :], preferred_element_type=F32)
    x1 = _stream_tile(xl_ref, xc_ref) + mod_ref[2:3, :] * y
    x1_ref[...] = x1
    h2 = _rmsnorm_mod(x1, g_ref[...], mod_ref[3:4, :], mod_ref[4:5, :])
    _rows_to_slabs(h2, stage, hp_ref)

    h_hi, h_lo = _split_bf16(h2)
    lg = jnp.dot(h_hi, wrh_ref[...], preferred_element_type=F32)
    lg = lg + jnp.dot(h_lo, wrh_ref[...], preferred_element_type=F32)
    lg = lg + jnp.dot(h_hi, wrl_ref[...], preferred_element_type=F32)
    lg = jnp.transpose(lg + br_ref[...])[0:R_ROWS, :]
    cls, w_a, w_b = _route(lg)

    onehot = jnp.where(lax.broadcasted_iota(I32, (R_ROWS, TM), 0) == cls, 1.0, 0.0)
    prefix = jnp.dot(onehot.astype(BF16), tri_ref[...], preferred_element_type=F32)
    rank = jnp.sum(onehot * (prefix - 1.0 + carry[:, 0:1]), axis=0, keepdims=True)
    total = carry[...] + jnp.sum(onehot, axis=1, keepdims=True)
    carry[...] = total
    cnt_ref[...] = total.astype(I32)
    cls_ref[...] = jnp.concatenate([cls, rank.astype(I32), jnp.zeros((6, TM), I32)], axis=0)
    gw_ref[...] = jnp.concatenate([w_a, w_b, jnp.zeros((6, TM), F32)], axis=0)


def _out_call(l, o_a, o_b_lat, o_b_ctx, o_c, o_d, w_out, x_lat, x_ctx, ctx_base, mod, g_norm2, wr_hi, wr_lo,
              b_route, w_down_e):
    row = lambda: pl.BlockSpec((TM, GW), lambda i: (i, 0))
    tri = jnp.asarray(np.triu(np.ones((TM, TM), np.float32)), BF16)
    return pl.pallas_call(
        _out_kernel,
        out_shape=(jax.ShapeDtypeStruct((NT, D), F32), jax.ShapeDtypeStruct((NT * SLAB, 128), BF16),
                   jax.ShapeDtypeStruct((8, NT), I32), jax.ShapeDtypeStruct((8, NT), F32),
                   jax.ShapeDtypeStruct((R_ROWS, 128), I32),
                   jax.ShapeDtypeStruct((N_EXPERTS * D_EXPERT, D), BF16)),
        grid=(N_TILES,),
        in_specs=[row(),
                  pl.BlockSpec((TM, GW), lambda i: (jnp.minimum(i, LAT_TILES - 1), 0)),
                  pl.BlockSpec((TM, GW), lambda i: (jnp.maximum(i - LAT_TILES, 0), 0)),
                  row(), row(),
                  pl.BlockSpec((None, D, D), lambda i: (l, 0, 0), pipeline_mode=pl.Buffered(1))]
        + _stream_specs(ctx_base) + [
                  pl.BlockSpec((None, None, 8, D), lambda i: (l, _mod_row(i), 0, 0)),
                  pl.BlockSpec((None, 1, D), lambda i: (l, 0, 0)),
                  pl.BlockSpec((None, D, 128), lambda i: (l, 0, 0)),
                  pl.BlockSpec((None, D, 128), lambda i: (l, 0, 0)),
                  pl.BlockSpec((None, 1, 128), lambda i: (l, 0, 0)),
                  pl.BlockSpec((TM, TM), lambda i: (0, 0)),
                  _cast_spec(W2_ROWS, D, l)],
        out_specs=(pl.BlockSpec((TM, D), lambda i: (i, 0)),
                   pl.BlockSpec((TM * SLAB, 128), lambda i: (i, 0)),
                   pl.BlockSpec((8, TM), lambda i: (0, i)), pl.BlockSpec((8, TM), lambda i: (0, i)),
                   pl.BlockSpec((R_ROWS, 128), lambda i: (0, 0)),
                   _cast_out_spec(W2_ROWS, D)),
        scratch_shapes=[pltpu.VMEM((R_ROWS, 128), F32), pltpu.VMEM((TM * SLAB, 128), F32)],
        compiler_params=_cparams(1),
        name="out_proj_route",
    )(o_a, o_b_lat, o_b_ctx, o_c, o_d, w_out, x_lat, x_ctx, mod, g_norm2.reshape(DEPTH, 1, D), wr_hi, wr_lo,
      b_route, tri, w_down_e.reshape(DEPTH * N_EXPERTS * D_EXPERT, D))


ROW_UNROLL = 8


def _gate_column(g_ref, src_ref, base):
    sub = lax.broadcasted_iota(I32, (8, 128), 0)
    pieces = []
    for g in range(TME // 8):
        v = jnp.zeros((8, 128), F32)
        for k in range(8):
            v = jnp.where(sub == k, g_ref[src_ref[base + 8 * g + k]], v)
        pieces.append(v)
    return jnp.concatenate(pieces, axis=0)


def _expert_kernel(ea_ref, eb_ref, nvalid_ref, pos_ref, gwa_ref, gwb_ref, hp_hbm, w1a_ref, w3a_ref, w2a_ref, w1b_ref,
                   w3b_ref, w2b_ref, yt_hbm, src_ref, dst_ref, hbuf, ybuf, stage, gsem, ssem, zsem):
    i = pl.program_id(0)
    nvalid = nvalid_ref[0]
    slot = i % 2

    def slab(ref, row):
        return ref.at[pl.ds(pl.multiple_of(row * SLAB, SLAB), SLAB), :]

    def gather_copy(tile, sl, j):
        return pltpu.make_async_copy(slab(hp_hbm, src_ref[tile * TME + j]), slab(hbuf.at[sl], j), gsem.at[sl])

    def scatter_copy(tile, sl, j):
        return pltpu.make_async_copy(slab(ybuf.at[sl], j), slab(yt_hbm, dst_ref[tile * TME + j]), ssem.at[sl])

    def for_rows(fn):
        def body(j, carry):
            fn(j)
            return carry

        lax.fori_loop(0, TME, body, 0, unroll=ROW_UNROLL)

    def on_slot(value, fn):
        for sl in range(2):
            @pl.when(value == sl)
            def _(sl=sl):
                fn(sl)

    @pl.when(i == 0)
    def _():
        def init(p, carry):
            src_ref[p] = NT - 1
            dst_ref[p] = NT + (p & (2 * TME - 1))
            return carry

        lax.fori_loop(0, P_ROWS, init, 0, unroll=ROW_UNROLL)

        def place(t, carry):
            p = pos_ref[t]
            src_ref[p] = t
            dst_ref[p] = t
            return carry

        lax.fori_loop(0, NT, place, 0, unroll=ROW_UNROLL)
        ybuf[...] = jnp.zeros_like(ybuf)
        for sl in range(2):
            spare = pltpu.make_async_copy(ybuf.at[sl], yt_hbm.at[pl.ds((NT + sl * TME) * SLAB, TME * SLAB), :], zsem)
            spare.start()
            spare.wait()
        for_rows(lambda j: gather_copy(0, 0, j).start())

    @pl.when(i < nvalid)
    def _():
        on_slot(slot, lambda sl: for_rows(lambda j: gather_copy(0, sl, 0).wait()))

        @pl.when(i + 1 < nvalid)
        def _():
            on_slot(1 - slot, lambda sl: for_rows(lambda j: gather_copy(i + 1, sl, j).start()))

        h = _slabs_to_rows(hbuf.at[slot], stage, TME).astype(BF16)
        y = None
        for g_ref, w1_ref, w3_ref, w2_ref in ((gwa_ref, w1a_ref, w3a_ref, w2a_ref),
                                              (gwb_ref, w1b_ref, w3b_ref, w2b_ref)):
            gate = jnp.dot(h, w1_ref[...], preferred_element_type=F32)
            up = jnp.dot(h, w3_ref[...], preferred_element_type=F32)
            gcol = _gate_column(g_ref, src_ref, i * TME)
            hid = gate * jax.nn.sigmoid(gate) * up * jnp.concatenate([gcol] * (D_EXPERT // 128), axis=1)
            yk = jnp.dot(hid.astype(BF16), w2_ref[...], preferred_element_type=F32)
            y = yk if y is None else y + yk

        @pl.when(i >= 2)
        def _():
            on_slot(slot, lambda sl: for_rows(lambda j: scatter_copy(0, sl, 0).wait()))

        _rows_to_slabs(y, stage, ybuf.at[slot])
        on_slot(slot, lambda sl: for_rows(lambda j: scatter_copy(i, sl, j).start()))

        @pl.when(i == nvalid - 1)
        def _():
            @pl.when(i >= 1)
            def _():
                on_slot(1 - slot, lambda sl: for_rows(lambda j: scatter_copy(0, sl, 0).wait()))
            on_slot(slot, lambda sl: for_rows(lambda j: scatter_copy(0, sl, 0).wait()))


Y_ROWS = NT + 2 * TME


def _expert_call(tile_ea, tile_eb, nvalid, pos, gwa, gwb, hp, w1b, w3b, w2b):
    up_a = pl.BlockSpec((None, D, D_EXPERT), lambda i, ea, eb, *_: (ea[i], 0, 0))
    up_b = pl.BlockSpec((None, D, D_EXPERT), lambda i, ea, eb, *_: (eb[i], 0, 0))
    down_a = pl.BlockSpec((None, D_EXPERT, D), lambda i, ea, eb, *_: (ea[i], 0, 0))
    down_b = pl.BlockSpec((None, D_EXPERT, D), lambda i, ea, eb, *_: (eb[i], 0, 0))
    return pl.pallas_call(
        _expert_kernel,
        out_shape=jax.ShapeDtypeStruct((Y_ROWS * SLAB, 128), BF16),
        grid_spec=pltpu.PrefetchScalarGridSpec(
            num_scalar_prefetch=6,
            grid=(P_TILES,),
            in_specs=[pl.BlockSpec(memory_space=pl.ANY),
                      up_a, up_a, down_a, up_b, up_b, down_b],
            out_specs=pl.BlockSpec(memory_space=pl.ANY),
            scratch_shapes=[pltpu.SMEM((P_ROWS,), I32), pltpu.SMEM((P_ROWS,), I32),
                            pltpu.VMEM((2, TME * SLAB, 128), BF16), pltpu.VMEM((2, TME * SLAB, 128), BF16),
                            pltpu.VMEM((TME * SLAB, 128), F32),
                            pltpu.SemaphoreType.DMA((2,)), pltpu.SemaphoreType.DMA((2,)),
                            pltpu.SemaphoreType.DMA]),
        compiler_params=_cparams(1),
        name="moe_experts",
    )(tile_ea, tile_eb, nvalid, pos, gwa, gwb, hp, w1b, w3b, w2b, w1b, w3b, w2b)


def _combine_kernel(x1_ref, mod_ref, gf_ref, y_ref, o_ref, stage, *, final):
    x2 = x1_ref[...] + mod_ref[5:6, :] * _slabs_to_rows(y_ref, stage, TM)
    if final:
        ms = jnp.mean(x2 * x2, axis=-1, keepdims=True)
        x2 = x2 * lax.rsqrt(ms + EPS) * gf_ref[...]
    o_ref[...] = x2


def _combine_call(l, x1, mod, g_final, yt, final):
    n_tiles = LAT_TILES if final else N_TILES
    return pl.pallas_call(
        functools.partial(_combine_kernel, final=final),
        out_shape=jax.ShapeDtypeStruct((n_tiles * TM, D), F32),
        grid=(n_tiles,),
        in_specs=[pl.BlockSpec((TM, D), lambda i: (i, 0)),
                  pl.BlockSpec((None, None, 8, D), lambda i: (l, _mod_row(i), 0, 0)),
                  pl.BlockSpec((1, D), lambda i: (0, 0)),
                  pl.BlockSpec((TM * SLAB, 128), lambda i: (i, 0))],
        out_specs=pl.BlockSpec((TM, D), lambda i: (i, 0)),
        scratch_shapes=[pltpu.VMEM((TM * SLAB, 128), F32)],
        compiler_params=_cparams(1),
        name="moe_combine",
    )(x1, mod, g_final.reshape(1, D), yt)


def _slot_plan(cls, rank, counts):
    padded = ((counts + TME - 1) // TME) * TME
    ends = jnp.cumsum(padded)
    starts = ends - padded
    pos = starts[cls] + rank
    tile_start = jnp.arange(P_TILES, dtype=I32) * TME
    tile_cls = jnp.minimum(jnp.sum((tile_start[:, None] >= ends[None, :]).astype(I32), axis=1), N_CLASSES - 1)
    nvalid = (ends[-1] // TME).reshape(1)
    grp = tile_cls // 6
    tile_ea = grp * PER_GROUP + jnp.asarray(_SLOT_A)[tile_cls % 6]
    tile_eb = grp * PER_GROUP + jnp.asarray(_SLOT_B)[tile_cls % 6]
    return pos, tile_ea, tile_eb, nvalid


def _rope_tables():
    t = jnp.arange(SEQ, dtype=I32)
    row = (t // GRID_W).astype(F32)
    col = (t % GRID_W).astype(F32)
    half = HEAD_DIM // 4
    freqs = ROPE_BASE ** (-jnp.arange(0, 2 * half, 2, dtype=F32) / (2 * half))
    ang_r = row[:, None] * freqs
    ang_c = col[:, None] * freqs
    cos_h = jnp.concatenate([jnp.cos(ang_r), jnp.cos(ang_r), jnp.cos(ang_c), jnp.cos(ang_c)], axis=1)
    sin_h = jnp.concatenate([-jnp.sin(ang_r), jnp.sin(ang_r), -jnp.sin(ang_c), jnp.sin(ang_c)], axis=1)
    cos_t = jnp.concatenate([jnp.tile(cos_h, (1, 2)), jnp.ones((TM, 128), F32)], axis=0)
    sin_t = jnp.concatenate([jnp.tile(sin_h, (1, 2)), jnp.zeros((TM, 128), F32)], axis=0)
    return cos_t, sin_t


def kernel(x, c, ctx, c_ctx, w_mod, b_mod, g_norm1, g_norm2, w_in, w_out, attn_sink, sconv_w, cconv_w, cconv_b,
           cnorm_g, cnorm_b, w_route_group, b_route_group, w_route_expert, b_route_expert, w_gate_e, w_up_e,
           w_down_e, g_final):
    x_lat, x_ctx, ctx_base = x.reshape(N_LAT, D), ctx.reshape(N_CTX, D), 0
    c8 = jnp.zeros((8, D), F32).at[0:BATCH].set(c).at[BATCH].set(c_ctx)
    q_w, k_w, v_w = w_in[:, :, 0:512], w_in[:, :, 512:640], w_in[:, :, 640:768]
    swap = lambda w: jnp.concatenate([w[:, :, HEAD_DIM:], w[:, :, :HEAD_DIM]], axis=-1)
    wqkv = jnp.concatenate([q_w, k_w, v_w, swap(k_w), swap(v_w)], axis=-1).astype(BF16)
    wf = w_in[:, :, 768:1280].astype(BF16)
    wcv = w_in[:, :, 1280:].astype(BF16)
    w_out_b = w_out.astype(BF16)
    w_route = jnp.concatenate([w_route_expert, w_route_group,
                               jnp.zeros((DEPTH, D, 128 - N_EXPERTS - N_GROUPS), F32)], axis=2)
    wr_hi = w_route.astype(BF16)
    wr_lo = (w_route - wr_hi.astype(F32)).astype(BF16)
    b_route = jnp.concatenate([b_route_expert, b_route_group,
                               jnp.zeros((DEPTH, 128 - N_EXPERTS - N_GROUPS), F32)], axis=1).reshape(DEPTH, 1, 128)
    cos_t, sin_t = _rope_tables()

    mod = _mod_call(c8, w_mod, b_mod)
    mod = mod.reshape(DEPTH, 8, N_MOD, D)[:, :3]
    mod = jnp.pad(mod, ((0, 0), (0, 0), (0, 8 - N_MOD), (0, 0)))

    out = None
    for l in range(DEPTH):
        qkv, f, cv, w1b, w3b = _in_call(l, x_lat, x_ctx, ctx_base, mod, g_norm1, wqkv, wf, wcv, cos_t, sin_t,
                                        w_gate_e, w_up_e)
        o_a = _attn_call(l, attn_sink, qkv)
        o_b_lat = _fft_call(f, 64, 0, BATCH)
        o_b_ctx = _fft_call(f, 16, N_LAT // CTX, BATCH)
        o_c, o_d = _conv_call(l, cv, sconv_w, cconv_w, cconv_b, cnorm_g, cnorm_b)
        x1, hp, cls8, gw8, cnt, w2b = _out_call(l, o_a, o_b_lat, o_b_ctx, o_c, o_d, w_out_b, x_lat, x_ctx, ctx_base,
                                                mod, g_norm2, wr_hi, wr_lo, b_route, w_down_e)
        pos, tile_ea, tile_eb, nvalid = _slot_plan(cls8[0], cls8[1], cnt[:N_CLASSES, 0])
        yt = _expert_call(tile_ea, tile_eb, nvalid, pos, gw8[0], gw8[1], hp, w1b.reshape(N_EXPERTS, D, D_EXPERT),
                          w3b.reshape(N_EXPERTS, D, D_EXPERT), w2b.reshape(N_EXPERTS, D_EXPERT, D))
        final = l == DEPTH - 1
        res = _combine_call(l, x1, mod, g_final, yt, final)
        if final:
            out = res
        else:
            x_lat, x_ctx, ctx_base = res, res, LAT_TILES
    return out.reshape(BATCH, SEQ, D)
```

```python
import functools

import numpy as np
import jax
import jax.numpy as jnp
from jax import lax
from jax.experimental import pallas as pl
from jax.experimental.pallas import tpu as pltpu

F32 = jnp.float32
BF16 = jnp.bfloat16
I32 = jnp.int32

D = 2048
BATCH = 2
SEQ = 4096
CTX = 256
DEPTH = 2
GRID_W = 64
N_LAT = BATCH * SEQ
N_CTX = BATCH * CTX
NT = N_LAT + N_CTX
GW = 512
HEAD_DIM = 64
N_HEADS = 8
WINDOW = 128
ROPE_BASE = 10000.0
N_MOD = 6
EPS = 1e-6
NEG_INF = -1e30
N_GROUPS = 4
PER_GROUP = 4
N_EXPERTS = 16
D_EXPERT = 512
N_CLASSES = N_GROUPS * 6
CONF_W = 31
CONF_PAD = 15

TM = 256
N_TILES = NT // TM
LAT_TILES = N_LAT // TM
SEQ_TILES = SEQ // TM
TQ = 128
HALO = 16
TME = 128
P_ROWS = NT + N_CLASSES * TME
P_TILES = P_ROWS // TME
SLAB = D // 128
VMEM_LIMIT = 56 * 1024 * 1024


def _cparams(n_axes=1, **kw):
    return pltpu.CompilerParams(dimension_semantics=("arbitrary",) * n_axes,
                                vmem_limit_bytes=VMEM_LIMIT, **kw)


def _split_bf16(v):
    hi = v.astype(BF16)
    lo = (v - hi.astype(F32)).astype(BF16)
    return hi, lo


def _mod_row(i):
    return jnp.minimum(i // SEQ_TILES, 2)


MOD_TN = 1536


def _mod_kernel(c_ref, w_ref, b_ref, o_ref):
    c = c_ref[...]
    s = c * jax.nn.sigmoid(c)
    s_hi, s_lo = _split_bf16(s)
    w_hi, w_lo = _split_bf16(w_ref[...])
    acc = jnp.dot(s_hi, w_hi, preferred_element_type=F32)
    acc = acc + jnp.dot(s_lo, w_hi, preferred_element_type=F32)
    acc = acc + jnp.dot(s_hi, w_lo, preferred_element_type=F32)
    o_ref[...] = acc + b_ref[...]


def _mod_call(c8, w_mod, b_mod):
    n = N_MOD * D
    return pl.pallas_call(
        _mod_kernel,
        out_shape=jax.ShapeDtypeStruct((DEPTH, 8, n), F32),
        grid=(DEPTH, n // MOD_TN),
        in_specs=[
            pl.BlockSpec((8, D), lambda l, j: (0, 0)),
            pl.BlockSpec((None, D, MOD_TN), lambda l, j: (l, 0, j)),
            pl.BlockSpec((None, 1, MOD_TN), lambda l, j: (l, 0, j)),
        ],
        out_specs=pl.BlockSpec((None, 8, MOD_TN), lambda l, j: (l, 0, j)),
        compiler_params=_cparams(2),
        name="adaln_mod",
    )(c8, w_mod, b_mod.reshape(DEPTH, 1, n))


QKV_W = 1024
CV_IN = 5 * GW
CV_OUT = 3 * GW


def _rmsnorm_mod(x, g, shift, scale):
    ms = jnp.mean(x * x, axis=-1, keepdims=True)
    y = x * lax.rsqrt(ms + EPS) * g
    return y * (1.0 + scale) + shift


def _stream_specs(ctx_base):
    return [pl.BlockSpec((TM, D), lambda i, *_: (jnp.minimum(i, LAT_TILES - 1), 0)),
            pl.BlockSpec((TM, D), lambda i, *_: (ctx_base + jnp.maximum(i - LAT_TILES, 0), 0))]


def _stream_tile(xl_ref, xc_ref):
    return jnp.where(pl.program_id(0) < LAT_TILES, xl_ref[...], xc_ref[...])


CAST_STEPS = 32


def _cast_spec(rows, cols, l):
    return pl.BlockSpec((rows, cols), lambda i, *_: (l * CAST_STEPS + jnp.minimum(i, CAST_STEPS - 1), 0))


def _cast_out_spec(rows, cols):
    return pl.BlockSpec((rows, cols), lambda i, *_: (jnp.minimum(i, CAST_STEPS - 1), 0))


def _in_kernel(xl_ref, xc_ref, mod_ref, g_ref, wqkv_ref, wf_ref, wcv_ref, cos_ref, sin_ref, w1_ref, w3_ref,
               qkv_ref, f_ref, cv_ref, w1b_ref, w3b_ref):
    w1b_ref[...] = w1_ref[...].astype(BF16)
    w3b_ref[...] = w3_ref[...].astype(BF16)
    h = _rmsnorm_mod(_stream_tile(xl_ref, xc_ref), g_ref[...], mod_ref[0:1, :], mod_ref[1:2, :])
    hb = h.astype(BF16)
    qkv = jnp.dot(hb, wqkv_ref[...], preferred_element_type=F32)
    cos = cos_ref[...]
    sin = sin_ref[...]
    lane = lax.broadcasted_iota(I32, (TM, 128), 1)
    first_half = (lane % 32) < 16

    def rope(t):
        rot = jnp.where(first_half, pltpu.roll(t, 112, 1), pltpu.roll(t, 16, 1))
        return t * cos + rot * sin

    scale = HEAD_DIM ** -0.5
    for c in range(4):
        qkv_ref[:, c * 128:(c + 1) * 128] = (rope(qkv[:, c * 128:(c + 1) * 128]) * scale).astype(BF16)
    qkv_ref[:, 512:640] = rope(qkv[:, 512:640]).astype(BF16)
    qkv_ref[:, 640:768] = qkv[:, 640:768].astype(BF16)
    qkv_ref[:, 768:896] = rope(qkv[:, 768:896]).astype(BF16)
    qkv_ref[:, 896:1024] = qkv[:, 896:1024].astype(BF16)

    f_ref[...] = jnp.dot(hb, wf_ref[...], preferred_element_type=F32).astype(BF16)

    cv = jnp.dot(hb, wcv_ref[...], preferred_element_type=F32)
    cv_ref[:, 0:GW] = cv[:, 0:GW]
    cv_ref[:, GW:2 * GW] = cv[:, GW:2 * GW] * cv[:, 2 * GW:3 * GW]
    cv_ref[:, 2 * GW:3 * GW] = cv[:, 3 * GW:4 * GW] * jax.nn.sigmoid(cv[:, 4 * GW:5 * GW])


W13_ROWS = N_EXPERTS * D // CAST_STEPS
W2_ROWS = N_EXPERTS * D_EXPERT // CAST_STEPS


def _in_call(l, x_lat, x_ctx, ctx_base, mod, g_norm1, wqkv, wf, wcv, cos_t, sin_t, w_gate_e, w_up_e):
    const = dict(pipeline_mode=pl.Buffered(1))
    flat = lambda w: w.reshape(DEPTH * N_EXPERTS * D, D_EXPERT)
    return pl.pallas_call(
        _in_kernel,
        out_shape=(jax.ShapeDtypeStruct((NT, QKV_W), BF16),
                   jax.ShapeDtypeStruct((NT, GW), BF16),
                   jax.ShapeDtypeStruct((NT, CV_OUT), F32),
                   jax.ShapeDtypeStruct((N_EXPERTS * D, D_EXPERT), BF16),
                   jax.ShapeDtypeStruct((N_EXPERTS * D, D_EXPERT), BF16)),
        grid=(N_TILES,),
        in_specs=_stream_specs(ctx_base) + [
            pl.BlockSpec((None, None, 8, D), lambda i: (l, _mod_row(i), 0, 0)),
            pl.BlockSpec((None, 1, D), lambda i: (l, 0, 0)),
            pl.BlockSpec((None, D, QKV_W), lambda i: (l, 0, 0), **const),
            pl.BlockSpec((None, D, GW), lambda i: (l, 0, 0), **const),
            pl.BlockSpec((None, D, CV_IN), lambda i: (l, 0, 0), **const),
            pl.BlockSpec((TM, 128), lambda i: (jnp.where(i < LAT_TILES, i % SEQ_TILES, SEQ_TILES), 0)),
            pl.BlockSpec((TM, 128), lambda i: (jnp.where(i < LAT_TILES, i % SEQ_TILES, SEQ_TILES), 0)),
            _cast_spec(W13_ROWS, D_EXPERT, l), _cast_spec(W13_ROWS, D_EXPERT, l),
        ],
        out_specs=(pl.BlockSpec((TM, QKV_W), lambda i: (i, 0)),
                   pl.BlockSpec((TM, GW), lambda i: (i, 0)),
                   pl.BlockSpec((TM, CV_OUT), lambda i: (i, 0)),
                   _cast_out_spec(W13_ROWS, D_EXPERT), _cast_out_spec(W13_ROWS, D_EXPERT)),
        compiler_params=_cparams(1),
        name="norm_in_proj",
    )(x_lat, x_ctx, mod, g_norm1.reshape(DEPTH, 1, D), wqkv, wf, wcv, cos_t, sin_t, flat(w_gate_e), flat(w_up_e))


N_QBLK_LAT = N_LAT // TQ
N_QBLK = NT // TQ
QBLK_PER_SEQ = SEQ // TQ
KWIN = 3 * TQ


def _attn_heads(sink_ref, l, q_ref, keys, vals, bias, o_ref):
    lane = lax.broadcasted_iota(I32, (TQ, 128), 1)
    row2 = lax.broadcasted_iota(I32, (2 * TQ, 1), 0)
    zero = jnp.zeros((TQ, 128), BF16)
    if bias is not None:
        bias = jnp.concatenate([bias, bias], axis=0)
    for hk in range(2):
        c0, c1 = 2 * hk, 2 * hk + 1
        for p in range(2):
            kx = keys[0] if hk == p else keys[1]
            vx = vals[0] if hk == p else vals[1]
            in_half = (lane // HEAD_DIM) == p
            qs = jnp.concatenate([jnp.where(in_half, q_ref[:, c0 * 128:(c0 + 1) * 128], zero),
                                  jnp.where(in_half, q_ref[:, c1 * 128:(c1 + 1) * 128], zero)], axis=0)
            s = lax.dot_general(qs, kx, (((1,), (1,)), ((), ())), preferred_element_type=F32)
            if bias is not None:
                s = s + bias
            sink = jnp.where(row2 < TQ, sink_ref[l, 2 * c0 + p], sink_ref[l, 2 * c1 + p])
            m = jnp.maximum(jnp.max(s, axis=-1, keepdims=True), sink)
            e = jnp.exp(s - m)
            den = jnp.sum(e, axis=-1, keepdims=True) + jnp.exp(sink - m)
            o = (jnp.dot(e.astype(BF16), vx, preferred_element_type=F32) / den).astype(BF16)
            for k, c in enumerate((c0, c1)):
                lo = c * 128 + p * HEAD_DIM
                o_ref[:, lo:lo + HEAD_DIM] = o[k * TQ:(k + 1) * TQ, p * HEAD_DIM:(p + 1) * HEAD_DIM]


def _attn_kernel(sink_ref, q_ref, k_ref, v_ref, ks_ref, vs_ref, kc_ref, vc_ref, kcs_ref, vcs_ref, o_ref, *, l):
    i = pl.program_id(0)

    @pl.when(i < N_QBLK_LAT)
    def _latent():
        n = i % QBLK_PER_SEQ
        ws = pl.multiple_of(jnp.clip((n - 1) * TQ, 0, SEQ - KWIN), TQ)
        keys = [jnp.concatenate([r[pl.ds(ws, KWIN), :], c[...]], axis=0)
                for r, c in ((k_ref, kc_ref), (ks_ref, kcs_ref))]
        vals = [jnp.concatenate([r[pl.ds(ws, KWIN), :], c[...]], axis=0)
                for r, c in ((v_ref, vc_ref), (vs_ref, vcs_ref))]
        nk = KWIN + CTX
        qpos = n * TQ + lax.broadcasted_iota(I32, (TQ, nk), 0)
        col = lax.broadcasted_iota(I32, (TQ, nk), 1)
        valid = (jnp.abs(ws + col - qpos) <= WINDOW) | (col >= KWIN)
        _attn_heads(sink_ref, l, q_ref, keys, vals, jnp.where(valid, 0.0, NEG_INF), o_ref)

    @pl.when(i >= N_QBLK_LAT)
    def _context():
        _attn_heads(sink_ref, l, q_ref, [kc_ref[...], kcs_ref[...]], [vc_ref[...], vcs_ref[...]], None, o_ref)


def _attn_call(l, attn_sink, qkv):
    def seq_blk(i):
        return jnp.minimum(i // QBLK_PER_SEQ, 1)

    def ctx_blk(i):
        b = jnp.where(i < N_QBLK_LAT, i // QBLK_PER_SEQ, (i - N_QBLK_LAT) // (CTX // TQ))
        return N_LAT // CTX + b

    seq_specs = [pl.BlockSpec((SEQ, 128), functools.partial(lambda i, col: (seq_blk(i), col), col=col))
                 for col in (4, 5, 6, 7)]
    ctx_specs = [pl.BlockSpec((CTX, 128), functools.partial(lambda i, col: (ctx_blk(i), col), col=col))
                 for col in (4, 5, 6, 7)]
    return pl.pallas_call(
        functools.partial(_attn_kernel, l=l),
        out_shape=jax.ShapeDtypeStruct((NT, GW), BF16),
        grid=(N_QBLK,),
        in_specs=[pl.BlockSpec(memory_space=pltpu.SMEM),
                  pl.BlockSpec((TQ, GW), lambda i: (i, 0))] + seq_specs + ctx_specs,
        out_specs=pl.BlockSpec((TQ, GW), lambda i: (i, 0)),
        compiler_params=_cparams(1),
        name="window_attention",
    )(attn_sink, qkv, *([qkv] * 8))


FC = 256
FFT_UNROLL = 4


def _dft_tables(n):
    t = n * n
    j = np.arange(n)
    ang = 2.0 * np.pi * np.outer(j, j) / n
    cn, sn = np.cos(ang), np.sin(ang)
    m1 = np.block([[cn, sn], [-sn, cn]])
    m2 = np.concatenate([cn, sn], axis=1)
    angc = 2.0 * np.pi * np.outer(np.arange(128), np.arange(128)) / 128.0
    wc = np.concatenate([np.cos(angc), -np.sin(angc)], axis=1)
    angt = 2.0 * np.pi * np.outer(j, j) / t
    tw_re = np.broadcast_to(np.cos(angt)[:, :, None], (n, n, 128))
    tw_im = np.broadcast_to(-np.sin(angt)[:, :, None], (n, n, 128))
    def hi_lo(m):
        return jnp.stack(_split_bf16(jnp.asarray(m, F32)))

    return (hi_lo(m1), hi_lo(m2), hi_lo(wc),
            jnp.asarray(np.ascontiguousarray(tw_re), F32), jnp.asarray(np.ascontiguousarray(tw_im), F32))


def _fft_kernel(f_ref, m1_ref, m2_ref, wc_ref, twr_ref, twi_ref, o_ref, are, aim, hre, him, *, n):
    t = n * n
    ng = FC // 128
    for g in range(ng):
        fg = f_ref[:, g * 128:(g + 1) * 128]
        a = (jnp.dot(fg, wc_ref[0], preferred_element_type=F32)
             + jnp.dot(fg, wc_ref[1], preferred_element_type=F32))
        are[g] = a[:, :128]
        aim[g] = a[:, 128:]

    def table_dot(m_ref, v):
        return (jnp.dot(m_ref[0], v, preferred_element_type=F32)
                + jnp.dot(m_ref[1], v, preferred_element_type=F32))

    def gather_rows(re, im, first):
        rows = [pl.ds(first + u, n, stride=n) for u in range(FFT_UNROLL)]
        top = jnp.concatenate([re[g, r, :] for r in rows for g in range(ng)], axis=1)
        bot = jnp.concatenate([im[g, r, :] for r in rows for g in range(ng)], axis=1)
        return jnp.concatenate([top, bot], axis=0).astype(BF16)

    def stage1(it, carry):
        t1 = it * FFT_UNROLL
        h = table_dot(m1_ref, gather_rows(are, aim, t1))
        for u in range(FFT_UNROLL):
            dst = pl.ds(pl.multiple_of((t1 + u) * n, n), n)
            twr = twr_ref[t1 + u]
            twi = twi_ref[t1 + u]
            for g in range(ng):
                lanes = slice((u * ng + g) * 128, (u * ng + g + 1) * 128)
                h_re, h_im = h[:n, lanes], h[n:, lanes]
                hre[g, dst, :] = h_re * twr - h_im * twi
                him[g, dst, :] = h_re * twi + h_im * twr
        return carry

    lax.fori_loop(0, n // FFT_UNROLL, stage1, 0)

    def stage2(it, carry):
        k2 = it * FFT_UNROLL
        z = table_dot(m2_ref, gather_rows(hre, him, k2))
        z = z * ((t * 128.0) ** -0.5)
        for u in range(FFT_UNROLL):
            for g in range(ng):
                are[g, pl.ds(k2 + u, n, stride=n), :] = z[:, (u * ng + g) * 128:(u * ng + g + 1) * 128]
        return carry

    lax.fori_loop(0, n // FFT_UNROLL, stage2, 0)
    for g in range(ng):
        o_ref[:, g * 128:(g + 1) * 128] = are[g].astype(BF16)


def _fft_call(f, n, first_blk, n_seq):
    t = n * n
    m1, m2, wc, twr, twi = _dft_tables(n)
    full = lambda shape: pl.BlockSpec(shape, lambda b, h: (0,) * len(shape))
    return pl.pallas_call(
        functools.partial(_fft_kernel, n=n),
        out_shape=jax.ShapeDtypeStruct((n_seq * t, GW), BF16),
        grid=(n_seq, GW // FC),
        in_specs=[pl.BlockSpec((t, FC), lambda b, h: (first_blk + b, h)),
                  full((2, 2 * n, 2 * n)), full((2, n, 2 * n)), full((2, 128, 256)),
                  full((n, n, 128)), full((n, n, 128))],
        out_specs=pl.BlockSpec((t, FC), lambda b, h: (b, h)),
        scratch_shapes=[pltpu.VMEM((FC // 128, t, 128), F32)] * 4,
        compiler_params=_cparams(2),
        name="fourier_mix_%d" % t,
    )(f, m1, m2, wc, twr, twi)


CONV_ROWS = 32
GSH_ROWS = TM + 2 * HALO - 8


def _conv_kernel(sb_ref, p_ref, pp_ref, pn_ref, g_ref, gp_ref, gn_ref, sw_ref, cw_ref, cb_ref, ng_ref, nb_ref,
                 oc_ref, od_ref, pbuf, gbuf, gsh):
    i = pl.program_id(0)
    seq_start = jnp.logical_or(i % SEQ_TILES == 0, i >= LAT_TILES)
    seq_end = jnp.logical_or(i % SEQ_TILES == SEQ_TILES - 1, i >= LAT_TILES)
    for buf, main, prev, nxt in ((pbuf, p_ref, pp_ref, pn_ref), (gbuf, g_ref, gp_ref, gn_ref)):
        buf[0:HALO, :] = jnp.where(seq_start, 0.0, prev[...])
        buf[HALO:HALO + TM, :] = main[...]
        buf[HALO + TM:HALO + TM + HALO, :] = jnp.where(seq_end, 0.0, nxt[...])
    for s in range(8):
        gsh[s] = gbuf[s:s + GSH_ROWS, :]
    sw = sw_ref[...]
    cw = cw_ref[...]
    for r0 in range(0, TM, CONV_ROWS):
        base = HALO + r0
        acc = sw[0:1, :] * pbuf[base - 1:base - 1 + CONV_ROWS, :]
        acc = acc + sw[1:2, :] * pbuf[base:base + CONV_ROWS, :]
        acc = acc + sw[2:3, :] * pbuf[base + 1:base + 1 + CONV_ROWS, :]
        oc_ref[r0:r0 + CONV_ROWS, :] = (sb_ref[r0:r0 + CONV_ROWS, :] * acc).astype(BF16)

        z = None
        for j in range(CONF_W):
            o = base - CONF_PAD + j
            q = o - o % 8
            term = cw[j:j + 1, :] * gsh[o % 8, q:q + CONV_ROWS, :]
            z = term if z is None else z + term
        z = z + cb_ref[...]
        mu = jnp.mean(z, axis=-1, keepdims=True)
        zc = z - mu
        var = jnp.mean(zc * zc, axis=-1, keepdims=True)
        y = zc * lax.rsqrt(var + EPS) * ng_ref[...] + nb_ref[...]
        od_ref[r0:r0 + CONV_ROWS, :] = (y * jax.nn.sigmoid(y)).astype(BF16)


def _conv_call(l, cv, sconv_w, cconv_w, cconv_b, cnorm_g, cnorm_b):
    hb = TM // HALO
    last = NT // HALO - 1

    def halo_specs(col):
        return [pl.BlockSpec((TM, GW), lambda i: (i, col)),
                pl.BlockSpec((HALO, GW), lambda i: (jnp.maximum(i * hb - 1, 0), col)),
                pl.BlockSpec((HALO, GW), lambda i: (jnp.minimum((i + 1) * hb, last), col))]

    vec = lambda: pl.BlockSpec((None, 1, GW), lambda i: (l, 0, 0))
    cw = jnp.pad(cconv_w, ((0, 0), (0, 32 - CONF_W), (0, 0)))
    sw = jnp.pad(sconv_w, ((0, 0), (0, 8 - 3), (0, 0)))
    return pl.pallas_call(
        _conv_kernel,
        out_shape=(jax.ShapeDtypeStruct((NT, GW), BF16), jax.ShapeDtypeStruct((NT, GW), BF16)),
        grid=(N_TILES,),
        in_specs=[pl.BlockSpec((TM, GW), lambda i: (i, 0))] + halo_specs(1) + halo_specs(2) + [
            pl.BlockSpec((None, 8, GW), lambda i: (l, 0, 0)),
            pl.BlockSpec((None, 32, GW), lambda i: (l, 0, 0)),
            vec(), vec(), vec()],
        out_specs=(pl.BlockSpec((TM, GW), lambda i: (i, 0)), pl.BlockSpec((TM, GW), lambda i: (i, 0))),
        scratch_shapes=[pltpu.VMEM((TM + 2 * HALO, GW), F32)] * 2 + [pltpu.VMEM((8, GSH_ROWS, GW), F32)],
        compiler_params=_cparams(1),
        name="conv_mixers",
    )(cv, cv, cv, cv, cv, cv, cv, sw, cw, cconv_b.reshape(DEPTH, 1, GW), cnorm_g.reshape(DEPTH, 1, GW),
      cnorm_b.reshape(DEPTH, 1, GW))


R_ROWS = 32
_SLOT_A = np.array([0, 2, 2, 0, 0, 1], np.int32)
_SLOT_B = np.array([1, 1, 3, 3, 2, 3], np.int32)


def _route(lg):
    gl = [lg[N_EXPERTS + g:N_EXPERTS + g + 1, :] for g in range(N_GROUPS)]
    gmax = functools.reduce(jnp.maximum, gl)
    pg = 1.0 / functools.reduce(jnp.add, [jnp.exp(v - gmax) for v in gl])
    gsel = jnp.full_like(gmax, N_GROUPS - 1).astype(I32)
    for g in range(N_GROUPS - 2, -1, -1):
        gsel = jnp.where(gl[g] == gmax, g, gsel)
    el = []
    for j in range(PER_GROUP):
        v = lg[j:j + 1, :]
        for g in range(1, N_GROUPS):
            v = jnp.where(gsel == g, lg[g * PER_GROUP + j:g * PER_GROUP + j + 1, :], v)
        el.append(v)
    v1 = functools.reduce(jnp.maximum, el)
    i1 = jnp.full_like(gsel, PER_GROUP - 1)
    for j in range(PER_GROUP - 2, -1, -1):
        i1 = jnp.where(el[j] == v1, j, i1)
    rest = [jnp.where(i1 == j, -jnp.inf, el[j]) for j in range(PER_GROUP)]
    v2 = functools.reduce(jnp.maximum, rest)
    i2 = jnp.full_like(gsel, PER_GROUP - 1)
    for j in range(PER_GROUP - 2, -1, -1):
        i2 = jnp.where(jnp.logical_and(rest[j] == v2, i1 != j), j, i2)
    e2 = jnp.exp(v2 - v1)
    w1 = pg / (1.0 + e2)
    w2 = pg * e2 / (1.0 + e2)
    lo = jnp.minimum(i1, i2)
    hi = jnp.maximum(i1, i2)
    w_lo = jnp.where(i1 < i2, w1, w2)
    w_hi = jnp.where(i1 < i2, w2, w1)
    pair = jnp.where(lo == 0, jnp.where(hi == 1, 0, jnp.where(hi == 2, 4, 3)),
                     jnp.where(lo == 1, jnp.where(hi == 2, 1, 5), 2))
    cls = gsel * 6 + pair
    a_is_hi = pair == 1
    return cls, jnp.where(a_is_hi, w_hi, w_lo), jnp.where(a_is_hi, w_lo, w_hi)


def _rows_to_slabs(v, stage_ref, out_ref):
    n = v.shape[0]
    for c in range(SLAB):
        stage_ref[pl.ds(c, n, stride=SLAB), :] = v[:, c * 128:(c + 1) * 128]
    out_ref[...] = stage_ref[...].astype(out_ref.dtype)


def _slabs_to_rows(src_ref, stage_ref, n):
    stage_ref[...] = src_ref[...].astype(F32)
    return jnp.concatenate([stage_ref[pl.ds(c, n, stride=SLAB), :] for c in range(SLAB)], axis=1)


def _out_kernel(oa_ref, obl_ref, obc_ref, oc_ref, od_ref, w_ref, xl_ref, xc_ref, mod_ref, g_ref, wrh_ref, wrl_ref,
                br_ref, tri_ref, w2_ref, x1_ref, hp_ref, cls_ref, gw_ref, cnt_ref, w2b_ref, carry, stage):
    i = pl.program_id(0)
    w2b_ref[...] = w2_ref[...].astype(BF16)

    @pl.when(i == 0)
    def _():
        carry[...] = jnp.zeros_like(carry)

    ob = jnp.where(i < LAT_TILES, obl_ref[...], obc_ref[...])
    y = jnp.dot(oa_ref[...], w_ref[0:GW, :], preferred_element_type=F32)
    y = y + jnp.dot(ob, w_ref[GW:2 * GW, :], preferred_element_type=F32)
    y = y + jnp.dot(oc_ref[...], w_ref[2 * GW:3 * GW, :], preferred_element_type=F32)
    y = y + jnp.dot(od_ref[...], w_ref[3 * GW:4 * GW, :], preferred_element_type=F32)
    x1 = _stream_tile(xl_ref, xc_ref) + mod_ref[2:3, :] * y
    x1_ref[...] = x1
    h2 = _rmsnorm_mod(x1, g_ref[...], mod_ref[3:4, :], mod_ref[4:5, :])
    _rows_to_slabs(h2, stage, hp_ref)

    h_hi, h_lo = _split_bf16(h2)
    lg = jnp.dot(h_hi, wrh_ref[...], preferred_element_type=F32)
    lg = lg + jnp.dot(h_lo, wrh_ref[...], preferred_element_type=F32)
    lg = lg + jnp.dot(h_hi, wrl_ref[...], preferred_element_type=F32)
    lg = jnp.transpose(lg + br_ref[...])[0:R_ROWS, :]
    cls, w_a, w_b = _route(lg)

    onehot = jnp.where(lax.broadcasted_iota(I32, (R_ROWS, TM), 0) == cls, 1.0, 0.0)
    prefix = jnp.dot(onehot.astype(BF16), tri_ref[...], preferred_element_type=F32)
    rank = jnp.sum(onehot * (prefix - 1.0 + carry[:, 0:1]), axis=0, keepdims=True)
    total = carry[...] + jnp.sum(onehot, axis=1, keepdims=True)
    carry[...] = total
    cnt_ref[...] = total.astype(I32)
    cls_ref[...] = jnp.concatenate([cls, rank.astype(I32), jnp.zeros((6, TM), I32)], axis=0)
    gw_ref[...] = jnp.concatenate([w_a, w_b, jnp.zeros((6, TM), F32)], axis=0)


def _out_call(l, o_a, o_b_lat, o_b_ctx, o_c, o_d, w_out, x_lat, x_ctx, ctx_base, mod, g_norm2, wr_hi, wr_lo,
              b_route, w_down_e):
    row = lambda: pl.BlockSpec((TM, GW), lambda i: (i, 0))
    tri = jnp.asarray(np.triu(np.ones((TM, TM), np.float32)), BF16)
    return pl.pallas_call(
        _out_kernel,
        out_shape=(jax.ShapeDtypeStruct((NT, D), F32), jax.ShapeDtypeStruct((NT * SLAB, 128), BF16),
                   jax.ShapeDtypeStruct((8, NT), I32), jax.ShapeDtypeStruct((8, NT), F32),
                   jax.ShapeDtypeStruct((R_ROWS, 128), I32),
                   jax.ShapeDtypeStruct((N_EXPERTS * D_EXPERT, D), BF16)),
        grid=(N_TILES,),
        in_specs=[row(),
                  pl.BlockSpec((TM, GW), lambda i: (jnp.minimum(i, LAT_TILES - 1), 0)),
                  pl.BlockSpec((TM, GW), lambda i: (jnp.maximum(i - LAT_TILES, 0), 0)),
                  row(), row(),
                  pl.BlockSpec((None, D, D), lambda i: (l, 0, 0), pipeline_mode=pl.Buffered(1))]
        + _stream_specs(ctx_base) + [
                  pl.BlockSpec((None, None, 8, D), lambda i: (l, _mod_row(i), 0, 0)),
                  pl.BlockSpec((None, 1, D), lambda i: (l, 0, 0)),
                  pl.BlockSpec((None, D, 128), lambda i: (l, 0, 0)),
                  pl.BlockSpec((None, D, 128), lambda i: (l, 0, 0)),
                  pl.BlockSpec((None, 1, 128), lambda i: (l, 0, 0)),
                  pl.BlockSpec((TM, TM), lambda i: (0, 0)),
                  _cast_spec(W2_ROWS, D, l)],
        out_specs=(pl.BlockSpec((TM, D), lambda i: (i, 0)),
                   pl.BlockSpec((TM * SLAB, 128), lambda i: (i, 0)),
                   pl.BlockSpec((8, TM), lambda i: (0, i)), pl.BlockSpec((8, TM), lambda i: (0, i)),
                   pl.BlockSpec((R_ROWS, 128), lambda i: (0, 0)),
                   _cast_out_spec(W2_ROWS, D)),
        scratch_shapes=[pltpu.VMEM((R_ROWS, 128), F32), pltpu.VMEM((TM * SLAB, 128), F32)],
        compiler_params=_cparams(1),
        name="out_proj_route",
    )(o_a, o_b_lat, o_b_ctx, o_c, o_d, w_out, x_lat, x_ctx, mod, g_norm2.reshape(DEPTH, 1, D), wr_hi, wr_lo,
      b_route, tri, w_down_e.reshape(DEPTH * N_EXPERTS * D_EXPERT, D))


ROW_UNROLL = 8


def _gate_column(g_ref, src_ref, base):
    sub = lax.broadcasted_iota(I32, (8, 128), 0)
    pieces = []
    for g in range(TME // 8):
        v = jnp.zeros((8, 128), F32)
        for k in range(8):
            v = jnp.where(sub == k, g_ref[src_ref[base + 8 * g + k]], v)
        pieces.append(v)
    return jnp.concatenate(pieces, axis=0)


def _expert_kernel(ea_ref, eb_ref, nvalid_ref, starts_ref, counts_ref, cls_ref, rank_ref, gwa_ref, gwb_ref, hp_hbm, w1a_ref, w3a_ref, w2a_ref, w1b_ref,
                   w3b_ref, w2b_ref, yt_hbm, src_ref, dst_ref, hbuf0, hbuf1, ybuf0, ybuf1, stage, gsem, ssem, zsem):
    i = pl.program_id(0)
    nvalid = nvalid_ref[0]
    hbufs, ybufs = (hbuf0, hbuf1), (ybuf0, ybuf1)

    def slab(ref, row):
        return ref.at[pl.ds(pl.multiple_of(row * SLAB, SLAB), SLAB), :]

    def gather_copy(tile, sl, j):
        return pltpu.make_async_copy(slab(hp_hbm, src_ref[tile * TME + j]), slab(hbufs[sl], j), gsem.at[sl])

    def scatter_copy(tile, sl, j):
        return pltpu.make_async_copy(slab(ybufs[sl], j), slab(yt_hbm, dst_ref[(tile + 1) * TME + j]), ssem.at[sl])

    def for_rows(fn):
        def body(j, carry):
            fn(j)
            return carry

        lax.fori_loop(0, TME, body, 0, unroll=ROW_UNROLL)

    @pl.when(i == 0)
    def _():
        def unused(p, carry):
            src_ref[p] = NT - 1
            dst_ref[p + TME] = NT + ((p + TME) & (2 * TME - 1))
            return carry

        def dummy_tile(j, carry):
            dst_ref[j] = NT + j
            src_ref[nvalid * TME + j] = NT - 1
            return carry

        lax.fori_loop(0, TME, dummy_tile, 0, unroll=ROW_UNROLL)

        def class_tail(c, carry):
            lo = starts_ref[c] + counts_ref[c]
            hi = starts_ref[c] + ((counts_ref[c] + TME - 1) // TME) * TME
            lax.fori_loop(lo, hi, unused, 0)
            return carry

        lax.fori_loop(0, N_CLASSES, class_tail, 0)

        def place(t, carry):
            p = starts_ref[cls_ref[t]] + rank_ref[t]
            src_ref[p] = t
            dst_ref[p + TME] = t
            return carry

        lax.fori_loop(0, NT, place, 0, unroll=ROW_UNROLL)
        for sl in range(2):
            ybufs[sl][...] = jnp.zeros_like(ybufs[sl])
            spare = pltpu.make_async_copy(ybufs[sl], yt_hbm.at[pl.ds((NT + sl * TME) * SLAB, TME * SLAB), :], zsem)
            spare.start()
            spare.wait()
        for_rows(lambda j: gather_copy(0, 0, j).start())

    def tile_body(sl):
        for_rows(lambda j: gather_copy(0, sl, 0).wait())
        h = _slabs_to_rows(hbufs[sl], stage, TME).astype(BF16)
        for j in range(TME):
            gather_copy(i + 1, 1 - sl, j).start()
        for j in range(TME):
            scatter_copy(i - 1, 1 - sl, j).start()
        y = None
        for g_ref, w1_ref, w3_ref, w2_ref in ((gwa_ref, w1a_ref, w3a_ref, w2a_ref),
                                              (gwb_ref, w1b_ref, w3b_ref, w2b_ref)):
            gate = jnp.dot(h, w1_ref[...], preferred_element_type=F32)
            up = jnp.dot(h, w3_ref[...], preferred_element_type=F32)
            gcol = _gate_column(g_ref, src_ref, i * TME)
            hid = gate * jax.nn.sigmoid(gate) * up * jnp.concatenate([gcol] * (D_EXPERT // 128), axis=1)
            yk = jnp.dot(hid.astype(BF16), w2_ref[...], preferred_element_type=F32)
            y = yk if y is None else y + yk

        @pl.when(i >= 1)
        def _():
            for_rows(lambda j: scatter_copy(0, sl, 0).wait())

        _rows_to_slabs(y, stage, ybufs[sl])

        @pl.when(i == nvalid - 1)
        def _():
            for_rows(lambda j: scatter_copy(i, sl, j).start())
            for_rows(lambda j: scatter_copy(0, 1 - sl, 0).wait())
            for_rows(lambda j: scatter_copy(0, sl, 0).wait())
            for_rows(lambda j: gather_copy(0, 1 - sl, 0).wait())

    for sl in range(2):
        @pl.when(jnp.logical_and(i < nvalid, i % 2 == sl))
        def _(sl=sl):
            tile_body(sl)


Y_ROWS = NT + 2 * TME


def _expert_call(tile_ea, tile_eb, nvalid, starts, counts, cls, rank, gwa, gwb, hp, w1b, w3b, w2b):
    up_a = pl.BlockSpec((None, D, D_EXPERT), lambda i, ea, eb, *_: (ea[i], 0, 0))
    up_b = pl.BlockSpec((None, D, D_EXPERT), lambda i, ea, eb, *_: (eb[i], 0, 0))
    down_a = pl.BlockSpec((None, D_EXPERT, D), lambda i, ea, eb, *_: (ea[i], 0, 0))
    down_b = pl.BlockSpec((None, D_EXPERT, D), lambda i, ea, eb, *_: (eb[i], 0, 0))
    return pl.pallas_call(
        _expert_kernel,
        out_shape=jax.ShapeDtypeStruct((Y_ROWS * SLAB, 128), BF16),
        grid_spec=pltpu.PrefetchScalarGridSpec(
            num_scalar_prefetch=9,
            grid=(P_TILES,),
            in_specs=[pl.BlockSpec(memory_space=pl.ANY),
                      up_a, up_a, down_a, up_b, up_b, down_b],
            out_specs=pl.BlockSpec(memory_space=pl.ANY),
            scratch_shapes=[pltpu.SMEM((P_ROWS + TME,), I32), pltpu.SMEM((P_ROWS + TME,), I32)]
            + [pltpu.VMEM((TME * SLAB, 128), BF16)] * 4
            + [pltpu.VMEM((TME * SLAB, 128), F32),
                            pltpu.SemaphoreType.DMA((2,)), pltpu.SemaphoreType.DMA((2,)),
                            pltpu.SemaphoreType.DMA]),
        compiler_params=_cparams(1),
        name="moe_experts",
    )(tile_ea, tile_eb, nvalid, starts, counts, cls, rank, gwa, gwb, hp, w1b, w3b, w2b, w1b, w3b, w2b)


def _combine_kernel(x1_ref, mod_ref, gf_ref, y_ref, o_ref, stage, *, final):
    x2 = x1_ref[...] + mod_ref[5:6, :] * _slabs_to_rows(y_ref, stage, TM)
    if final:
        ms = jnp.mean(x2 * x2, axis=-1, keepdims=True)
        x2 = x2 * lax.rsqrt(ms + EPS) * gf_ref[...]
    o_ref[...] = x2


def _combine_call(l, x1, mod, g_final, yt, final):
    n_tiles = LAT_TILES if final else N_TILES
    return pl.pallas_call(
        functools.partial(_combine_kernel, final=final),
        out_shape=jax.ShapeDtypeStruct((n_tiles * TM, D), F32),
        grid=(n_tiles,),
        in_specs=[pl.BlockSpec((TM, D), lambda i: (i, 0)),
                  pl.BlockSpec((None, None, 8, D), lambda i: (l, _mod_row(i), 0, 0)),
                  pl.BlockSpec((1, D), lambda i: (0, 0)),
                  pl.BlockSpec((TM * SLAB, 128), lambda i: (i, 0))],
        out_specs=pl.BlockSpec((TM, D), lambda i: (i, 0)),
        scratch_shapes=[pltpu.VMEM((TM * SLAB, 128), F32)],
        compiler_params=_cparams(1),
        name="moe_combine",
    )(x1, mod, g_final.reshape(1, D), yt)


def _slot_plan(counts):
    padded = ((counts + TME - 1) // TME) * TME
    ends = jnp.cumsum(padded)
    starts = ends - padded
    tile_start = jnp.arange(P_TILES, dtype=I32) * TME
    tile_cls = jnp.minimum(jnp.sum((tile_start[:, None] >= ends[None, :]).astype(I32), axis=1), N_CLASSES - 1)
    nvalid = (ends[-1] // TME).reshape(1)
    grp = tile_cls // 6
    tile_ea = grp * PER_GROUP + jnp.asarray(_SLOT_A)[tile_cls % 6]
    tile_eb = grp * PER_GROUP + jnp.asarray(_SLOT_B)[tile_cls % 6]
    return starts, tile_ea, tile_eb, nvalid


def _rope_tables():
    t = jnp.arange(SEQ, dtype=I32)
    row = (t // GRID_W).astype(F32)
    col = (t % GRID_W).astype(F32)
    half = HEAD_DIM // 4
    freqs = ROPE_BASE ** (-jnp.arange(0, 2 * half, 2, dtype=F32) / (2 * half))
    ang_r = row[:, None] * freqs
    ang_c = col[:, None] * freqs
    cos_h = jnp.concatenate([jnp.cos(ang_r), jnp.cos(ang_r), jnp.cos(ang_c), jnp.cos(ang_c)], axis=1)
    sin_h = jnp.concatenate([-jnp.sin(ang_r), jnp.sin(ang_r), -jnp.sin(ang_c), jnp.sin(ang_c)], axis=1)
    cos_t = jnp.concatenate([jnp.tile(cos_h, (1, 2)), jnp.ones((TM, 128), F32)], axis=0)
    sin_t = jnp.concatenate([jnp.tile(sin_h, (1, 2)), jnp.zeros((TM, 128), F32)], axis=0)
    return cos_t, sin_t


def kernel(x, c, ctx, c_ctx, w_mod, b_mod, g_norm1, g_norm2, w_in, w_out, attn_sink, sconv_w, cconv_w, cconv_b,
           cnorm_g, cnorm_b, w_route_group, b_route_group, w_route_expert, b_route_expert, w_gate_e, w_up_e,
           w_down_e, g_final):
    x_lat, x_ctx, ctx_base = x.reshape(N_LAT, D), ctx.reshape(N_CTX, D), 0
    c8 = jnp.zeros((8, D), F32).at[0:BATCH].set(c).at[BATCH].set(c_ctx)
    q_w, k_w, v_w = w_in[:, :, 0:512], w_in[:, :, 512:640], w_in[:, :, 640:768]
    swap = lambda w: jnp.concatenate([w[:, :, HEAD_DIM:], w[:, :, :HEAD_DIM]], axis=-1)
    wqkv = jnp.concatenate([q_w, k_w, v_w, swap(k_w), swap(v_w)], axis=-1).astype(BF16)
    wf = w_in[:, :, 768:1280].astype(BF16)
    wcv = w_in[:, :, 1280:].astype(BF16)
    w_out_b = w_out.astype(BF16)
    w_route = jnp.concatenate([w_route_expert, w_route_group,
                               jnp.zeros((DEPTH, D, 128 - N_EXPERTS - N_GROUPS), F32)], axis=2)
    wr_hi = w_route.astype(BF16)
    wr_lo = (w_route - wr_hi.astype(F32)).astype(BF16)
    b_route = jnp.concatenate([b_route_expert, b_route_group,
                               jnp.zeros((DEPTH, 128 - N_EXPERTS - N_GROUPS), F32)], axis=1).reshape(DEPTH, 1, 128)
    cos_t, sin_t = _rope_tables()

    mod = _mod_call(c8, w_mod, b_mod)
    mod = mod.reshape(DEPTH, 8, N_MOD, D)[:, :3]
    mod = jnp.pad(mod, ((0, 0), (0, 0), (0, 8 - N_MOD), (0, 0)))

    out = None
    for l in range(DEPTH):
        qkv, f, cv, w1b, w3b = _in_call(l, x_lat, x_ctx, ctx_base, mod, g_norm1, wqkv, wf, wcv, cos_t, sin_t,
                                        w_gate_e, w_up_e)
        o_a = _attn_call(l, attn_sink, qkv)
        o_b_lat = _fft_call(f, 64, 0, BATCH)
        o_b_ctx = _fft_call(f, 16, N_LAT // CTX, BATCH)
        o_c, o_d = _conv_call(l, cv, sconv_w, cconv_w, cconv_b, cnorm_g, cnorm_b)
        x1, hp, cls8, gw8, cnt, w2b = _out_call(l, o_a, o_b_lat, o_b_ctx, o_c, o_d, w_out_b, x_lat, x_ctx, ctx_base,
                                                mod, g_norm2, wr_hi, wr_lo, b_route, w_down_e)
        counts = cnt[:N_CLASSES, 0]
        starts, tile_ea, tile_eb, nvalid = _slot_plan(counts)
        yt = _expert_call(tile_ea, tile_eb, nvalid, starts, counts, cls8[0], cls8[1], gw8[0], gw8[1], hp, w1b.reshape(N_EXPERTS, D, D_EXPERT),
                          w3b.reshape(N_EXPERTS, D, D_EXPERT), w2b.reshape(N_EXPERTS, D_EXPERT, D))
        final = l == DEPTH - 1
        res = _combine_call(l, x1, mod, g_final, yt, final)
        if final:
            out = res
        else:
            x_lat, x_ctx, ctx_base = res, res, LAT_TILES
    return out.reshape(BATCH, SEQ, D)
```

```python
import functools

import numpy as np
import jax
import jax.numpy as jnp
from jax import lax
from jax.experimental import pallas as pl
from jax.experimental.pallas import tpu as pltpu

F32 = jnp.float32
BF16 = jnp.bfloat16
I32 = jnp.int32

D = 2048
BATCH = 2
SEQ = 4096
CTX = 256
DEPTH = 2
GRID_W = 64
N_LAT = BATCH * SEQ
N_CTX = BATCH * CTX
NT = N_LAT + N_CTX
GW = 512
HEAD_DIM = 64
N_HEADS = 8
WINDOW = 128
ROPE_BASE = 10000.0
N_MOD = 6
EPS = 1e-6
NEG_INF = -1e30
N_GROUPS = 4
PER_GROUP = 4
N_EXPERTS = 16
D_EXPERT = 512
N_CLASSES = N_GROUPS * 6
CONF_W = 31
CONF_PAD = 15

TM = 256
N_TILES = NT // TM
LAT_TILES = N_LAT // TM
SEQ_TILES = SEQ // TM
TQ = 128
HALO = 16
TME = 128
P_ROWS = NT + N_CLASSES * TME
P_TILES = P_ROWS // TME
SLAB = D // 128
VMEM_LIMIT = 56 * 1024 * 1024


def _cparams(n_axes=1, **kw):
    return pltpu.CompilerParams(dimension_semantics=("arbitrary",) * n_axes,
                                vmem_limit_bytes=VMEM_LIMIT, **kw)


def _split_bf16(v):
    hi = v.astype(BF16)
    lo = (v - hi.astype(F32)).astype(BF16)
    return hi, lo


def _mod_row(i):
    return jnp.minimum(i // SEQ_TILES, 2)


MOD_TN = 1536


def _mod_kernel(c_ref, w_ref, b_ref, o_ref):
    c = c_ref[...]
    s = c * jax.nn.sigmoid(c)
    s_hi, s_lo = _split_bf16(s)
    w_hi, w_lo = _split_bf16(w_ref[...])
    acc = jnp.dot(s_hi, w_hi, preferred_element_type=F32)
    acc = acc + jnp.dot(s_lo, w_hi, preferred_element_type=F32)
    acc = acc + jnp.dot(s_hi, w_lo, preferred_element_type=F32)
    o_ref[...] = acc + b_ref[...]


def _mod_call(c8, w_mod, b_mod):
    n = N_MOD * D
    return pl.pallas_call(
        _mod_kernel,
        out_shape=jax.ShapeDtypeStruct((DEPTH, 8, n), F32),
        grid=(DEPTH, n // MOD_TN),
        in_specs=[
            pl.BlockSpec((8, D), lambda l, j: (0, 0)),
            pl.BlockSpec((None, D, MOD_TN), lambda l, j: (l, 0, j)),
            pl.BlockSpec((None, 1, MOD_TN), lambda l, j: (l, 0, j)),
        ],
        out_specs=pl.BlockSpec((None, 8, MOD_TN), lambda l, j: (l, 0, j)),
        compiler_params=_cparams(2),
        name="adaln_mod",
    )(c8, w_mod, b_mod.reshape(DEPTH, 1, n))


QKV_W = 1024
CV_IN = 5 * GW
CV_OUT = 3 * GW


def _rmsnorm_mod(x, g, shift, scale):
    ms = jnp.mean(x * x, axis=-1, keepdims=True)
    y = x * lax.rsqrt(ms + EPS) * g
    return y * (1.0 + scale) + shift


def _stream_specs(ctx_base):
    return [pl.BlockSpec((TM, D), lambda i, *_: (jnp.minimum(i, LAT_TILES - 1), 0)),
            pl.BlockSpec((TM, D), lambda i, *_: (ctx_base + jnp.maximum(i - LAT_TILES, 0), 0))]


def _stream_tile(xl_ref, xc_ref):
    return jnp.where(pl.program_id(0) < LAT_TILES, xl_ref[...], xc_ref[...])


CAST_STEPS = 32


def _cast_spec(rows, cols, l):
    return pl.BlockSpec((rows, cols), lambda i, *_: (l * CAST_STEPS + jnp.minimum(i, CAST_STEPS - 1), 0))


def _cast_out_spec(rows, cols):
    return pl.BlockSpec((rows, cols), lambda i, *_: (jnp.minimum(i, CAST_STEPS - 1), 0))


def _in_kernel(xl_ref, xc_ref, mod_ref, g_ref, wqkv_ref, wf_ref, wcv_ref, cos_ref, sin_ref, w1_ref, w3_ref,
               qkv_ref, f_ref, cv_ref, w1b_ref, w3b_ref):
    w1b_ref[...] = w1_ref[...].astype(BF16)
    w3b_ref[...] = w3_ref[...].astype(BF16)
    h = _rmsnorm_mod(_stream_tile(xl_ref, xc_ref), g_ref[...], mod_ref[0:1, :], mod_ref[1:2, :])
    hb = h.astype(BF16)
    qkv = jnp.dot(hb, wqkv_ref[...], preferred_element_type=F32)
    cos = cos_ref[...]
    sin = sin_ref[...]
    lane = lax.broadcasted_iota(I32, (TM, 128), 1)
    first_half = (lane % 32) < 16

    def rope(t):
        rot = jnp.where(first_half, pltpu.roll(t, 112, 1), pltpu.roll(t, 16, 1))
        return t * cos + rot * sin

    scale = HEAD_DIM ** -0.5
    for c in range(4):
        qkv_ref[:, c * 128:(c + 1) * 128] = (rope(qkv[:, c * 128:(c + 1) * 128]) * scale).astype(BF16)
    qkv_ref[:, 512:640] = rope(qkv[:, 512:640]).astype(BF16)
    qkv_ref[:, 640:768] = qkv[:, 640:768].astype(BF16)
    qkv_ref[:, 768:896] = rope(qkv[:, 768:896]).astype(BF16)
    qkv_ref[:, 896:1024] = qkv[:, 896:1024].astype(BF16)

    f_ref[...] = jnp.dot(hb, wf_ref[...], preferred_element_type=F32).astype(BF16)

    cv = jnp.dot(hb, wcv_ref[...], preferred_element_type=F32)
    cv_ref[:, 0:GW] = cv[:, 0:GW]
    cv_ref[:, GW:2 * GW] = cv[:, GW:2 * GW] * cv[:, 2 * GW:3 * GW]
    cv_ref[:, 2 * GW:3 * GW] = cv[:, 3 * GW:4 * GW] * jax.nn.sigmoid(cv[:, 4 * GW:5 * GW])


W13_ROWS = N_EXPERTS * D // CAST_STEPS
W2_ROWS = N_EXPERTS * D_EXPERT // CAST_STEPS


def _in_call(l, x_lat, x_ctx, ctx_base, mod, g_norm1, wqkv, wf, wcv, cos_t, sin_t, w_gate_e, w_up_e):
    const = dict(pipeline_mode=pl.Buffered(1))
    flat = lambda w: w.reshape(DEPTH * N_EXPERTS * D, D_EXPERT)
    return pl.pallas_call(
        _in_kernel,
        out_shape=(jax.ShapeDtypeStruct((NT, QKV_W), BF16),
                   jax.ShapeDtypeStruct((NT, GW), BF16),
                   jax.ShapeDtypeStruct((NT, CV_OUT), F32),
                   jax.ShapeDtypeStruct((N_EXPERTS * D, D_EXPERT), BF16),
                   jax.ShapeDtypeStruct((N_EXPERTS * D, D_EXPERT), BF16)),
        grid=(N_TILES,),
        in_specs=_stream_specs(ctx_base) + [
            pl.BlockSpec((None, None, 8, D), lambda i: (l, _mod_row(i), 0, 0)),
            pl.BlockSpec((None, 1, D), lambda i: (l, 0, 0)),
            pl.BlockSpec((None, D, QKV_W), lambda i: (l, 0, 0), **const),
            pl.BlockSpec((None, D, GW), lambda i: (l, 0, 0), **const),
            pl.BlockSpec((None, D, CV_IN), lambda i: (l, 0, 0), **const),
            pl.BlockSpec((TM, 128), lambda i: (jnp.where(i < LAT_TILES, i % SEQ_TILES, SEQ_TILES), 0)),
            pl.BlockSpec((TM, 128), lambda i: (jnp.where(i < LAT_TILES, i % SEQ_TILES, SEQ_TILES), 0)),
            _cast_spec(W13_ROWS, D_EXPERT, l), _cast_spec(W13_ROWS, D_EXPERT, l),
        ],
        out_specs=(pl.BlockSpec((TM, QKV_W), lambda i: (i, 0)),
                   pl.BlockSpec((TM, GW), lambda i: (i, 0)),
                   pl.BlockSpec((TM, CV_OUT), lambda i: (i, 0)),
                   _cast_out_spec(W13_ROWS, D_EXPERT), _cast_out_spec(W13_ROWS, D_EXPERT)),
        compiler_params=_cparams(1),
        name="norm_in_proj",
    )(x_lat, x_ctx, mod, g_norm1.reshape(DEPTH, 1, D), wqkv, wf, wcv, cos_t, sin_t, flat(w_gate_e), flat(w_up_e))


N_QBLK_LAT = N_LAT // TQ
N_QBLK = NT // TQ
QBLK_PER_SEQ = SEQ // TQ
KWIN = 3 * TQ


def _attn_heads(sink_ref, l, q_ref, keys, vals, bias, o_ref):
    lane = lax.broadcasted_iota(I32, (TQ, 128), 1)
    row2 = lax.broadcasted_iota(I32, (2 * TQ, 1), 0)
    zero = jnp.zeros((TQ, 128), BF16)
    if bias is not None:
        bias = jnp.concatenate([bias, bias], axis=0)
    for hk in range(2):
        c0, c1 = 2 * hk, 2 * hk + 1
        for p in range(2):
            kx = keys[0] if hk == p else keys[1]
            vx = vals[0] if hk == p else vals[1]
            in_half = (lane // HEAD_DIM) == p
            qs = jnp.concatenate([jnp.where(in_half, q_ref[:, c0 * 128:(c0 + 1) * 128], zero),
                                  jnp.where(in_half, q_ref[:, c1 * 128:(c1 + 1) * 128], zero)], axis=0)
            s = lax.dot_general(qs, kx, (((1,), (1,)), ((), ())), preferred_element_type=F32)
            if bias is not None:
                s = s + bias
            sink = jnp.where(row2 < TQ, sink_ref[l, 2 * c0 + p], sink_ref[l, 2 * c1 + p])
            m = jnp.maximum(jnp.max(s, axis=-1, keepdims=True), sink)
            e = jnp.exp(s - m)
            den = jnp.sum(e, axis=-1, keepdims=True) + jnp.exp(sink - m)
            o = (jnp.dot(e.astype(BF16), vx, preferred_element_type=F32) / den).astype(BF16)
            for k, c in enumerate((c0, c1)):
                lo = c * 128 + p * HEAD_DIM
                o_ref[:, lo:lo + HEAD_DIM] = o[k * TQ:(k + 1) * TQ, p * HEAD_DIM:(p + 1) * HEAD_DIM]


def _attn_kernel(sink_ref, q_ref, k_ref, v_ref, ks_ref, vs_ref, kc_ref, vc_ref, kcs_ref, vcs_ref, o_ref, *, l):
    i = pl.program_id(0)

    @pl.when(i < N_QBLK_LAT)
    def _latent():
        n = i % QBLK_PER_SEQ
        ws = pl.multiple_of(jnp.clip((n - 1) * TQ, 0, SEQ - KWIN), TQ)
        keys = [jnp.concatenate([r[pl.ds(ws, KWIN), :], c[...]], axis=0)
                for r, c in ((k_ref, kc_ref), (ks_ref, kcs_ref))]
        vals = [jnp.concatenate([r[pl.ds(ws, KWIN), :], c[...]], axis=0)
                for r, c in ((v_ref, vc_ref), (vs_ref, vcs_ref))]
        nk = KWIN + CTX
        qpos = n * TQ + lax.broadcasted_iota(I32, (TQ, nk), 0)
        col = lax.broadcasted_iota(I32, (TQ, nk), 1)
        valid = (jnp.abs(ws + col - qpos) <= WINDOW) | (col >= KWIN)
        _attn_heads(sink_ref, l, q_ref, keys, vals, jnp.where(valid, 0.0, NEG_INF), o_ref)

    @pl.when(i >= N_QBLK_LAT)
    def _context():
        _attn_heads(sink_ref, l, q_ref, [kc_ref[...], kcs_ref[...]], [vc_ref[...], vcs_ref[...]], None, o_ref)


def _attn_call(l, attn_sink, qkv):
    def seq_blk(i):
        return jnp.minimum(i // QBLK_PER_SEQ, 1)

    def ctx_blk(i):
        b = jnp.where(i < N_QBLK_LAT, i // QBLK_PER_SEQ, (i - N_QBLK_LAT) // (CTX // TQ))
        return N_LAT // CTX + b

    seq_specs = [pl.BlockSpec((SEQ, 128), functools.partial(lambda i, col: (seq_blk(i), col), col=col))
                 for col in (4, 5, 6, 7)]
    ctx_specs = [pl.BlockSpec((CTX, 128), functools.partial(lambda i, col: (ctx_blk(i), col), col=col))
                 for col in (4, 5, 6, 7)]
    return pl.pallas_call(
        functools.partial(_attn_kernel, l=l),
        out_shape=jax.ShapeDtypeStruct((NT, GW), BF16),
        grid=(N_QBLK,),
        in_specs=[pl.BlockSpec(memory_space=pltpu.SMEM),
                  pl.BlockSpec((TQ, GW), lambda i: (i, 0))] + seq_specs + ctx_specs,
        out_specs=pl.BlockSpec((TQ, GW), lambda i: (i, 0)),
        compiler_params=_cparams(1),
        name="window_attention",
    )(attn_sink, qkv, *([qkv] * 8))


FC = 256
FFT_UNROLL = 4


def _dft_tables(n):
    t = n * n
    j = np.arange(n)
    ang = 2.0 * np.pi * np.outer(j, j) / n
    cn, sn = np.cos(ang), np.sin(ang)
    m1 = np.block([[cn, sn], [-sn, cn]])
    m2 = np.concatenate([cn, sn], axis=1)
    angc = 2.0 * np.pi * np.outer(np.arange(128), np.arange(128)) / 128.0
    wc = np.concatenate([np.cos(angc), -np.sin(angc)], axis=1)
    angt = 2.0 * np.pi * np.outer(j, j) / t
    tw_re = np.broadcast_to(np.cos(angt)[:, :, None], (n, n, 128))
    tw_im = np.broadcast_to(-np.sin(angt)[:, :, None], (n, n, 128))
    def hi_lo(m):
        return jnp.stack(_split_bf16(jnp.asarray(m, F32)))

    return (hi_lo(m1), hi_lo(m2), hi_lo(wc),
            jnp.asarray(np.ascontiguousarray(tw_re), F32), jnp.asarray(np.ascontiguousarray(tw_im), F32))


def _fft_kernel(f_ref, m1_ref, m2_ref, wc_ref, twr_ref, twi_ref, o_ref, are, aim, hre, him, *, n):
    t = n * n
    ng = FC // 128
    for g in range(ng):
        fg = f_ref[:, g * 128:(g + 1) * 128]
        a = (jnp.dot(fg, wc_ref[0], preferred_element_type=F32)
             + jnp.dot(fg, wc_ref[1], preferred_element_type=F32))
        are[g] = a[:, :128]
        aim[g] = a[:, 128:]

    def table_dot(m_ref, v):
        return (jnp.dot(m_ref[0], v, preferred_element_type=F32)
                + jnp.dot(m_ref[1], v, preferred_element_type=F32))

    def gather_rows(re, im, first):
        rows = [pl.ds(first + u, n, stride=n) for u in range(FFT_UNROLL)]
        top = jnp.concatenate([re[g, r, :] for r in rows for g in range(ng)], axis=1)
        bot = jnp.concatenate([im[g, r, :] for r in rows for g in range(ng)], axis=1)
        return jnp.concatenate([top, bot], axis=0).astype(BF16)

    def stage1(it, carry):
        t1 = it * FFT_UNROLL
        h = table_dot(m1_ref, gather_rows(are, aim, t1))
        for u in range(FFT_UNROLL):
            dst = pl.ds(pl.multiple_of((t1 + u) * n, n), n)
            twr = twr_ref[t1 + u]
            twi = twi_ref[t1 + u]
            for g in range(ng):
                lanes = slice((u * ng + g) * 128, (u * ng + g + 1) * 128)
                h_re, h_im = h[:n, lanes], h[n:, lanes]
                hre[g, dst, :] = h_re * twr - h_im * twi
                him[g, dst, :] = h_re * twi + h_im * twr
        return carry

    lax.fori_loop(0, n // FFT_UNROLL, stage1, 0)

    def stage2(it, carry):
        k2 = it * FFT_UNROLL
        z = table_dot(m2_ref, gather_rows(hre, him, k2))
        z = z * ((t * 128.0) ** -0.5)
        for u in range(FFT_UNROLL):
            for g in range(ng):
                are[g, pl.ds(k2 + u, n, stride=n), :] = z[:, (u * ng + g) * 128:(u * ng + g + 1) * 128]
        return carry

    lax.fori_loop(0, n // FFT_UNROLL, stage2, 0)
    for g in range(ng):
        o_ref[:, g * 128:(g + 1) * 128] = are[g].astype(BF16)


def _fft_call(f, n, first_blk, n_seq):
    t = n * n
    m1, m2, wc, twr, twi = _dft_tables(n)
    full = lambda shape: pl.BlockSpec(shape, lambda b, h: (0,) * len(shape))
    return pl.pallas_call(
        functools.partial(_fft_kernel, n=n),
        out_shape=jax.ShapeDtypeStruct((n_seq * t, GW), BF16),
        grid=(n_seq, GW // FC),
        in_specs=[pl.BlockSpec((t, FC), lambda b, h: (first_blk + b, h)),
                  full((2, 2 * n, 2 * n)), full((2, n, 2 * n)), full((2, 128, 256)),
                  full((n, n, 128)), full((n, n, 128))],
        out_specs=pl.BlockSpec((t, FC), lambda b, h: (b, h)),
        scratch_shapes=[pltpu.VMEM((FC // 128, t, 128), F32)] * 4,
        compiler_params=_cparams(2),
        name="fourier_mix_%d" % t,
    )(f, m1, m2, wc, twr, twi)


CONV_ROWS = 32
GSH_ROWS = TM + 2 * HALO - 8


def _conv_kernel(sb_ref, p_ref, pp_ref, pn_ref, g_ref, gp_ref, gn_ref, sw_ref, cw_ref, cb_ref, ng_ref, nb_ref,
                 oc_ref, od_ref, pbuf, gbuf, gsh):
    i = pl.program_id(0)
    seq_start = jnp.logical_or(i % SEQ_TILES == 0, i >= LAT_TILES)
    seq_end = jnp.logical_or(i % SEQ_TILES == SEQ_TILES - 1, i >= LAT_TILES)
    for buf, main, prev, nxt in ((pbuf, p_ref, pp_ref, pn_ref), (gbuf, g_ref, gp_ref, gn_ref)):
        buf[0:HALO, :] = jnp.where(seq_start, 0.0, prev[...])
        buf[HALO:HALO + TM, :] = main[...]
        buf[HALO + TM:HALO + TM + HALO, :] = jnp.where(seq_end, 0.0, nxt[...])
    for s in range(8):
        gsh[s] = gbuf[s:s + GSH_ROWS, :]
    sw = sw_ref[...]
    cw = cw_ref[...]
    for r0 in range(0, TM, CONV_ROWS):
        base = HALO + r0
        acc = sw[0:1, :] * pbuf[base - 1:base - 1 + CONV_ROWS, :]
        acc = acc + sw[1:2, :] * pbuf[base:base + CONV_ROWS, :]
        acc = acc + sw[2:3, :] * pbuf[base + 1:base + 1 + CONV_ROWS, :]
        oc_ref[r0:r0 + CONV_ROWS, :] = (sb_ref[r0:r0 + CONV_ROWS, :] * acc).astype(BF16)

        z = None
        for j in range(CONF_W):
            o = base - CONF_PAD + j
            q = o - o % 8
            term = cw[j:j + 1, :] * gsh[o % 8, q:q + CONV_ROWS, :]
            z = term if z is None else z + term
        z = z + cb_ref[...]
        mu = jnp.mean(z, axis=-1, keepdims=True)
        zc = z - mu
        var = jnp.mean(zc * zc, axis=-1, keepdims=True)
        y = zc * lax.rsqrt(var + EPS) * ng_ref[...] + nb_ref[...]
        od_ref[r0:r0 + CONV_ROWS, :] = (y * jax.nn.sigmoid(y)).astype(BF16)


def _conv_call(l, cv, sconv_w, cconv_w, cconv_b, cnorm_g, cnorm_b):
    hb = TM // HALO
    last = NT // HALO - 1

    def halo_specs(col):
        return [pl.BlockSpec((TM, GW), lambda i: (i, col)),
                pl.BlockSpec((HALO, GW), lambda i: (jnp.maximum(i * hb - 1, 0), col)),
                pl.BlockSpec((HALO, GW), lambda i: (jnp.minimum((i + 1) * hb, last), col))]

    vec = lambda: pl.BlockSpec((None, 1, GW), lambda i: (l, 0, 0))
    cw = jnp.pad(cconv_w, ((0, 0), (0, 32 - CONF_W), (0, 0)))
    sw = jnp.pad(sconv_w, ((0, 0), (0, 8 - 3), (0, 0)))
    return pl.pallas_call(
        _conv_kernel,
        out_shape=(jax.ShapeDtypeStruct((NT, GW), BF16), jax.ShapeDtypeStruct((NT, GW), BF16)),
        grid=(N_TILES,),
        in_specs=[pl.BlockSpec((TM, GW), lambda i: (i, 0))] + halo_specs(1) + halo_specs(2) + [
            pl.BlockSpec((None, 8, GW), lambda i: (l, 0, 0)),
            pl.BlockSpec((None, 32, GW), lambda i: (l, 0, 0)),
            vec(), vec(), vec()],
        out_specs=(pl.BlockSpec((TM, GW), lambda i: (i, 0)), pl.BlockSpec((TM, GW), lambda i: (i, 0))),
        scratch_shapes=[pltpu.VMEM((TM + 2 * HALO, GW), F32)] * 2 + [pltpu.VMEM((8, GSH_ROWS, GW), F32)],
        compiler_params=_cparams(1),
        name="conv_mixers",
    )(cv, cv, cv, cv, cv, cv, cv, sw, cw, cconv_b.reshape(DEPTH, 1, GW), cnorm_g.reshape(DEPTH, 1, GW),
      cnorm_b.reshape(DEPTH, 1, GW))


R_ROWS = 32
_SLOT_A = np.array([0, 2, 2, 0, 0, 1], np.int32)
_SLOT_B = np.array([1, 1, 3, 3, 2, 3], np.int32)


def _route(lg):
    gl = [lg[N_EXPERTS + g:N_EXPERTS + g + 1, :] for g in range(N_GROUPS)]
    gmax = functools.reduce(jnp.maximum, gl)
    pg = 1.0 / functools.reduce(jnp.add, [jnp.exp(v - gmax) for v in gl])
    gsel = jnp.full_like(gmax, N_GROUPS - 1).astype(I32)
    for g in range(N_GROUPS - 2, -1, -1):
        gsel = jnp.where(gl[g] == gmax, g, gsel)
    el = []
    for j in range(PER_GROUP):
        v = lg[j:j + 1, :]
        for g in range(1, N_GROUPS):
            v = jnp.where(gsel == g, lg[g * PER_GROUP + j:g * PER_GROUP + j + 1, :], v)
        el.append(v)
    v1 = functools.reduce(jnp.maximum, el)
    i1 = jnp.full_like(gsel, PER_GROUP - 1)
    for j in range(PER_GROUP - 2, -1, -1):
        i1 = jnp.where(el[j] == v1, j, i1)
    rest = [jnp.where(i1 == j, -jnp.inf, el[j]) for j in range(PER_GROUP)]
    v2 = functools.reduce(jnp.maximum, rest)
    i2 = jnp.full_like(gsel, PER_GROUP - 1)
    for j in range(PER_GROUP - 2, -1, -1):
        i2 = jnp.where(jnp.logical_and(rest[j] == v2, i1 != j), j, i2)
    e2 = jnp.exp(v2 - v1)
    w1 = pg / (1.0 + e2)
    w2 = pg * e2 / (1.0 + e2)
    lo = jnp.minimum(i1, i2)
    hi = jnp.maximum(i1, i2)
    w_lo = jnp.where(i1 < i2, w1, w2)
    w_hi = jnp.where(i1 < i2, w2, w1)
    pair = jnp.where(lo == 0, jnp.where(hi == 1, 0, jnp.where(hi == 2, 4, 3)),
                     jnp.where(lo == 1, jnp.where(hi == 2, 1, 5), 2))
    cls = gsel * 6 + pair
    a_is_hi = pair == 1
    return cls, jnp.where(a_is_hi, w_hi, w_lo), jnp.where(a_is_hi, w_lo, w_hi)


def _rows_to_slabs(v, stage_ref, out_ref):
    n = v.shape[0]
    for c in range(SLAB):
        stage_ref[pl.ds(c, n, stride=SLAB), :] = v[:, c * 128:(c + 1) * 128]
    out_ref[...] = stage_ref[...].astype(out_ref.dtype)


def _slabs_to_rows(src_ref, stage_ref, n):
    stage_ref[...] = src_ref[...].astype(F32)
    return jnp.concatenate([stage_ref[pl.ds(c, n, stride=SLAB), :] for c in range(SLAB)], axis=1)


def _out_kernel(oa_ref, obl_ref, obc_ref, oc_ref, od_ref, w_ref, xl_ref, xc_ref, mod_ref, g_ref, wrh_ref, wrl_ref,
                br_ref, tri_ref, w2_ref, x1_ref, hp_ref, cls_ref, gw_ref, cnt_ref, w2b_ref, carry, stage):
    i = pl.program_id(0)
    w2b_ref[...] = w2_ref[...].astype(BF16)

    @pl.when(i == 0)
    def _():
        carry[...] = jnp.zeros_like(carry)

    ob = jnp.where(i < LAT_TILES, obl_ref[...], obc_ref[...])
    y = jnp.dot(oa_ref[...], w_ref[0:GW, :], preferred_element_type=F32)
    y = y + jnp.dot(ob, w_ref[GW:2 * GW, :], preferred_element_type=F32)
    y = y + jnp.dot(oc_ref[...], w_ref[2 * GW:3 * GW, :], preferred_element_type=F32)
    y = y + jnp.dot(od_ref[...], w_ref[3 * GW:4 * GW, :], preferred_element_type=F32)
    x1 = _stream_tile(xl_ref, xc_ref) + mod_ref[2:3, :] * y
    x1_ref[...] = x1
    h2 = _rmsnorm_mod(x1, g_ref[...], mod_ref[3:4, :], mod_ref[4:5, :])
    _rows_to_slabs(h2, stage, hp_ref)

    h_hi, h_lo = _split_bf16(h2)
    lg = jnp.dot(h_hi, wrh_ref[...], preferred_element_type=F32)
    lg = lg + jnp.dot(h_lo, wrh_ref[...], preferred_element_type=F32)
    lg = lg + jnp.dot(h_hi, wrl_ref[...], preferred_element_type=F32)
    lg = jnp.transpose(lg + br_ref[...])[0:R_ROWS, :]
    cls, w_a, w_b = _route(lg)

    onehot = jnp.where(lax.broadcasted_iota(I32, (R_ROWS, TM), 0) == cls, 1.0, 0.0)
    prefix = jnp.dot(onehot.astype(BF16), tri_ref[...], preferred_element_type=F32)
    rank = jnp.sum(onehot * (prefix - 1.0 + carry[:, 0:1]), axis=0, keepdims=True)
    total = carry[...] + jnp.sum(onehot, axis=1, keepdims=True)
    carry[...] = total
    cnt_ref[...] = total.astype(I32)
    cls_ref[...] = jnp.concatenate([cls, rank.astype(I32), jnp.zeros((6, TM), I32)], axis=0)
    gw_ref[...] = jnp.concatenate([w_a, w_b, jnp.zeros((6, TM), F32)], axis=0)


def _out_call(l, o_a, o_b_lat, o_b_ctx, o_c, o_d, w_out, x_lat, x_ctx, ctx_base, mod, g_norm2, wr_hi, wr_lo,
              b_route, w_down_e):
    row = lambda: pl.BlockSpec((TM, GW), lambda i: (i, 0))
    tri = jnp.asarray(np.triu(np.ones((TM, TM), np.float32)), BF16)
    return pl.pallas_call(
        _out_kernel,
        out_shape=(jax.ShapeDtypeStruct((NT, D), F32), jax.ShapeDtypeStruct((NT * SLAB, 128), BF16),
                   jax.ShapeDtypeStruct((8, NT), I32), jax.ShapeDtypeStruct((8, NT), F32),
                   jax.ShapeDtypeStruct((R_ROWS, 128), I32),
                   jax.ShapeDtypeStruct((N_EXPERTS * D_EXPERT, D), BF16)),
        grid=(N_TILES,),
        in_specs=[row(),
                  pl.BlockSpec((TM, GW), lambda i: (jnp.minimum(i, LAT_TILES - 1), 0)),
                  pl.BlockSpec((TM, GW), lambda i: (jnp.maximum(i - LAT_TILES, 0), 0)),
                  row(), row(),
                  pl.BlockSpec((None, D, D), lambda i: (l, 0, 0), pipeline_mode=pl.Buffered(1))]
        + _stream_specs(ctx_base) + [
                  pl.BlockSpec((None, None, 8, D), lambda i: (l, _mod_row(i), 0, 0)),
                  pl.BlockSpec((None, 1, D), lambda i: (l, 0, 0)),
                  pl.BlockSpec((None, D, 128), lambda i: (l, 0, 0)),
                  pl.BlockSpec((None, D, 128), lambda i: (l, 0, 0)),
                  pl.BlockSpec((None, 1, 128), lambda i: (l, 0, 0)),
                  pl.BlockSpec((TM, TM), lambda i: (0, 0)),
                  _cast_spec(W2_ROWS, D, l)],
        out_specs=(pl.BlockSpec((TM, D), lambda i: (i, 0)),
                   pl.BlockSpec((TM * SLAB, 128), lambda i: (i, 0)),
                   pl.BlockSpec((8, TM), lambda i: (0, i)), pl.BlockSpec((8, TM), lambda i: (0, i)),
                   pl.BlockSpec((R_ROWS, 128), lambda i: (0, 0)),
                   _cast_out_spec(W2_ROWS, D)),
        scratch_shapes=[pltpu.VMEM((R_ROWS, 128), F32), pltpu.VMEM((TM * SLAB, 128), F32)],
        compiler_params=_cparams(1),
        name="out_proj_route",
    )(o_a, o_b_lat, o_b_ctx, o_c, o_d, w_out, x_lat, x_ctx, mod, g_norm2.reshape(DEPTH, 1, D), wr_hi, wr_lo,
      b_route, tri, w_down_e.reshape(DEPTH * N_EXPERTS * D_EXPERT, D))


ROW_UNROLL = 8


def _gate_column(g_ref, src_ref, base):
    sub = lax.broadcasted_iota(I32, (8, 128), 0)
    pieces = []
    for g in range(TME // 8):
        v = jnp.zeros((8, 128), F32)
        for k in range(8):
            v = jnp.where(sub == k, g_ref[src_ref[base + 8 * g + k]], v)
        pieces.append(v)
    return jnp.concatenate(pieces, axis=0)


def _expert_kernel(ea_ref, eb_ref, nvalid_ref, starts_ref, counts_ref, cls_ref, rank_ref, gwa_ref, gwb_ref, hp_hbm, w1a_ref, w3a_ref, w2a_ref, w1b_ref,
                   w3b_ref, w2b_ref, yt_hbm, src_ref, dst_ref, hbuf0, hbuf1, ybuf0, ybuf1, stage, gsem, ssem, zsem):
    i = pl.program_id(0)
    nvalid = nvalid_ref[0]
    hbufs, ybufs = (hbuf0, hbuf1), (ybuf0, ybuf1)

    def slab(ref, row):
        return ref.at[pl.ds(pl.multiple_of(row * SLAB, SLAB), SLAB), :]

    def gather_copy(tile, sl, j):
        return pltpu.make_async_copy(slab(hp_hbm, src_ref[tile * TME + j]), slab(hbufs[sl], j), gsem.at[sl])

    def scatter_copy(tile, sl, j):
        return pltpu.make_async_copy(slab(ybufs[sl], j), slab(yt_hbm, dst_ref[(tile + 1) * TME + j]), ssem.at[sl])

    def for_rows(fn):
        def body(j, carry):
            fn(j)
            return carry

        lax.fori_loop(0, TME, body, 0, unroll=ROW_UNROLL)

    @pl.when(i == 0)
    def _():
        def unused(p, carry):
            src_ref[p] = NT - 1
            dst_ref[p + TME] = NT + ((p + TME) & (2 * TME - 1))
            return carry

        def dummy_tile(j, carry):
            dst_ref[j] = NT + j
            src_ref[nvalid * TME + j] = NT - 1
            return carry

        lax.fori_loop(0, TME, dummy_tile, 0, unroll=ROW_UNROLL)

        def class_tail(c, carry):
            lo = starts_ref[c] + counts_ref[c]
            hi = starts_ref[c] + ((counts_ref[c] + TME - 1) // TME) * TME
            lax.fori_loop(lo, hi, unused, 0)
            return carry

        lax.fori_loop(0, N_CLASSES, class_tail, 0)

        def place(t, carry):
            p = starts_ref[cls_ref[t]] + rank_ref[t]
            src_ref[p] = t
            dst_ref[p + TME] = t
            return carry

        lax.fori_loop(0, NT, place, 0, unroll=ROW_UNROLL)
        for sl in range(2):
            ybufs[sl][...] = jnp.zeros_like(ybufs[sl])
            spare = pltpu.make_async_copy(ybufs[sl], yt_hbm.at[pl.ds((NT + sl * TME) * SLAB, TME * SLAB), :], zsem)
            spare.start()
            spare.wait()
        for_rows(lambda j: gather_copy(0, 0, j).start())

    def tile_body(sl):
        for_rows(lambda j: gather_copy(0, sl, 0).wait())
        h = _slabs_to_rows(hbufs[sl], stage, TME).astype(BF16)
        for j in range(TME):
            gather_copy(i + 1, 1 - sl, j).start()
        for j in range(TME):
            scatter_copy(i - 1, 1 - sl, j).start(priority=1)
        y = None
        for g_ref, w1_ref, w3_ref, w2_ref in ((gwa_ref, w1a_ref, w3a_ref, w2a_ref),
                                              (gwb_ref, w1b_ref, w3b_ref, w2b_ref)):
            gate = jnp.dot(h, w1_ref[...], preferred_element_type=F32)
            up = jnp.dot(h, w3_ref[...], preferred_element_type=F32)
            gcol = _gate_column(g_ref, src_ref, i * TME)
            hid = gate * jax.nn.sigmoid(gate) * up * jnp.concatenate([gcol] * (D_EXPERT // 128), axis=1)
            yk = jnp.dot(hid.astype(BF16), w2_ref[...], preferred_element_type=F32)
            y = yk if y is None else y + yk

        @pl.when(i >= 1)
        def _():
            for_rows(lambda j: scatter_copy(0, sl, 0).wait())

        _rows_to_slabs(y, stage, ybufs[sl])

        @pl.when(i == nvalid - 1)
        def _():
            for_rows(lambda j: scatter_copy(i, sl, j).start())
            for_rows(lambda j: scatter_copy(0, 1 - sl, 0).wait())
            for_rows(lambda j: scatter_copy(0, sl, 0).wait())
            for_rows(lambda j: gather_copy(0, 1 - sl, 0).wait())

    for sl in range(2):
        @pl.when(jnp.logical_and(i < nvalid, i % 2 == sl))
        def _(sl=sl):
            tile_body(sl)


Y_ROWS = NT + 2 * TME


def _expert_call(tile_ea, tile_eb, nvalid, starts, counts, cls, rank, gwa, gwb, hp, w1b, w3b, w2b):
    up_a = pl.BlockSpec((None, D, D_EXPERT), lambda i, ea, eb, *_: (ea[i], 0, 0))
    up_b = pl.BlockSpec((None, D, D_EXPERT), lambda i, ea, eb, *_: (eb[i], 0, 0))
    down_a = pl.BlockSpec((None, D_EXPERT, D), lambda i, ea, eb, *_: (ea[i], 0, 0))
    down_b = pl.BlockSpec((None, D_EXPERT, D), lambda i, ea, eb, *_: (eb[i], 0, 0))
    return pl.pallas_call(
        _expert_kernel,
        out_shape=jax.ShapeDtypeStruct((Y_ROWS * SLAB, 128), BF16),
        grid_spec=pltpu.PrefetchScalarGridSpec(
            num_scalar_prefetch=9,
            grid=(P_TILES,),
            in_specs=[pl.BlockSpec(memory_space=pl.ANY),
                      up_a, up_a, down_a, up_b, up_b, down_b],
            out_specs=pl.BlockSpec(memory_space=pl.ANY),
            scratch_shapes=[pltpu.SMEM((P_ROWS + TME,), I32), pltpu.SMEM((P_ROWS + TME,), I32)]
            + [pltpu.VMEM((TME * SLAB, 128), BF16)] * 4
            + [pltpu.VMEM((TME * SLAB, 128), F32),
                            pltpu.SemaphoreType.DMA((2,)), pltpu.SemaphoreType.DMA((2,)),
                            pltpu.SemaphoreType.DMA]),
        compiler_params=_cparams(1),
        name="moe_experts",
    )(tile_ea, tile_eb, nvalid, starts, counts, cls, rank, gwa, gwb, hp, w1b, w3b, w2b, w1b, w3b, w2b)


def _combine_kernel(x1_ref, mod_ref, gf_ref, y_ref, o_ref, stage, *, final):
    x2 = x1_ref[...] + mod_ref[5:6, :] * _slabs_to_rows(y_ref, stage, TM)
    if final:
        ms = jnp.mean(x2 * x2, axis=-1, keepdims=True)
        x2 = x2 * lax.rsqrt(ms + EPS) * gf_ref[...]
    o_ref[...] = x2


def _combine_call(l, x1, mod, g_final, yt, final):
    n_tiles = LAT_TILES if final else N_TILES
    return pl.pallas_call(
        functools.partial(_combine_kernel, final=final),
        out_shape=jax.ShapeDtypeStruct((n_tiles * TM, D), F32),
        grid=(n_tiles,),
        in_specs=[pl.BlockSpec((TM, D), lambda i: (i, 0)),
                  pl.BlockSpec((None, None, 8, D), lambda i: (l, _mod_row(i), 0, 0)),
                  pl.BlockSpec((1, D), lambda i: (0, 0)),
                  pl.BlockSpec((TM * SLAB, 128), lambda i: (i, 0))],
        out_specs=pl.BlockSpec((TM, D), lambda i: (i, 0)),
        scratch_shapes=[pltpu.VMEM((TM * SLAB, 128), F32)],
        compiler_params=_cparams(1),
        name="moe_combine",
    )(x1, mod, g_final.reshape(1, D), yt)


def _slot_plan(counts):
    padded = ((counts + TME - 1) // TME) * TME
    ends = jnp.cumsum(padded)
    starts = ends - padded
    tile_start = jnp.arange(P_TILES, dtype=I32) * TME
    tile_cls = jnp.minimum(jnp.sum((tile_start[:, None] >= ends[None, :]).astype(I32), axis=1), N_CLASSES - 1)
    nvalid = (ends[-1] // TME).reshape(1)
    grp = tile_cls // 6
    tile_ea = grp * PER_GROUP + jnp.asarray(_SLOT_A)[tile_cls % 6]
    tile_eb = grp * PER_GROUP + jnp.asarray(_SLOT_B)[tile_cls % 6]
    return starts, tile_ea, tile_eb, nvalid


def _rope_tables():
    t = jnp.arange(SEQ, dtype=I32)
    row = (t // GRID_W).astype(F32)
    col = (t % GRID_W).astype(F32)
    half = HEAD_DIM // 4
    freqs = ROPE_BASE ** (-jnp.arange(0, 2 * half, 2, dtype=F32) / (2 * half))
    ang_r = row[:, None] * freqs
    ang_c = col[:, None] * freqs
    cos_h = jnp.concatenate([jnp.cos(ang_r), jnp.cos(ang_r), jnp.cos(ang_c), jnp.cos(ang_c)], axis=1)
    sin_h = jnp.concatenate([-jnp.sin(ang_r), jnp.sin(ang_r), -jnp.sin(ang_c), jnp.sin(ang_c)], axis=1)
    cos_t = jnp.concatenate([jnp.tile(cos_h, (1, 2)), jnp.ones((TM, 128), F32)], axis=0)
    sin_t = jnp.concatenate([jnp.tile(sin_h, (1, 2)), jnp.zeros((TM, 128), F32)], axis=0)
    return cos_t, sin_t


def kernel(x, c, ctx, c_ctx, w_mod, b_mod, g_norm1, g_norm2, w_in, w_out, attn_sink, sconv_w, cconv_w, cconv_b,
           cnorm_g, cnorm_b, w_route_group, b_route_group, w_route_expert, b_route_expert, w_gate_e, w_up_e,
           w_down_e, g_final):
    x_lat, x_ctx, ctx_base = x.reshape(N_LAT, D), ctx.reshape(N_CTX, D), 0
    c8 = jnp.zeros((8, D), F32).at[0:BATCH].set(c).at[BATCH].set(c_ctx)
    q_w, k_w, v_w = w_in[:, :, 0:512], w_in[:, :, 512:640], w_in[:, :, 640:768]
    swap = lambda w: jnp.concatenate([w[:, :, HEAD_DIM:], w[:, :, :HEAD_DIM]], axis=-1)
    wqkv = jnp.concatenate([q_w, k_w, v_w, swap(k_w), swap(v_w)], axis=-1).astype(BF16)
    wf = w_in[:, :, 768:1280].astype(BF16)
    wcv = w_in[:, :, 1280:].astype(BF16)
    w_out_b = w_out.astype(BF16)
    w_route = jnp.concatenate([w_route_expert, w_route_group,
                               jnp.zeros((DEPTH, D, 128 - N_EXPERTS - N_GROUPS), F32)], axis=2)
    wr_hi = w_route.astype(BF16)
    wr_lo = (w_route - wr_hi.astype(F32)).astype(BF16)
    b_route = jnp.concatenate([b_route_expert, b_route_group,
                               jnp.zeros((DEPTH, 128 - N_EXPERTS - N_GROUPS), F32)], axis=1).reshape(DEPTH, 1, 128)
    cos_t, sin_t = _rope_tables()

    mod = _mod_call(c8, w_mod, b_mod)
    mod = mod.reshape(DEPTH, 8, N_MOD, D)[:, :3]
    mod = jnp.pad(mod, ((0, 0), (0, 0), (0, 8 - N_MOD), (0, 0)))

    out = None
    for l in range(DEPTH):
        qkv, f, cv, w1b, w3b = _in_call(l, x_lat, x_ctx, ctx_base, mod, g_norm1, wqkv, wf, wcv, cos_t, sin_t,
                                        w_gate_e, w_up_e)
        o_a = _attn_call(l, attn_sink, qkv)
        o_b_lat = _fft_call(f, 64, 0, BATCH)
        o_b_ctx = _fft_call(f, 16, N_LAT // CTX, BATCH)
        o_c, o_d = _conv_call(l, cv, sconv_w, cconv_w, cconv_b, cnorm_g, cnorm_b)
        x1, hp, cls8, gw8, cnt, w2b = _out_call(l, o_a, o_b_lat, o_b_ctx, o_c, o_d, w_out_b, x_lat, x_ctx, ctx_base,
                                                mod, g_norm2, wr_hi, wr_lo, b_route, w_down_e)
        counts = cnt[:N_CLASSES, 0]
        starts, tile_ea, tile_eb, nvalid = _slot_plan(counts)
        yt = _expert_call(tile_ea, tile_eb, nvalid, starts, counts, cls8[0], cls8[1], gw8[0], gw8[1], hp, w1b.reshape(N_EXPERTS, D, D_EXPERT),
                          w3b.reshape(N_EXPERTS, D, D_EXPERT), w2b.reshape(N_EXPERTS, D_EXPERT, D))
        final = l == DEPTH - 1
        res = _combine_call(l, x1, mod, g_final, yt, final)
        if final:
            out = res
        else:
            x_lat, x_ctx, ctx_base = res, res, LAT_TILES
    return out.reshape(BATCH, SEQ, D)
```

```python
import functools

import numpy as np
import jax
import jax.numpy as jnp
from jax import lax
from jax.experimental import pallas as pl
from jax.experimental.pallas import tpu as pltpu

F32 = jnp.float32
BF16 = jnp.bfloat16
I32 = jnp.int32

D = 2048
BATCH = 2
SEQ = 4096
CTX = 256
DEPTH = 2
GRID_W = 64
N_LAT = BATCH * SEQ
N_CTX = BATCH * CTX
NT = N_LAT + N_CTX
GW = 512
HEAD_DIM = 64
N_HEADS = 8
WINDOW = 128
ROPE_BASE = 10000.0
N_MOD = 6
EPS = 1e-6
NEG_INF = -1e30
N_GROUPS = 4
PER_GROUP = 4
N_EXPERTS = 16
D_EXPERT = 512
N_CLASSES = N_GROUPS * 6
CONF_W = 31
CONF_PAD = 15

TM = 256
N_TILES = NT // TM
LAT_TILES = N_LAT // TM
SEQ_TILES = SEQ // TM
TQ = 128
HALO = 16
TME = 128
P_ROWS = NT + N_CLASSES * TME
P_TILES = P_ROWS // TME
SLAB = D // 128
VMEM_LIMIT = 56 * 1024 * 1024


def _cparams(n_axes=1, **kw):
    return pltpu.CompilerParams(dimension_semantics=("arbitrary",) * n_axes,
                                vmem_limit_bytes=VMEM_LIMIT, **kw)


def _split_bf16(v):
    hi = v.astype(BF16)
    lo = (v - hi.astype(F32)).astype(BF16)
    return hi, lo


def _mod_row(i):
    return jnp.minimum(i // SEQ_TILES, 2)


MOD_TN = 1536


def _mod_kernel(c_ref, w_ref, b_ref, o_ref):
    c = c_ref[...]
    s = c * jax.nn.sigmoid(c)
    s_hi, s_lo = _split_bf16(s)
    w_hi, w_lo = _split_bf16(w_ref[...])
    acc = jnp.dot(s_hi, w_hi, preferred_element_type=F32)
    acc = acc + jnp.dot(s_lo, w_hi, preferred_element_type=F32)
    acc = acc + jnp.dot(s_hi, w_lo, preferred_element_type=F32)
    o_ref[...] = acc + b_ref[...]


def _mod_call(c8, w_mod, b_mod):
    n = N_MOD * D
    return pl.pallas_call(
        _mod_kernel,
        out_shape=jax.ShapeDtypeStruct((DEPTH, 8, n), F32),
        grid=(DEPTH, n // MOD_TN),
        in_specs=[
            pl.BlockSpec((8, D), lambda l, j: (0, 0)),
            pl.BlockSpec((None, D, MOD_TN), lambda l, j: (l, 0, j)),
            pl.BlockSpec((None, 1, MOD_TN), lambda l, j: (l, 0, j)),
        ],
        out_specs=pl.BlockSpec((None, 8, MOD_TN), lambda l, j: (l, 0, j)),
        compiler_params=_cparams(2),
        name="adaln_mod",
    )(c8, w_mod, b_mod.reshape(DEPTH, 1, n))


QKV_W = 1024
CV_IN = 5 * GW
CV_OUT = 3 * GW


def _rmsnorm_mod(x, g, shift, scale):
    ms = jnp.mean(x * x, axis=-1, keepdims=True)
    y = x * lax.rsqrt(ms + EPS) * g
    return y * (1.0 + scale) + shift


def _stream_specs(ctx_base):
    return [pl.BlockSpec((TM, D), lambda i, *_: (jnp.minimum(i, LAT_TILES - 1), 0)),
            pl.BlockSpec((TM, D), lambda i, *_: (ctx_base + jnp.maximum(i - LAT_TILES, 0), 0))]


def _stream_tile(xl_ref, xc_ref):
    return jnp.where(pl.program_id(0) < LAT_TILES, xl_ref[...], xc_ref[...])


CAST_STEPS = 32


def _cast_spec(rows, cols, l):
    return pl.BlockSpec((rows, cols), lambda i, *_: (l * CAST_STEPS + jnp.minimum(i, CAST_STEPS - 1), 0))


def _cast_out_spec(rows, cols):
    return pl.BlockSpec((rows, cols), lambda i, *_: (jnp.minimum(i, CAST_STEPS - 1), 0))


def _in_kernel(xl_ref, xc_ref, mod_ref, g_ref, wqkv_ref, wf_ref, wcv_ref, cos_ref, sin_ref, w1_ref, w3_ref,
               qkv_ref, f_ref, cv_ref, w1b_ref, w3b_ref):
    w1b_ref[...] = w1_ref[...].astype(BF16)
    w3b_ref[...] = w3_ref[...].astype(BF16)
    h = _rmsnorm_mod(_stream_tile(xl_ref, xc_ref), g_ref[...], mod_ref[0:1, :], mod_ref[1:2, :])
    hb = h.astype(BF16)
    qkv = jnp.dot(hb, wqkv_ref[...], preferred_element_type=F32)
    cos = cos_ref[...]
    sin = sin_ref[...]
    lane = lax.broadcasted_iota(I32, (TM, 128), 1)
    first_half = (lane % 32) < 16

    def rope(t):
        rot = jnp.where(first_half, pltpu.roll(t, 112, 1), pltpu.roll(t, 16, 1))
        return t * cos + rot * sin

    scale = HEAD_DIM ** -0.5
    for c in range(4):
        qkv_ref[:, c * 128:(c + 1) * 128] = (rope(qkv[:, c * 128:(c + 1) * 128]) * scale).astype(BF16)
    qkv_ref[:, 512:640] = rope(qkv[:, 512:640]).astype(BF16)
    qkv_ref[:, 640:768] = qkv[:, 640:768].astype(BF16)
    qkv_ref[:, 768:896] = rope(qkv[:, 768:896]).astype(BF16)
    qkv_ref[:, 896:1024] = qkv[:, 896:1024].astype(BF16)

    f_ref[...] = jnp.dot(hb, wf_ref[...], preferred_element_type=F32).astype(BF16)

    cv = jnp.dot(hb, wcv_ref[...], preferred_element_type=F32)
    cv_ref[:, 0:GW] = cv[:, 0:GW]
    cv_ref[:, GW:2 * GW] = cv[:, GW:2 * GW] * cv[:, 2 * GW:3 * GW]
    cv_ref[:, 2 * GW:3 * GW] = cv[:, 3 * GW:4 * GW] * jax.nn.sigmoid(cv[:, 4 * GW:5 * GW])


W13_ROWS = N_EXPERTS * D // CAST_STEPS
W2_ROWS = N_EXPERTS * D_EXPERT // CAST_STEPS


def _in_call(l, x_lat, x_ctx, ctx_base, mod, g_norm1, wqkv, wf, wcv, cos_t, sin_t, w_gate_e, w_up_e):
    const = dict(pipeline_mode=pl.Buffered(1))
    flat = lambda w: w.reshape(DEPTH * N_EXPERTS * D, D_EXPERT)
    return pl.pallas_call(
        _in_kernel,
        out_shape=(jax.ShapeDtypeStruct((NT, QKV_W), BF16),
                   jax.ShapeDtypeStruct((NT, GW), BF16),
                   jax.ShapeDtypeStruct((NT, CV_OUT), F32),
                   jax.ShapeDtypeStruct((N_EXPERTS * D, D_EXPERT), BF16),
                   jax.ShapeDtypeStruct((N_EXPERTS * D, D_EXPERT), BF16)),
        grid=(N_TILES,),
        in_specs=_stream_specs(ctx_base) + [
            pl.BlockSpec((None, None, 8, D), lambda i: (l, _mod_row(i), 0, 0)),
            pl.BlockSpec((None, 1, D), lambda i: (l, 0, 0)),
            pl.BlockSpec((None, D, QKV_W), lambda i: (l, 0, 0), **const),
            pl.BlockSpec((None, D, GW), lambda i: (l, 0, 0), **const),
            pl.BlockSpec((None, D, CV_IN), lambda i: (l, 0, 0), **const),
            pl.BlockSpec((TM, 128), lambda i: (jnp.where(i < LAT_TILES, i % SEQ_TILES, SEQ_TILES), 0)),
            pl.BlockSpec((TM, 128), lambda i: (jnp.where(i < LAT_TILES, i % SEQ_TILES, SEQ_TILES), 0)),
            _cast_spec(W13_ROWS, D_EXPERT, l), _cast_spec(W13_ROWS, D_EXPERT, l),
        ],
        out_specs=(pl.BlockSpec((TM, QKV_W), lambda i: (i, 0)),
                   pl.BlockSpec((TM, GW), lambda i: (i, 0)),
                   pl.BlockSpec((TM, CV_OUT), lambda i: (i, 0)),
                   _cast_out_spec(W13_ROWS, D_EXPERT), _cast_out_spec(W13_ROWS, D_EXPERT)),
        compiler_params=_cparams(1),
        name="norm_in_proj",
    )(x_lat, x_ctx, mod, g_norm1.reshape(DEPTH, 1, D), wqkv, wf, wcv, cos_t, sin_t, flat(w_gate_e), flat(w_up_e))


N_QBLK_LAT = N_LAT // TQ
N_QBLK = NT // TQ
QBLK_PER_SEQ = SEQ // TQ
KWIN = 3 * TQ


def _attn_heads(sink_ref, l, q_ref, keys, vals, bias, o_ref):
    lane = lax.broadcasted_iota(I32, (TQ, 128), 1)
    row2 = lax.broadcasted_iota(I32, (2 * TQ, 1), 0)
    zero = jnp.zeros((TQ, 128), BF16)
    if bias is not None:
        bias = jnp.concatenate([bias, bias], axis=0)
    for hk in range(2):
        c0, c1 = 2 * hk, 2 * hk + 1
        for p in range(2):
            kx = keys[0] if hk == p else keys[1]
            vx = vals[0] if hk == p else vals[1]
            in_half = (lane // HEAD_DIM) == p
            qs = jnp.concatenate([jnp.where(in_half, q_ref[:, c0 * 128:(c0 + 1) * 128], zero),
                                  jnp.where(in_half, q_ref[:, c1 * 128:(c1 + 1) * 128], zero)], axis=0)
            s = lax.dot_general(qs, kx, (((1,), (1,)), ((), ())), preferred_element_type=F32)
            if bias is not None:
                s = s + bias
            sink = jnp.where(row2 < TQ, sink_ref[l, 2 * c0 + p], sink_ref[l, 2 * c1 + p])
            m = jnp.maximum(jnp.max(s, axis=-1, keepdims=True), sink)
            e = jnp.exp(s - m)
            den = jnp.sum(e, axis=-1, keepdims=True) + jnp.exp(sink - m)
            o = (jnp.dot(e.astype(BF16), vx, preferred_element_type=F32) / den).astype(BF16)
            for k, c in enumerate((c0, c1)):
                lo = c * 128 + p * HEAD_DIM
                o_ref[:, lo:lo + HEAD_DIM] = o[k * TQ:(k + 1) * TQ, p * HEAD_DIM:(p + 1) * HEAD_DIM]


def _attn_kernel(sink_ref, q_ref, k_ref, v_ref, ks_ref, vs_ref, kc_ref, vc_ref, kcs_ref, vcs_ref, o_ref, *, l):
    i = pl.program_id(0)

    @pl.when(i < N_QBLK_LAT)
    def _latent():
        n = i % QBLK_PER_SEQ
        ws = pl.multiple_of(jnp.clip((n - 1) * TQ, 0, SEQ - KWIN), TQ)
        keys = [jnp.concatenate([r[pl.ds(ws, KWIN), :], c[...]], axis=0)
                for r, c in ((k_ref, kc_ref), (ks_ref, kcs_ref))]
        vals = [jnp.concatenate([r[pl.ds(ws, KWIN), :], c[...]], axis=0)
                for r, c in ((v_ref, vc_ref), (vs_ref, vcs_ref))]
        nk = KWIN + CTX
        qpos = n * TQ + lax.broadcasted_iota(I32, (TQ, nk), 0)
        col = lax.broadcasted_iota(I32, (TQ, nk), 1)
        valid = (jnp.abs(ws + col - qpos) <= WINDOW) | (col >= KWIN)
        _attn_heads(sink_ref, l, q_ref, keys, vals, jnp.where(valid, 0.0, NEG_INF), o_ref)

    @pl.when(i >= N_QBLK_LAT)
    def _context():
        _attn_heads(sink_ref, l, q_ref, [kc_ref[...], kcs_ref[...]], [vc_ref[...], vcs_ref[...]], None, o_ref)


def _attn_call(l, attn_sink, qkv):
    def seq_blk(i):
        return jnp.minimum(i // QBLK_PER_SEQ, 1)

    def ctx_blk(i):
        b = jnp.where(i < N_QBLK_LAT, i // QBLK_PER_SEQ, (i - N_QBLK_LAT) // (CTX // TQ))
        return N_LAT // CTX + b

    seq_specs = [pl.BlockSpec((SEQ, 128), functools.partial(lambda i, col: (seq_blk(i), col), col=col))
                 for col in (4, 5, 6, 7)]
    ctx_specs = [pl.BlockSpec((CTX, 128), functools.partial(lambda i, col: (ctx_blk(i), col), col=col))
                 for col in (4, 5, 6, 7)]
    return pl.pallas_call(
        functools.partial(_attn_kernel, l=l),
        out_shape=jax.ShapeDtypeStruct((NT, GW), BF16),
        grid=(N_QBLK,),
        in_specs=[pl.BlockSpec(memory_space=pltpu.SMEM),
                  pl.BlockSpec((TQ, GW), lambda i: (i, 0))] + seq_specs + ctx_specs,
        out_specs=pl.BlockSpec((TQ, GW), lambda i: (i, 0)),
        compiler_params=_cparams(1),
        name="window_attention",
    )(attn_sink, qkv, *([qkv] * 8))


FC = 256
FFT_UNROLL = 4


def _dft_tables(n):
    t = n * n
    j = np.arange(n)
    ang = 2.0 * np.pi * np.outer(j, j) / n
    cn, sn = np.cos(ang), np.sin(ang)
    m1 = np.block([[cn, sn], [-sn, cn]])
    m2 = np.concatenate([cn, sn], axis=1)
    angc = 2.0 * np.pi * np.outer(np.arange(128), np.arange(128)) / 128.0
    wc = np.concatenate([np.cos(angc), -np.sin(angc)], axis=1)
    angt = 2.0 * np.pi * np.outer(j, j) / t
    tw_re = np.broadcast_to(np.cos(angt)[:, :, None], (n, n, 128))
    tw_im = np.broadcast_to(-np.sin(angt)[:, :, None], (n, n, 128))
    def hi_lo(m):
        return jnp.stack(_split_bf16(jnp.asarray(m, F32)))

    return (hi_lo(m1), hi_lo(m2), hi_lo(wc),
            jnp.asarray(np.ascontiguousarray(tw_re), F32), jnp.asarray(np.ascontiguousarray(tw_im), F32))


def _fft_kernel(f_ref, m1_ref, m2_ref, wc_ref, twr_ref, twi_ref, o_ref, are, aim, hre, him, *, n):
    t = n * n
    ng = FC // 128
    for g in range(ng):
        fg = f_ref[:, g * 128:(g + 1) * 128]
        a = (jnp.dot(fg, wc_ref[0], preferred_element_type=F32)
             + jnp.dot(fg, wc_ref[1], preferred_element_type=F32))
        are[g] = a[:, :128]
        aim[g] = a[:, 128:]

    def table_dot(m_ref, v):
        return (jnp.dot(m_ref[0], v, preferred_element_type=F32)
                + jnp.dot(m_ref[1], v, preferred_element_type=F32))

    def gather_rows(re, im, first):
        rows = [pl.ds(first + u, n, stride=n) for u in range(FFT_UNROLL)]
        top = jnp.concatenate([re[g, r, :] for r in rows for g in range(ng)], axis=1)
        bot = jnp.concatenate([im[g, r, :] for r in rows for g in range(ng)], axis=1)
        return jnp.concatenate([top, bot], axis=0).astype(BF16)

    def stage1(it, carry):
        t1 = it * FFT_UNROLL
        h = table_dot(m1_ref, gather_rows(are, aim, t1))
        for u in range(FFT_UNROLL):
            dst = pl.ds(pl.multiple_of((t1 + u) * n, n), n)
            twr = twr_ref[t1 + u]
            twi = twi_ref[t1 + u]
            for g in range(ng):
                lanes = slice((u * ng + g) * 128, (u * ng + g + 1) * 128)
                h_re, h_im = h[:n, lanes], h[n:, lanes]
                hre[g, dst, :] = h_re * twr - h_im * twi
                him[g, dst, :] = h_re * twi + h_im * twr
        return carry

    lax.fori_loop(0, n // FFT_UNROLL, stage1, 0)

    def stage2(it, carry):
        k2 = it * FFT_UNROLL
        z = table_dot(m2_ref, gather_rows(hre, him, k2))
        z = z * ((t * 128.0) ** -0.5)
        for u in range(FFT_UNROLL):
            for g in range(ng):
                are[g, pl.ds(k2 + u, n, stride=n), :] = z[:, (u * ng + g) * 128:(u * ng + g + 1) * 128]
        return carry

    lax.fori_loop(0, n // FFT_UNROLL, stage2, 0)
    for g in range(ng):
        o_ref[:, g * 128:(g + 1) * 128] = are[g].astype(BF16)


def _fft_call(f, n, first_blk, n_seq):
    t = n * n
    m1, m2, wc, twr, twi = _dft_tables(n)
    full = lambda shape: pl.BlockSpec(shape, lambda b, h: (0,) * len(shape))
    return pl.pallas_call(
        functools.partial(_fft_kernel, n=n),
        out_shape=jax.ShapeDtypeStruct((n_seq * t, GW), BF16),
        grid=(n_seq, GW // FC),
        in_specs=[pl.BlockSpec((t, FC), lambda b, h: (first_blk + b, h)),
                  full((2, 2 * n, 2 * n)), full((2, n, 2 * n)), full((2, 128, 256)),
                  full((n, n, 128)), full((n, n, 128))],
        out_specs=pl.BlockSpec((t, FC), lambda b, h: (b, h)),
        scratch_shapes=[pltpu.VMEM((FC // 128, t, 128), F32)] * 4,
        compiler_params=_cparams(2),
        name="fourier_mix_%d" % t,
    )(f, m1, m2, wc, twr, twi)


CONV_ROWS = 32
GSH_ROWS = TM + 2 * HALO - 8


def _conv_kernel(sb_ref, p_ref, pp_ref, pn_ref, g_ref, gp_ref, gn_ref, sw_ref, cw_ref, cb_ref, ng_ref, nb_ref,
                 oc_ref, od_ref, pbuf, gbuf, gsh):
    i = pl.program_id(0)
    seq_start = jnp.logical_or(i % SEQ_TILES == 0, i >= LAT_TILES)
    seq_end = jnp.logical_or(i % SEQ_TILES == SEQ_TILES - 1, i >= LAT_TILES)
    for buf, main, prev, nxt in ((pbuf, p_ref, pp_ref, pn_ref), (gbuf, g_ref, gp_ref, gn_ref)):
        buf[0:HALO, :] = jnp.where(seq_start, 0.0, prev[...])
        buf[HALO:HALO + TM, :] = main[...]
        buf[HALO + TM:HALO + TM + HALO, :] = jnp.where(seq_end, 0.0, nxt[...])
    for s in range(8):
        gsh[s] = gbuf[s:s + GSH_ROWS, :]
    sw = sw_ref[...]
    cw = cw_ref[...]
    for r0 in range(0, TM, CONV_ROWS):
        base = HALO + r0
        acc = sw[0:1, :] * pbuf[base - 1:base - 1 + CONV_ROWS, :]
        acc = acc + sw[1:2, :] * pbuf[base:base + CONV_ROWS, :]
        acc = acc + sw[2:3, :] * pbuf[base + 1:base + 1 + CONV_ROWS, :]
        oc_ref[r0:r0 + CONV_ROWS, :] = (sb_ref[r0:r0 + CONV_ROWS, :] * acc).astype(BF16)

        z = None
        for j in range(CONF_W):
            o = base - CONF_PAD + j
            q = o - o % 8
            term = cw[j:j + 1, :] * gsh[o % 8, q:q + CONV_ROWS, :]
            z = term if z is None else z + term
        z = z + cb_ref[...]
        mu = jnp.mean(z, axis=-1, keepdims=True)
        zc = z - mu
        var = jnp.mean(zc * zc, axis=-1, keepdims=True)
        y = zc * lax.rsqrt(var + EPS) * ng_ref[...] + nb_ref[...]
        od_ref[r0:r0 + CONV_ROWS, :] = (y * jax.nn.sigmoid(y)).astype(BF16)


def _conv_call(l, cv, sconv_w, cconv_w, cconv_b, cnorm_g, cnorm_b):
    hb = TM // HALO
    last = NT // HALO - 1

    def halo_specs(col):
        return [pl.BlockSpec((TM, GW), lambda i: (i, col)),
                pl.BlockSpec((HALO, GW), lambda i: (jnp.maximum(i * hb - 1, 0), col)),
                pl.BlockSpec((HALO, GW), lambda i: (jnp.minimum((i + 1) * hb, last), col))]

    vec = lambda: pl.BlockSpec((None, 1, GW), lambda i: (l, 0, 0))
    cw = jnp.pad(cconv_w, ((0, 0), (0, 32 - CONF_W), (0, 0)))
    sw = jnp.pad(sconv_w, ((0, 0), (0, 8 - 3), (0, 0)))
    return pl.pallas_call(
        _conv_kernel,
        out_shape=(jax.ShapeDtypeStruct((NT, GW), BF16), jax.ShapeDtypeStruct((NT, GW), BF16)),
        grid=(N_TILES,),
        in_specs=[pl.BlockSpec((TM, GW), lambda i: (i, 0))] + halo_specs(1) + halo_specs(2) + [
            pl.BlockSpec((None, 8, GW), lambda i: (l, 0, 0)),
            pl.BlockSpec((None, 32, GW), lambda i: (l, 0, 0)),
            vec(), vec(), vec()],
        out_specs=(pl.BlockSpec((TM, GW), lambda i: (i, 0)), pl.BlockSpec((TM, GW), lambda i: (i, 0))),
        scratch_shapes=[pltpu.VMEM((TM + 2 * HALO, GW), F32)] * 2 + [pltpu.VMEM((8, GSH_ROWS, GW), F32)],
        compiler_params=_cparams(1),
        name="conv_mixers",
    )(cv, cv, cv, cv, cv, cv, cv, sw, cw, cconv_b.reshape(DEPTH, 1, GW), cnorm_g.reshape(DEPTH, 1, GW),
      cnorm_b.reshape(DEPTH, 1, GW))


R_ROWS = 32
_SLOT_A = np.array([0, 2, 2, 0, 0, 1], np.int32)
_SLOT_B = np.array([1, 1, 3, 3, 2, 3], np.int32)


def _route(lg):
    gl = [lg[N_EXPERTS + g:N_EXPERTS + g + 1, :] for g in range(N_GROUPS)]
    gmax = functools.reduce(jnp.maximum, gl)
    pg = 1.0 / functools.reduce(jnp.add, [jnp.exp(v - gmax) for v in gl])
    gsel = jnp.full_like(gmax, N_GROUPS - 1).astype(I32)
    for g in range(N_GROUPS - 2, -1, -1):
        gsel = jnp.where(gl[g] == gmax, g, gsel)
    el = []
    for j in range(PER_GROUP):
        v = lg[j:j + 1, :]
        for g in range(1, N_GROUPS):
            v = jnp.where(gsel == g, lg[g * PER_GROUP + j:g * PER_GROUP + j + 1, :], v)
        el.append(v)
    v1 = functools.reduce(jnp.maximum, el)
    i1 = jnp.full_like(gsel, PER_GROUP - 1)
    for j in range(PER_GROUP - 2, -1, -1):
        i1 = jnp.where(el[j] == v1, j, i1)
    rest = [jnp.where(i1 == j, -jnp.inf, el[j]) for j in range(PER_GROUP)]
    v2 = functools.reduce(jnp.maximum, rest)
    i2 = jnp.full_like(gsel, PER_GROUP - 1)
    for j in range(PER_GROUP - 2, -1, -1):
        i2 = jnp.where(jnp.logical_and(rest[j] == v2, i1 != j), j, i2)
    e2 = jnp.exp(v2 - v1)
    w1 = pg / (1.0 + e2)
    w2 = pg * e2 / (1.0 + e2)
    lo = jnp.minimum(i1, i2)
    hi = jnp.maximum(i1, i2)
    w_lo = jnp.where(i1 < i2, w1, w2)
    w_hi = jnp.where(i1 < i2, w2, w1)
    pair = jnp.where(lo == 0, jnp.where(hi == 1, 0, jnp.where(hi == 2, 4, 3)),
                     jnp.where(lo == 1, jnp.where(hi == 2, 1, 5), 2))
    cls = gsel * 6 + pair
    a_is_hi = pair == 1
    return cls, jnp.where(a_is_hi, w_hi, w_lo), jnp.where(a_is_hi, w_lo, w_hi)


def _rows_to_slabs(v, stage_ref, out_ref):
    n = v.shape[0]
    for c in range(SLAB):
        stage_ref[pl.ds(c, n, stride=SLAB), :] = v[:, c * 128:(c + 1) * 128]
    out_ref[...] = stage_ref[...].astype(out_ref.dtype)


def _slabs_to_rows(src_ref, stage_ref, n):
    stage_ref[...] = src_ref[...].astype(F32)
    return jnp.concatenate([stage_ref[pl.ds(c, n, stride=SLAB), :] for c in range(SLAB)], axis=1)


def _out_kernel(oa_ref, obl_ref, obc_ref, oc_ref, od_ref, w_ref, xl_ref, xc_ref, mod_ref, g_ref, wrh_ref, wrl_ref,
                br_ref, tri_ref, w2_ref, x1_ref, hp_ref, cls_ref, gw_ref, cnt_ref, w2b_ref, carry, stage):
    i = pl.program_id(0)
    w2b_ref[...] = w2_ref[...].astype(BF16)

    @pl.when(i == 0)
    def _():
        carry[...] = jnp.zeros_like(carry)

    ob = jnp.where(i < LAT_TILES, obl_ref[...], obc_ref[...])
    y = jnp.dot(oa_ref[...], w_ref[0:GW, :], preferred_element_type=F32)
    y = y + jnp.dot(ob, w_ref[GW:2 * GW, :], preferred_element_type=F32)
    y = y + jnp.dot(oc_ref[...], w_ref[2 * GW:3 * GW, :], preferred_element_type=F32)
    y = y + jnp.dot(od_ref[...], w_ref[3 * GW:4 * GW, :], preferred_element_type=F32)
    x1 = _stream_tile(xl_ref, xc_ref) + mod_ref[2:3, :] * y
    x1_ref[...] = x1
    h2 = _rmsnorm_mod(x1, g_ref[...], mod_ref[3:4, :], mod_ref[4:5, :])
    _rows_to_slabs(h2, stage, hp_ref)

    h_hi, h_lo = _split_bf16(h2)
    lg = jnp.dot(h_hi, wrh_ref[...], preferred_element_type=F32)
    lg = lg + jnp.dot(h_lo, wrh_ref[...], preferred_element_type=F32)
    lg = lg + jnp.dot(h_hi, wrl_ref[...], preferred_element_type=F32)
    lg = jnp.transpose(lg + br_ref[...])[0:R_ROWS, :]
    cls, w_a, w_b = _route(lg)

    onehot = jnp.where(lax.broadcasted_iota(I32, (R_ROWS, TM), 0) == cls, 1.0, 0.0)
    prefix = jnp.dot(onehot.astype(BF16), tri_ref[...], preferred_element_type=F32)
    rank = jnp.sum(onehot * (prefix - 1.0 + carry[:, 0:1]), axis=0, keepdims=True)
    total = carry[...] + jnp.sum(onehot, axis=1, keepdims=True)
    carry[...] = total
    cnt_ref[...] = total.astype(I32)
    cls_ref[...] = jnp.concatenate([cls, rank.astype(I32), jnp.zeros((6, TM), I32)], axis=0)
    gw_ref[...] = jnp.concatenate([w_a, w_b, jnp.zeros((6, TM), F32)], axis=0)


def _out_call(l, o_a, o_b_lat, o_b_ctx, o_c, o_d, w_out, x_lat, x_ctx, ctx_base, mod, g_norm2, wr_hi, wr_lo,
              b_route, w_down_e):
    row = lambda: pl.BlockSpec((TM, GW), lambda i: (i, 0))
    tri = jnp.asarray(np.triu(np.ones((TM, TM), np.float32)), BF16)
    return pl.pallas_call(
        _out_kernel,
        out_shape=(jax.ShapeDtypeStruct((NT, D), F32), jax.ShapeDtypeStruct((NT * SLAB, 128), BF16),
                   jax.ShapeDtypeStruct((8, NT), I32), jax.ShapeDtypeStruct((8, NT), F32),
                   jax.ShapeDtypeStruct((R_ROWS, 128), I32),
                   jax.ShapeDtypeStruct((N_EXPERTS * D_EXPERT, D), BF16)),
        grid=(N_TILES,),
        in_specs=[row(),
                  pl.BlockSpec((TM, GW), lambda i: (jnp.minimum(i, LAT_TILES - 1), 0)),
                  pl.BlockSpec((TM, GW), lambda i: (jnp.maximum(i - LAT_TILES, 0), 0)),
                  row(), row(),
                  pl.BlockSpec((None, D, D), lambda i: (l, 0, 0), pipeline_mode=pl.Buffered(1))]
        + _stream_specs(ctx_base) + [
                  pl.BlockSpec((None, None, 8, D), lambda i: (l, _mod_row(i), 0, 0)),
                  pl.BlockSpec((None, 1, D), lambda i: (l, 0, 0)),
                  pl.BlockSpec((None, D, 128), lambda i: (l, 0, 0)),
                  pl.BlockSpec((None, D, 128), lambda i: (l, 0, 0)),
                  pl.BlockSpec((None, 1, 128), lambda i: (l, 0, 0)),
                  pl.BlockSpec((TM, TM), lambda i: (0, 0)),
                  _cast_spec(W2_ROWS, D, l)],
        out_specs=(pl.BlockSpec((TM, D), lambda i: (i, 0)),
                   pl.BlockSpec((TM * SLAB, 128), lambda i: (i, 0)),
                   pl.BlockSpec((8, TM), lambda i: (0, i)), pl.BlockSpec((8, TM), lambda i: (0, i)),
                   pl.BlockSpec((R_ROWS, 128), lambda i: (0, 0)),
                   _cast_out_spec(W2_ROWS, D)),
        scratch_shapes=[pltpu.VMEM((R_ROWS, 128), F32), pltpu.VMEM((TM * SLAB, 128), F32)],
        compiler_params=_cparams(1),
        name="out_proj_route",
    )(o_a, o_b_lat, o_b_ctx, o_c, o_d, w_out, x_lat, x_ctx, mod, g_norm2.reshape(DEPTH, 1, D), wr_hi, wr_lo,
      b_route, tri, w_down_e.reshape(DEPTH * N_EXPERTS * D_EXPERT, D))


ROW_UNROLL = 8


def _gate_column(g_ref, src_ref, base):
    sub = lax.broadcasted_iota(I32, (8, 128), 0)
    pieces = []
    for g in range(TME // 8):
        v = jnp.zeros((8, 128), F32)
        for k in range(8):
            v = jnp.where(sub == k, g_ref[src_ref[base + 8 * g + k]], v)
        pieces.append(v)
    return jnp.concatenate(pieces, axis=0)


DMA_PIECES = 16


def _expert_kernel(ea_ref, eb_ref, nvalid_ref, zero_ref, starts_ref, counts_ref, cls_ref, rank_ref, gwa_ref, gwb_ref, hp_hbm, w1a_ref, w3a_ref, w2a_ref, w1b_ref,
                   w3b_ref, w2b_ref, yt_hbm, src_ref, dst_ref, hbuf0, hbuf1, ybuf0, ybuf1, stage, gsem, ssem, zsem):
    i = pl.program_id(0)
    nvalid = nvalid_ref[0]
    hbufs, ybufs = (hbuf0, hbuf1), (ybuf0, ybuf1)

    def slab(ref, row):
        return ref.at[pl.ds(pl.multiple_of(row * SLAB, SLAB), SLAB), :]

    def gather_copy(tile, sl, j, z=0):
        return pltpu.make_async_copy(slab(hp_hbm, src_ref[tile * TME + j] + z), slab(hbufs[sl], j), gsem.at[sl])

    def scatter_copy(tile, sl, j, z=0):
        return pltpu.make_async_copy(slab(ybufs[sl], j), slab(yt_hbm, dst_ref[(tile + 1) * TME + j] + z),
                                     ssem.at[sl])

    def for_rows(fn):
        def body(j, carry):
            fn(j)
            return carry

        lax.fori_loop(0, TME, body, 0, unroll=ROW_UNROLL)

    @pl.when(i == 0)
    def _():
        def unused(p, carry):
            src_ref[p] = NT - 1
            dst_ref[p + TME] = NT + ((p + TME) & (2 * TME - 1))
            return carry

        def dummy_tile(j, carry):
            dst_ref[j] = NT + j
            src_ref[nvalid * TME + j] = NT - 1
            return carry

        lax.fori_loop(0, TME, dummy_tile, 0, unroll=ROW_UNROLL)

        def class_tail(c, carry):
            lo = starts_ref[c] + counts_ref[c]
            hi = starts_ref[c] + ((counts_ref[c] + TME - 1) // TME) * TME
            lax.fori_loop(lo, hi, unused, 0)
            return carry

        lax.fori_loop(0, N_CLASSES, class_tail, 0)

        def place(t, carry):
            p = starts_ref[cls_ref[t]] + rank_ref[t]
            src_ref[p] = t
            dst_ref[p + TME] = t
            return carry

        lax.fori_loop(0, NT, place, 0, unroll=ROW_UNROLL)
        for sl in range(2):
            ybufs[sl][...] = jnp.zeros_like(ybufs[sl])
            spare = pltpu.make_async_copy(ybufs[sl], yt_hbm.at[pl.ds((NT + sl * TME) * SLAB, TME * SLAB), :], zsem)
            spare.start()
            spare.wait()
        for_rows(lambda j: gather_copy(0, 0, j).start())

    def tile_body(sl):
        for_rows(lambda j: gather_copy(0, sl, 0).wait())
        h = _slabs_to_rows(hbufs[sl], stage, TME).astype(BF16)
        pieces = iter(range(DMA_PIECES))

        def paced_dot(a, b):
            r = jnp.dot(a, b, preferred_element_type=F32)
            z = pltpu.bitcast(r[0:8, 0:128], I32)[0, 0] * zero_ref[0]
            first = next(pieces) * (TME // DMA_PIECES)
            for j in range(first, first + TME // DMA_PIECES):
                gather_copy(i + 1, 1 - sl, j, z).start()
                scatter_copy(i - 1, 1 - sl, j, z).start(priority=1)
            return r

        y = [None] * (D // D_EXPERT)
        for g_ref, w1_ref, w3_ref, w2_ref in ((gwa_ref, w1a_ref, w3a_ref, w2a_ref),
                                              (gwb_ref, w1b_ref, w3b_ref, w2b_ref)):
            half = D_EXPERT // 2
            gate = jnp.concatenate([paced_dot(h, w1_ref[:, c * half:(c + 1) * half]) for c in range(2)], axis=1)
            up = jnp.concatenate([paced_dot(h, w3_ref[:, c * half:(c + 1) * half]) for c in range(2)], axis=1)
            gcol = _gate_column(g_ref, src_ref, i * TME)
            hid = (gate * jax.nn.sigmoid(gate) * up * jnp.concatenate([gcol] * (D_EXPERT // 128), axis=1)).astype(BF16)
            for c in range(len(y)):
                yc = paced_dot(hid, w2_ref[:, c * D_EXPERT:(c + 1) * D_EXPERT])
                y[c] = yc if y[c] is None else y[c] + yc
        y = jnp.concatenate(y, axis=1)

        @pl.when(i >= 1)
        def _():
            for_rows(lambda j: scatter_copy(0, sl, 0).wait())

        _rows_to_slabs(y, stage, ybufs[sl])

        @pl.when(i == nvalid - 1)
        def _():
            for_rows(lambda j: scatter_copy(i, sl, j).start())
            for_rows(lambda j: scatter_copy(0, 1 - sl, 0).wait())
            for_rows(lambda j: scatter_copy(0, sl, 0).wait())
            for_rows(lambda j: gather_copy(0, 1 - sl, 0).wait())

    for sl in range(2):
        @pl.when(jnp.logical_and(i < nvalid, i % 2 == sl))
        def _(sl=sl):
            tile_body(sl)


Y_ROWS = NT + 2 * TME


def _expert_call(tile_ea, tile_eb, nvalid, starts, counts, cls, rank, gwa, gwb, hp, w1b, w3b, w2b):
    up_a = pl.BlockSpec((None, D, D_EXPERT), lambda i, ea, eb, *_: (ea[i], 0, 0))
    up_b = pl.BlockSpec((None, D, D_EXPERT), lambda i, ea, eb, *_: (eb[i], 0, 0))
    down_a = pl.BlockSpec((None, D_EXPERT, D), lambda i, ea, eb, *_: (ea[i], 0, 0))
    down_b = pl.BlockSpec((None, D_EXPERT, D), lambda i, ea, eb, *_: (eb[i], 0, 0))
    return pl.pallas_call(
        _expert_kernel,
        out_shape=jax.ShapeDtypeStruct((Y_ROWS * SLAB, 128), BF16),
        grid_spec=pltpu.PrefetchScalarGridSpec(
            num_scalar_prefetch=10,
            grid=(P_TILES,),
            in_specs=[pl.BlockSpec(memory_space=pl.ANY),
                      up_a, up_a, down_a, up_b, up_b, down_b],
            out_specs=pl.BlockSpec(memory_space=pl.ANY),
            scratch_shapes=[pltpu.SMEM((P_ROWS + TME,), I32), pltpu.SMEM((P_ROWS + TME,), I32)]
            + [pltpu.VMEM((TME * SLAB, 128), BF16)] * 4
            + [pltpu.VMEM((TME * SLAB, 128), F32),
                            pltpu.SemaphoreType.DMA((2,)), pltpu.SemaphoreType.DMA((2,)),
                            pltpu.SemaphoreType.DMA]),
        compiler_params=_cparams(1),
        name="moe_experts",
    )(tile_ea, tile_eb, nvalid, jnp.zeros((1,), I32), starts, counts, cls, rank, gwa, gwb, hp, w1b, w3b, w2b,
      w1b, w3b, w2b)


def _combine_kernel(x1_ref, mod_ref, gf_ref, y_ref, o_ref, stage, *, final):
    x2 = x1_ref[...] + mod_ref[5:6, :] * _slabs_to_rows(y_ref, stage, TM)
    if final:
        ms = jnp.mean(x2 * x2, axis=-1, keepdims=True)
        x2 = x2 * lax.rsqrt(ms + EPS) * gf_ref[...]
    o_ref[...] = x2


def _combine_call(l, x1, mod, g_final, yt, final):
    n_tiles = LAT_TILES if final else N_TILES
    return pl.pallas_call(
        functools.partial(_combine_kernel, final=final),
        out_shape=jax.ShapeDtypeStruct((n_tiles * TM, D), F32),
        grid=(n_tiles,),
        in_specs=[pl.BlockSpec((TM, D), lambda i: (i, 0)),
                  pl.BlockSpec((None, None, 8, D), lambda i: (l, _mod_row(i), 0, 0)),
                  pl.BlockSpec((1, D), lambda i: (0, 0)),
                  pl.BlockSpec((TM * SLAB, 128), lambda i: (i, 0))],
        out_specs=pl.BlockSpec((TM, D), lambda i: (i, 0)),
        scratch_shapes=[pltpu.VMEM((TM * SLAB, 128), F32)],
        compiler_params=_cparams(1),
        name="moe_combine",
    )(x1, mod, g_final.reshape(1, D), yt)


def _slot_plan(counts):
    padded = ((counts + TME - 1) // TME) * TME
    ends = jnp.cumsum(padded)
    starts = ends - padded
    tile_start = jnp.arange(P_TILES, dtype=I32) * TME
    tile_cls = jnp.minimum(jnp.sum((tile_start[:, None] >= ends[None, :]).astype(I32), axis=1), N_CLASSES - 1)
    nvalid = (ends[-1] // TME).reshape(1)
    grp = tile_cls // 6
    tile_ea = grp * PER_GROUP + jnp.asarray(_SLOT_A)[tile_cls % 6]
    tile_eb = grp * PER_GROUP + jnp.asarray(_SLOT_B)[tile_cls % 6]
    return starts, tile_ea, tile_eb, nvalid


def _rope_tables():
    t = jnp.arange(SEQ, dtype=I32)
    row = (t // GRID_W).astype(F32)
    col = (t % GRID_W).astype(F32)
    half = HEAD_DIM // 4
    freqs = ROPE_BASE ** (-jnp.arange(0, 2 * half, 2, dtype=F32) / (2 * half))
    ang_r = row[:, None] * freqs
    ang_c = col[:, None] * freqs
    cos_h = jnp.concatenate([jnp.cos(ang_r), jnp.cos(ang_r), jnp.cos(ang_c), jnp.cos(ang_c)], axis=1)
    sin_h = jnp.concatenate([-jnp.sin(ang_r), jnp.sin(ang_r), -jnp.sin(ang_c), jnp.sin(ang_c)], axis=1)
    cos_t = jnp.concatenate([jnp.tile(cos_h, (1, 2)), jnp.ones((TM, 128), F32)], axis=0)
    sin_t = jnp.concatenate([jnp.tile(sin_h, (1, 2)), jnp.zeros((TM, 128), F32)], axis=0)
    return cos_t, sin_t


def kernel(x, c, ctx, c_ctx, w_mod, b_mod, g_norm1, g_norm2, w_in, w_out, attn_sink, sconv_w, cconv_w, cconv_b,
           cnorm_g, cnorm_b, w_route_group, b_route_group, w_route_expert, b_route_expert, w_gate_e, w_up_e,
           w_down_e, g_final):
    x_lat, x_ctx, ctx_base = x.reshape(N_LAT, D), ctx.reshape(N_CTX, D), 0
    c8 = jnp.zeros((8, D), F32).at[0:BATCH].set(c).at[BATCH].set(c_ctx)
    q_w, k_w, v_w = w_in[:, :, 0:512], w_in[:, :, 512:640], w_in[:, :, 640:768]
    swap = lambda w: jnp.concatenate([w[:, :, HEAD_DIM:], w[:, :, :HEAD_DIM]], axis=-1)
    wqkv = jnp.concatenate([q_w, k_w, v_w, swap(k_w), swap(v_w)], axis=-1).astype(BF16)
    wf = w_in[:, :, 768:1280].astype(BF16)
    wcv = w_in[:, :, 1280:].astype(BF16)
    w_out_b = w_out.astype(BF16)
    w_route = jnp.concatenate([w_route_expert, w_route_group,
                               jnp.zeros((DEPTH, D, 128 - N_EXPERTS - N_GROUPS), F32)], axis=2)
    wr_hi = w_route.astype(BF16)
    wr_lo = (w_route - wr_hi.astype(F32)).astype(BF16)
    b_route = jnp.concatenate([b_route_expert, b_route_group,
                               jnp.zeros((DEPTH, 128 - N_EXPERTS - N_GROUPS), F32)], axis=1).reshape(DEPTH, 1, 128)
    cos_t, sin_t = _rope_tables()

    mod = _mod_call(c8, w_mod, b_mod)
    mod = mod.reshape(DEPTH, 8, N_MOD, D)[:, :3]
    mod = jnp.pad(mod, ((0, 0), (0, 0), (0, 8 - N_MOD), (0, 0)))

    out = None
    for l in range(DEPTH):
        qkv, f, cv, w1b, w3b = _in_call(l, x_lat, x_ctx, ctx_base, mod, g_norm1, wqkv, wf, wcv, cos_t, sin_t,
                                        w_gate_e, w_up_e)
        o_a = _attn_call(l, attn_sink, qkv)
        o_b_lat = _fft_call(f, 64, 0, BATCH)
        o_b_ctx = _fft_call(f, 16, N_LAT // CTX, BATCH)
        o_c, o_d = _conv_call(l, cv, sconv_w, cconv_w, cconv_b, cnorm_g, cnorm_b)
        x1, hp, cls8, gw8, cnt, w2b = _out_call(l, o_a, o_b_lat, o_b_ctx, o_c, o_d, w_out_b, x_lat, x_ctx, ctx_base,
                                                mod, g_norm2, wr_hi, wr_lo, b_route, w_down_e)
        counts = cnt[:N_CLASSES, 0]
        starts, tile_ea, tile_eb, nvalid = _slot_plan(counts)
        yt = _expert_call(tile_ea, tile_eb, nvalid, starts, counts, cls8[0], cls8[1], gw8[0], gw8[1], hp, w1b.reshape(N_EXPERTS, D, D_EXPERT),
                          w3b.reshape(N_EXPERTS, D, D_EXPERT), w2b.reshape(N_EXPERTS, D_EXPERT, D))
        final = l == DEPTH - 1
        res = _combine_call(l, x1, mod, g_final, yt, final)
        if final:
            out = res
        else:
            x_lat, x_ctx, ctx_base = res, res, LAT_TILES
    return out.reshape(BATCH, SEQ, D)
```

```python
import functools

import numpy as np
import jax
import jax.numpy as jnp
from jax import lax
from jax.experimental import pallas as pl
from jax.experimental.pallas import tpu as pltpu

F32 = jnp.float32
BF16 = jnp.bfloat16
I32 = jnp.int32

D = 2048
BATCH = 2
SEQ = 4096
CTX = 256
DEPTH = 2
GRID_W = 64
N_LAT = BATCH * SEQ
N_CTX = BATCH * CTX
NT = N_LAT + N_CTX
GW = 512
HEAD_DIM = 64
N_HEADS = 8
WINDOW = 128
ROPE_BASE = 10000.0
N_MOD = 6
EPS = 1e-6
NEG_INF = -1e30
N_GROUPS = 4
PER_GROUP = 4
N_EXPERTS = 16
D_EXPERT = 512
N_CLASSES = N_GROUPS * 6
CONF_W = 31
CONF_PAD = 15

TM = 256
N_TILES = NT // TM
LAT_TILES = N_LAT // TM
SEQ_TILES = SEQ // TM
TQ = 128
HALO = 16
TME = 128
P_ROWS = NT + N_CLASSES * TME
P_TILES = P_ROWS // TME
SLAB = D // 128
VMEM_LIMIT = 56 * 1024 * 1024


def _cparams(n_axes=1, **kw):
    return pltpu.CompilerParams(dimension_semantics=("arbitrary",) * n_axes,
                                vmem_limit_bytes=VMEM_LIMIT, **kw)


def _split_bf16(v):
    hi = v.astype(BF16)
    lo = (v - hi.astype(F32)).astype(BF16)
    return hi, lo


def _mod_row(i):
    return jnp.minimum(i // SEQ_TILES, 2)


MOD_TN = 1536


MOD_STEPS = N_MOD * D // MOD_TN
W_ROWS = D // MOD_STEPS


def _mod_kernel(c_ref, w_ref, b_ref, win_ref, wout_ref, o_ref, wqkv_ref, wf_ref, wcv_ref, woutb_ref):
    w_in = win_ref[...]
    wqkv_ref[:, 0:768] = w_in[:, 0:768].astype(BF16)
    wqkv_ref[:, 768:896] = pltpu.roll(w_in[:, 512:640], HEAD_DIM, 1).astype(BF16)
    wqkv_ref[:, 896:1024] = pltpu.roll(w_in[:, 640:768], HEAD_DIM, 1).astype(BF16)
    wf_ref[...] = w_in[:, 768:768 + GW].astype(BF16)
    wcv_ref[...] = w_in[:, 768 + GW:].astype(BF16)
    woutb_ref[...] = wout_ref[...].astype(BF16)
    c = c_ref[...]
    s = c * jax.nn.sigmoid(c)
    s_hi, s_lo = _split_bf16(s)
    w_hi, w_lo = _split_bf16(w_ref[...])
    acc = jnp.dot(s_hi, w_hi, preferred_element_type=F32)
    acc = acc + jnp.dot(s_lo, w_hi, preferred_element_type=F32)
    acc = acc + jnp.dot(s_hi, w_lo, preferred_element_type=F32)
    o_ref[...] = acc + b_ref[...]


def _mod_call(c8, w_mod, b_mod, w_in, w_out):
    n = N_MOD * D
    rows = lambda cols: pl.BlockSpec((None, W_ROWS, cols), lambda l, j: (l, j, 0))
    return pl.pallas_call(
        _mod_kernel,
        out_shape=(jax.ShapeDtypeStruct((DEPTH, 8, n), F32),
                   jax.ShapeDtypeStruct((DEPTH, D, QKV_W), BF16), jax.ShapeDtypeStruct((DEPTH, D, GW), BF16),
                   jax.ShapeDtypeStruct((DEPTH, D, CV_IN), BF16), jax.ShapeDtypeStruct((DEPTH, D, D), BF16)),
        grid=(DEPTH, MOD_STEPS),
        in_specs=[
            pl.BlockSpec((8, D), lambda l, j: (0, 0)),
            pl.BlockSpec((None, D, MOD_TN), lambda l, j: (l, 0, j)),
            pl.BlockSpec((None, 1, MOD_TN), lambda l, j: (l, 0, j)),
            rows(w_in.shape[-1]), rows(D),
        ],
        out_specs=(pl.BlockSpec((None, 8, MOD_TN), lambda l, j: (l, 0, j)),
                   rows(QKV_W), rows(GW), rows(CV_IN), rows(D)),
        compiler_params=_cparams(2),
        name="adaln_mod",
    )(c8, w_mod, b_mod.reshape(DEPTH, 1, n), w_in, w_out)


QKV_W = 1024
CV_IN = 5 * GW
CV_OUT = 3 * GW


def _rmsnorm_mod(x, g, shift, scale):
    ms = jnp.mean(x * x, axis=-1, keepdims=True)
    y = x * lax.rsqrt(ms + EPS) * g
    return y * (1.0 + scale) + shift


def _stream_specs(ctx_base):
    return [pl.BlockSpec((TM, D), lambda i, *_: (jnp.minimum(i, LAT_TILES - 1), 0)),
            pl.BlockSpec((TM, D), lambda i, *_: (ctx_base + jnp.maximum(i - LAT_TILES, 0), 0))]


def _stream_tile(xl_ref, xc_ref):
    return jnp.where(pl.program_id(0) < LAT_TILES, xl_ref[...], xc_ref[...])


CAST_STEPS = 32


def _cast_spec(rows, cols, l):
    return pl.BlockSpec((rows, cols), lambda i, *_: (l * CAST_STEPS + jnp.minimum(i, CAST_STEPS - 1), 0))


def _cast_out_spec(rows, cols):
    return pl.BlockSpec((rows, cols), lambda i, *_: (jnp.minimum(i, CAST_STEPS - 1), 0))


def _in_kernel(xl_ref, xc_ref, mod_ref, g_ref, wqkv_ref, wf_ref, wcv_ref, cos_ref, sin_ref, w1_ref, w3_ref,
               qkv_ref, f_ref, cv_ref, w1b_ref, w3b_ref):
    w1b_ref[...] = w1_ref[...].astype(BF16)
    w3b_ref[...] = w3_ref[...].astype(BF16)
    h = _rmsnorm_mod(_stream_tile(xl_ref, xc_ref), g_ref[...], mod_ref[0:1, :], mod_ref[1:2, :])
    hb = h.astype(BF16)
    qkv = jnp.dot(hb, wqkv_ref[...], preferred_element_type=F32)
    cos = cos_ref[...]
    sin = sin_ref[...]
    lane = lax.broadcasted_iota(I32, (TM, 128), 1)
    first_half = (lane % 32) < 16

    def rope(t):
        rot = jnp.where(first_half, pltpu.roll(t, 112, 1), pltpu.roll(t, 16, 1))
        return t * cos + rot * sin

    scale = HEAD_DIM ** -0.5
    for c in range(4):
        qkv_ref[:, c * 128:(c + 1) * 128] = (rope(qkv[:, c * 128:(c + 1) * 128]) * scale).astype(BF16)
    qkv_ref[:, 512:640] = rope(qkv[:, 512:640]).astype(BF16)
    qkv_ref[:, 640:768] = qkv[:, 640:768].astype(BF16)
    qkv_ref[:, 768:896] = rope(qkv[:, 768:896]).astype(BF16)
    qkv_ref[:, 896:1024] = qkv[:, 896:1024].astype(BF16)

    f_ref[...] = jnp.dot(hb, wf_ref[...], preferred_element_type=F32).astype(BF16)

    cv = jnp.dot(hb, wcv_ref[...], preferred_element_type=F32)
    cv_ref[:, 0:GW] = cv[:, 0:GW]
    cv_ref[:, GW:2 * GW] = cv[:, GW:2 * GW] * cv[:, 2 * GW:3 * GW]
    cv_ref[:, 2 * GW:3 * GW] = cv[:, 3 * GW:4 * GW] * jax.nn.sigmoid(cv[:, 4 * GW:5 * GW])


W13_ROWS = N_EXPERTS * D // CAST_STEPS
W2_ROWS = N_EXPERTS * D_EXPERT // CAST_STEPS


def _in_call(l, x_lat, x_ctx, ctx_base, mod, g_norm1, wqkv, wf, wcv, cos_t, sin_t, w_gate_e, w_up_e):
    const = dict(pipeline_mode=pl.Buffered(1))
    flat = lambda w: w.reshape(DEPTH * N_EXPERTS * D, D_EXPERT)
    return pl.pallas_call(
        _in_kernel,
        out_shape=(jax.ShapeDtypeStruct((NT, QKV_W), BF16),
                   jax.ShapeDtypeStruct((NT, GW), BF16),
                   jax.ShapeDtypeStruct((NT, CV_OUT), F32),
                   jax.ShapeDtypeStruct((N_EXPERTS * D, D_EXPERT), BF16),
                   jax.ShapeDtypeStruct((N_EXPERTS * D, D_EXPERT), BF16)),
        grid=(N_TILES,),
        in_specs=_stream_specs(ctx_base) + [
            pl.BlockSpec((None, None, 8, D), lambda i: (l, _mod_row(i), 0, 0)),
            pl.BlockSpec((None, 1, D), lambda i: (l, 0, 0)),
            pl.BlockSpec((None, D, QKV_W), lambda i: (l, 0, 0), **const),
            pl.BlockSpec((None, D, GW), lambda i: (l, 0, 0), **const),
            pl.BlockSpec((None, D, CV_IN), lambda i: (l, 0, 0), **const),
            pl.BlockSpec((TM, 128), lambda i: (jnp.where(i < LAT_TILES, i % SEQ_TILES, SEQ_TILES), 0)),
            pl.BlockSpec((TM, 128), lambda i: (jnp.where(i < LAT_TILES, i % SEQ_TILES, SEQ_TILES), 0)),
            _cast_spec(W13_ROWS, D_EXPERT, l), _cast_spec(W13_ROWS, D_EXPERT, l),
        ],
        out_specs=(pl.BlockSpec((TM, QKV_W), lambda i: (i, 0)),
                   pl.BlockSpec((TM, GW), lambda i: (i, 0)),
                   pl.BlockSpec((TM, CV_OUT), lambda i: (i, 0)),
                   _cast_out_spec(W13_ROWS, D_EXPERT), _cast_out_spec(W13_ROWS, D_EXPERT)),
        compiler_params=_cparams(1),
        name="norm_in_proj",
    )(x_lat, x_ctx, mod, g_norm1.reshape(DEPTH, 1, D), wqkv, wf, wcv, cos_t, sin_t, flat(w_gate_e), flat(w_up_e))


N_QBLK_LAT = N_LAT // TQ
N_QBLK = NT // TQ
QBLK_PER_SEQ = SEQ // TQ
KWIN = 3 * TQ


def _attn_heads(sink_ref, l, q_ref, keys, vals, bias, o_ref):
    lane = lax.broadcasted_iota(I32, (TQ, 128), 1)
    row2 = lax.broadcasted_iota(I32, (2 * TQ, 1), 0)
    zero = jnp.zeros((TQ, 128), BF16)
    if bias is not None:
        bias = jnp.concatenate([bias, bias], axis=0)
    for hk in range(2):
        c0, c1 = 2 * hk, 2 * hk + 1
        for p in range(2):
            kx = keys[0] if hk == p else keys[1]
            vx = vals[0] if hk == p else vals[1]
            in_half = (lane // HEAD_DIM) == p
            qs = jnp.concatenate([jnp.where(in_half, q_ref[:, c0 * 128:(c0 + 1) * 128], zero),
                                  jnp.where(in_half, q_ref[:, c1 * 128:(c1 + 1) * 128], zero)], axis=0)
            s = lax.dot_general(qs, kx, (((1,), (1,)), ((), ())), preferred_element_type=F32)
            if bias is not None:
                s = s + bias
            sink = jnp.where(row2 < TQ, sink_ref[l, 2 * c0 + p], sink_ref[l, 2 * c1 + p])
            m = jnp.maximum(jnp.max(s, axis=-1, keepdims=True), sink)
            e = jnp.exp(s - m)
            den = jnp.sum(e, axis=-1, keepdims=True) + jnp.exp(sink - m)
            o = (jnp.dot(e.astype(BF16), vx, preferred_element_type=F32) / den).astype(BF16)
            for k, c in enumerate((c0, c1)):
                lo = c * 128 + p * HEAD_DIM
                o_ref[:, lo:lo + HEAD_DIM] = o[k * TQ:(k + 1) * TQ, p * HEAD_DIM:(p + 1) * HEAD_DIM]


def _attn_kernel(sink_ref, q_ref, k_ref, v_ref, ks_ref, vs_ref, kc_ref, vc_ref, kcs_ref, vcs_ref, o_ref, *, l):
    i = pl.program_id(0)

    @pl.when(i < N_QBLK_LAT)
    def _latent():
        n = i % QBLK_PER_SEQ
        ws = pl.multiple_of(jnp.clip((n - 1) * TQ, 0, SEQ - KWIN), TQ)
        keys = [jnp.concatenate([r[pl.ds(ws, KWIN), :], c[...]], axis=0)
                for r, c in ((k_ref, kc_ref), (ks_ref, kcs_ref))]
        vals = [jnp.concatenate([r[pl.ds(ws, KWIN), :], c[...]], axis=0)
                for r, c in ((v_ref, vc_ref), (vs_ref, vcs_ref))]
        nk = KWIN + CTX
        qpos = n * TQ + lax.broadcasted_iota(I32, (TQ, nk), 0)
        col = lax.broadcasted_iota(I32, (TQ, nk), 1)
        valid = (jnp.abs(ws + col - qpos) <= WINDOW) | (col >= KWIN)
        _attn_heads(sink_ref, l, q_ref, keys, vals, jnp.where(valid, 0.0, NEG_INF), o_ref)

    @pl.when(i >= N_QBLK_LAT)
    def _context():
        _attn_heads(sink_ref, l, q_ref, [kc_ref[...], kcs_ref[...]], [vc_ref[...], vcs_ref[...]], None, o_ref)


def _attn_call(l, attn_sink, qkv):
    def seq_blk(i):
        return jnp.minimum(i // QBLK_PER_SEQ, 1)

    def ctx_blk(i):
        b = jnp.where(i < N_QBLK_LAT, i // QBLK_PER_SEQ, (i - N_QBLK_LAT) // (CTX // TQ))
        return N_LAT // CTX + b

    seq_specs = [pl.BlockSpec((SEQ, 128), functools.partial(lambda i, col: (seq_blk(i), col), col=col))
                 for col in (4, 5, 6, 7)]
    ctx_specs = [pl.BlockSpec((CTX, 128), functools.partial(lambda i, col: (ctx_blk(i), col), col=col))
                 for col in (4, 5, 6, 7)]
    return pl.pallas_call(
        functools.partial(_attn_kernel, l=l),
        out_shape=jax.ShapeDtypeStruct((NT, GW), BF16),
        grid=(N_QBLK,),
        in_specs=[pl.BlockSpec(memory_space=pltpu.SMEM),
                  pl.BlockSpec((TQ, GW), lambda i: (i, 0))] + seq_specs + ctx_specs,
        out_specs=pl.BlockSpec((TQ, GW), lambda i: (i, 0)),
        compiler_params=_cparams(1),
        name="window_attention",
    )(attn_sink, qkv, *([qkv] * 8))


FC = 256
FFT_UNROLL = 4


def _dft_tables(n):
    t = n * n
    j = np.arange(n)
    ang = 2.0 * np.pi * np.outer(j, j) / n
    cn, sn = np.cos(ang), np.sin(ang)
    m1 = np.block([[cn, sn], [-sn, cn]])
    m2 = np.concatenate([cn, sn], axis=1)
    angc = 2.0 * np.pi * np.outer(np.arange(128), np.arange(128)) / 128.0
    wc = np.concatenate([np.cos(angc), -np.sin(angc)], axis=1)
    angt = 2.0 * np.pi * np.outer(j, j) / t
    tw_re = np.broadcast_to(np.cos(angt)[:, :, None], (n, n, 128))
    tw_im = np.broadcast_to(-np.sin(angt)[:, :, None], (n, n, 128))
    def hi_lo(m):
        return jnp.stack(_split_bf16(jnp.asarray(m, F32)))

    return (hi_lo(m1), hi_lo(m2), hi_lo(wc),
            jnp.asarray(np.ascontiguousarray(tw_re), F32), jnp.asarray(np.ascontiguousarray(tw_im), F32))


def _fft_kernel(f_ref, m1_ref, m2_ref, wc_ref, twr_ref, twi_ref, o_ref, are, aim, hre, him, *, n):
    t = n * n
    ng = FC // 128
    for g in range(ng):
        fg = f_ref[:, g * 128:(g + 1) * 128]
        a = (jnp.dot(fg, wc_ref[0], preferred_element_type=F32)
             + jnp.dot(fg, wc_ref[1], preferred_element_type=F32))
        are[g] = a[:, :128]
        aim[g] = a[:, 128:]

    def table_dot(m_ref, v):
        return (jnp.dot(m_ref[0], v, preferred_element_type=F32)
                + jnp.dot(m_ref[1], v, preferred_element_type=F32))

    def gather_rows(re, im, first):
        rows = [pl.ds(first + u, n, stride=n) for u in range(FFT_UNROLL)]
        top = jnp.concatenate([re[g, r, :] for r in rows for g in range(ng)], axis=1)
        bot = jnp.concatenate([im[g, r, :] for r in rows for g in range(ng)], axis=1)
        return jnp.concatenate([top, bot], axis=0).astype(BF16)

    def stage1(it, carry):
        t1 = it * FFT_UNROLL
        h = table_dot(m1_ref, gather_rows(are, aim, t1))
        for u in range(FFT_UNROLL):
            dst = pl.ds(pl.multiple_of((t1 + u) * n, n), n)
            twr = twr_ref[t1 + u]
            twi = twi_ref[t1 + u]
            for g in range(ng):
                lanes = slice((u * ng + g) * 128, (u * ng + g + 1) * 128)
                h_re, h_im = h[:n, lanes], h[n:, lanes]
                hre[g, dst, :] = h_re * twr - h_im * twi
                him[g, dst, :] = h_re * twi + h_im * twr
        return carry

    lax.fori_loop(0, n // FFT_UNROLL, stage1, 0)

    def stage2(it, carry):
        k2 = it * FFT_UNROLL
        z = table_dot(m2_ref, gather_rows(hre, him, k2))
        z = z * ((t * 128.0) ** -0.5)
        for u in range(FFT_UNROLL):
            for g in range(ng):
                are[g, pl.ds(k2 + u, n, stride=n), :] = z[:, (u * ng + g) * 128:(u * ng + g + 1) * 128]
        return carry

    lax.fori_loop(0, n // FFT_UNROLL, stage2, 0)
    for g in range(ng):
        o_ref[:, g * 128:(g + 1) * 128] = are[g].astype(BF16)


def _fft_call(f, n, first_blk, n_seq):
    t = n * n
    m1, m2, wc, twr, twi = _dft_tables(n)
    full = lambda shape: pl.BlockSpec(shape, lambda b, h: (0,) * len(shape))
    return pl.pallas_call(
        functools.partial(_fft_kernel, n=n),
        out_shape=jax.ShapeDtypeStruct((n_seq * t, GW), BF16),
        grid=(n_seq, GW // FC),
        in_specs=[pl.BlockSpec((t, FC), lambda b, h: (first_blk + b, h)),
                  full((2, 2 * n, 2 * n)), full((2, n, 2 * n)), full((2, 128, 256)),
                  full((n, n, 128)), full((n, n, 128))],
        out_specs=pl.BlockSpec((t, FC), lambda b, h: (b, h)),
        scratch_shapes=[pltpu.VMEM((FC // 128, t, 128), F32)] * 4,
        compiler_params=_cparams(2),
        name="fourier_mix_%d" % t,
    )(f, m1, m2, wc, twr, twi)


CONV_ROWS = 32
GSH_ROWS = TM + 2 * HALO - 8


def _conv_kernel(sb_ref, p_ref, pp_ref, pn_ref, g_ref, gp_ref, gn_ref, sw_ref, cw_ref, cb_ref, ng_ref, nb_ref,
                 oc_ref, od_ref, pbuf, gbuf, gsh):
    i = pl.program_id(0)
    seq_start = jnp.logical_or(i % SEQ_TILES == 0, i >= LAT_TILES)
    seq_end = jnp.logical_or(i % SEQ_TILES == SEQ_TILES - 1, i >= LAT_TILES)
    for buf, main, prev, nxt in ((pbuf, p_ref, pp_ref, pn_ref), (gbuf, g_ref, gp_ref, gn_ref)):
        buf[0:HALO, :] = jnp.where(seq_start, 0.0, prev[...])
        buf[HALO:HALO + TM, :] = main[...]
        buf[HALO + TM:HALO + TM + HALO, :] = jnp.where(seq_end, 0.0, nxt[...])
    for s in range(8):
        gsh[s] = gbuf[s:s + GSH_ROWS, :]
    sw = sw_ref[...]
    cw = cw_ref[...]
    for r0 in range(0, TM, CONV_ROWS):
        base = HALO + r0
        acc = sw[0:1, :] * pbuf[base - 1:base - 1 + CONV_ROWS, :]
        acc = acc + sw[1:2, :] * pbuf[base:base + CONV_ROWS, :]
        acc = acc + sw[2:3, :] * pbuf[base + 1:base + 1 + CONV_ROWS, :]
        oc_ref[r0:r0 + CONV_ROWS, :] = (sb_ref[r0:r0 + CONV_ROWS, :] * acc).astype(BF16)

        z = None
        for j in range(CONF_W):
            o = base - CONF_PAD + j
            q = o - o % 8
            term = cw[j:j + 1, :] * gsh[o % 8, q:q + CONV_ROWS, :]
            z = term if z is None else z + term
        z = z + cb_ref[...]
        mu = jnp.mean(z, axis=-1, keepdims=True)
        zc = z - mu
        var = jnp.mean(zc * zc, axis=-1, keepdims=True)
        y = zc * lax.rsqrt(var + EPS) * ng_ref[...] + nb_ref[...]
        od_ref[r0:r0 + CONV_ROWS, :] = (y * jax.nn.sigmoid(y)).astype(BF16)


def _conv_call(l, cv, sconv_w, cconv_w, cconv_b, cnorm_g, cnorm_b):
    hb = TM // HALO
    last = NT // HALO - 1

    def halo_specs(col):
        return [pl.BlockSpec((TM, GW), lambda i: (i, col)),
                pl.BlockSpec((HALO, GW), lambda i: (jnp.maximum(i * hb - 1, 0), col)),
                pl.BlockSpec((HALO, GW), lambda i: (jnp.minimum((i + 1) * hb, last), col))]

    vec = lambda: pl.BlockSpec((None, 1, GW), lambda i: (l, 0, 0))
    cw = jnp.pad(cconv_w, ((0, 0), (0, 32 - CONF_W), (0, 0)))
    sw = jnp.pad(sconv_w, ((0, 0), (0, 8 - 3), (0, 0)))
    return pl.pallas_call(
        _conv_kernel,
        out_shape=(jax.ShapeDtypeStruct((NT, GW), BF16), jax.ShapeDtypeStruct((NT, GW), BF16)),
        grid=(N_TILES,),
        in_specs=[pl.BlockSpec((TM, GW), lambda i: (i, 0))] + halo_specs(1) + halo_specs(2) + [
            pl.BlockSpec((None, 8, GW), lambda i: (l, 0, 0)),
            pl.BlockSpec((None, 32, GW), lambda i: (l, 0, 0)),
            vec(), vec(), vec()],
        out_specs=(pl.BlockSpec((TM, GW), lambda i: (i, 0)), pl.BlockSpec((TM, GW), lambda i: (i, 0))),
        scratch_shapes=[pltpu.VMEM((TM + 2 * HALO, GW), F32)] * 2 + [pltpu.VMEM((8, GSH_ROWS, GW), F32)],
        compiler_params=_cparams(1),
        name="conv_mixers",
    )(cv, cv, cv, cv, cv, cv, cv, sw, cw, cconv_b.reshape(DEPTH, 1, GW), cnorm_g.reshape(DEPTH, 1, GW),
      cnorm_b.reshape(DEPTH, 1, GW))


R_ROWS = 32
_SLOT_A = np.array([0, 2, 2, 0, 0, 1], np.int32)
_SLOT_B = np.array([1, 1, 3, 3, 2, 3], np.int32)


def _route(lg):
    gl = [lg[N_EXPERTS + g:N_EXPERTS + g + 1, :] for g in range(N_GROUPS)]
    gmax = functools.reduce(jnp.maximum, gl)
    pg = 1.0 / functools.reduce(jnp.add, [jnp.exp(v - gmax) for v in gl])
    gsel = jnp.full_like(gmax, N_GROUPS - 1).astype(I32)
    for g in range(N_GROUPS - 2, -1, -1):
        gsel = jnp.where(gl[g] == gmax, g, gsel)
    el = []
    for j in range(PER_GROUP):
        v = lg[j:j + 1, :]
        for g in range(1, N_GROUPS):
            v = jnp.where(gsel == g, lg[g * PER_GROUP + j:g * PER_GROUP + j + 1, :], v)
        el.append(v)
    v1 = functools.reduce(jnp.maximum, el)
    i1 = jnp.full_like(gsel, PER_GROUP - 1)
    for j in range(PER_GROUP - 2, -1, -1):
        i1 = jnp.where(el[j] == v1, j, i1)
    rest = [jnp.where(i1 == j, -jnp.inf, el[j]) for j in range(PER_GROUP)]
    v2 = functools.reduce(jnp.maximum, rest)
    i2 = jnp.full_like(gsel, PER_GROUP - 1)
    for j in range(PER_GROUP - 2, -1, -1):
        i2 = jnp.where(jnp.logical_and(rest[j] == v2, i1 != j), j, i2)
    e2 = jnp.exp(v2 - v1)
    w1 = pg / (1.0 + e2)
    w2 = pg * e2 / (1.0 + e2)
    lo = jnp.minimum(i1, i2)
    hi = jnp.maximum(i1, i2)
    w_lo = jnp.where(i1 < i2, w1, w2)
    w_hi = jnp.where(i1 < i2, w2, w1)
    pair = jnp.where(lo == 0, jnp.where(hi == 1, 0, jnp.where(hi == 2, 4, 3)),
                     jnp.where(lo == 1, jnp.where(hi == 2, 1, 5), 2))
    cls = gsel * 6 + pair
    a_is_hi = pair == 1
    return cls, jnp.where(a_is_hi, w_hi, w_lo), jnp.where(a_is_hi, w_lo, w_hi)


def _rows_to_slabs(v, stage_ref, out_ref):
    n = v.shape[0]
    for c in range(SLAB):
        stage_ref[pl.ds(c, n, stride=SLAB), :] = v[:, c * 128:(c + 1) * 128]
    out_ref[...] = stage_ref[...].astype(out_ref.dtype)


def _slabs_to_rows(src_ref, stage_ref, n):
    stage_ref[...] = src_ref[...].astype(F32)
    return jnp.concatenate([stage_ref[pl.ds(c, n, stride=SLAB), :] for c in range(SLAB)], axis=1)


def _out_kernel(oa_ref, obl_ref, obc_ref, oc_ref, od_ref, w_ref, xl_ref, xc_ref, mod_ref, g_ref, wrh_ref, wrl_ref,
                br_ref, tri_ref, w2_ref, x1_ref, hp_ref, cls_ref, gw_ref, cnt_ref, w2b_ref, carry, stage):
    i = pl.program_id(0)
    w2b_ref[...] = w2_ref[...].astype(BF16)

    @pl.when(i == 0)
    def _():
        carry[...] = jnp.zeros_like(carry)

    ob = jnp.where(i < LAT_TILES, obl_ref[...], obc_ref[...])
    y = jnp.dot(oa_ref[...], w_ref[0:GW, :], preferred_element_type=F32)
    y = y + jnp.dot(ob, w_ref[GW:2 * GW, :], preferred_element_type=F32)
    y = y + jnp.dot(oc_ref[...], w_ref[2 * GW:3 * GW, :], preferred_element_type=F32)
    y = y + jnp.dot(od_ref[...], w_ref[3 * GW:4 * GW, :], preferred_element_type=F32)
    x1 = _stream_tile(xl_ref, xc_ref) + mod_ref[2:3, :] * y
    x1_ref[...] = x1
    h2 = _rmsnorm_mod(x1, g_ref[...], mod_ref[3:4, :], mod_ref[4:5, :])
    _rows_to_slabs(h2, stage, hp_ref)

    h_hi, h_lo = _split_bf16(h2)
    lg = jnp.dot(h_hi, wrh_ref[...], preferred_element_type=F32)
    lg = lg + jnp.dot(h_lo, wrh_ref[...], preferred_element_type=F32)
    lg = lg + jnp.dot(h_hi, wrl_ref[...], preferred_element_type=F32)
    lg = jnp.transpose(lg + br_ref[...])[0:R_ROWS, :]
    cls, w_a, w_b = _route(lg)

    onehot = jnp.where(lax.broadcasted_iota(I32, (R_ROWS, TM), 0) == cls, 1.0, 0.0)
    prefix = jnp.dot(onehot.astype(BF16), tri_ref[...], preferred_element_type=F32)
    rank = jnp.sum(onehot * (prefix - 1.0 + carry[:, 0:1]), axis=0, keepdims=True)
    total = carry[...] + jnp.sum(onehot, axis=1, keepdims=True)
    carry[...] = total
    cnt_ref[...] = total.astype(I32)
    cls_ref[...] = jnp.concatenate([cls, rank.astype(I32), jnp.zeros((6, TM), I32)], axis=0)
    gw_ref[...] = jnp.concatenate([w_a, w_b, jnp.zeros((6, TM), F32)], axis=0)


def _out_call(l, o_a, o_b_lat, o_b_ctx, o_c, o_d, w_out, x_lat, x_ctx, ctx_base, mod, g_norm2, wr_hi, wr_lo,
              b_route, w_down_e):
    row = lambda: pl.BlockSpec((TM, GW), lambda i: (i, 0))
    tri = jnp.asarray(np.triu(np.ones((TM, TM), np.float32)), BF16)
    return pl.pallas_call(
        _out_kernel,
        out_shape=(jax.ShapeDtypeStruct((NT, D), F32), jax.ShapeDtypeStruct((NT * SLAB, 128), BF16),
                   jax.ShapeDtypeStruct((8, NT), I32), jax.ShapeDtypeStruct((8, NT), F32),
                   jax.ShapeDtypeStruct((R_ROWS, 128), I32),
                   jax.ShapeDtypeStruct((N_EXPERTS * D_EXPERT, D), BF16)),
        grid=(N_TILES,),
        in_specs=[row(),
                  pl.BlockSpec((TM, GW), lambda i: (jnp.minimum(i, LAT_TILES - 1), 0)),
                  pl.BlockSpec((TM, GW), lambda i: (jnp.maximum(i - LAT_TILES, 0), 0)),
                  row(), row(),
                  pl.BlockSpec((None, D, D), lambda i: (l, 0, 0), pipeline_mode=pl.Buffered(1))]
        + _stream_specs(ctx_base) + [
                  pl.BlockSpec((None, None, 8, D), lambda i: (l, _mod_row(i), 0, 0)),
                  pl.BlockSpec((None, 1, D), lambda i: (l, 0, 0)),
                  pl.BlockSpec((None, D, 128), lambda i: (l, 0, 0)),
                  pl.BlockSpec((None, D, 128), lambda i: (l, 0, 0)),
                  pl.BlockSpec((None, 1, 128), lambda i: (l, 0, 0)),
                  pl.BlockSpec((TM, TM), lambda i: (0, 0)),
                  _cast_spec(W2_ROWS, D, l)],
        out_specs=(pl.BlockSpec((TM, D), lambda i: (i, 0)),
                   pl.BlockSpec((TM * SLAB, 128), lambda i: (i, 0)),
                   pl.BlockSpec((8, TM), lambda i: (0, i)), pl.BlockSpec((8, TM), lambda i: (0, i)),
                   pl.BlockSpec((R_ROWS, 128), lambda i: (0, 0)),
                   _cast_out_spec(W2_ROWS, D)),
        scratch_shapes=[pltpu.VMEM((R_ROWS, 128), F32), pltpu.VMEM((TM * SLAB, 128), F32)],
        compiler_params=_cparams(1),
        name="out_proj_route",
    )(o_a, o_b_lat, o_b_ctx, o_c, o_d, w_out, x_lat, x_ctx, mod, g_norm2.reshape(DEPTH, 1, D), wr_hi, wr_lo,
      b_route, tri, w_down_e.reshape(DEPTH * N_EXPERTS * D_EXPERT, D))


ROW_UNROLL = 8


def _gate_column(g_ref, src_ref, base):
    sub = lax.broadcasted_iota(I32, (8, 128), 0)
    pieces = []
    for g in range(TME // 8):
        v = jnp.zeros((8, 128), F32)
        for k in range(8):
            v = jnp.where(sub == k, g_ref[src_ref[base + 8 * g + k]], v)
        pieces.append(v)
    return jnp.concatenate(pieces, axis=0)


DMA_PIECES = 8


def _expert_kernel(ea_ref, eb_ref, nvalid_ref, zero_ref, starts_ref, counts_ref, cls_ref, rank_ref, gwa_ref, gwb_ref, hp_hbm, w1a_ref, w3a_ref, w2a_ref, w1b_ref,
                   w3b_ref, w2b_ref, yt_hbm, src_ref, dst_ref, hbuf0, hbuf1, ybuf0, ybuf1, stage, gsem, ssem, zsem):
    i = pl.program_id(0)
    nvalid = nvalid_ref[0]
    hbufs, ybufs = (hbuf0, hbuf1), (ybuf0, ybuf1)

    def slab(ref, row):
        return ref.at[pl.ds(pl.multiple_of(row * SLAB, SLAB), SLAB), :]

    def gather_copy(tile, sl, j, z=0):
        return pltpu.make_async_copy(slab(hp_hbm, src_ref[tile * TME + j] + z), slab(hbufs[sl], j), gsem.at[sl])

    def scatter_copy(tile, sl, j, z=0):
        return pltpu.make_async_copy(slab(ybufs[sl], j), slab(yt_hbm, dst_ref[(tile + 1) * TME + j] + z),
                                     ssem.at[sl])

    def for_rows(fn):
        def body(j, carry):
            fn(j)
            return carry

        lax.fori_loop(0, TME, body, 0, unroll=ROW_UNROLL)

    @pl.when(i == 0)
    def _():
        def unused(p, carry):
            src_ref[p] = NT - 1
            dst_ref[p + TME] = NT + ((p + TME) & (2 * TME - 1))
            return carry

        def dummy_tile(j, carry):
            dst_ref[j] = NT + j
            src_ref[nvalid * TME + j] = NT - 1
            return carry

        lax.fori_loop(0, TME, dummy_tile, 0, unroll=ROW_UNROLL)

        def class_tail(c, carry):
            lo = starts_ref[c] + counts_ref[c]
            hi = starts_ref[c] + ((counts_ref[c] + TME - 1) // TME) * TME
            lax.fori_loop(lo, hi, unused, 0)
            return carry

        lax.fori_loop(0, N_CLASSES, class_tail, 0)

        def place(t, carry):
            p = starts_ref[cls_ref[t]] + rank_ref[t]
            src_ref[p] = t
            dst_ref[p + TME] = t
            return carry

        lax.fori_loop(0, NT, place, 0, unroll=ROW_UNROLL)
        for sl in range(2):
            ybufs[sl][...] = jnp.zeros_like(ybufs[sl])
            spare = pltpu.make_async_copy(ybufs[sl], yt_hbm.at[pl.ds((NT + sl * TME) * SLAB, TME * SLAB), :], zsem)
            spare.start()
            spare.wait()
        for_rows(lambda j: gather_copy(0, 0, j).start())

    def tile_body(sl):
        for_rows(lambda j: gather_copy(0, sl, 0).wait())
        h = _slabs_to_rows(hbufs[sl], stage, TME).astype(BF16)
        pieces = iter(range(DMA_PIECES))

        def paced_dot(a, b):
            r = jnp.dot(a, b, preferred_element_type=F32)
            piece = next(pieces, None)
            if piece is None:
                return r
            z = pltpu.bitcast(r[0:8, 0:128], I32)[0, 0] * zero_ref[0]
            first = piece * (TME // DMA_PIECES)
            for j in range(first, first + TME // DMA_PIECES):
                gather_copy(i + 1, 1 - sl, j, z).start()
                scatter_copy(i - 1, 1 - sl, j, z).start(priority=1)
            return r

        y = [None] * (D // D_EXPERT)
        for g_ref, w1_ref, w3_ref, w2_ref in ((gwa_ref, w1a_ref, w3a_ref, w2a_ref),
                                              (gwb_ref, w1b_ref, w3b_ref, w2b_ref)):
            half = D_EXPERT // 2
            gate = jnp.concatenate([paced_dot(h, w1_ref[:, c * half:(c + 1) * half]) for c in range(2)], axis=1)
            up = jnp.concatenate([paced_dot(h, w3_ref[:, c * half:(c + 1) * half]) for c in range(2)], axis=1)
            gcol = _gate_column(g_ref, src_ref, i * TME)
            hid = (gate * jax.nn.sigmoid(gate) * up * jnp.concatenate([gcol] * (D_EXPERT // 128), axis=1)).astype(BF16)
            for c in range(len(y)):
                yc = paced_dot(hid, w2_ref[:, c * D_EXPERT:(c + 1) * D_EXPERT])
                y[c] = yc if y[c] is None else y[c] + yc
        y = jnp.concatenate(y, axis=1)

        @pl.when(i >= 1)
        def _():
            for_rows(lambda j: scatter_copy(0, sl, 0).wait())

        _rows_to_slabs(y, stage, ybufs[sl])

        @pl.when(i == nvalid - 1)
        def _():
            for_rows(lambda j: scatter_copy(i, sl, j).start())
            for_rows(lambda j: scatter_copy(0, 1 - sl, 0).wait())
            for_rows(lambda j: scatter_copy(0, sl, 0).wait())
            for_rows(lambda j: gather_copy(0, 1 - sl, 0).wait())

    for sl in range(2):
        @pl.when(jnp.logical_and(i < nvalid, i % 2 == sl))
        def _(sl=sl):
            tile_body(sl)


Y_ROWS = NT + 2 * TME


def _expert_call(tile_ea, tile_eb, nvalid, starts, counts, cls, rank, gwa, gwb, hp, w1b, w3b, w2b):
    up_a = pl.BlockSpec((None, D, D_EXPERT), lambda i, ea, eb, *_: (ea[i], 0, 0))
    up_b = pl.BlockSpec((None, D, D_EXPERT), lambda i, ea, eb, *_: (eb[i], 0, 0))
    down_a = pl.BlockSpec((None, D_EXPERT, D), lambda i, ea, eb, *_: (ea[i], 0, 0))
    down_b = pl.BlockSpec((None, D_EXPERT, D), lambda i, ea, eb, *_: (eb[i], 0, 0))
    return pl.pallas_call(
        _expert_kernel,
        out_shape=jax.ShapeDtypeStruct((Y_ROWS * SLAB, 128), BF16),
        grid_spec=pltpu.PrefetchScalarGridSpec(
            num_scalar_prefetch=10,
            grid=(P_TILES,),
            in_specs=[pl.BlockSpec(memory_space=pl.ANY),
                      up_a, up_a, down_a, up_b, up_b, down_b],
            out_specs=pl.BlockSpec(memory_space=pl.ANY),
            scratch_shapes=[pltpu.SMEM((P_ROWS + TME,), I32), pltpu.SMEM((P_ROWS + TME,), I32)]
            + [pltpu.VMEM((TME * SLAB, 128), BF16)] * 4
            + [pltpu.VMEM((TME * SLAB, 128), F32),
                            pltpu.SemaphoreType.DMA((2,)), pltpu.SemaphoreType.DMA((2,)),
                            pltpu.SemaphoreType.DMA]),
        compiler_params=_cparams(1),
        name="moe_experts",
    )(tile_ea, tile_eb, nvalid, jnp.zeros((1,), I32), starts, counts, cls, rank, gwa, gwb, hp, w1b, w3b, w2b,
      w1b, w3b, w2b)


def _combine_kernel(x1_ref, mod_ref, gf_ref, y_ref, o_ref, stage, *, final):
    x2 = x1_ref[...] + mod_ref[5:6, :] * _slabs_to_rows(y_ref, stage, TM)
    if final:
        ms = jnp.mean(x2 * x2, axis=-1, keepdims=True)
        x2 = x2 * lax.rsqrt(ms + EPS) * gf_ref[...]
    o_ref[...] = x2


def _combine_call(l, x1, mod, g_final, yt, final):
    n_tiles = LAT_TILES if final else N_TILES
    return pl.pallas_call(
        functools.partial(_combine_kernel, final=final),
        out_shape=jax.ShapeDtypeStruct((n_tiles * TM, D), F32),
        grid=(n_tiles,),
        in_specs=[pl.BlockSpec((TM, D), lambda i: (i, 0)),
                  pl.BlockSpec((None, None, 8, D), lambda i: (l, _mod_row(i), 0, 0)),
                  pl.BlockSpec((1, D), lambda i: (0, 0)),
                  pl.BlockSpec((TM * SLAB, 128), lambda i: (i, 0))],
        out_specs=pl.BlockSpec((TM, D), lambda i: (i, 0)),
        scratch_shapes=[pltpu.VMEM((TM * SLAB, 128), F32)],
        compiler_params=_cparams(1),
        name="moe_combine",
    )(x1, mod, g_final.reshape(1, D), yt)


def _slot_plan(counts):
    padded = ((counts + TME - 1) // TME) * TME
    ends = jnp.cumsum(padded)
    starts = ends - padded
    tile_start = jnp.arange(P_TILES, dtype=I32) * TME
    tile_cls = jnp.minimum(jnp.sum((tile_start[:, None] >= ends[None, :]).astype(I32), axis=1), N_CLASSES - 1)
    nvalid = (ends[-1] // TME).reshape(1)
    grp = tile_cls // 6
    tile_ea = grp * PER_GROUP + jnp.asarray(_SLOT_A)[tile_cls % 6]
    tile_eb = grp * PER_GROUP + jnp.asarray(_SLOT_B)[tile_cls % 6]
    return starts, tile_ea, tile_eb, nvalid


def _rope_tables():
    t = jnp.arange(SEQ, dtype=I32)
    row = (t // GRID_W).astype(F32)
    col = (t % GRID_W).astype(F32)
    half = HEAD_DIM // 4
    freqs = ROPE_BASE ** (-jnp.arange(0, 2 * half, 2, dtype=F32) / (2 * half))
    ang_r = row[:, None] * freqs
    ang_c = col[:, None] * freqs
    cos_h = jnp.concatenate([jnp.cos(ang_r), jnp.cos(ang_r), jnp.cos(ang_c), jnp.cos(ang_c)], axis=1)
    sin_h = jnp.concatenate([-jnp.sin(ang_r), jnp.sin(ang_r), -jnp.sin(ang_c), jnp.sin(ang_c)], axis=1)
    cos_t = jnp.concatenate([jnp.tile(cos_h, (1, 2)), jnp.ones((TM, 128), F32)], axis=0)
    sin_t = jnp.concatenate([jnp.tile(sin_h, (1, 2)), jnp.zeros((TM, 128), F32)], axis=0)
    return cos_t, sin_t


def kernel(x, c, ctx, c_ctx, w_mod, b_mod, g_norm1, g_norm2, w_in, w_out, attn_sink, sconv_w, cconv_w, cconv_b,
           cnorm_g, cnorm_b, w_route_group, b_route_group, w_route_expert, b_route_expert, w_gate_e, w_up_e,
           w_down_e, g_final):
    x_lat, x_ctx, ctx_base = x.reshape(N_LAT, D), ctx.reshape(N_CTX, D), 0
    c8 = jnp.zeros((8, D), F32).at[0:BATCH].set(c).at[BATCH].set(c_ctx)
    w_route = jnp.concatenate([w_route_expert, w_route_group,
                               jnp.zeros((DEPTH, D, 128 - N_EXPERTS - N_GROUPS), F32)], axis=2)
    wr_hi = w_route.astype(BF16)
    wr_lo = (w_route - wr_hi.astype(F32)).astype(BF16)
    b_route = jnp.concatenate([b_route_expert, b_route_group,
                               jnp.zeros((DEPTH, 128 - N_EXPERTS - N_GROUPS), F32)], axis=1).reshape(DEPTH, 1, 128)
    cos_t, sin_t = _rope_tables()

    mod, wqkv, wf, wcv, w_out_b = _mod_call(c8, w_mod, b_mod, w_in, w_out)
    mod = mod.reshape(DEPTH, 8, N_MOD, D)[:, :3]
    mod = jnp.pad(mod, ((0, 0), (0, 0), (0, 8 - N_MOD), (0, 0)))

    out = None
    for l in range(DEPTH):
        qkv, f, cv, w1b, w3b = _in_call(l, x_lat, x_ctx, ctx_base, mod, g_norm1, wqkv, wf, wcv, cos_t, sin_t,
                                        w_gate_e, w_up_e)
        o_a = _attn_call(l, attn_sink, qkv)
        o_b_lat = _fft_call(f, 64, 0, BATCH)
        o_b_ctx = _fft_call(f, 16, N_LAT // CTX, BATCH)
        o_c, o_d = _conv_call(l, cv, sconv_w, cconv_w, cconv_b, cnorm_g, cnorm_b)
        x1, hp, cls8, gw8, cnt, w2b = _out_call(l, o_a, o_b_lat, o_b_ctx, o_c, o_d, w_out_b, x_lat, x_ctx, ctx_base,
                                                mod, g_norm2, wr_hi, wr_lo, b_route, w_down_e)
        counts = cnt[:N_CLASSES, 0]
        starts, tile_ea, tile_eb, nvalid = _slot_plan(counts)
        yt = _expert_call(tile_ea, tile_eb, nvalid, starts, counts, cls8[0], cls8[1], gw8[0], gw8[1], hp, w1b.reshape(N_EXPERTS, D, D_EXPERT),
                          w3b.reshape(N_EXPERTS, D, D_EXPERT), w2b.reshape(N_EXPERTS, D_EXPERT, D))
        final = l == DEPTH - 1
        res = _combine_call(l, x1, mod, g_final, yt, final)
        if final:
            out = res
        else:
            x_lat, x_ctx, ctx_base = res, res, LAT_TILES
    return out.reshape(BATCH, SEQ, D)
```

```python
import functools

import numpy as np
import jax
import jax.numpy as jnp
from jax import lax
from jax.experimental import pallas as pl
from jax.experimental.pallas import tpu as pltpu

F32 = jnp.float32
BF16 = jnp.bfloat16
I32 = jnp.int32

D = 2048
BATCH = 2
SEQ = 4096
CTX = 256
DEPTH = 2
GRID_W = 64
N_LAT = BATCH * SEQ
N_CTX = BATCH * CTX
NT = N_LAT + N_CTX
GW = 512
HEAD_DIM = 64
N_HEADS = 8
WINDOW = 128
ROPE_BASE = 10000.0
N_MOD = 6
EPS = 1e-6
NEG_INF = -1e30
N_GROUPS = 4
PER_GROUP = 4
N_EXPERTS = 16
D_EXPERT = 512
N_CLASSES = N_GROUPS * 6
CONF_W = 31
CONF_PAD = 15

TM = 256
N_TILES = NT // TM
LAT_TILES = N_LAT // TM
SEQ_TILES = SEQ // TM
TQ = 128
HALO = 16
TME = 128
P_ROWS = NT + N_CLASSES * TME
P_TILES = P_ROWS // TME
SLAB = D // 128
VMEM_LIMIT = 56 * 1024 * 1024


def _cparams(n_axes=1, **kw):
    return pltpu.CompilerParams(dimension_semantics=("arbitrary",) * n_axes,
                                vmem_limit_bytes=VMEM_LIMIT, **kw)


def _split_bf16(v):
    hi = v.astype(BF16)
    lo = (v - hi.astype(F32)).astype(BF16)
    return hi, lo


def _mod_row(i):
    return jnp.minimum(i // SEQ_TILES, 2)


MOD_TN = 1536


MOD_STEPS = N_MOD * D // MOD_TN
W_ROWS = D // MOD_STEPS


def _mod_kernel(c_ref, w_ref, b_ref, win_ref, wout_ref, o_ref, wqkv_ref, wf_ref, wcv_ref, woutb_ref):
    w_in = win_ref[...]
    wqkv_ref[:, 0:768] = w_in[:, 0:768].astype(BF16)
    wqkv_ref[:, 768:896] = pltpu.roll(w_in[:, 512:640], HEAD_DIM, 1).astype(BF16)
    wqkv_ref[:, 896:1024] = pltpu.roll(w_in[:, 640:768], HEAD_DIM, 1).astype(BF16)
    wf_ref[...] = w_in[:, 768:768 + GW].astype(BF16)
    wcv_ref[...] = w_in[:, 768 + GW:].astype(BF16)
    woutb_ref[...] = wout_ref[...].astype(BF16)
    c = c_ref[...]
    s = c * jax.nn.sigmoid(c)
    s_hi, s_lo = _split_bf16(s)
    w_hi, w_lo = _split_bf16(w_ref[...])
    acc = jnp.dot(s_hi, w_hi, preferred_element_type=F32)
    acc = acc + jnp.dot(s_lo, w_hi, preferred_element_type=F32)
    acc = acc + jnp.dot(s_hi, w_lo, preferred_element_type=F32)
    o_ref[...] = acc + b_ref[...]


def _mod_call(c8, w_mod, b_mod, w_in, w_out):
    n = N_MOD * D
    rows = lambda cols: pl.BlockSpec((None, W_ROWS, cols), lambda l, j: (l, j, 0))
    return pl.pallas_call(
        _mod_kernel,
        out_shape=(jax.ShapeDtypeStruct((DEPTH, 8, n), F32),
                   jax.ShapeDtypeStruct((DEPTH, D, QKV_W), BF16), jax.ShapeDtypeStruct((DEPTH, D, GW), BF16),
                   jax.ShapeDtypeStruct((DEPTH, D, CV_IN), BF16), jax.ShapeDtypeStruct((DEPTH, D, D), BF16)),
        grid=(DEPTH, MOD_STEPS),
        in_specs=[
            pl.BlockSpec((8, D), lambda l, j: (0, 0)),
            pl.BlockSpec((None, D, MOD_TN), lambda l, j: (l, 0, j)),
            pl.BlockSpec((None, 1, MOD_TN), lambda l, j: (l, 0, j)),
            rows(w_in.shape[-1]), rows(D),
        ],
        out_specs=(pl.BlockSpec((None, 8, MOD_TN), lambda l, j: (l, 0, j)),
                   rows(QKV_W), rows(GW), rows(CV_IN), rows(D)),
        compiler_params=_cparams(2),
        name="adaln_mod",
    )(c8, w_mod, b_mod.reshape(DEPTH, 1, n), w_in, w_out)


QKV_W = 1024
CV_IN = 5 * GW
CV_OUT = 3 * GW


def _rmsnorm_mod(x, g, shift, scale):
    ms = jnp.mean(x * x, axis=-1, keepdims=True)
    y = x * lax.rsqrt(ms + EPS) * g
    return y * (1.0 + scale) + shift


def _stream_specs(l):
    if l == 0:
        return [pl.BlockSpec((TM, D), lambda i, *_: (jnp.minimum(i, LAT_TILES - 1), 0)),
                pl.BlockSpec((TM, D), lambda i, *_: (jnp.maximum(i - LAT_TILES, 0), 0))]
    return [pl.BlockSpec((TM, D), lambda i, *_: (i, 0)),
            pl.BlockSpec((TM * SLAB, 128), lambda i, *_: (i, 0)),
            pl.BlockSpec((None, None, 8, D), lambda i, *_: (l - 1, _mod_row(i), 0, 0))]


def _stream_tile(l, refs, stage):
    if l == 0:
        xl_ref, xc_ref = refs
        return jnp.where(pl.program_id(0) < LAT_TILES, xl_ref[...], xc_ref[...])
    x1_ref, yt_ref, mod_prev_ref = refs
    return x1_ref[...] + mod_prev_ref[5:6, :] * _slabs_to_rows(yt_ref, stage, TM)


def _n_stream(l):
    return 2 if l == 0 else 3


def _layer_tiles(l):
    return LAT_TILES if l == DEPTH - 1 else N_TILES


CAST_STEPS = 32


def _cast_spec(rows, cols, l):
    return pl.BlockSpec((rows, cols), lambda i, *_: (l * CAST_STEPS + jnp.minimum(i, CAST_STEPS - 1), 0))


def _cast_out_spec(rows, cols):
    return pl.BlockSpec((rows, cols), lambda i, *_: (jnp.minimum(i, CAST_STEPS - 1), 0))


def _in_kernel(*refs, l):
    stream, refs = refs[:_n_stream(l)], refs[_n_stream(l):]
    (mod_ref, g_ref, wqkv_ref, wf_ref, wcv_ref, cos_ref, sin_ref, w1_ref, w3_ref,
     qkv_ref, f_ref, cv_ref, w1b_ref, w3b_ref) = refs[:14]
    stage = refs[14] if l > 0 else None
    w1b_ref[...] = w1_ref[...].astype(BF16)
    w3b_ref[...] = w3_ref[...].astype(BF16)
    h = _rmsnorm_mod(_stream_tile(l, stream, stage), g_ref[...], mod_ref[0:1, :], mod_ref[1:2, :])
    hb = h.astype(BF16)
    qkv = jnp.dot(hb, wqkv_ref[...], preferred_element_type=F32)
    cos = cos_ref[...]
    sin = sin_ref[...]
    lane = lax.broadcasted_iota(I32, (TM, 128), 1)
    first_half = (lane % 32) < 16

    def rope(t):
        rot = jnp.where(first_half, pltpu.roll(t, 112, 1), pltpu.roll(t, 16, 1))
        return t * cos + rot * sin

    scale = HEAD_DIM ** -0.5
    for c in range(4):
        qkv_ref[:, c * 128:(c + 1) * 128] = (rope(qkv[:, c * 128:(c + 1) * 128]) * scale).astype(BF16)
    qkv_ref[:, 512:640] = rope(qkv[:, 512:640]).astype(BF16)
    qkv_ref[:, 640:768] = qkv[:, 640:768].astype(BF16)
    qkv_ref[:, 768:896] = rope(qkv[:, 768:896]).astype(BF16)
    qkv_ref[:, 896:1024] = qkv[:, 896:1024].astype(BF16)

    f_ref[...] = jnp.dot(hb, wf_ref[...], preferred_element_type=F32).astype(BF16)

    cv = jnp.dot(hb, wcv_ref[...], preferred_element_type=F32)
    cv_ref[:, 0:GW] = cv[:, 0:GW]
    cv_ref[:, GW:2 * GW] = cv[:, GW:2 * GW] * cv[:, 2 * GW:3 * GW]
    cv_ref[:, 2 * GW:3 * GW] = cv[:, 3 * GW:4 * GW] * jax.nn.sigmoid(cv[:, 4 * GW:5 * GW])


W13_ROWS = N_EXPERTS * D // CAST_STEPS
W2_ROWS = N_EXPERTS * D_EXPERT // CAST_STEPS


def _in_call(l, stream, mod, g_norm1, wqkv, wf, wcv, cos_t, sin_t, w_gate_e, w_up_e):
    const = dict(pipeline_mode=pl.Buffered(1))
    flat = lambda w: w.reshape(DEPTH * N_EXPERTS * D, D_EXPERT)
    return pl.pallas_call(
        functools.partial(_in_kernel, l=l),
        out_shape=(jax.ShapeDtypeStruct((NT, QKV_W), BF16),
                   jax.ShapeDtypeStruct((NT, GW), BF16),
                   jax.ShapeDtypeStruct((NT, CV_OUT), F32),
                   jax.ShapeDtypeStruct((N_EXPERTS * D, D_EXPERT), BF16),
                   jax.ShapeDtypeStruct((N_EXPERTS * D, D_EXPERT), BF16)),
        grid=(N_TILES,),
        in_specs=_stream_specs(l) + [
            pl.BlockSpec((None, None, 8, D), lambda i: (l, _mod_row(i), 0, 0)),
            pl.BlockSpec((None, 1, D), lambda i: (l, 0, 0)),
            pl.BlockSpec((None, D, QKV_W), lambda i: (l, 0, 0), **const),
            pl.BlockSpec((None, D, GW), lambda i: (l, 0, 0), **const),
            pl.BlockSpec((None, D, CV_IN), lambda i: (l, 0, 0), **const),
            pl.BlockSpec((TM, 128), lambda i: (jnp.where(i < LAT_TILES, i % SEQ_TILES, SEQ_TILES), 0)),
            pl.BlockSpec((TM, 128), lambda i: (jnp.where(i < LAT_TILES, i % SEQ_TILES, SEQ_TILES), 0)),
            _cast_spec(W13_ROWS, D_EXPERT, l), _cast_spec(W13_ROWS, D_EXPERT, l),
        ],
        out_specs=(pl.BlockSpec((TM, QKV_W), lambda i: (i, 0)),
                   pl.BlockSpec((TM, GW), lambda i: (i, 0)),
                   pl.BlockSpec((TM, CV_OUT), lambda i: (i, 0)),
                   _cast_out_spec(W13_ROWS, D_EXPERT), _cast_out_spec(W13_ROWS, D_EXPERT)),
        scratch_shapes=[pltpu.VMEM((TM * SLAB, 128), F32)] if l > 0 else [],
        compiler_params=_cparams(1),
        name="norm_in_proj",
    )(*stream, mod, g_norm1.reshape(DEPTH, 1, D), wqkv, wf, wcv, cos_t, sin_t, flat(w_gate_e), flat(w_up_e))


N_QBLK_LAT = N_LAT // TQ
N_QBLK = NT // TQ
QBLK_PER_SEQ = SEQ // TQ
KWIN = 3 * TQ


def _attn_heads(sink_ref, l, q_ref, keys, vals, bias, o_ref):
    lane = lax.broadcasted_iota(I32, (TQ, 128), 1)
    row2 = lax.broadcasted_iota(I32, (2 * TQ, 1), 0)
    zero = jnp.zeros((TQ, 128), BF16)
    if bias is not None:
        bias = jnp.concatenate([bias, bias], axis=0)
    for hk in range(2):
        c0, c1 = 2 * hk, 2 * hk + 1
        for p in range(2):
            kx = keys[0] if hk == p else keys[1]
            vx = vals[0] if hk == p else vals[1]
            in_half = (lane // HEAD_DIM) == p
            qs = jnp.concatenate([jnp.where(in_half, q_ref[:, c0 * 128:(c0 + 1) * 128], zero),
                                  jnp.where(in_half, q_ref[:, c1 * 128:(c1 + 1) * 128], zero)], axis=0)
            s = lax.dot_general(qs, kx, (((1,), (1,)), ((), ())), preferred_element_type=F32)
            if bias is not None:
                s = s + bias
            sink = jnp.where(row2 < TQ, sink_ref[l, 2 * c0 + p], sink_ref[l, 2 * c1 + p])
            m = jnp.maximum(jnp.max(s, axis=-1, keepdims=True), sink)
            e = jnp.exp(s - m)
            den = jnp.sum(e, axis=-1, keepdims=True) + jnp.exp(sink - m)
            o = (jnp.dot(e.astype(BF16), vx, preferred_element_type=F32) / den).astype(BF16)
            for k, c in enumerate((c0, c1)):
                lo = c * 128 + p * HEAD_DIM
                o_ref[:, lo:lo + HEAD_DIM] = o[k * TQ:(k + 1) * TQ, p * HEAD_DIM:(p + 1) * HEAD_DIM]


def _attn_kernel(sink_ref, q_ref, k_ref, v_ref, ks_ref, vs_ref, kc_ref, vc_ref, kcs_ref, vcs_ref, o_ref, *, l):
    i = pl.program_id(0)

    @pl.when(i < N_QBLK_LAT)
    def _latent():
        n = i % QBLK_PER_SEQ
        ws = pl.multiple_of(jnp.clip((n - 1) * TQ, 0, SEQ - KWIN), TQ)
        keys = [jnp.concatenate([r[pl.ds(ws, KWIN), :], c[...]], axis=0)
                for r, c in ((k_ref, kc_ref), (ks_ref, kcs_ref))]
        vals = [jnp.concatenate([r[pl.ds(ws, KWIN), :], c[...]], axis=0)
                for r, c in ((v_ref, vc_ref), (vs_ref, vcs_ref))]
        nk = KWIN + CTX
        qpos = n * TQ + lax.broadcasted_iota(I32, (TQ, nk), 0)
        col = lax.broadcasted_iota(I32, (TQ, nk), 1)
        valid = (jnp.abs(ws + col - qpos) <= WINDOW) | (col >= KWIN)
        _attn_heads(sink_ref, l, q_ref, keys, vals, jnp.where(valid, 0.0, NEG_INF), o_ref)

    @pl.when(i >= N_QBLK_LAT)
    def _context():
        _attn_heads(sink_ref, l, q_ref, [kc_ref[...], kcs_ref[...]], [vc_ref[...], vcs_ref[...]], None, o_ref)


def _attn_call(l, attn_sink, qkv):
    def seq_blk(i):
        return jnp.minimum(i // QBLK_PER_SEQ, 1)

    def ctx_blk(i):
        b = jnp.where(i < N_QBLK_LAT, i // QBLK_PER_SEQ, (i - N_QBLK_LAT) // (CTX // TQ))
        return N_LAT // CTX + b

    seq_specs = [pl.BlockSpec((SEQ, 128), functools.partial(lambda i, col: (seq_blk(i), col), col=col))
                 for col in (4, 5, 6, 7)]
    ctx_specs = [pl.BlockSpec((CTX, 128), functools.partial(lambda i, col: (ctx_blk(i), col), col=col))
                 for col in (4, 5, 6, 7)]
    n_blk = _layer_tiles(l) * TM // TQ
    return pl.pallas_call(
        functools.partial(_attn_kernel, l=l),
        out_shape=jax.ShapeDtypeStruct((n_blk * TQ, GW), BF16),
        grid=(n_blk,),
        in_specs=[pl.BlockSpec(memory_space=pltpu.SMEM),
                  pl.BlockSpec((TQ, GW), lambda i: (i, 0))] + seq_specs + ctx_specs,
        out_specs=pl.BlockSpec((TQ, GW), lambda i: (i, 0)),
        compiler_params=_cparams(1),
        name="window_attention",
    )(attn_sink, qkv, *([qkv] * 8))


FC = 256
FFT_UNROLL = 4


def _dft_tables(n):
    t = n * n
    j = np.arange(n)
    ang = 2.0 * np.pi * np.outer(j, j) / n
    cn, sn = np.cos(ang), np.sin(ang)
    m1 = np.block([[cn, sn], [-sn, cn]])
    m2 = np.concatenate([cn, sn], axis=1)
    angc = 2.0 * np.pi * np.outer(np.arange(128), np.arange(128)) / 128.0
    wc = np.concatenate([np.cos(angc), -np.sin(angc)], axis=1)
    angt = 2.0 * np.pi * np.outer(j, j) / t
    tw_re = np.broadcast_to(np.cos(angt)[:, :, None], (n, n, 128))
    tw_im = np.broadcast_to(-np.sin(angt)[:, :, None], (n, n, 128))
    def hi_lo(m):
        return jnp.stack(_split_bf16(jnp.asarray(m, F32)))

    return (hi_lo(m1), hi_lo(m2), hi_lo(wc),
            jnp.asarray(np.ascontiguousarray(tw_re), F32), jnp.asarray(np.ascontiguousarray(tw_im), F32))


def _fft_kernel(f_ref, m1_ref, m2_ref, wc_ref, twr_ref, twi_ref, o_ref, are, aim, hre, him, *, n):
    t = n * n
    ng = FC // 128
    for g in range(ng):
        fg = f_ref[:, g * 128:(g + 1) * 128]
        a = (jnp.dot(fg, wc_ref[0], preferred_element_type=F32)
             + jnp.dot(fg, wc_ref[1], preferred_element_type=F32))
        are[g] = a[:, :128]
        aim[g] = a[:, 128:]

    def table_dot(m_ref, v):
        return (jnp.dot(m_ref[0], v, preferred_element_type=F32)
                + jnp.dot(m_ref[1], v, preferred_element_type=F32))

    def gather_rows(re, im, first):
        rows = [pl.ds(first + u, n, stride=n) for u in range(FFT_UNROLL)]
        top = jnp.concatenate([re[g, r, :] for r in rows for g in range(ng)], axis=1)
        bot = jnp.concatenate([im[g, r, :] for r in rows for g in range(ng)], axis=1)
        return jnp.concatenate([top, bot], axis=0).astype(BF16)

    def stage1(it, carry):
        t1 = it * FFT_UNROLL
        h = table_dot(m1_ref, gather_rows(are, aim, t1))
        for u in range(FFT_UNROLL):
            dst = pl.ds(pl.multiple_of((t1 + u) * n, n), n)
            twr = twr_ref[t1 + u]
            twi = twi_ref[t1 + u]
            for g in range(ng):
                lanes = slice((u * ng + g) * 128, (u * ng + g + 1) * 128)
                h_re, h_im = h[:n, lanes], h[n:, lanes]
                hre[g, dst, :] = h_re * twr - h_im * twi
                him[g, dst, :] = h_re * twi + h_im * twr
        return carry

    lax.fori_loop(0, n // FFT_UNROLL, stage1, 0)

    def stage2(it, carry):
        k2 = it * FFT_UNROLL
        z = table_dot(m2_ref, gather_rows(hre, him, k2))
        z = z * ((t * 128.0) ** -0.5)
        for u in range(FFT_UNROLL):
            for g in range(ng):
                are[g, pl.ds(k2 + u, n, stride=n), :] = z[:, (u * ng + g) * 128:(u * ng + g + 1) * 128]
        return carry

    lax.fori_loop(0, n // FFT_UNROLL, stage2, 0)
    for g in range(ng):
        o_ref[:, g * 128:(g + 1) * 128] = are[g].astype(BF16)


def _fft_call(f, n, first_blk, n_seq):
    t = n * n
    m1, m2, wc, twr, twi = _dft_tables(n)
    full = lambda shape: pl.BlockSpec(shape, lambda b, h: (0,) * len(shape))
    return pl.pallas_call(
        functools.partial(_fft_kernel, n=n),
        out_shape=jax.ShapeDtypeStruct((n_seq * t, GW), BF16),
        grid=(n_seq, GW // FC),
        in_specs=[pl.BlockSpec((t, FC), lambda b, h: (first_blk + b, h)),
                  full((2, 2 * n, 2 * n)), full((2, n, 2 * n)), full((2, 128, 256)),
                  full((n, n, 128)), full((n, n, 128))],
        out_specs=pl.BlockSpec((t, FC), lambda b, h: (b, h)),
        scratch_shapes=[pltpu.VMEM((FC // 128, t, 128), F32)] * 4,
        compiler_params=_cparams(2),
        name="fourier_mix_%d" % t,
    )(f, m1, m2, wc, twr, twi)


CONV_ROWS = 32
GSH_ROWS = TM + 2 * HALO - 8


def _conv_kernel(sb_ref, p_ref, pp_ref, pn_ref, g_ref, gp_ref, gn_ref, sw_ref, cw_ref, cb_ref, ng_ref, nb_ref,
                 oc_ref, od_ref, pbuf, gbuf, gsh):
    i = pl.program_id(0)
    seq_start = jnp.logical_or(i % SEQ_TILES == 0, i >= LAT_TILES)
    seq_end = jnp.logical_or(i % SEQ_TILES == SEQ_TILES - 1, i >= LAT_TILES)
    for buf, main, prev, nxt in ((pbuf, p_ref, pp_ref, pn_ref), (gbuf, g_ref, gp_ref, gn_ref)):
        buf[0:HALO, :] = jnp.where(seq_start, 0.0, prev[...])
        buf[HALO:HALO + TM, :] = main[...]
        buf[HALO + TM:HALO + TM + HALO, :] = jnp.where(seq_end, 0.0, nxt[...])
    for s in range(8):
        gsh[s] = gbuf[s:s + GSH_ROWS, :]
    sw = sw_ref[...]
    cw = cw_ref[...]
    for r0 in range(0, TM, CONV_ROWS):
        base = HALO + r0
        acc = sw[0:1, :] * pbuf[base - 1:base - 1 + CONV_ROWS, :]
        acc = acc + sw[1:2, :] * pbuf[base:base + CONV_ROWS, :]
        acc = acc + sw[2:3, :] * pbuf[base + 1:base + 1 + CONV_ROWS, :]
        oc_ref[r0:r0 + CONV_ROWS, :] = (sb_ref[r0:r0 + CONV_ROWS, :] * acc).astype(BF16)

        z = None
        for j in range(CONF_W):
            o = base - CONF_PAD + j
            q = o - o % 8
            term = cw[j:j + 1, :] * gsh[o % 8, q:q + CONV_ROWS, :]
            z = term if z is None else z + term
        z = z + cb_ref[...]
        mu = jnp.mean(z, axis=-1, keepdims=True)
        zc = z - mu
        var = jnp.mean(zc * zc, axis=-1, keepdims=True)
        y = zc * lax.rsqrt(var + EPS) * ng_ref[...] + nb_ref[...]
        od_ref[r0:r0 + CONV_ROWS, :] = (y * jax.nn.sigmoid(y)).astype(BF16)


def _conv_call(l, cv, sconv_w, cconv_w, cconv_b, cnorm_g, cnorm_b):
    hb = TM // HALO
    last = NT // HALO - 1

    def halo_specs(col):
        return [pl.BlockSpec((TM, GW), lambda i: (i, col)),
                pl.BlockSpec((HALO, GW), lambda i: (jnp.maximum(i * hb - 1, 0), col)),
                pl.BlockSpec((HALO, GW), lambda i: (jnp.minimum((i + 1) * hb, last), col))]

    vec = lambda: pl.BlockSpec((None, 1, GW), lambda i: (l, 0, 0))
    cw = jnp.pad(cconv_w, ((0, 0), (0, 32 - CONF_W), (0, 0)))
    sw = jnp.pad(sconv_w, ((0, 0), (0, 8 - 3), (0, 0)))
    return pl.pallas_call(
        _conv_kernel,
        out_shape=(jax.ShapeDtypeStruct((_layer_tiles(l) * TM, GW), BF16),
                   jax.ShapeDtypeStruct((_layer_tiles(l) * TM, GW), BF16)),
        grid=(_layer_tiles(l),),
        in_specs=[pl.BlockSpec((TM, GW), lambda i: (i, 0))] + halo_specs(1) + halo_specs(2) + [
            pl.BlockSpec((None, 8, GW), lambda i: (l, 0, 0)),
            pl.BlockSpec((None, 32, GW), lambda i: (l, 0, 0)),
            vec(), vec(), vec()],
        out_specs=(pl.BlockSpec((TM, GW), lambda i: (i, 0)), pl.BlockSpec((TM, GW), lambda i: (i, 0))),
        scratch_shapes=[pltpu.VMEM((TM + 2 * HALO, GW), F32)] * 2 + [pltpu.VMEM((8, GSH_ROWS, GW), F32)],
        compiler_params=_cparams(1),
        name="conv_mixers",
    )(cv, cv, cv, cv, cv, cv, cv, sw, cw, cconv_b.reshape(DEPTH, 1, GW), cnorm_g.reshape(DEPTH, 1, GW),
      cnorm_b.reshape(DEPTH, 1, GW))


R_ROWS = 32
_SLOT_A = np.array([0, 2, 2, 0, 0, 1], np.int32)
_SLOT_B = np.array([1, 1, 3, 3, 2, 3], np.int32)


def _route(lg):
    gl = [lg[N_EXPERTS + g:N_EXPERTS + g + 1, :] for g in range(N_GROUPS)]
    gmax = functools.reduce(jnp.maximum, gl)
    pg = 1.0 / functools.reduce(jnp.add, [jnp.exp(v - gmax) for v in gl])
    gsel = jnp.full_like(gmax, N_GROUPS - 1).astype(I32)
    for g in range(N_GROUPS - 2, -1, -1):
        gsel = jnp.where(gl[g] == gmax, g, gsel)
    el = []
    for j in range(PER_GROUP):
        v = lg[j:j + 1, :]
        for g in range(1, N_GROUPS):
            v = jnp.where(gsel == g, lg[g * PER_GROUP + j:g * PER_GROUP + j + 1, :], v)
        el.append(v)
    v1 = functools.reduce(jnp.maximum, el)
    i1 = jnp.full_like(gsel, PER_GROUP - 1)
    for j in range(PER_GROUP - 2, -1, -1):
        i1 = jnp.where(el[j] == v1, j, i1)
    rest = [jnp.where(i1 == j, -jnp.inf, el[j]) for j in range(PER_GROUP)]
    v2 = functools.reduce(jnp.maximum, rest)
    i2 = jnp.full_like(gsel, PER_GROUP - 1)
    for j in range(PER_GROUP - 2, -1, -1):
        i2 = jnp.where(jnp.logical_and(rest[j] == v2, i1 != j), j, i2)
    e2 = jnp.exp(v2 - v1)
    w1 = pg / (1.0 + e2)
    w2 = pg * e2 / (1.0 + e2)
    lo = jnp.minimum(i1, i2)
    hi = jnp.maximum(i1, i2)
    w_lo = jnp.where(i1 < i2, w1, w2)
    w_hi = jnp.where(i1 < i2, w2, w1)
    pair = jnp.where(lo == 0, jnp.where(hi == 1, 0, jnp.where(hi == 2, 4, 3)),
                     jnp.where(lo == 1, jnp.where(hi == 2, 1, 5), 2))
    cls = gsel * 6 + pair
    a_is_hi = pair == 1
    return cls, jnp.where(a_is_hi, w_hi, w_lo), jnp.where(a_is_hi, w_lo, w_hi)


def _rows_to_slabs(v, stage_ref, out_ref):
    n = v.shape[0]
    for c in range(SLAB):
        stage_ref[pl.ds(c, n, stride=SLAB), :] = v[:, c * 128:(c + 1) * 128]
    out_ref[...] = stage_ref[...].astype(out_ref.dtype)


def _slabs_to_rows(src_ref, stage_ref, n):
    stage_ref[...] = src_ref[...].astype(F32)
    return jnp.concatenate([stage_ref[pl.ds(c, n, stride=SLAB), :] for c in range(SLAB)], axis=1)


def _out_kernel(*refs, l):
    oa_ref, obl_ref, obc_ref, oc_ref, od_ref, w_ref = refs[:6]
    stream, refs = refs[6:6 + _n_stream(l)], refs[6 + _n_stream(l):]
    (mod_ref, g_ref, wrh_ref, wrl_ref, br_ref, tri_ref, w2_ref,
     x1_ref, hp_ref, cls_ref, gw_ref, cnt_ref, w2b_ref, carry, stage) = refs
    i = pl.program_id(0)
    w2b_ref[...] = w2_ref[...].astype(BF16)

    @pl.when(i == 0)
    def _():
        carry[...] = jnp.zeros_like(carry)

    ob = jnp.where(i < LAT_TILES, obl_ref[...], obc_ref[...])
    y = jnp.dot(oa_ref[...], w_ref[0:GW, :], preferred_element_type=F32)
    y = y + jnp.dot(ob, w_ref[GW:2 * GW, :], preferred_element_type=F32)
    y = y + jnp.dot(oc_ref[...], w_ref[2 * GW:3 * GW, :], preferred_element_type=F32)
    y = y + jnp.dot(od_ref[...], w_ref[3 * GW:4 * GW, :], preferred_element_type=F32)
    x1 = _stream_tile(l, stream, stage) + mod_ref[2:3, :] * y
    x1_ref[...] = x1
    h2 = _rmsnorm_mod(x1, g_ref[...], mod_ref[3:4, :], mod_ref[4:5, :])
    _rows_to_slabs(h2, stage, hp_ref)

    h_hi, h_lo = _split_bf16(h2)
    lg = jnp.dot(h_hi, wrh_ref[...], preferred_element_type=F32)
    lg = lg + jnp.dot(h_lo, wrh_ref[...], preferred_element_type=F32)
    lg = lg + jnp.dot(h_hi, wrl_ref[...], preferred_element_type=F32)
    lg = jnp.transpose(lg + br_ref[...])[0:R_ROWS, :]
    cls, w_a, w_b = _route(lg)

    onehot = jnp.where(lax.broadcasted_iota(I32, (R_ROWS, TM), 0) == cls, 1.0, 0.0)
    prefix = jnp.dot(onehot.astype(BF16), tri_ref[...], preferred_element_type=F32)
    rank = jnp.sum(onehot * (prefix - 1.0 + carry[:, 0:1]), axis=0, keepdims=True)
    total = carry[...] + jnp.sum(onehot, axis=1, keepdims=True)
    carry[...] = total
    cnt_ref[...] = total.astype(I32)
    cls_ref[...] = jnp.concatenate([cls, rank.astype(I32), jnp.zeros((6, TM), I32)], axis=0)
    gw_ref[...] = jnp.concatenate([w_a, w_b, jnp.zeros((6, TM), F32)], axis=0)


def _out_call(l, o_a, o_b_lat, o_b_ctx, o_c, o_d, w_out, stream, mod, g_norm2, wr_hi, wr_lo, b_route, w_down_e):
    row = lambda: pl.BlockSpec((TM, GW), lambda i: (i, 0))
    tri = jnp.asarray(np.triu(np.ones((TM, TM), np.float32)), BF16)
    n_rows = _layer_tiles(l) * TM
    return pl.pallas_call(
        functools.partial(_out_kernel, l=l),
        out_shape=(jax.ShapeDtypeStruct((n_rows, D), F32), jax.ShapeDtypeStruct((n_rows * SLAB, 128), BF16),
                   jax.ShapeDtypeStruct((8, n_rows), I32), jax.ShapeDtypeStruct((8, n_rows), F32),
                   jax.ShapeDtypeStruct((R_ROWS, 128), I32),
                   jax.ShapeDtypeStruct((N_EXPERTS * D_EXPERT, D), BF16)),
        grid=(_layer_tiles(l),),
        in_specs=[row(),
                  pl.BlockSpec((TM, GW), lambda i: (jnp.minimum(i, LAT_TILES - 1), 0)),
                  pl.BlockSpec((TM, GW), lambda i: (jnp.maximum(i - LAT_TILES, 0), 0)),
                  row(), row(),
                  pl.BlockSpec((None, D, D), lambda i: (l, 0, 0), pipeline_mode=pl.Buffered(1))]
        + _stream_specs(l) + [
                  pl.BlockSpec((None, None, 8, D), lambda i: (l, _mod_row(i), 0, 0)),
                  pl.BlockSpec((None, 1, D), lambda i: (l, 0, 0)),
                  pl.BlockSpec((None, D, 128), lambda i: (l, 0, 0)),
                  pl.BlockSpec((None, D, 128), lambda i: (l, 0, 0)),
                  pl.BlockSpec((None, 1, 128), lambda i: (l, 0, 0)),
                  pl.BlockSpec((TM, TM), lambda i: (0, 0)),
                  _cast_spec(W2_ROWS, D, l)],
        out_specs=(pl.BlockSpec((TM, D), lambda i: (i, 0)),
                   pl.BlockSpec((TM * SLAB, 128), lambda i: (i, 0)),
                   pl.BlockSpec((8, TM), lambda i: (0, i)), pl.BlockSpec((8, TM), lambda i: (0, i)),
                   pl.BlockSpec((R_ROWS, 128), lambda i: (0, 0)),
                   _cast_out_spec(W2_ROWS, D)),
        scratch_shapes=[pltpu.VMEM((R_ROWS, 128), F32), pltpu.VMEM((TM * SLAB, 128), F32)],
        compiler_params=_cparams(1),
        name="out_proj_route",
    )(o_a, o_b_lat, o_b_ctx, o_c, o_d, w_out, *stream, mod, g_norm2.reshape(DEPTH, 1, D), wr_hi, wr_lo,
      b_route, tri, w_down_e.reshape(DEPTH * N_EXPERTS * D_EXPERT, D))


ROW_UNROLL = 8


def _gate_column(g_ref, src_ref, base):
    sub = lax.broadcasted_iota(I32, (8, 128), 0)
    pieces = []
    for g in range(TME // 8):
        v = jnp.zeros((8, 128), F32)
        for k in range(8):
            v = jnp.where(sub == k, g_ref[src_ref[base + 8 * g + k]], v)
        pieces.append(v)
    return jnp.concatenate(pieces, axis=0)


DMA_PIECES = 8


def _expert_kernel(ea_ref, eb_ref, nvalid_ref, zero_ref, starts_ref, counts_ref, cls_ref, rank_ref, gwa_ref, gwb_ref, hp_hbm, w1a_ref, w3a_ref, w2a_ref, w1b_ref,
                   w3b_ref, w2b_ref, yt_hbm, src_ref, dst_ref, hbuf0, hbuf1, ybuf0, ybuf1, stage, gsem, ssem, zsem,
                   *, n_tok):
    i = pl.program_id(0)
    nvalid = nvalid_ref[0]
    hbufs, ybufs = (hbuf0, hbuf1), (ybuf0, ybuf1)

    def slab(ref, row):
        return ref.at[pl.ds(pl.multiple_of(row * SLAB, SLAB), SLAB), :]

    def gather_copy(tile, sl, j, z=0):
        return pltpu.make_async_copy(slab(hp_hbm, src_ref[tile * TME + j] + z), slab(hbufs[sl], j), gsem.at[sl])

    def scatter_copy(tile, sl, j, z=0):
        return pltpu.make_async_copy(slab(ybufs[sl], j), slab(yt_hbm, dst_ref[(tile + 1) * TME + j] + z),
                                     ssem.at[sl])

    def for_rows(fn):
        def body(j, carry):
            fn(j)
            return carry

        lax.fori_loop(0, TME, body, 0, unroll=ROW_UNROLL)

    @pl.when(i == 0)
    def _():
        def unused(p, carry):
            src_ref[p] = n_tok - 1
            dst_ref[p + TME] = n_tok + ((p + TME) & (2 * TME - 1))
            return carry

        def dummy_tile(j, carry):
            dst_ref[j] = n_tok + j
            src_ref[nvalid * TME + j] = n_tok - 1
            return carry

        lax.fori_loop(0, TME, dummy_tile, 0, unroll=ROW_UNROLL)

        def class_tail(c, carry):
            lo = starts_ref[c] + counts_ref[c]
            hi = starts_ref[c] + ((counts_ref[c] + TME - 1) // TME) * TME
            lax.fori_loop(lo, hi, unused, 0)
            return carry

        lax.fori_loop(0, N_CLASSES, class_tail, 0)

        def place(t, carry):
            p = starts_ref[cls_ref[t]] + rank_ref[t]
            src_ref[p] = t
            dst_ref[p + TME] = t
            return carry

        lax.fori_loop(0, n_tok, place, 0, unroll=ROW_UNROLL)
        for sl in range(2):
            ybufs[sl][...] = jnp.zeros_like(ybufs[sl])
            spare = pltpu.make_async_copy(ybufs[sl], yt_hbm.at[pl.ds((n_tok + sl * TME) * SLAB, TME * SLAB), :],
                                          zsem)
            spare.start()
            spare.wait()
        for_rows(lambda j: gather_copy(0, 0, j).start())

    def tile_body(sl):
        for_rows(lambda j: gather_copy(0, sl, 0).wait())
        h = _slabs_to_rows(hbufs[sl], stage, TME).astype(BF16)
        pieces = iter(range(DMA_PIECES))

        def paced_dot(a, b):
            r = jnp.dot(a, b, preferred_element_type=F32)
            piece = next(pieces, None)
            if piece is None:
                return r
            z = pltpu.bitcast(r[0:8, 0:128], I32)[0, 0] * zero_ref[0]
            first = piece * (TME // DMA_PIECES)
            for j in range(first, first + TME // DMA_PIECES):
                gather_copy(i + 1, 1 - sl, j, z).start()
                scatter_copy(i - 1, 1 - sl, j, z).start(priority=1)
            return r

        y = [None] * (D // D_EXPERT)
        for g_ref, w1_ref, w3_ref, w2_ref in ((gwa_ref, w1a_ref, w3a_ref, w2a_ref),
                                              (gwb_ref, w1b_ref, w3b_ref, w2b_ref)):
            half = D_EXPERT // 2
            gate = jnp.concatenate([paced_dot(h, w1_ref[:, c * half:(c + 1) * half]) for c in range(2)], axis=1)
            up = jnp.concatenate([paced_dot(h, w3_ref[:, c * half:(c + 1) * half]) for c in range(2)], axis=1)
            gcol = _gate_column(g_ref, src_ref, i * TME)
            hid = (gate * jax.nn.sigmoid(gate) * up * jnp.concatenate([gcol] * (D_EXPERT // 128), axis=1)).astype(BF16)
            for c in range(len(y)):
                yc = paced_dot(hid, w2_ref[:, c * D_EXPERT:(c + 1) * D_EXPERT])
                y[c] = yc if y[c] is None else y[c] + yc
        y = jnp.concatenate(y, axis=1)

        @pl.when(i >= 1)
        def _():
            for_rows(lambda j: scatter_copy(0, sl, 0).wait())

        _rows_to_slabs(y, stage, ybufs[sl])

        @pl.when(i == nvalid - 1)
        def _():
            for_rows(lambda j: scatter_copy(i, sl, j).start())
            for_rows(lambda j: scatter_copy(0, 1 - sl, 0).wait())
            for_rows(lambda j: scatter_copy(0, sl, 0).wait())
            for_rows(lambda j: gather_copy(0, 1 - sl, 0).wait())

    for sl in range(2):
        @pl.when(jnp.logical_and(i < nvalid, i % 2 == sl))
        def _(sl=sl):
            tile_body(sl)


def _expert_call(tile_ea, tile_eb, nvalid, starts, counts, cls, rank, gwa, gwb, hp, w1b, w3b, w2b):
    n_tok = cls.shape[0]
    up_a = pl.BlockSpec((None, D, D_EXPERT), lambda i, ea, eb, *_: (ea[i], 0, 0))
    up_b = pl.BlockSpec((None, D, D_EXPERT), lambda i, ea, eb, *_: (eb[i], 0, 0))
    down_a = pl.BlockSpec((None, D_EXPERT, D), lambda i, ea, eb, *_: (ea[i], 0, 0))
    down_b = pl.BlockSpec((None, D_EXPERT, D), lambda i, ea, eb, *_: (eb[i], 0, 0))
    return pl.pallas_call(
        functools.partial(_expert_kernel, n_tok=n_tok),
        out_shape=jax.ShapeDtypeStruct(((n_tok + 2 * TME) * SLAB, 128), BF16),
        grid_spec=pltpu.PrefetchScalarGridSpec(
            num_scalar_prefetch=10,
            grid=(P_TILES,),
            in_specs=[pl.BlockSpec(memory_space=pl.ANY),
                      up_a, up_a, down_a, up_b, up_b, down_b],
            out_specs=pl.BlockSpec(memory_space=pl.ANY),
            scratch_shapes=[pltpu.SMEM((P_ROWS + TME,), I32), pltpu.SMEM((P_ROWS + TME,), I32)]
            + [pltpu.VMEM((TME * SLAB, 128), BF16)] * 4
            + [pltpu.VMEM((TME * SLAB, 128), F32),
                            pltpu.SemaphoreType.DMA((2,)), pltpu.SemaphoreType.DMA((2,)),
                            pltpu.SemaphoreType.DMA]),
        compiler_params=_cparams(1),
        name="moe_experts",
    )(tile_ea, tile_eb, nvalid, jnp.zeros((1,), I32), starts, counts, cls, rank, gwa, gwb, hp, w1b, w3b, w2b,
      w1b, w3b, w2b)


def _final_kernel(x1_ref, yt_ref, mod_ref, gf_ref, o_ref, stage):
    x2 = _stream_tile(DEPTH, (x1_ref, yt_ref, mod_ref), stage)
    ms = jnp.mean(x2 * x2, axis=-1, keepdims=True)
    o_ref[...] = x2 * lax.rsqrt(ms + EPS) * gf_ref[...]


def _final_call(x1, mod, g_final, yt):
    return pl.pallas_call(
        _final_kernel,
        out_shape=jax.ShapeDtypeStruct((N_LAT, D), F32),
        grid=(LAT_TILES,),
        in_specs=_stream_specs(DEPTH) + [pl.BlockSpec((1, D), lambda i: (0, 0))],
        out_specs=pl.BlockSpec((TM, D), lambda i: (i, 0)),
        scratch_shapes=[pltpu.VMEM((TM * SLAB, 128), F32)],
        compiler_params=_cparams(1),
        name="final_norm",
    )(x1, yt, mod, g_final.reshape(1, D))


def _slot_plan(counts):
    padded = ((counts + TME - 1) // TME) * TME
    ends = jnp.cumsum(padded)
    starts = ends - padded
    tile_start = jnp.arange(P_TILES, dtype=I32) * TME
    tile_cls = jnp.minimum(jnp.sum((tile_start[:, None] >= ends[None, :]).astype(I32), axis=1), N_CLASSES - 1)
    nvalid = (ends[-1] // TME).reshape(1)
    grp = tile_cls // 6
    tile_ea = grp * PER_GROUP + jnp.asarray(_SLOT_A)[tile_cls % 6]
    tile_eb = grp * PER_GROUP + jnp.asarray(_SLOT_B)[tile_cls % 6]
    return starts, tile_ea, tile_eb, nvalid


def _rope_tables():
    t = jnp.arange(SEQ, dtype=I32)
    row = (t // GRID_W).astype(F32)
    col = (t % GRID_W).astype(F32)
    half = HEAD_DIM // 4
    freqs = ROPE_BASE ** (-jnp.arange(0, 2 * half, 2, dtype=F32) / (2 * half))
    ang_r = row[:, None] * freqs
    ang_c = col[:, None] * freqs
    cos_h = jnp.concatenate([jnp.cos(ang_r), jnp.cos(ang_r), jnp.cos(ang_c), jnp.cos(ang_c)], axis=1)
    sin_h = jnp.concatenate([-jnp.sin(ang_r), jnp.sin(ang_r), -jnp.sin(ang_c), jnp.sin(ang_c)], axis=1)
    cos_t = jnp.concatenate([jnp.tile(cos_h, (1, 2)), jnp.ones((TM, 128), F32)], axis=0)
    sin_t = jnp.concatenate([jnp.tile(sin_h, (1, 2)), jnp.zeros((TM, 128), F32)], axis=0)
    return cos_t, sin_t


def kernel(x, c, ctx, c_ctx, w_mod, b_mod, g_norm1, g_norm2, w_in, w_out, attn_sink, sconv_w, cconv_w, cconv_b,
           cnorm_g, cnorm_b, w_route_group, b_route_group, w_route_expert, b_route_expert, w_gate_e, w_up_e,
           w_down_e, g_final):
    c8 = jnp.zeros((8, D), F32).at[0:BATCH].set(c).at[BATCH].set(c_ctx)
    w_route = jnp.concatenate([w_route_expert, w_route_group,
                               jnp.zeros((DEPTH, D, 128 - N_EXPERTS - N_GROUPS), F32)], axis=2)
    wr_hi = w_route.astype(BF16)
    wr_lo = (w_route - wr_hi.astype(F32)).astype(BF16)
    b_route = jnp.concatenate([b_route_expert, b_route_group,
                               jnp.zeros((DEPTH, 128 - N_EXPERTS - N_GROUPS), F32)], axis=1).reshape(DEPTH, 1, 128)
    cos_t, sin_t = _rope_tables()

    mod, wqkv, wf, wcv, w_out_b = _mod_call(c8, w_mod, b_mod, w_in, w_out)
    mod = mod.reshape(DEPTH, 8, N_MOD, D)[:, :3]
    mod = jnp.pad(mod, ((0, 0), (0, 0), (0, 8 - N_MOD), (0, 0)))

    stream = (x.reshape(N_LAT, D), ctx.reshape(N_CTX, D))
    for l in range(DEPTH):
        last = l == DEPTH - 1
        qkv, f, cv, w1b, w3b = _in_call(l, stream, mod, g_norm1, wqkv, wf, wcv, cos_t, sin_t, w_gate_e, w_up_e)
        o_a = _attn_call(l, attn_sink, qkv)
        o_b_lat = _fft_call(f, 64, 0, BATCH)
        o_b_ctx = o_b_lat if last else _fft_call(f, 16, N_LAT // CTX, BATCH)
        o_c, o_d = _conv_call(l, cv, sconv_w, cconv_w, cconv_b, cnorm_g, cnorm_b)
        x1, hp, cls8, gw8, cnt, w2b = _out_call(l, o_a, o_b_lat, o_b_ctx, o_c, o_d, w_out_b, stream, mod, g_norm2,
                                                wr_hi, wr_lo, b_route, w_down_e)
        counts = cnt[:N_CLASSES, 0]
        starts, tile_ea, tile_eb, nvalid = _slot_plan(counts)
        yt = _expert_call(tile_ea, tile_eb, nvalid, starts, counts, cls8[0], cls8[1], gw8[0], gw8[1], hp,
                          w1b.reshape(N_EXPERTS, D, D_EXPERT), w3b.reshape(N_EXPERTS, D, D_EXPERT),
                          w2b.reshape(N_EXPERTS, D_EXPERT, D))
        stream = (x1, yt, mod)
    return _final_call(x1, mod, g_final, yt).reshape(BATCH, SEQ, D)
```

```python
import functools

import numpy as np
import jax
import jax.numpy as jnp
from jax import lax
from jax.experimental import pallas as pl
from jax.experimental.pallas import tpu as pltpu

F32 = jnp.float32
BF16 = jnp.bfloat16
I32 = jnp.int32

D = 2048
BATCH = 2
SEQ = 4096
CTX = 256
DEPTH = 2
GRID_W = 64
N_LAT = BATCH * SEQ
N_CTX = BATCH * CTX
NT = N_LAT + N_CTX
GW = 512
HEAD_DIM = 64
N_HEADS = 8
WINDOW = 128
ROPE_BASE = 10000.0
N_MOD = 6
EPS = 1e-6
NEG_INF = -1e30
N_GROUPS = 4
PER_GROUP = 4
N_EXPERTS = 16
D_EXPERT = 512
N_CLASSES = N_GROUPS * 6
CONF_W = 31
CONF_PAD = 15

TM = 256
N_TILES = NT // TM
LAT_TILES = N_LAT // TM
SEQ_TILES = SEQ // TM
TQ = 128
HALO = 16
TME = 128
P_ROWS = NT + N_CLASSES * TME
P_TILES = P_ROWS // TME
SLAB = D // 128
VMEM_LIMIT = 56 * 1024 * 1024


def _cparams(n_axes=1, **kw):
    return pltpu.CompilerParams(dimension_semantics=("arbitrary",) * n_axes,
                                vmem_limit_bytes=VMEM_LIMIT, **kw)


def _split_bf16(v):
    hi = v.astype(BF16)
    lo = (v - hi.astype(F32)).astype(BF16)
    return hi, lo


def _mod_row(i):
    return jnp.minimum(i // SEQ_TILES, 2)


MOD_TN = 1536


MOD_STEPS = N_MOD * D // MOD_TN
W_ROWS = D // MOD_STEPS


def _mod_kernel(c_ref, w_ref, b_ref, win_ref, wout_ref, o_ref, wqkv_ref, wf_ref, wcv_ref, woutb_ref):
    w_in = win_ref[...]
    wqkv_ref[:, 0:768] = w_in[:, 0:768].astype(BF16)
    wqkv_ref[:, 768:896] = pltpu.roll(w_in[:, 512:640], HEAD_DIM, 1).astype(BF16)
    wqkv_ref[:, 896:1024] = pltpu.roll(w_in[:, 640:768], HEAD_DIM, 1).astype(BF16)
    wf_ref[...] = w_in[:, 768:768 + GW].astype(BF16)
    wcv_ref[...] = w_in[:, 768 + GW:].astype(BF16)
    woutb_ref[...] = wout_ref[...].astype(BF16)
    c = c_ref[...]
    s = c * jax.nn.sigmoid(c)
    s_hi, s_lo = _split_bf16(s)
    w = w_ref[...].astype(BF16)
    acc = jnp.dot(s_hi, w, preferred_element_type=F32) + jnp.dot(s_lo, w, preferred_element_type=F32)
    o_ref[...] = acc + b_ref[...]


def _mod_call(c8, w_mod, b_mod, w_in, w_out):
    n = N_MOD * D
    rows = lambda cols: pl.BlockSpec((None, W_ROWS, cols), lambda l, j: (l, j, 0))
    return pl.pallas_call(
        _mod_kernel,
        out_shape=(jax.ShapeDtypeStruct((DEPTH, 8, n), F32),
                   jax.ShapeDtypeStruct((DEPTH, D, QKV_W), BF16), jax.ShapeDtypeStruct((DEPTH, D, GW), BF16),
                   jax.ShapeDtypeStruct((DEPTH, D, CV_IN), BF16), jax.ShapeDtypeStruct((DEPTH, D, D), BF16)),
        grid=(DEPTH, MOD_STEPS),
        in_specs=[
            pl.BlockSpec((8, D), lambda l, j: (0, 0)),
            pl.BlockSpec((None, D, MOD_TN), lambda l, j: (l, 0, j)),
            pl.BlockSpec((None, 1, MOD_TN), lambda l, j: (l, 0, j)),
            rows(w_in.shape[-1]), rows(D),
        ],
        out_specs=(pl.BlockSpec((None, 8, MOD_TN), lambda l, j: (l, 0, j)),
                   rows(QKV_W), rows(GW), rows(CV_IN), rows(D)),
        compiler_params=_cparams(2),
        name="adaln_mod",
    )(c8, w_mod, b_mod.reshape(DEPTH, 1, n), w_in, w_out)


QKV_W = 1024
CV_IN = 5 * GW
CV_OUT = 3 * GW


def _rmsnorm_mod(x, g, shift, scale):
    ms = jnp.mean(x * x, axis=-1, keepdims=True)
    y = x * lax.rsqrt(ms + EPS) * g
    return y * (1.0 + scale) + shift


def _stream_specs(l):
    if l == 0:
        return [pl.BlockSpec((TM, D), lambda i, *_: (jnp.minimum(i, LAT_TILES - 1), 0)),
                pl.BlockSpec((TM, D), lambda i, *_: (jnp.maximum(i - LAT_TILES, 0), 0))]
    return [pl.BlockSpec((TM, D), lambda i, *_: (i, 0)),
            pl.BlockSpec((TM * SLAB, 128), lambda i, *_: (i, 0)),
            pl.BlockSpec((None, None, 8, D), lambda i, *_: (l - 1, _mod_row(i), 0, 0))]


def _stream_tile(l, refs, stage):
    if l == 0:
        xl_ref, xc_ref = refs
        return jnp.where(pl.program_id(0) < LAT_TILES, xl_ref[...], xc_ref[...])
    x1_ref, yt_ref, mod_prev_ref = refs
    return x1_ref[...] + mod_prev_ref[5:6, :] * _slabs_to_rows(yt_ref, stage, TM)


def _n_stream(l):
    return 2 if l == 0 else 3


def _layer_tiles(l):
    return LAT_TILES if l == DEPTH - 1 else N_TILES


CAST_STEPS = 32


def _cast_spec(rows, cols, l):
    return pl.BlockSpec((rows, cols), lambda i, *_: (l * CAST_STEPS + jnp.minimum(i, CAST_STEPS - 1), 0))


def _cast_out_spec(rows, cols):
    return pl.BlockSpec((rows, cols), lambda i, *_: (jnp.minimum(i, CAST_STEPS - 1), 0))


def _in_kernel(*refs, l):
    stream, refs = refs[:_n_stream(l)], refs[_n_stream(l):]
    (mod_ref, g_ref, wqkv_ref, wf_ref, wcv_ref, cos_ref, sin_ref, w1_ref, w3_ref,
     qkv_ref, f_ref, cv_ref, w1b_ref, w3b_ref) = refs[:14]
    stage = refs[14] if l > 0 else None
    w1b_ref[...] = w1_ref[...].astype(BF16)
    w3b_ref[...] = w3_ref[...].astype(BF16)
    h = _rmsnorm_mod(_stream_tile(l, stream, stage), g_ref[...], mod_ref[0:1, :], mod_ref[1:2, :])
    hb = h.astype(BF16)
    qkv = jnp.dot(hb, wqkv_ref[...], preferred_element_type=F32)
    cos = cos_ref[...]
    sin = sin_ref[...]
    lane = lax.broadcasted_iota(I32, (TM, 128), 1)
    first_half = (lane % 32) < 16

    def rope(t):
        rot = jnp.where(first_half, pltpu.roll(t, 112, 1), pltpu.roll(t, 16, 1))
        return t * cos + rot * sin

    scale = HEAD_DIM ** -0.5
    for c in range(4):
        qkv_ref[:, c * 128:(c + 1) * 128] = (rope(qkv[:, c * 128:(c + 1) * 128]) * scale).astype(BF16)
    qkv_ref[:, 512:640] = rope(qkv[:, 512:640]).astype(BF16)
    qkv_ref[:, 640:768] = qkv[:, 640:768].astype(BF16)
    qkv_ref[:, 768:896] = rope(qkv[:, 768:896]).astype(BF16)
    qkv_ref[:, 896:1024] = qkv[:, 896:1024].astype(BF16)

    f_ref[...] = jnp.dot(hb, wf_ref[...], preferred_element_type=F32).astype(BF16)

    cv = jnp.dot(hb, wcv_ref[...], preferred_element_type=F32)
    cv_ref[:, 0:GW] = cv[:, 0:GW]
    cv_ref[:, GW:2 * GW] = cv[:, GW:2 * GW] * cv[:, 2 * GW:3 * GW]
    cv_ref[:, 2 * GW:3 * GW] = cv[:, 3 * GW:4 * GW] * jax.nn.sigmoid(cv[:, 4 * GW:5 * GW])


W13_ROWS = N_EXPERTS * D // CAST_STEPS
W2_ROWS = N_EXPERTS * D_EXPERT // CAST_STEPS


def _in_call(l, stream, mod, g_norm1, wqkv, wf, wcv, cos_t, sin_t, w_gate_e, w_up_e):
    const = dict(pipeline_mode=pl.Buffered(1))
    flat = lambda w: w.reshape(DEPTH * N_EXPERTS * D, D_EXPERT)
    return pl.pallas_call(
        functools.partial(_in_kernel, l=l),
        out_shape=(jax.ShapeDtypeStruct((NT, QKV_W), BF16),
                   jax.ShapeDtypeStruct((NT, GW), BF16),
                   jax.ShapeDtypeStruct((NT, CV_OUT), F32),
                   jax.ShapeDtypeStruct((N_EXPERTS * D, D_EXPERT), BF16),
                   jax.ShapeDtypeStruct((N_EXPERTS * D, D_EXPERT), BF16)),
        grid=(N_TILES,),
        in_specs=_stream_specs(l) + [
            pl.BlockSpec((None, None, 8, D), lambda i: (l, _mod_row(i), 0, 0)),
            pl.BlockSpec((None, 1, D), lambda i: (l, 0, 0)),
            pl.BlockSpec((None, D, QKV_W), lambda i: (l, 0, 0), **const),
            pl.BlockSpec((None, D, GW), lambda i: (l, 0, 0), **const),
            pl.BlockSpec((None, D, CV_IN), lambda i: (l, 0, 0), **const),
            pl.BlockSpec((TM, 128), lambda i: (jnp.where(i < LAT_TILES, i % SEQ_TILES, SEQ_TILES), 0)),
            pl.BlockSpec((TM, 128), lambda i: (jnp.where(i < LAT_TILES, i % SEQ_TILES, SEQ_TILES), 0)),
            _cast_spec(W13_ROWS, D_EXPERT, l), _cast_spec(W13_ROWS, D_EXPERT, l),
        ],
        out_specs=(pl.BlockSpec((TM, QKV_W), lambda i: (i, 0)),
                   pl.BlockSpec((TM, GW), lambda i: (i, 0)),
                   pl.BlockSpec((TM, CV_OUT), lambda i: (i, 0)),
                   _cast_out_spec(W13_ROWS, D_EXPERT), _cast_out_spec(W13_ROWS, D_EXPERT)),
        scratch_shapes=[pltpu.VMEM((TM * SLAB, 128), F32)] if l > 0 else [],
        compiler_params=_cparams(1),
        name="norm_in_proj",
    )(*stream, mod, g_norm1.reshape(DEPTH, 1, D), wqkv, wf, wcv, cos_t, sin_t, flat(w_gate_e), flat(w_up_e))


N_QBLK_LAT = N_LAT // TQ
N_QBLK = NT // TQ
QBLK_PER_SEQ = SEQ // TQ
KWIN = 3 * TQ


def _attn_heads(sink_ref, l, q_ref, keys, vals, bias, o_ref):
    lane = lax.broadcasted_iota(I32, (TQ, 128), 1)
    row2 = lax.broadcasted_iota(I32, (2 * TQ, 1), 0)
    zero = jnp.zeros((TQ, 128), BF16)
    if bias is not None:
        bias = jnp.concatenate([bias, bias], axis=0)
    for hk in range(2):
        c0, c1 = 2 * hk, 2 * hk + 1
        for p in range(2):
            kx = keys[0] if hk == p else keys[1]
            vx = vals[0] if hk == p else vals[1]
            in_half = (lane // HEAD_DIM) == p
            qs = jnp.concatenate([jnp.where(in_half, q_ref[:, c0 * 128:(c0 + 1) * 128], zero),
                                  jnp.where(in_half, q_ref[:, c1 * 128:(c1 + 1) * 128], zero)], axis=0)
            s = lax.dot_general(qs, kx, (((1,), (1,)), ((), ())), preferred_element_type=F32)
            if bias is not None:
                s = s + bias
            sink = jnp.where(row2 < TQ, sink_ref[l, 2 * c0 + p], sink_ref[l, 2 * c1 + p])
            m = jnp.maximum(jnp.max(s, axis=-1, keepdims=True), sink)
            e = jnp.exp(s - m)
            den = jnp.sum(e, axis=-1, keepdims=True) + jnp.exp(sink - m)
            o = (jnp.dot(e.astype(BF16), vx, preferred_element_type=F32) / den).astype(BF16)
            for k, c in enumerate((c0, c1)):
                lo = c * 128 + p * HEAD_DIM
                o_ref[:, lo:lo + HEAD_DIM] = o[k * TQ:(k + 1) * TQ, p * HEAD_DIM:(p + 1) * HEAD_DIM]


def _attn_kernel(sink_ref, q_ref, k_ref, v_ref, ks_ref, vs_ref, kc_ref, vc_ref, kcs_ref, vcs_ref, o_ref, *, l):
    i = pl.program_id(0)

    @pl.when(i < N_QBLK_LAT)
    def _latent():
        n = i % QBLK_PER_SEQ
        ws = pl.multiple_of(jnp.clip((n - 1) * TQ, 0, SEQ - KWIN), TQ)
        keys = [jnp.concatenate([r[pl.ds(ws, KWIN), :], c[...]], axis=0)
                for r, c in ((k_ref, kc_ref), (ks_ref, kcs_ref))]
        vals = [jnp.concatenate([r[pl.ds(ws, KWIN), :], c[...]], axis=0)
                for r, c in ((v_ref, vc_ref), (vs_ref, vcs_ref))]
        nk = KWIN + CTX
        qpos = n * TQ + lax.broadcasted_iota(I32, (TQ, nk), 0)
        col = lax.broadcasted_iota(I32, (TQ, nk), 1)
        valid = (jnp.abs(ws + col - qpos) <= WINDOW) | (col >= KWIN)
        _attn_heads(sink_ref, l, q_ref, keys, vals, jnp.where(valid, 0.0, NEG_INF), o_ref)

    @pl.when(i >= N_QBLK_LAT)
    def _context():
        _attn_heads(sink_ref, l, q_ref, [kc_ref[...], kcs_ref[...]], [vc_ref[...], vcs_ref[...]], None, o_ref)


def _attn_call(l, attn_sink, qkv):
    def seq_blk(i):
        return jnp.minimum(i // QBLK_PER_SEQ, 1)

    def ctx_blk(i):
        b = jnp.where(i < N_QBLK_LAT, i // QBLK_PER_SEQ, (i - N_QBLK_LAT) // (CTX // TQ))
        return N_LAT // CTX + b

    seq_specs = [pl.BlockSpec((SEQ, 128), functools.partial(lambda i, col: (seq_blk(i), col), col=col))
                 for col in (4, 5, 6, 7)]
    ctx_specs = [pl.BlockSpec((CTX, 128), functools.partial(lambda i, col: (ctx_blk(i), col), col=col))
                 for col in (4, 5, 6, 7)]
    n_blk = _layer_tiles(l) * TM // TQ
    return pl.pallas_call(
        functools.partial(_attn_kernel, l=l),
        out_shape=jax.ShapeDtypeStruct((n_blk * TQ, GW), BF16),
        grid=(n_blk,),
        in_specs=[pl.BlockSpec(memory_space=pltpu.SMEM),
                  pl.BlockSpec((TQ, GW), lambda i: (i, 0))] + seq_specs + ctx_specs,
        out_specs=pl.BlockSpec((TQ, GW), lambda i: (i, 0)),
        compiler_params=_cparams(1),
        name="window_attention",
    )(attn_sink, qkv, *([qkv] * 8))


FC = 256
FFT_UNROLL = 4


def _dft_tables(n):
    t = n * n
    j = np.arange(n)
    ang = 2.0 * np.pi * np.outer(j, j) / n
    cn, sn = np.cos(ang), np.sin(ang)
    m1 = np.block([[cn, sn], [-sn, cn]])
    m2 = np.concatenate([cn, sn], axis=1)
    angc = 2.0 * np.pi * np.outer(np.arange(128), np.arange(128)) / 128.0
    wc = np.concatenate([np.cos(angc), -np.sin(angc)], axis=1)
    angt = 2.0 * np.pi * np.outer(j, j) / t
    tw_re = np.broadcast_to(np.cos(angt)[:, :, None], (n, n, 128))
    tw_im = np.broadcast_to(-np.sin(angt)[:, :, None], (n, n, 128))
    def hi_lo(m):
        return jnp.stack(_split_bf16(jnp.asarray(m, F32)))

    return (hi_lo(m1), hi_lo(m2), hi_lo(wc),
            jnp.asarray(np.ascontiguousarray(tw_re), F32), jnp.asarray(np.ascontiguousarray(tw_im), F32))


def _fft_kernel(f_ref, m1_ref, m2_ref, wc_ref, twr_ref, twi_ref, o_ref, are, aim, hre, him, *, n):
    t = n * n
    ng = FC // 128
    for g in range(ng):
        fg = f_ref[:, g * 128:(g + 1) * 128]
        a = (jnp.dot(fg, wc_ref[0], preferred_element_type=F32)
             + jnp.dot(fg, wc_ref[1], preferred_element_type=F32))
        are[g] = a[:, :128]
        aim[g] = a[:, 128:]

    def table_dot(m_ref, v):
        return (jnp.dot(m_ref[0], v, preferred_element_type=F32)
                + jnp.dot(m_ref[1], v, preferred_element_type=F32))

    def gather_rows(re, im, first):
        rows = [pl.ds(first + u, n, stride=n) for u in range(FFT_UNROLL)]
        top = jnp.concatenate([re[g, r, :] for r in rows for g in range(ng)], axis=1)
        bot = jnp.concatenate([im[g, r, :] for r in rows for g in range(ng)], axis=1)
        return jnp.concatenate([top, bot], axis=0).astype(BF16)

    def stage1(it, carry):
        t1 = it * FFT_UNROLL
        h = table_dot(m1_ref, gather_rows(are, aim, t1))
        for u in range(FFT_UNROLL):
            dst = pl.ds(pl.multiple_of((t1 + u) * n, n), n)
            twr = twr_ref[t1 + u]
            twi = twi_ref[t1 + u]
            for g in range(ng):
                lanes = slice((u * ng + g) * 128, (u * ng + g + 1) * 128)
                h_re, h_im = h[:n, lanes], h[n:, lanes]
                hre[g, dst, :] = h_re * twr - h_im * twi
                him[g, dst, :] = h_re * twi + h_im * twr
        return carry

    lax.fori_loop(0, n // FFT_UNROLL, stage1, 0)

    def stage2(it, carry):
        k2 = it * FFT_UNROLL
        z = table_dot(m2_ref, gather_rows(hre, him, k2))
        z = z * ((t * 128.0) ** -0.5)
        for u in range(FFT_UNROLL):
            for g in range(ng):
                are[g, pl.ds(k2 + u, n, stride=n), :] = z[:, (u * ng + g) * 128:(u * ng + g + 1) * 128]
        return carry

    lax.fori_loop(0, n // FFT_UNROLL, stage2, 0)
    for g in range(ng):
        o_ref[:, g * 128:(g + 1) * 128] = are[g].astype(BF16)


def _fft_call(f, n, first_blk, n_seq):
    t = n * n
    m1, m2, wc, twr, twi = _dft_tables(n)
    full = lambda shape: pl.BlockSpec(shape, lambda b, h: (0,) * len(shape))
    return pl.pallas_call(
        functools.partial(_fft_kernel, n=n),
        out_shape=jax.ShapeDtypeStruct((n_seq * t, GW), BF16),
        grid=(n_seq, GW // FC),
        in_specs=[pl.BlockSpec((t, FC), lambda b, h: (first_blk + b, h)),
                  full((2, 2 * n, 2 * n)), full((2, n, 2 * n)), full((2, 128, 256)),
                  full((n, n, 128)), full((n, n, 128))],
        out_specs=pl.BlockSpec((t, FC), lambda b, h: (b, h)),
        scratch_shapes=[pltpu.VMEM((FC // 128, t, 128), F32)] * 4,
        compiler_params=_cparams(2),
        name="fourier_mix_%d" % t,
    )(f, m1, m2, wc, twr, twi)


CONV_ROWS = 32
GSH_ROWS = TM + 2 * HALO - 8


def _conv_kernel(sb_ref, p_ref, pp_ref, pn_ref, g_ref, gp_ref, gn_ref, sw_ref, cw_ref, cb_ref, ng_ref, nb_ref,
                 oc_ref, od_ref, pbuf, gbuf, gsh):
    i = pl.program_id(0)
    seq_start = jnp.logical_or(i % SEQ_TILES == 0, i >= LAT_TILES)
    seq_end = jnp.logical_or(i % SEQ_TILES == SEQ_TILES - 1, i >= LAT_TILES)
    for buf, main, prev, nxt in ((pbuf, p_ref, pp_ref, pn_ref), (gbuf, g_ref, gp_ref, gn_ref)):
        buf[0:HALO, :] = jnp.where(seq_start, 0.0, prev[...])
        buf[HALO:HALO + TM, :] = main[...]
        buf[HALO + TM:HALO + TM + HALO, :] = jnp.where(seq_end, 0.0, nxt[...])
    for s in range(8):
        gsh[s] = gbuf[s:s + GSH_ROWS, :]
    sw = sw_ref[...]
    cw = cw_ref[...]
    for r0 in range(0, TM, CONV_ROWS):
        base = HALO + r0
        acc = sw[0:1, :] * pbuf[base - 1:base - 1 + CONV_ROWS, :]
        acc = acc + sw[1:2, :] * pbuf[base:base + CONV_ROWS, :]
        acc = acc + sw[2:3, :] * pbuf[base + 1:base + 1 + CONV_ROWS, :]
        oc_ref[r0:r0 + CONV_ROWS, :] = (sb_ref[r0:r0 + CONV_ROWS, :] * acc).astype(BF16)

        z = None
        for j in range(CONF_W):
            o = base - CONF_PAD + j
            q = o - o % 8
            term = cw[j:j + 1, :] * gsh[o % 8, q:q + CONV_ROWS, :]
            z = term if z is None else z + term
        z = z + cb_ref[...]
        mu = jnp.mean(z, axis=-1, keepdims=True)
        zc = z - mu
        var = jnp.mean(zc * zc, axis=-1, keepdims=True)
        y = zc * lax.rsqrt(var + EPS) * ng_ref[...] + nb_ref[...]
        od_ref[r0:r0 + CONV_ROWS, :] = (y * jax.nn.sigmoid(y)).astype(BF16)


def _conv_call(l, cv, sconv_w, cconv_w, cconv_b, cnorm_g, cnorm_b):
    hb = TM // HALO
    last = NT // HALO - 1

    def halo_specs(col):
        return [pl.BlockSpec((TM, GW), lambda i: (i, col)),
                pl.BlockSpec((HALO, GW), lambda i: (jnp.maximum(i * hb - 1, 0), col)),
                pl.BlockSpec((HALO, GW), lambda i: (jnp.minimum((i + 1) * hb, last), col))]

    vec = lambda: pl.BlockSpec((None, 1, GW), lambda i: (l, 0, 0))
    cw = jnp.pad(cconv_w, ((0, 0), (0, 32 - CONF_W), (0, 0)))
    sw = jnp.pad(sconv_w, ((0, 0), (0, 8 - 3), (0, 0)))
    return pl.pallas_call(
        _conv_kernel,
        out_shape=(jax.ShapeDtypeStruct((_layer_tiles(l) * TM, GW), BF16),
                   jax.ShapeDtypeStruct((_layer_tiles(l) * TM, GW), BF16)),
        grid=(_layer_tiles(l),),
        in_specs=[pl.BlockSpec((TM, GW), lambda i: (i, 0))] + halo_specs(1) + halo_specs(2) + [
            pl.BlockSpec((None, 8, GW), lambda i: (l, 0, 0)),
            pl.BlockSpec((None, 32, GW), lambda i: (l, 0, 0)),
            vec(), vec(), vec()],
        out_specs=(pl.BlockSpec((TM, GW), lambda i: (i, 0)), pl.BlockSpec((TM, GW), lambda i: (i, 0))),
        scratch_shapes=[pltpu.VMEM((TM + 2 * HALO, GW), F32)] * 2 + [pltpu.VMEM((8, GSH_ROWS, GW), F32)],
        compiler_params=_cparams(1),
        name="conv_mixers",
    )(cv, cv, cv, cv, cv, cv, cv, sw, cw, cconv_b.reshape(DEPTH, 1, GW), cnorm_g.reshape(DEPTH, 1, GW),
      cnorm_b.reshape(DEPTH, 1, GW))


R_ROWS = 32
_SLOT_A = np.array([0, 2, 2, 0, 0, 1], np.int32)
_SLOT_B = np.array([1, 1, 3, 3, 2, 3], np.int32)


def _route(lg):
    gl = [lg[N_EXPERTS + g:N_EXPERTS + g + 1, :] for g in range(N_GROUPS)]
    gmax = functools.reduce(jnp.maximum, gl)
    pg = 1.0 / functools.reduce(jnp.add, [jnp.exp(v - gmax) for v in gl])
    gsel = jnp.full_like(gmax, N_GROUPS - 1).astype(I32)
    for g in range(N_GROUPS - 2, -1, -1):
        gsel = jnp.where(gl[g] == gmax, g, gsel)
    el = []
    for j in range(PER_GROUP):
        v = lg[j:j + 1, :]
        for g in range(1, N_GROUPS):
            v = jnp.where(gsel == g, lg[g * PER_GROUP + j:g * PER_GROUP + j + 1, :], v)
        el.append(v)
    v1 = functools.reduce(jnp.maximum, el)
    i1 = jnp.full_like(gsel, PER_GROUP - 1)
    for j in range(PER_GROUP - 2, -1, -1):
        i1 = jnp.where(el[j] == v1, j, i1)
    rest = [jnp.where(i1 == j, -jnp.inf, el[j]) for j in range(PER_GROUP)]
    v2 = functools.reduce(jnp.maximum, rest)
    i2 = jnp.full_like(gsel, PER_GROUP - 1)
    for j in range(PER_GROUP - 2, -1, -1):
        i2 = jnp.where(jnp.logical_and(rest[j] == v2, i1 != j), j, i2)
    e2 = jnp.exp(v2 - v1)
    w1 = pg / (1.0 + e2)
    w2 = pg * e2 / (1.0 + e2)
    lo = jnp.minimum(i1, i2)
    hi = jnp.maximum(i1, i2)
    w_lo = jnp.where(i1 < i2, w1, w2)
    w_hi = jnp.where(i1 < i2, w2, w1)
    pair = jnp.where(lo == 0, jnp.where(hi == 1, 0, jnp.where(hi == 2, 4, 3)),
                     jnp.where(lo == 1, jnp.where(hi == 2, 1, 5), 2))
    cls = gsel * 6 + pair
    a_is_hi = pair == 1
    return cls, jnp.where(a_is_hi, w_hi, w_lo), jnp.where(a_is_hi, w_lo, w_hi)


def _rows_to_slabs(v, stage_ref, out_ref):
    n = v.shape[0]
    for c in range(SLAB):
        stage_ref[pl.ds(c, n, stride=SLAB), :] = v[:, c * 128:(c + 1) * 128]
    out_ref[...] = stage_ref[...].astype(out_ref.dtype)


def _slabs_to_rows(src_ref, stage_ref, n):
    stage_ref[...] = src_ref[...].astype(F32)
    return jnp.concatenate([stage_ref[pl.ds(c, n, stride=SLAB), :] for c in range(SLAB)], axis=1)


def _out_kernel(*refs, l):
    oa_ref, obl_ref, obc_ref, oc_ref, od_ref, w_ref = refs[:6]
    stream, refs = refs[6:6 + _n_stream(l)], refs[6 + _n_stream(l):]
    (mod_ref, g_ref, wrh_ref, wrl_ref, br_ref, tri_ref, w2_ref,
     x1_ref, hp_ref, cls_ref, gw_ref, cnt_ref, w2b_ref, carry, stage) = refs
    i = pl.program_id(0)
    w2b_ref[...] = w2_ref[...].astype(BF16)

    @pl.when(i == 0)
    def _():
        carry[...] = jnp.zeros_like(carry)

    ob = jnp.where(i < LAT_TILES, obl_ref[...], obc_ref[...])
    y = jnp.dot(oa_ref[...], w_ref[0:GW, :], preferred_element_type=F32)
    y = y + jnp.dot(ob, w_ref[GW:2 * GW, :], preferred_element_type=F32)
    y = y + jnp.dot(oc_ref[...], w_ref[2 * GW:3 * GW, :], preferred_element_type=F32)
    y = y + jnp.dot(od_ref[...], w_ref[3 * GW:4 * GW, :], preferred_element_type=F32)
    x1 = _stream_tile(l, stream, stage) + mod_ref[2:3, :] * y
    x1_ref[...] = x1
    h2 = _rmsnorm_mod(x1, g_ref[...], mod_ref[3:4, :], mod_ref[4:5, :])
    _rows_to_slabs(h2, stage, hp_ref)

    h_hi, h_lo = _split_bf16(h2)
    lg = jnp.dot(h_hi, wrh_ref[...], preferred_element_type=F32)
    lg = lg + jnp.dot(h_lo, wrh_ref[...], preferred_element_type=F32)
    lg = lg + jnp.dot(h_hi, wrl_ref[...], preferred_element_type=F32)
    lg = jnp.transpose(lg + br_ref[...])[0:R_ROWS, :]
    cls, w_a, w_b = _route(lg)

    onehot = jnp.where(lax.broadcasted_iota(I32, (R_ROWS, TM), 0) == cls, 1.0, 0.0)
    prefix = jnp.dot(onehot.astype(BF16), tri_ref[...], preferred_element_type=F32)
    rank = jnp.sum(onehot * (prefix - 1.0 + carry[:, 0:1]), axis=0, keepdims=True)
    total = carry[...] + jnp.sum(onehot, axis=1, keepdims=True)
    carry[...] = total
    cnt_ref[...] = total.astype(I32)
    cls_ref[...] = jnp.concatenate([cls, rank.astype(I32), jnp.zeros((6, TM), I32)], axis=0)
    gw_ref[...] = jnp.concatenate([w_a, w_b, jnp.zeros((6, TM), F32)], axis=0)


def _out_call(l, o_a, o_b_lat, o_b_ctx, o_c, o_d, w_out, stream, mod, g_norm2, wr_hi, wr_lo, b_route, w_down_e):
    row = lambda: pl.BlockSpec((TM, GW), lambda i: (i, 0))
    tri = jnp.asarray(np.triu(np.ones((TM, TM), np.float32)), BF16)
    n_rows = _layer_tiles(l) * TM
    return pl.pallas_call(
        functools.partial(_out_kernel, l=l),
        out_shape=(jax.ShapeDtypeStruct((n_rows, D), F32), jax.ShapeDtypeStruct((n_rows * SLAB, 128), BF16),
                   jax.ShapeDtypeStruct((8, n_rows), I32), jax.ShapeDtypeStruct((8, n_rows), F32),
                   jax.ShapeDtypeStruct((R_ROWS, 128), I32),
                   jax.ShapeDtypeStruct((N_EXPERTS * D_EXPERT, D), BF16)),
        grid=(_layer_tiles(l),),
        in_specs=[row(),
                  pl.BlockSpec((TM, GW), lambda i: (jnp.minimum(i, LAT_TILES - 1), 0)),
                  pl.BlockSpec((TM, GW), lambda i: (jnp.maximum(i - LAT_TILES, 0), 0)),
                  row(), row(),
                  pl.BlockSpec((None, D, D), lambda i: (l, 0, 0), pipeline_mode=pl.Buffered(1))]
        + _stream_specs(l) + [
                  pl.BlockSpec((None, None, 8, D), lambda i: (l, _mod_row(i), 0, 0)),
                  pl.BlockSpec((None, 1, D), lambda i: (l, 0, 0)),
                  pl.BlockSpec((None, D, 128), lambda i: (l, 0, 0)),
                  pl.BlockSpec((None, D, 128), lambda i: (l, 0, 0)),
                  pl.BlockSpec((None, 1, 128), lambda i: (l, 0, 0)),
                  pl.BlockSpec((TM, TM), lambda i: (0, 0)),
                  _cast_spec(W2_ROWS, D, l)],
        out_specs=(pl.BlockSpec((TM, D), lambda i: (i, 0)),
                   pl.BlockSpec((TM * SLAB, 128), lambda i: (i, 0)),
                   pl.BlockSpec((8, TM), lambda i: (0, i)), pl.BlockSpec((8, TM), lambda i: (0, i)),
                   pl.BlockSpec((R_ROWS, 128), lambda i: (0, 0)),
                   _cast_out_spec(W2_ROWS, D)),
        scratch_shapes=[pltpu.VMEM((R_ROWS, 128), F32), pltpu.VMEM((TM * SLAB, 128), F32)],
        compiler_params=_cparams(1),
        name="out_proj_route",
    )(o_a, o_b_lat, o_b_ctx, o_c, o_d, w_out, *stream, mod, g_norm2.reshape(DEPTH, 1, D), wr_hi, wr_lo,
      b_route, tri, w_down_e.reshape(DEPTH * N_EXPERTS * D_EXPERT, D))


ROW_UNROLL = 8


def _gate_column(g_ref, src_ref, base):
    sub = lax.broadcasted_iota(I32, (8, 128), 0)
    pieces = []
    for g in range(TME // 8):
        v = jnp.zeros((8, 128), F32)
        for k in range(8):
            v = jnp.where(sub == k, g_ref[src_ref[base + 8 * g + k]], v)
        pieces.append(v)
    return jnp.concatenate(pieces, axis=0)


DMA_PIECES = 8


def _expert_kernel(ea_ref, eb_ref, nvalid_ref, zero_ref, rows_ref, starts_ref, counts_ref, cls_ref, rank_ref, gwa_ref,
                   gwb_ref, hp_hbm, w1a_ref, w3a_ref, w2a_ref, w1b_ref, w3b_ref, w2b_ref, yt_hbm, src_ref,
                   hbuf0, hbuf1, ybuf0, ybuf1, stage, gsem, ssem, *, n_tok):
    i = pl.program_id(0)
    nvalid = nvalid_ref[0]
    hbufs, ybufs = (hbuf0, hbuf1), (ybuf0, ybuf1)
    rows_prev2, rows_prev, rows_cur, rows_next = (rows_ref[i + k] for k in range(4))

    def slab(ref, row):
        return ref.at[pl.ds(pl.multiple_of(row * SLAB, SLAB), SLAB), :]

    def gather_copy(tile, sl, j, z=0):
        return pltpu.make_async_copy(slab(hp_hbm, src_ref[tile * TME + j] + z), slab(hbufs[sl], j), gsem.at[sl])

    def scatter_copy(tile, sl, j, z=0):
        return pltpu.make_async_copy(slab(ybufs[sl], j), slab(yt_hbm, src_ref[tile * TME + j] + z), ssem.at[sl])

    def for_rows(n, fn):
        def body(j, carry):
            fn(j)
            return carry

        lax.fori_loop(0, n, body, 0)

    @pl.when(i == 0)
    def _():
        def unused(p, carry):
            src_ref[p] = n_tok - 1
            return carry

        for buf in hbufs:
            buf[...] = jnp.zeros_like(buf)

        def class_tail(c, carry):
            lo = starts_ref[c] + counts_ref[c]
            hi = starts_ref[c] + ((counts_ref[c] + TME - 1) // TME) * TME
            lax.fori_loop(lo, hi, unused, 0)
            return carry

        lax.fori_loop(0, N_CLASSES, class_tail, 0)

        def place(t, carry):
            p = starts_ref[cls_ref[t]] + rank_ref[t]
            src_ref[p] = t
            return carry

        lax.fori_loop(0, n_tok, place, 0, unroll=ROW_UNROLL)
        for_rows(rows_cur, lambda j: gather_copy(0, 0, j).start())

    def tile_body(sl, paced):
        for_rows(rows_cur, lambda j: gather_copy(0, sl, 0).wait())
        h = _slabs_to_rows(hbufs[sl], stage, TME).astype(BF16)
        if not paced:
            for_rows(rows_next, lambda j: gather_copy(i + 1, 1 - sl, j).start())
            for_rows(rows_prev, lambda j: scatter_copy(i - 1, 1 - sl, j).start(priority=1))
        pieces = iter(range(DMA_PIECES if paced else 0))

        def paced_dot(a, b):
            r = jnp.dot(a, b, preferred_element_type=F32)
            piece = next(pieces, None)
            if piece is None:
                return r
            z = pltpu.bitcast(r[0:8, 0:128], I32)[0, 0] * zero_ref[0]
            first = piece * (TME // DMA_PIECES)
            for j in range(first, first + TME // DMA_PIECES):
                gather_copy(i + 1, 1 - sl, j, z).start()
                scatter_copy(i - 1, 1 - sl, j, z).start(priority=1)
            return r

        y = [None] * (D // D_EXPERT)
        for g_ref, w1_ref, w3_ref, w2_ref in ((gwa_ref, w1a_ref, w3a_ref, w2a_ref),
                                              (gwb_ref, w1b_ref, w3b_ref, w2b_ref)):
            half = D_EXPERT // 2
            gate = jnp.concatenate([paced_dot(h, w1_ref[:, c * half:(c + 1) * half]) for c in range(2)], axis=1)
            up = jnp.concatenate([paced_dot(h, w3_ref[:, c * half:(c + 1) * half]) for c in range(2)], axis=1)
            gcol = _gate_column(g_ref, src_ref, i * TME)
            hid = (gate * jax.nn.sigmoid(gate) * up * jnp.concatenate([gcol] * (D_EXPERT // 128), axis=1)).astype(BF16)
            for c in range(len(y)):
                yc = paced_dot(hid, w2_ref[:, c * D_EXPERT:(c + 1) * D_EXPERT])
                y[c] = yc if y[c] is None else y[c] + yc
        y = jnp.concatenate(y, axis=1)

        for_rows(rows_prev2, lambda j: scatter_copy(0, sl, 0).wait())
        _rows_to_slabs(y, stage, ybufs[sl])

        @pl.when(i == nvalid - 1)
        def _():
            for_rows(rows_cur, lambda j: scatter_copy(i, sl, j).start())
            for_rows(rows_prev, lambda j: scatter_copy(0, 1 - sl, 0).wait())
            for_rows(rows_cur, lambda j: scatter_copy(0, sl, 0).wait())

    full_neighbours = jnp.logical_and(rows_next == TME, rows_prev == TME)
    for sl in range(2):
        for paced in (True, False):
            @pl.when((i < nvalid) & (i % 2 == sl) & (full_neighbours == paced))
            def _(sl=sl, paced=paced):
                tile_body(sl, paced)


def _expert_call(tile_ea, tile_eb, nvalid, tile_rows, starts, counts, cls, rank, gwa, gwb, hp, w1b, w3b, w2b):
    n_tok = cls.shape[0]
    up_a = pl.BlockSpec((None, D, D_EXPERT), lambda i, ea, eb, *_: (ea[i], 0, 0))
    up_b = pl.BlockSpec((None, D, D_EXPERT), lambda i, ea, eb, *_: (eb[i], 0, 0))
    down_a = pl.BlockSpec((None, D_EXPERT, D), lambda i, ea, eb, *_: (ea[i], 0, 0))
    down_b = pl.BlockSpec((None, D_EXPERT, D), lambda i, ea, eb, *_: (eb[i], 0, 0))
    return pl.pallas_call(
        functools.partial(_expert_kernel, n_tok=n_tok),
        out_shape=jax.ShapeDtypeStruct((n_tok * SLAB, 128), BF16),
        grid_spec=pltpu.PrefetchScalarGridSpec(
            num_scalar_prefetch=11,
            grid=(P_TILES,),
            in_specs=[pl.BlockSpec(memory_space=pl.ANY),
                      up_a, up_a, down_a, up_b, up_b, down_b],
            out_specs=pl.BlockSpec(memory_space=pl.ANY),
            scratch_shapes=[pltpu.SMEM((P_ROWS,), I32)]
            + [pltpu.VMEM((TME * SLAB, 128), BF16)] * 4
            + [pltpu.VMEM((TME * SLAB, 128), F32),
               pltpu.SemaphoreType.DMA((2,)), pltpu.SemaphoreType.DMA((2,))]),
        compiler_params=_cparams(1),
        name="moe_experts",
    )(tile_ea, tile_eb, nvalid, jnp.zeros((1,), I32), tile_rows, starts, counts, cls, rank, gwa, gwb, hp,
      w1b, w3b, w2b, w1b, w3b, w2b)


def _final_kernel(x1_ref, yt_ref, mod_ref, gf_ref, o_ref, stage):
    x2 = _stream_tile(DEPTH, (x1_ref, yt_ref, mod_ref), stage)
    ms = jnp.mean(x2 * x2, axis=-1, keepdims=True)
    o_ref[...] = x2 * lax.rsqrt(ms + EPS) * gf_ref[...]


def _final_call(x1, mod, g_final, yt):
    return pl.pallas_call(
        _final_kernel,
        out_shape=jax.ShapeDtypeStruct((N_LAT, D), F32),
        grid=(LAT_TILES,),
        in_specs=_stream_specs(DEPTH) + [pl.BlockSpec((1, D), lambda i: (0, 0))],
        out_specs=pl.BlockSpec((TM, D), lambda i: (i, 0)),
        scratch_shapes=[pltpu.VMEM((TM * SLAB, 128), F32)],
        compiler_params=_cparams(1),
        name="final_norm",
    )(x1, yt, mod, g_final.reshape(1, D))


def _slot_plan(counts):
    padded = ((counts + TME - 1) // TME) * TME
    ends = jnp.cumsum(padded)
    starts = ends - padded
    tile_start = jnp.arange(P_TILES, dtype=I32) * TME
    tile_cls = jnp.minimum(jnp.sum((tile_start[:, None] >= ends[None, :]).astype(I32), axis=1), N_CLASSES - 1)
    nvalid = (ends[-1] // TME).reshape(1)
    grp = tile_cls // 6
    tile_ea = grp * PER_GROUP + jnp.asarray(_SLOT_A)[tile_cls % 6]
    tile_eb = grp * PER_GROUP + jnp.asarray(_SLOT_B)[tile_cls % 6]
    tile_rows = jnp.clip(starts[tile_cls] + counts[tile_cls] - tile_start, 0, TME)
    tile_rows = jnp.where(tile_start < ends[-1], tile_rows, 0)
    tile_rows = jnp.concatenate([jnp.zeros((2,), I32), tile_rows, jnp.zeros((1,), I32)])
    return starts, tile_ea, tile_eb, nvalid, tile_rows


def _rope_tables():
    t = jnp.arange(SEQ, dtype=I32)
    row = (t // GRID_W).astype(F32)
    col = (t % GRID_W).astype(F32)
    half = HEAD_DIM // 4
    freqs = ROPE_BASE ** (-jnp.arange(0, 2 * half, 2, dtype=F32) / (2 * half))
    ang_r = row[:, None] * freqs
    ang_c = col[:, None] * freqs
    cos_h = jnp.concatenate([jnp.cos(ang_r), jnp.cos(ang_r), jnp.cos(ang_c), jnp.cos(ang_c)], axis=1)
    sin_h = jnp.concatenate([-jnp.sin(ang_r), jnp.sin(ang_r), -jnp.sin(ang_c), jnp.sin(ang_c)], axis=1)
    cos_t = jnp.concatenate([jnp.tile(cos_h, (1, 2)), jnp.ones((TM, 128), F32)], axis=0)
    sin_t = jnp.concatenate([jnp.tile(sin_h, (1, 2)), jnp.zeros((TM, 128), F32)], axis=0)
    return cos_t, sin_t


def kernel(x, c, ctx, c_ctx, w_mod, b_mod, g_norm1, g_norm2, w_in, w_out, attn_sink, sconv_w, cconv_w, cconv_b,
           cnorm_g, cnorm_b, w_route_group, b_route_group, w_route_expert, b_route_expert, w_gate_e, w_up_e,
           w_down_e, g_final):
    c8 = jnp.zeros((8, D), F32).at[0:BATCH].set(c).at[BATCH].set(c_ctx)
    w_route = jnp.concatenate([w_route_expert, w_route_group,
                               jnp.zeros((DEPTH, D, 128 - N_EXPERTS - N_GROUPS), F32)], axis=2)
    wr_hi = w_route.astype(BF16)
    wr_lo = (w_route - wr_hi.astype(F32)).astype(BF16)
    b_route = jnp.concatenate([b_route_expert, b_route_group,
                               jnp.zeros((DEPTH, 128 - N_EXPERTS - N_GROUPS), F32)], axis=1).reshape(DEPTH, 1, 128)
    cos_t, sin_t = _rope_tables()

    mod, wqkv, wf, wcv, w_out_b = _mod_call(c8, w_mod, b_mod, w_in, w_out)
    mod = mod.reshape(DEPTH, 8, N_MOD, D)[:, :3]
    mod = jnp.pad(mod, ((0, 0), (0, 0), (0, 8 - N_MOD), (0, 0)))

    stream = (x.reshape(N_LAT, D), ctx.reshape(N_CTX, D))
    for l in range(DEPTH):
        last = l == DEPTH - 1
        qkv, f, cv, w1b, w3b = _in_call(l, stream, mod, g_norm1, wqkv, wf, wcv, cos_t, sin_t, w_gate_e, w_up_e)
        o_a = _attn_call(l, attn_sink, qkv)
        o_b_lat = _fft_call(f, 64, 0, BATCH)
        o_b_ctx = o_b_lat if last else _fft_call(f, 16, N_LAT // CTX, BATCH)
        o_c, o_d = _conv_call(l, cv, sconv_w, cconv_w, cconv_b, cnorm_g, cnorm_b)
        x1, hp, cls8, gw8, cnt, w2b = _out_call(l, o_a, o_b_lat, o_b_ctx, o_c, o_d, w_out_b, stream, mod, g_norm2,
                                                wr_hi, wr_lo, b_route, w_down_e)
        counts = cnt[:N_CLASSES, 0]
        starts, tile_ea, tile_eb, nvalid, tile_rows = _slot_plan(counts)
        yt = _expert_call(tile_ea, tile_eb, nvalid, tile_rows, starts, counts, cls8[0], cls8[1], gw8[0], gw8[1], hp,
                          w1b.reshape(N_EXPERTS, D, D_EXPERT), w3b.reshape(N_EXPERTS, D, D_EXPERT),
                          w2b.reshape(N_EXPERTS, D_EXPERT, D))
        stream = (x1, yt, mod)
    return _final_call(x1, mod, g_final, yt).reshape(BATCH, SEQ, D)
```

```python
import functools

import numpy as np
import jax
import jax.numpy as jnp
from jax import lax
from jax.experimental import pallas as pl
from jax.experimental.pallas import tpu as pltpu

F32 = jnp.float32
BF16 = jnp.bfloat16
I32 = jnp.int32

D = 2048
BATCH = 2
SEQ = 4096
CTX = 256
DEPTH = 2
GRID_W = 64
N_LAT = BATCH * SEQ
N_CTX = BATCH * CTX
NT = N_LAT + N_CTX
GW = 512
HEAD_DIM = 64
N_HEADS = 8
WINDOW = 128
ROPE_BASE = 10000.0
N_MOD = 6
EPS = 1e-6
NEG_INF = -1e30
N_GROUPS = 4
PER_GROUP = 4
N_EXPERTS = 16
D_EXPERT = 512
N_CLASSES = N_GROUPS * 6
CONF_W = 31
CONF_PAD = 15

TM = 256
N_TILES = NT // TM
LAT_TILES = N_LAT // TM
SEQ_TILES = SEQ // TM
TQ = 128
HALO = 16
TME = 128
P_ROWS = NT + N_CLASSES * TME
P_TILES = P_ROWS // TME
SLAB = D // 128
VMEM_LIMIT = 56 * 1024 * 1024


def _cparams(n_axes=1, **kw):
    return pltpu.CompilerParams(dimension_semantics=("arbitrary",) * n_axes,
                                vmem_limit_bytes=VMEM_LIMIT, **kw)


def _split_bf16(v):
    hi = v.astype(BF16)
    lo = (v - hi.astype(F32)).astype(BF16)
    return hi, lo


def _mod_row(i):
    return jnp.minimum(i // SEQ_TILES, 2)


MOD_TN = 1536


MOD_STEPS = N_MOD * D // MOD_TN
W_ROWS = D // MOD_STEPS


def _mod_kernel(c_ref, w_ref, b_ref, win_ref, wout_ref, o_ref, wqkv_ref, wf_ref, wcv_ref, woutb_ref):
    w_in = win_ref[...]
    wqkv_ref[:, 0:768] = w_in[:, 0:768].astype(BF16)
    wqkv_ref[:, 768:896] = pltpu.roll(w_in[:, 512:640], HEAD_DIM, 1).astype(BF16)
    wqkv_ref[:, 896:1024] = pltpu.roll(w_in[:, 640:768], HEAD_DIM, 1).astype(BF16)
    wf_ref[...] = w_in[:, 768:768 + GW].astype(BF16)
    wcv_ref[...] = w_in[:, 768 + GW:].astype(BF16)
    woutb_ref[...] = wout_ref[...].astype(BF16)
    c = c_ref[...]
    s = c * jax.nn.sigmoid(c)
    s_hi, s_lo = _split_bf16(s)
    w_hi, w_lo = _split_bf16(w_ref[...])
    acc = jnp.dot(s_hi, w_hi, preferred_element_type=F32)
    acc = acc + jnp.dot(s_lo, w_hi, preferred_element_type=F32)
    acc = acc + jnp.dot(s_hi, w_lo, preferred_element_type=F32)
    o_ref[...] = acc + b_ref[...]


def _mod_call(c8, w_mod, b_mod, w_in, w_out):
    n = N_MOD * D
    rows = lambda cols: pl.BlockSpec((None, W_ROWS, cols), lambda l, j: (l, j, 0))
    return pl.pallas_call(
        _mod_kernel,
        out_shape=(jax.ShapeDtypeStruct((DEPTH, 8, n), F32),
                   jax.ShapeDtypeStruct((DEPTH, D, QKV_W), BF16), jax.ShapeDtypeStruct((DEPTH, D, GW), BF16),
                   jax.ShapeDtypeStruct((DEPTH, D, CV_IN), BF16), jax.ShapeDtypeStruct((DEPTH, D, D), BF16)),
        grid=(DEPTH, MOD_STEPS),
        in_specs=[
            pl.BlockSpec((8, D), lambda l, j: (0, 0)),
            pl.BlockSpec((None, D, MOD_TN), lambda l, j: (l, 0, j)),
            pl.BlockSpec((None, 1, MOD_TN), lambda l, j: (l, 0, j)),
            rows(w_in.shape[-1]), rows(D),
        ],
        out_specs=(pl.BlockSpec((None, 8, MOD_TN), lambda l, j: (l, 0, j)),
                   rows(QKV_W), rows(GW), rows(CV_IN), rows(D)),
        compiler_params=_cparams(2),
        name="adaln_mod",
    )(c8, w_mod, b_mod.reshape(DEPTH, 1, n), w_in, w_out)


QKV_W = 1024
CV_IN = 5 * GW
CV_OUT = 3 * GW


def _rmsnorm_mod(x, g, shift, scale):
    ms = jnp.mean(x * x, axis=-1, keepdims=True)
    y = x * lax.rsqrt(ms + EPS) * g
    return y * (1.0 + scale) + shift


def _stream_specs(l):
    if l == 0:
        return [pl.BlockSpec((TM, D), lambda i, *_: (jnp.minimum(i, LAT_TILES - 1), 0)),
                pl.BlockSpec((TM, D), lambda i, *_: (jnp.maximum(i - LAT_TILES, 0), 0))]
    return [pl.BlockSpec((TM, D), lambda i, *_: (i, 0)),
            pl.BlockSpec((TM * SLAB, 128), lambda i, *_: (i, 0)),
            pl.BlockSpec((None, None, 8, D), lambda i, *_: (l - 1, _mod_row(i), 0, 0))]


def _stream_tile(l, refs, stage):
    if l == 0:
        xl_ref, xc_ref = refs
        return jnp.where(pl.program_id(0) < LAT_TILES, xl_ref[...], xc_ref[...])
    x1_ref, yt_ref, mod_prev_ref = refs
    return x1_ref[...] + mod_prev_ref[5:6, :] * _slabs_to_rows(yt_ref, stage, TM)


def _n_stream(l):
    return 2 if l == 0 else 3


def _layer_tiles(l):
    return LAT_TILES if l == DEPTH - 1 else N_TILES


CAST_STEPS = 32


def _cast_spec(rows, cols, l):
    return pl.BlockSpec((rows, cols), lambda i, *_: (l * CAST_STEPS + jnp.minimum(i, CAST_STEPS - 1), 0))


def _cast_out_spec(rows, cols):
    return pl.BlockSpec((rows, cols), lambda i, *_: (jnp.minimum(i, CAST_STEPS - 1), 0))


def _in_kernel(*refs, l):
    stream, refs = refs[:_n_stream(l)], refs[_n_stream(l):]
    (mod_ref, g_ref, wqkv_ref, wf_ref, wcv_ref, cos_ref, sin_ref, w1_ref, w3_ref,
     qkv_ref, f_ref, cv_ref, w1b_ref, w3b_ref) = refs[:14]
    stage = refs[14] if l > 0 else None
    w1b_ref[...] = w1_ref[...].astype(BF16)
    w3b_ref[...] = w3_ref[...].astype(BF16)
    h = _rmsnorm_mod(_stream_tile(l, stream, stage), g_ref[...], mod_ref[0:1, :], mod_ref[1:2, :])
    hb = h.astype(BF16)
    qkv = jnp.dot(hb, wqkv_ref[...], preferred_element_type=F32)
    cos = cos_ref[...]
    sin = sin_ref[...]
    lane = lax.broadcasted_iota(I32, (TM, 128), 1)
    first_half = (lane % 32) < 16

    def rope(t):
        rot = jnp.where(first_half, pltpu.roll(t, 112, 1), pltpu.roll(t, 16, 1))
        return t * cos + rot * sin

    scale = HEAD_DIM ** -0.5
    for c in range(4):
        qkv_ref[:, c * 128:(c + 1) * 128] = (rope(qkv[:, c * 128:(c + 1) * 128]) * scale).astype(BF16)
    qkv_ref[:, 512:640] = rope(qkv[:, 512:640]).astype(BF16)
    qkv_ref[:, 640:768] = qkv[:, 640:768].astype(BF16)
    qkv_ref[:, 768:896] = rope(qkv[:, 768:896]).astype(BF16)
    qkv_ref[:, 896:1024] = qkv[:, 896:1024].astype(BF16)

    f_ref[...] = jnp.dot(hb, wf_ref[...], preferred_element_type=F32).astype(BF16)

    cv = jnp.dot(hb, wcv_ref[...], preferred_element_type=F32)
    cv_ref[:, 0:GW] = cv[:, 0:GW]
    cv_ref[:, GW:2 * GW] = cv[:, GW:2 * GW] * cv[:, 2 * GW:3 * GW]
    cv_ref[:, 2 * GW:3 * GW] = cv[:, 3 * GW:4 * GW] * jax.nn.sigmoid(cv[:, 4 * GW:5 * GW])


W13_ROWS = N_EXPERTS * D // CAST_STEPS
W2_ROWS = N_EXPERTS * D_EXPERT // CAST_STEPS


def _in_call(l, stream, mod, g_norm1, wqkv, wf, wcv, cos_t, sin_t, w_gate_e, w_up_e):
    const = dict(pipeline_mode=pl.Buffered(1))
    flat = lambda w: w.reshape(DEPTH * N_EXPERTS * D, D_EXPERT)
    return pl.pallas_call(
        functools.partial(_in_kernel, l=l),
        out_shape=(jax.ShapeDtypeStruct((NT, QKV_W), BF16),
                   jax.ShapeDtypeStruct((NT, GW), BF16),
                   jax.ShapeDtypeStruct((NT, CV_OUT), F32),
                   jax.ShapeDtypeStruct((N_EXPERTS * D, D_EXPERT), BF16),
                   jax.ShapeDtypeStruct((N_EXPERTS * D, D_EXPERT), BF16)),
        grid=(N_TILES,),
        in_specs=_stream_specs(l) + [
            pl.BlockSpec((None, None, 8, D), lambda i: (l, _mod_row(i), 0, 0)),
            pl.BlockSpec((None, 1, D), lambda i: (l, 0, 0)),
            pl.BlockSpec((None, D, QKV_W), lambda i: (l, 0, 0), **const),
            pl.BlockSpec((None, D, GW), lambda i: (l, 0, 0), **const),
            pl.BlockSpec((None, D, CV_IN), lambda i: (l, 0, 0), **const),
            pl.BlockSpec((TM, 128), lambda i: (jnp.where(i < LAT_TILES, i % SEQ_TILES, SEQ_TILES), 0)),
            pl.BlockSpec((TM, 128), lambda i: (jnp.where(i < LAT_TILES, i % SEQ_TILES, SEQ_TILES), 0)),
            _cast_spec(W13_ROWS, D_EXPERT, l), _cast_spec(W13_ROWS, D_EXPERT, l),
        ],
        out_specs=(pl.BlockSpec((TM, QKV_W), lambda i: (i, 0)),
                   pl.BlockSpec((TM, GW), lambda i: (i, 0)),
                   pl.BlockSpec((TM, CV_OUT), lambda i: (i, 0)),
                   _cast_out_spec(W13_ROWS, D_EXPERT), _cast_out_spec(W13_ROWS, D_EXPERT)),
        scratch_shapes=[pltpu.VMEM((TM * SLAB, 128), F32)] if l > 0 else [],
        compiler_params=_cparams(1),
        name="norm_in_proj",
    )(*stream, mod, g_norm1.reshape(DEPTH, 1, D), wqkv, wf, wcv, cos_t, sin_t, flat(w_gate_e), flat(w_up_e))


N_QBLK_LAT = N_LAT // TQ
N_QBLK = NT // TQ
QBLK_PER_SEQ = SEQ // TQ
KWIN = 3 * TQ


def _attn_heads(sink_ref, l, q_ref, keys, vals, bias, o_ref):
    lane = lax.broadcasted_iota(I32, (TQ, 128), 1)
    row2 = lax.broadcasted_iota(I32, (2 * TQ, 1), 0)
    zero = jnp.zeros((TQ, 128), BF16)
    if bias is not None:
        bias = jnp.concatenate([bias, bias], axis=0)
    for hk in range(2):
        c0, c1 = 2 * hk, 2 * hk + 1
        for p in range(2):
            kx = keys[0] if hk == p else keys[1]
            vx = vals[0] if hk == p else vals[1]
            in_half = (lane // HEAD_DIM) == p
            qs = jnp.concatenate([jnp.where(in_half, q_ref[:, c0 * 128:(c0 + 1) * 128], zero),
                                  jnp.where(in_half, q_ref[:, c1 * 128:(c1 + 1) * 128], zero)], axis=0)
            s = lax.dot_general(qs, kx, (((1,), (1,)), ((), ())), preferred_element_type=F32)
            if bias is not None:
                s = s + bias
            sink = jnp.where(row2 < TQ, sink_ref[l, 2 * c0 + p], sink_ref[l, 2 * c1 + p])
            m = jnp.maximum(jnp.max(s, axis=-1, keepdims=True), sink)
            e = jnp.exp(s - m)
            den = jnp.sum(e, axis=-1, keepdims=True) + jnp.exp(sink - m)
            o = (jnp.dot(e.astype(BF16), vx, preferred_element_type=F32) / den).astype(BF16)
            for k, c in enumerate((c0, c1)):
                lo = c * 128 + p * HEAD_DIM
                o_ref[:, lo:lo + HEAD_DIM] = o[k * TQ:(k + 1) * TQ, p * HEAD_DIM:(p + 1) * HEAD_DIM]


def _attn_kernel(sink_ref, q_ref, k_ref, v_ref, ks_ref, vs_ref, kc_ref, vc_ref, kcs_ref, vcs_ref, o_ref, *, l):
    i = pl.program_id(0)

    @pl.when(i < N_QBLK_LAT)
    def _latent():
        n = i % QBLK_PER_SEQ
        ws = pl.multiple_of(jnp.clip((n - 1) * TQ, 0, SEQ - KWIN), TQ)
        keys = [jnp.concatenate([r[pl.ds(ws, KWIN), :], c[...]], axis=0)
                for r, c in ((k_ref, kc_ref), (ks_ref, kcs_ref))]
        vals = [jnp.concatenate([r[pl.ds(ws, KWIN), :], c[...]], axis=0)
                for r, c in ((v_ref, vc_ref), (vs_ref, vcs_ref))]
        nk = KWIN + CTX
        qpos = n * TQ + lax.broadcasted_iota(I32, (TQ, nk), 0)
        col = lax.broadcasted_iota(I32, (TQ, nk), 1)
        valid = (jnp.abs(ws + col - qpos) <= WINDOW) | (col >= KWIN)
        _attn_heads(sink_ref, l, q_ref, keys, vals, jnp.where(valid, 0.0, NEG_INF), o_ref)

    @pl.when(i >= N_QBLK_LAT)
    def _context():
        _attn_heads(sink_ref, l, q_ref, [kc_ref[...], kcs_ref[...]], [vc_ref[...], vcs_ref[...]], None, o_ref)


def _attn_call(l, attn_sink, qkv):
    def seq_blk(i):
        return jnp.minimum(i // QBLK_PER_SEQ, 1)

    def ctx_blk(i):
        b = jnp.where(i < N_QBLK_LAT, i // QBLK_PER_SEQ, (i - N_QBLK_LAT) // (CTX // TQ))
        return N_LAT // CTX + b

    seq_specs = [pl.BlockSpec((SEQ, 128), functools.partial(lambda i, col: (seq_blk(i), col), col=col))
                 for col in (4, 5, 6, 7)]
    ctx_specs = [pl.BlockSpec((CTX, 128), functools.partial(lambda i, col: (ctx_blk(i), col), col=col))
                 for col in (4, 5, 6, 7)]
    n_blk = _layer_tiles(l) * TM // TQ
    return pl.pallas_call(
        functools.partial(_attn_kernel, l=l),
        out_shape=jax.ShapeDtypeStruct((n_blk * TQ, GW), BF16),
        grid=(n_blk,),
        in_specs=[pl.BlockSpec(memory_space=pltpu.SMEM),
                  pl.BlockSpec((TQ, GW), lambda i: (i, 0))] + seq_specs + ctx_specs,
        out_specs=pl.BlockSpec((TQ, GW), lambda i: (i, 0)),
        compiler_params=_cparams(1),
        name="window_attention",
    )(attn_sink, qkv, *([qkv] * 8))


FC = 256
FFT_UNROLL = 4


def _dft_tables(n):
    t = n * n
    j = np.arange(n)
    ang = 2.0 * np.pi * np.outer(j, j) / n
    cn, sn = np.cos(ang), np.sin(ang)
    m1 = np.block([[cn, sn], [-sn, cn]])
    m2 = np.concatenate([cn, sn], axis=1)
    angc = 2.0 * np.pi * np.outer(np.arange(128), np.arange(128)) / 128.0
    wc = np.concatenate([np.cos(angc), -np.sin(angc)], axis=1)
    angt = 2.0 * np.pi * np.outer(j, j) / t
    tw_re = np.broadcast_to(np.cos(angt)[:, :, None], (n, n, 128))
    tw_im = np.broadcast_to(-np.sin(angt)[:, :, None], (n, n, 128))
    def hi_lo(m):
        return jnp.stack(_split_bf16(jnp.asarray(m, F32)))

    return (hi_lo(m1), hi_lo(m2), hi_lo(wc),
            jnp.asarray(np.ascontiguousarray(tw_re), F32), jnp.asarray(np.ascontiguousarray(tw_im), F32))


def _fft_kernel(f_ref, m1_ref, m2_ref, wc_ref, twr_ref, twi_ref, o_ref, are, aim, hre, him, *, n):
    t = n * n
    ng = FC // 128
    for g in range(ng):
        fg = f_ref[:, g * 128:(g + 1) * 128]
        a = (jnp.dot(fg, wc_ref[0], preferred_element_type=F32)
             + jnp.dot(fg, wc_ref[1], preferred_element_type=F32))
        are[g] = a[:, :128]
        aim[g] = a[:, 128:]

    def table_dot(m_ref, v):
        return (jnp.dot(m_ref[0], v, preferred_element_type=F32)
                + jnp.dot(m_ref[1], v, preferred_element_type=F32))

    def gather_rows(re, im, first):
        rows = [pl.ds(first + u, n, stride=n) for u in range(FFT_UNROLL)]
        top = jnp.concatenate([re[g, r, :] for r in rows for g in range(ng)], axis=1)
        bot = jnp.concatenate([im[g, r, :] for r in rows for g in range(ng)], axis=1)
        return jnp.concatenate([top, bot], axis=0).astype(BF16)

    def stage1(it, carry):
        t1 = it * FFT_UNROLL
        h = table_dot(m1_ref, gather_rows(are, aim, t1))
        for u in range(FFT_UNROLL):
            dst = pl.ds(pl.multiple_of((t1 + u) * n, n), n)
            twr = twr_ref[t1 + u]
            twi = twi_ref[t1 + u]
            for g in range(ng):
                lanes = slice((u * ng + g) * 128, (u * ng + g + 1) * 128)
                h_re, h_im = h[:n, lanes], h[n:, lanes]
                hre[g, dst, :] = h_re * twr - h_im * twi
                him[g, dst, :] = h_re * twi + h_im * twr
        return carry

    lax.fori_loop(0, n // FFT_UNROLL, stage1, 0)

    def stage2(it, carry):
        k2 = it * FFT_UNROLL
        z = table_dot(m2_ref, gather_rows(hre, him, k2))
        z = z * ((t * 128.0) ** -0.5)
        for u in range(FFT_UNROLL):
            for g in range(ng):
                are[g, pl.ds(k2 + u, n, stride=n), :] = z[:, (u * ng + g) * 128:(u * ng + g + 1) * 128]
        return carry

    lax.fori_loop(0, n // FFT_UNROLL, stage2, 0)
    for g in range(ng):
        o_ref[:, g * 128:(g + 1) * 128] = are[g].astype(BF16)


def _fft_call(f, n, first_blk, n_seq):
    t = n * n
    m1, m2, wc, twr, twi = _dft_tables(n)
    full = lambda shape: pl.BlockSpec(shape, lambda b, h: (0,) * len(shape))
    return pl.pallas_call(
        functools.partial(_fft_kernel, n=n),
        out_shape=jax.ShapeDtypeStruct((n_seq * t, GW), BF16),
        grid=(n_seq, GW // FC),
        in_specs=[pl.BlockSpec((t, FC), lambda b, h: (first_blk + b, h)),
                  full((2, 2 * n, 2 * n)), full((2, n, 2 * n)), full((2, 128, 256)),
                  full((n, n, 128)), full((n, n, 128))],
        out_specs=pl.BlockSpec((t, FC), lambda b, h: (b, h)),
        scratch_shapes=[pltpu.VMEM((FC // 128, t, 128), F32)] * 4,
        compiler_params=_cparams(2),
        name="fourier_mix_%d" % t,
    )(f, m1, m2, wc, twr, twi)


CONV_ROWS = 32
GSH_ROWS = TM + 2 * HALO - 8


def _conv_kernel(sb_ref, p_ref, pp_ref, pn_ref, g_ref, gp_ref, gn_ref, sw_ref, cw_ref, cb_ref, ng_ref, nb_ref,
                 oc_ref, od_ref, pbuf, gbuf, gsh):
    i = pl.program_id(0)
    seq_start = jnp.logical_or(i % SEQ_TILES == 0, i >= LAT_TILES)
    seq_end = jnp.logical_or(i % SEQ_TILES == SEQ_TILES - 1, i >= LAT_TILES)
    for buf, main, prev, nxt in ((pbuf, p_ref, pp_ref, pn_ref), (gbuf, g_ref, gp_ref, gn_ref)):
        buf[0:HALO, :] = jnp.where(seq_start, 0.0, prev[...])
        buf[HALO:HALO + TM, :] = main[...]
        buf[HALO + TM:HALO + TM + HALO, :] = jnp.where(seq_end, 0.0, nxt[...])
    for s in range(8):
        gsh[s] = gbuf[s:s + GSH_ROWS, :]
    sw = sw_ref[...]
    cw = cw_ref[...]
    for r0 in range(0, TM, CONV_ROWS):
        base = HALO + r0
        acc = sw[0:1, :] * pbuf[base - 1:base - 1 + CONV_ROWS, :]
        acc = acc + sw[1:2, :] * pbuf[base:base + CONV_ROWS, :]
        acc = acc + sw[2:3, :] * pbuf[base + 1:base + 1 + CONV_ROWS, :]
        oc_ref[r0:r0 + CONV_ROWS, :] = (sb_ref[r0:r0 + CONV_ROWS, :] * acc).astype(BF16)

        z = None
        for j in range(CONF_W):
            o = base - CONF_PAD + j
            q = o - o % 8
            term = cw[j:j + 1, :] * gsh[o % 8, q:q + CONV_ROWS, :]
            z = term if z is None else z + term
        z = z + cb_ref[...]
        mu = jnp.mean(z, axis=-1, keepdims=True)
        zc = z - mu
        var = jnp.mean(zc * zc, axis=-1, keepdims=True)
        y = zc * lax.rsqrt(var + EPS) * ng_ref[...] + nb_ref[...]
        od_ref[r0:r0 + CONV_ROWS, :] = (y * jax.nn.sigmoid(y)).astype(BF16)


def _conv_call(l, cv, sconv_w, cconv_w, cconv_b, cnorm_g, cnorm_b):
    hb = TM // HALO
    last = NT // HALO - 1

    def halo_specs(col):
        return [pl.BlockSpec((TM, GW), lambda i: (i, col)),
                pl.BlockSpec((HALO, GW), lambda i: (jnp.maximum(i * hb - 1, 0), col)),
                pl.BlockSpec((HALO, GW), lambda i: (jnp.minimum((i + 1) * hb, last), col))]

    vec = lambda: pl.BlockSpec((None, 1, GW), lambda i: (l, 0, 0))
    cw = jnp.pad(cconv_w, ((0, 0), (0, 32 - CONF_W), (0, 0)))
    sw = jnp.pad(sconv_w, ((0, 0), (0, 8 - 3), (0, 0)))
    return pl.pallas_call(
        _conv_kernel,
        out_shape=(jax.ShapeDtypeStruct((_layer_tiles(l) * TM, GW), BF16),
                   jax.ShapeDtypeStruct((_layer_tiles(l) * TM, GW), BF16)),
        grid=(_layer_tiles(l),),
        in_specs=[pl.BlockSpec((TM, GW), lambda i: (i, 0))] + halo_specs(1) + halo_specs(2) + [
            pl.BlockSpec((None, 8, GW), lambda i: (l, 0, 0)),
            pl.BlockSpec((None, 32, GW), lambda i: (l, 0, 0)),
            vec(), vec(), vec()],
        out_specs=(pl.BlockSpec((TM, GW), lambda i: (i, 0)), pl.BlockSpec((TM, GW), lambda i: (i, 0))),
        scratch_shapes=[pltpu.VMEM((TM + 2 * HALO, GW), F32)] * 2 + [pltpu.VMEM((8, GSH_ROWS, GW), F32)],
        compiler_params=_cparams(1),
        name="conv_mixers",
    )(cv, cv, cv, cv, cv, cv, cv, sw, cw, cconv_b.reshape(DEPTH, 1, GW), cnorm_g.reshape(DEPTH, 1, GW),
      cnorm_b.reshape(DEPTH, 1, GW))


R_ROWS = 32
_SLOT_A = np.array([0, 2, 2, 0, 0, 1], np.int32)
_SLOT_B = np.array([1, 1, 3, 3, 2, 3], np.int32)


def _route(lg):
    gl = [lg[N_EXPERTS + g:N_EXPERTS + g + 1, :] for g in range(N_GROUPS)]
    gmax = functools.reduce(jnp.maximum, gl)
    pg = 1.0 / functools.reduce(jnp.add, [jnp.exp(v - gmax) for v in gl])
    gsel = jnp.full_like(gmax, N_GROUPS - 1).astype(I32)
    for g in range(N_GROUPS - 2, -1, -1):
        gsel = jnp.where(gl[g] == gmax, g, gsel)
    el = []
    for j in range(PER_GROUP):
        v = lg[j:j + 1, :]
        for g in range(1, N_GROUPS):
            v = jnp.where(gsel == g, lg[g * PER_GROUP + j:g * PER_GROUP + j + 1, :], v)
        el.append(v)
    v1 = functools.reduce(jnp.maximum, el)
    i1 = jnp.full_like(gsel, PER_GROUP - 1)
    for j in range(PER_GROUP - 2, -1, -1):
        i1 = jnp.where(el[j] == v1, j, i1)
    rest = [jnp.where(i1 == j, -jnp.inf, el[j]) for j in range(PER_GROUP)]
    v2 = functools.reduce(jnp.maximum, rest)
    i2 = jnp.full_like(gsel, PER_GROUP - 1)
    for j in range(PER_GROUP - 2, -1, -1):
        i2 = jnp.where(jnp.logical_and(rest[j] == v2, i1 != j), j, i2)
    e2 = jnp.exp(v2 - v1)
    w1 = pg / (1.0 + e2)
    w2 = pg * e2 / (1.0 + e2)
    lo = jnp.minimum(i1, i2)
    hi = jnp.maximum(i1, i2)
    w_lo = jnp.where(i1 < i2, w1, w2)
    w_hi = jnp.where(i1 < i2, w2, w1)
    pair = jnp.where(lo == 0, jnp.where(hi == 1, 0, jnp.where(hi == 2, 4, 3)),
                     jnp.where(lo == 1, jnp.where(hi == 2, 1, 5), 2))
    cls = gsel * 6 + pair
    a_is_hi = pair == 1
    return cls, jnp.where(a_is_hi, w_hi, w_lo), jnp.where(a_is_hi, w_lo, w_hi)


def _rows_to_slabs(v, stage_ref, out_ref):
    n = v.shape[0]
    for c in range(SLAB):
        stage_ref[pl.ds(c, n, stride=SLAB), :] = v[:, c * 128:(c + 1) * 128]
    out_ref[...] = stage_ref[...].astype(out_ref.dtype)


def _slabs_to_rows(src_ref, stage_ref, n):
    stage_ref[...] = src_ref[...].astype(F32)
    return jnp.concatenate([stage_ref[pl.ds(c, n, stride=SLAB), :] for c in range(SLAB)], axis=1)


def _out_kernel(*refs, l):
    oa_ref, obl_ref, obc_ref, oc_ref, od_ref, w_ref = refs[:6]
    stream, refs = refs[6:6 + _n_stream(l)], refs[6 + _n_stream(l):]
    (mod_ref, g_ref, wrh_ref, wrl_ref, br_ref, tri_ref, w2_ref,
     x1_ref, hp_ref, cls_ref, gw_ref, cnt_ref, w2b_ref, carry, stage) = refs
    i = pl.program_id(0)
    w2b_ref[...] = w2_ref[...].astype(BF16)

    @pl.when(i == 0)
    def _():
        carry[...] = jnp.zeros_like(carry)

    ob = jnp.where(i < LAT_TILES, obl_ref[...], obc_ref[...])
    y = jnp.dot(oa_ref[...], w_ref[0:GW, :], preferred_element_type=F32)
    y = y + jnp.dot(ob, w_ref[GW:2 * GW, :], preferred_element_type=F32)
    y = y + jnp.dot(oc_ref[...], w_ref[2 * GW:3 * GW, :], preferred_element_type=F32)
    y = y + jnp.dot(od_ref[...], w_ref[3 * GW:4 * GW, :], preferred_element_type=F32)
    x1 = _stream_tile(l, stream, stage) + mod_ref[2:3, :] * y
    x1_ref[...] = x1
    h2 = _rmsnorm_mod(x1, g_ref[...], mod_ref[3:4, :], mod_ref[4:5, :])
    _rows_to_slabs(h2, stage, hp_ref)

    h_hi, h_lo = _split_bf16(h2)
    lg = jnp.dot(h_hi, wrh_ref[...], preferred_element_type=F32)
    lg = lg + jnp.dot(h_lo, wrh_ref[...], preferred_element_type=F32)
    lg = lg + jnp.dot(h_hi, wrl_ref[...], preferred_element_type=F32)
    lg = jnp.transpose(lg + br_ref[...])[0:R_ROWS, :]
    cls, w_a, w_b = _route(lg)

    onehot = jnp.where(lax.broadcasted_iota(I32, (R_ROWS, TM), 0) == cls, 1.0, 0.0)
    prefix = jnp.dot(onehot.astype(BF16), tri_ref[...], preferred_element_type=F32)
    rank = jnp.sum(onehot * (prefix - 1.0 + carry[:, 0:1]), axis=0, keepdims=True)
    total = carry[...] + jnp.sum(onehot, axis=1, keepdims=True)
    carry[...] = total
    cnt_ref[...] = total.astype(I32)
    cls_ref[...] = jnp.concatenate([cls, rank.astype(I32), jnp.zeros((6, TM), I32)], axis=0)
    gw_ref[...] = jnp.concatenate([w_a, w_b, jnp.zeros((6, TM), F32)], axis=0)


def _out_call(l, o_a, o_b_lat, o_b_ctx, o_c, o_d, w_out, stream, mod, g_norm2, wr_hi, wr_lo, b_route, w_down_e):
    row = lambda: pl.BlockSpec((TM, GW), lambda i: (i, 0))
    tri = jnp.asarray(np.triu(np.ones((TM, TM), np.float32)), BF16)
    n_rows = _layer_tiles(l) * TM
    return pl.pallas_call(
        functools.partial(_out_kernel, l=l),
        out_shape=(jax.ShapeDtypeStruct((n_rows, D), F32), jax.ShapeDtypeStruct((n_rows * SLAB, 128), BF16),
                   jax.ShapeDtypeStruct((8, n_rows), I32), jax.ShapeDtypeStruct((8, n_rows), F32),
                   jax.ShapeDtypeStruct((R_ROWS, 128), I32),
                   jax.ShapeDtypeStruct((N_EXPERTS * D_EXPERT, D), BF16)),
        grid=(_layer_tiles(l),),
        in_specs=[row(),
                  pl.BlockSpec((TM, GW), lambda i: (jnp.minimum(i, LAT_TILES - 1), 0)),
                  pl.BlockSpec((TM, GW), lambda i: (jnp.maximum(i - LAT_TILES, 0), 0)),
                  row(), row(),
                  pl.BlockSpec((None, D, D), lambda i: (l, 0, 0), pipeline_mode=pl.Buffered(1))]
        + _stream_specs(l) + [
                  pl.BlockSpec((None, None, 8, D), lambda i: (l, _mod_row(i), 0, 0)),
                  pl.BlockSpec((None, 1, D), lambda i: (l, 0, 0)),
                  pl.BlockSpec((None, D, 128), lambda i: (l, 0, 0)),
                  pl.BlockSpec((None, D, 128), lambda i: (l, 0, 0)),
                  pl.BlockSpec((None, 1, 128), lambda i: (l, 0, 0)),
                  pl.BlockSpec((TM, TM), lambda i: (0, 0)),
                  _cast_spec(W2_ROWS, D, l)],
        out_specs=(pl.BlockSpec((TM, D), lambda i: (i, 0)),
                   pl.BlockSpec((TM * SLAB, 128), lambda i: (i, 0)),
                   pl.BlockSpec((8, TM), lambda i: (0, i)), pl.BlockSpec((8, TM), lambda i: (0, i)),
                   pl.BlockSpec((R_ROWS, 128), lambda i: (0, 0)),
                   _cast_out_spec(W2_ROWS, D)),
        scratch_shapes=[pltpu.VMEM((R_ROWS, 128), F32), pltpu.VMEM((TM * SLAB, 128), F32)],
        compiler_params=_cparams(1),
        name="out_proj_route",
    )(o_a, o_b_lat, o_b_ctx, o_c, o_d, w_out, *stream, mod, g_norm2.reshape(DEPTH, 1, D), wr_hi, wr_lo,
      b_route, tri, w_down_e.reshape(DEPTH * N_EXPERTS * D_EXPERT, D))


ROW_UNROLL = 8


def _gate_column(g_ref, src_ref, base):
    sub = lax.broadcasted_iota(I32, (8, 128), 0)
    pieces = []
    for g in range(TME // 8):
        v = jnp.zeros((8, 128), F32)
        for k in range(8):
            v = jnp.where(sub == k, g_ref[src_ref[base + 8 * g + k]], v)
        pieces.append(v)
    return jnp.concatenate(pieces, axis=0)


DMA_PIECES = 8


def _expert_kernel(ea_ref, eb_ref, nvalid_ref, zero_ref, starts_ref, counts_ref, cls_ref, rank_ref, gwa_ref, gwb_ref, hp_hbm, w1a_ref, w3a_ref, w2a_ref, w1b_ref,
                   w3b_ref, w2b_ref, yt_hbm, src_ref, dst_ref, hbuf0, hbuf1, ybuf0, ybuf1, stage, gsem, ssem, zsem,
                   *, n_tok):
    i = pl.program_id(0)
    nvalid = nvalid_ref[0]
    hbufs, ybufs = (hbuf0, hbuf1), (ybuf0, ybuf1)

    def slab(ref, row):
        return ref.at[pl.ds(pl.multiple_of(row * SLAB, SLAB), SLAB), :]

    def gather_copy(tile, sl, j, z=0):
        return pltpu.make_async_copy(slab(hp_hbm, src_ref[tile * TME + j] + z), slab(hbufs[sl], j), gsem.at[sl])

    def scatter_copy(tile, sl, j, z=0):
        return pltpu.make_async_copy(slab(ybufs[sl], j), slab(yt_hbm, dst_ref[(tile + 1) * TME + j] + z),
                                     ssem.at[sl])

    def for_rows(fn):
        def body(j, carry):
            fn(j)
            return carry

        lax.fori_loop(0, TME, body, 0, unroll=ROW_UNROLL)

    @pl.when(i == 0)
    def _():
        def unused(p, carry):
            src_ref[p] = n_tok - 1
            dst_ref[p + TME] = n_tok + ((p + TME) & (2 * TME - 1))
            return carry

        def dummy_tile(j, carry):
            dst_ref[j] = n_tok + j
            src_ref[nvalid * TME + j] = n_tok - 1
            return carry

        lax.fori_loop(0, TME, dummy_tile, 0, unroll=ROW_UNROLL)

        def class_tail(c, carry):
            lo = starts_ref[c] + counts_ref[c]
            hi = starts_ref[c] + ((counts_ref[c] + TME - 1) // TME) * TME
            lax.fori_loop(lo, hi, unused, 0)
            return carry

        lax.fori_loop(0, N_CLASSES, class_tail, 0)

        def place(t, carry):
            p = starts_ref[cls_ref[t]] + rank_ref[t]
            src_ref[p] = t
            dst_ref[p + TME] = t
            return carry

        lax.fori_loop(0, n_tok, place, 0, unroll=ROW_UNROLL)
        for sl in range(2):
            ybufs[sl][...] = jnp.zeros_like(ybufs[sl])
            spare = pltpu.make_async_copy(ybufs[sl], yt_hbm.at[pl.ds((n_tok + sl * TME) * SLAB, TME * SLAB), :],
                                          zsem)
            spare.start()
            spare.wait()
        for_rows(lambda j: gather_copy(0, 0, j).start())

    def tile_body(sl):
        for_rows(lambda j: gather_copy(0, sl, 0).wait())
        h = _slabs_to_rows(hbufs[sl], stage, TME).astype(BF16)
        pieces = iter(range(DMA_PIECES))

        def paced_dot(a, b):
            r = jnp.dot(a, b, preferred_element_type=F32)
            piece = next(pieces, None)
            if piece is None:
                return r
            z = pltpu.bitcast(r[0:8, 0:128], I32)[0, 0] * zero_ref[0]
            first = piece * (TME // DMA_PIECES)
            for j in range(first, first + TME // DMA_PIECES):
                gather_copy(i + 1, 1 - sl, j, z).start(priority=1)
                scatter_copy(i - 1, 1 - sl, j, z).start(priority=1)
            return r

        y = [None] * (D // D_EXPERT)
        for g_ref, w1_ref, w3_ref, w2_ref in ((gwa_ref, w1a_ref, w3a_ref, w2a_ref),
                                              (gwb_ref, w1b_ref, w3b_ref, w2b_ref)):
            half = D_EXPERT // 2
            gate = jnp.concatenate([paced_dot(h, w1_ref[:, c * half:(c + 1) * half]) for c in range(2)], axis=1)
            up = jnp.concatenate([paced_dot(h, w3_ref[:, c * half:(c + 1) * half]) for c in range(2)], axis=1)
            gcol = _gate_column(g_ref, src_ref, i * TME)
            hid = (gate * jax.nn.sigmoid(gate) * up * jnp.concatenate([gcol] * (D_EXPERT // 128), axis=1)).astype(BF16)
            for c in range(len(y)):
                yc = paced_dot(hid, w2_ref[:, c * D_EXPERT:(c + 1) * D_EXPERT])
                y[c] = yc if y[c] is None else y[c] + yc
        y = jnp.concatenate(y, axis=1)

        @pl.when(i >= 1)
        def _():
            for_rows(lambda j: scatter_copy(0, sl, 0).wait())

        _rows_to_slabs(y, stage, ybufs[sl])

        @pl.when(i == nvalid - 1)
        def _():
            for_rows(lambda j: scatter_copy(i, sl, j).start())
            for_rows(lambda j: scatter_copy(0, 1 - sl, 0).wait())
            for_rows(lambda j: scatter_copy(0, sl, 0).wait())
            for_rows(lambda j: gather_copy(0, 1 - sl, 0).wait())

    for sl in range(2):
        @pl.when(jnp.logical_and(i < nvalid, i % 2 == sl))
        def _(sl=sl):
            tile_body(sl)


def _expert_call(tile_ea, tile_eb, nvalid, starts, counts, cls, rank, gwa, gwb, hp, w1b, w3b, w2b):
    n_tok = cls.shape[0]
    up_a = pl.BlockSpec((None, D, D_EXPERT), lambda i, ea, eb, *_: (ea[i], 0, 0))
    up_b = pl.BlockSpec((None, D, D_EXPERT), lambda i, ea, eb, *_: (eb[i], 0, 0))
    down_a = pl.BlockSpec((None, D_EXPERT, D), lambda i, ea, eb, *_: (ea[i], 0, 0))
    down_b = pl.BlockSpec((None, D_EXPERT, D), lambda i, ea, eb, *_: (eb[i], 0, 0))
    return pl.pallas_call(
        functools.partial(_expert_kernel, n_tok=n_tok),
        out_shape=jax.ShapeDtypeStruct(((n_tok + 2 * TME) * SLAB, 128), BF16),
        grid_spec=pltpu.PrefetchScalarGridSpec(
            num_scalar_prefetch=10,
            grid=(P_TILES,),
            in_specs=[pl.BlockSpec(memory_space=pl.ANY),
                      up_a, up_a, down_a, up_b, up_b, down_b],
            out_specs=pl.BlockSpec(memory_space=pl.ANY),
            scratch_shapes=[pltpu.SMEM((P_ROWS + TME,), I32), pltpu.SMEM((P_ROWS + TME,), I32)]
            + [pltpu.VMEM((TME * SLAB, 128), BF16)] * 4
            + [pltpu.VMEM((TME * SLAB, 128), F32),
                            pltpu.SemaphoreType.DMA((2,)), pltpu.SemaphoreType.DMA((2,)),
                            pltpu.SemaphoreType.DMA]),
        compiler_params=_cparams(1),
        name="moe_experts",
    )(tile_ea, tile_eb, nvalid, jnp.zeros((1,), I32), starts, counts, cls, rank, gwa, gwb, hp, w1b, w3b, w2b,
      w1b, w3b, w2b)


def _final_kernel(x1_ref, yt_ref, mod_ref, gf_ref, o_ref, stage):
    x2 = _stream_tile(DEPTH, (x1_ref, yt_ref, mod_ref), stage)
    ms = jnp.mean(x2 * x2, axis=-1, keepdims=True)
    o_ref[...] = x2 * lax.rsqrt(ms + EPS) * gf_ref[...]


def _final_call(x1, mod, g_final, yt):
    return pl.pallas_call(
        _final_kernel,
        out_shape=jax.ShapeDtypeStruct((N_LAT, D), F32),
        grid=(LAT_TILES,),
        in_specs=_stream_specs(DEPTH) + [pl.BlockSpec((1, D), lambda i: (0, 0))],
        out_specs=pl.BlockSpec((TM, D), lambda i: (i, 0)),
        scratch_shapes=[pltpu.VMEM((TM * SLAB, 128), F32)],
        compiler_params=_cparams(1),
        name="final_norm",
    )(x1, yt, mod, g_final.reshape(1, D))


def _slot_plan(counts):
    padded = ((counts + TME - 1) // TME) * TME
    ends = jnp.cumsum(padded)
    starts = ends - padded
    tile_start = jnp.arange(P_TILES, dtype=I32) * TME
    tile_cls = jnp.minimum(jnp.sum((tile_start[:, None] >= ends[None, :]).astype(I32), axis=1), N_CLASSES - 1)
    nvalid = (ends[-1] // TME).reshape(1)
    grp = tile_cls // 6
    tile_ea = grp * PER_GROUP + jnp.asarray(_SLOT_A)[tile_cls % 6]
    tile_eb = grp * PER_GROUP + jnp.asarray(_SLOT_B)[tile_cls % 6]
    return starts, tile_ea, tile_eb, nvalid


def _rope_tables():
    t = jnp.arange(SEQ, dtype=I32)
    row = (t // GRID_W).astype(F32)
    col = (t % GRID_W).astype(F32)
    half = HEAD_DIM // 4
    freqs = ROPE_BASE ** (-jnp.arange(0, 2 * half, 2, dtype=F32) / (2 * half))
    ang_r = row[:, None] * freqs
    ang_c = col[:, None] * freqs
    cos_h = jnp.concatenate([jnp.cos(ang_r), jnp.cos(ang_r), jnp.cos(ang_c), jnp.cos(ang_c)], axis=1)
    sin_h = jnp.concatenate([-jnp.sin(ang_r), jnp.sin(ang_r), -jnp.sin(ang_c), jnp.sin(ang_c)], axis=1)
    cos_t = jnp.concatenate([jnp.tile(cos_h, (1, 2)), jnp.ones((TM, 128), F32)], axis=0)
    sin_t = jnp.concatenate([jnp.tile(sin_h, (1, 2)), jnp.zeros((TM, 128), F32)], axis=0)
    return cos_t, sin_t


def kernel(x, c, ctx, c_ctx, w_mod, b_mod, g_norm1, g_norm2, w_in, w_out, attn_sink, sconv_w, cconv_w, cconv_b,
           cnorm_g, cnorm_b, w_route_group, b_route_group, w_route_expert, b_route_expert, w_gate_e, w_up_e,
           w_down_e, g_final):
    c8 = jnp.zeros((8, D), F32).at[0:BATCH].set(c).at[BATCH].set(c_ctx)
    w_route = jnp.concatenate([w_route_expert, w_route_group,
                               jnp.zeros((DEPTH, D, 128 - N_EXPERTS - N_GROUPS), F32)], axis=2)
    wr_hi = w_route.astype(BF16)
    wr_lo = (w_route - wr_hi.astype(F32)).astype(BF16)
    b_route = jnp.concatenate([b_route_expert, b_route_group,
                               jnp.zeros((DEPTH, 128 - N_EXPERTS - N_GROUPS), F32)], axis=1).reshape(DEPTH, 1, 128)
    cos_t, sin_t = _rope_tables()

    mod, wqkv, wf, wcv, w_out_b = _mod_call(c8, w_mod, b_mod, w_in, w_out)
    mod = mod.reshape(DEPTH, 8, N_MOD, D)[:, :3]
    mod = jnp.pad(mod, ((0, 0), (0, 0), (0, 8 - N_MOD), (0, 0)))

    stream = (x.reshape(N_LAT, D), ctx.reshape(N_CTX, D))
    for l in range(DEPTH):
        last = l == DEPTH - 1
        qkv, f, cv, w1b, w3b = _in_call(l, stream, mod, g_norm1, wqkv, wf, wcv, cos_t, sin_t, w_gate_e, w_up_e)
        o_a = _attn_call(l, attn_sink, qkv)
        o_b_lat = _fft_call(f, 64, 0, BATCH)
        o_b_ctx = o_b_lat if last else _fft_call(f, 16, N_LAT // CTX, BATCH)
        o_c, o_d = _conv_call(l, cv, sconv_w, cconv_w, cconv_b, cnorm_g, cnorm_b)
        x1, hp, cls8, gw8, cnt, w2b = _out_call(l, o_a, o_b_lat, o_b_ctx, o_c, o_d, w_out_b, stream, mod, g_norm2,
                                                wr_hi, wr_lo, b_route, w_down_e)
        counts = cnt[:N_CLASSES, 0]
        starts, tile_ea, tile_eb, nvalid = _slot_plan(counts)
        yt = _expert_call(tile_ea, tile_eb, nvalid, starts, counts, cls8[0], cls8[1], gw8[0], gw8[1], hp,
                          w1b.reshape(N_EXPERTS, D, D_EXPERT), w3b.reshape(N_EXPERTS, D, D_EXPERT),
                          w2b.reshape(N_EXPERTS, D_EXPERT, D))
        stream = (x1, yt, mod)
    return _final_call(x1, mod, g_final, yt).reshape(BATCH, SEQ, D)
```

```python
import functools

import numpy as np
import jax
import jax.numpy as jnp
from jax import lax
from jax.experimental import pallas as pl
from jax.experimental.pallas import tpu as pltpu

F32 = jnp.float32
BF16 = jnp.bfloat16
I32 = jnp.int32

D = 2048
BATCH = 2
SEQ = 4096
CTX = 256
DEPTH = 2
GRID_W = 64
N_LAT = BATCH * SEQ
N_CTX = BATCH * CTX
NT = N_LAT + N_CTX
GW = 512
HEAD_DIM = 64
N_HEADS = 8
WINDOW = 128
ROPE_BASE = 10000.0
N_MOD = 6
EPS = 1e-6
NEG_INF = -1e30
N_GROUPS = 4
PER_GROUP = 4
N_EXPERTS = 16
D_EXPERT = 512
N_CLASSES = N_GROUPS * 6
CONF_W = 31
CONF_PAD = 15

TM = 256
N_TILES = NT // TM
LAT_TILES = N_LAT // TM
SEQ_TILES = SEQ // TM
TQ = 128
HALO = 16
TME = 128
P_ROWS = NT + N_CLASSES * TME
P_TILES = P_ROWS // TME
SLAB = D // 128
VMEM_LIMIT = 56 * 1024 * 1024


def _cparams(n_axes=1, **kw):
    return pltpu.CompilerParams(dimension_semantics=("arbitrary",) * n_axes,
                                vmem_limit_bytes=VMEM_LIMIT, **kw)


def _split_bf16(v):
    hi = v.astype(BF16)
    lo = (v - hi.astype(F32)).astype(BF16)
    return hi, lo


def _mod_row(i):
    return jnp.minimum(i // SEQ_TILES, 2)


MOD_TN = 1536


MOD_STEPS = N_MOD * D // MOD_TN
W_ROWS = D // MOD_STEPS


def _mod_kernel(c_ref, w_ref, b_ref, win_ref, wout_ref, o_ref, wqkv_ref, wf_ref, wcv_ref, woutb_ref):
    w_in = win_ref[...]
    wqkv_ref[:, 0:768] = w_in[:, 0:768].astype(BF16)
    wqkv_ref[:, 768:896] = pltpu.roll(w_in[:, 512:640], HEAD_DIM, 1).astype(BF16)
    wqkv_ref[:, 896:1024] = pltpu.roll(w_in[:, 640:768], HEAD_DIM, 1).astype(BF16)
    wf_ref[...] = w_in[:, 768:768 + GW].astype(BF16)
    wcv_ref[...] = w_in[:, 768 + GW:].astype(BF16)
    woutb_ref[...] = wout_ref[...].astype(BF16)
    c = c_ref[...]
    s = c * jax.nn.sigmoid(c)
    s_hi, s_lo = _split_bf16(s)
    w_hi, w_lo = _split_bf16(w_ref[...])
    acc = jnp.dot(s_hi, w_hi, preferred_element_type=F32)
    acc = acc + jnp.dot(s_lo, w_hi, preferred_element_type=F32)
    acc = acc + jnp.dot(s_hi, w_lo, preferred_element_type=F32)
    o_ref[...] = acc + b_ref[...]


def _mod_call(c8, w_mod, b_mod, w_in, w_out):
    n = N_MOD * D
    rows = lambda cols: pl.BlockSpec((None, W_ROWS, cols), lambda l, j: (l, j, 0))
    return pl.pallas_call(
        _mod_kernel,
        out_shape=(jax.ShapeDtypeStruct((DEPTH, 8, n), F32),
                   jax.ShapeDtypeStruct((DEPTH, D, QKV_W), BF16), jax.ShapeDtypeStruct((DEPTH, D, GW), BF16),
                   jax.ShapeDtypeStruct((DEPTH, D, CV_IN), BF16), jax.ShapeDtypeStruct((DEPTH, D, D), BF16)),
        grid=(DEPTH, MOD_STEPS),
        in_specs=[
            pl.BlockSpec((8, D), lambda l, j: (0, 0)),
            pl.BlockSpec((None, D, MOD_TN), lambda l, j: (l, 0, j)),
            pl.BlockSpec((None, 1, MOD_TN), lambda l, j: (l, 0, j)),
            rows(w_in.shape[-1]), rows(D),
        ],
        out_specs=(pl.BlockSpec((None, 8, MOD_TN), lambda l, j: (l, 0, j)),
                   rows(QKV_W), rows(GW), rows(CV_IN), rows(D)),
        compiler_params=_cparams(2),
        name="adaln_mod",
    )(c8, w_mod, b_mod.reshape(DEPTH, 1, n), w_in, w_out)


QKV_W = 1024
CV_IN = 5 * GW
CV_OUT = 3 * GW


def _rmsnorm_mod(x, g, shift, scale):
    ms = jnp.mean(x * x, axis=-1, keepdims=True)
    y = x * lax.rsqrt(ms + EPS) * g
    return y * (1.0 + scale) + shift


def _stream_specs(l):
    if l == 0:
        return [pl.BlockSpec((TM, D), lambda i, *_: (jnp.minimum(i, LAT_TILES - 1), 0)),
                pl.BlockSpec((TM, D), lambda i, *_: (jnp.maximum(i - LAT_TILES, 0), 0))]
    return [pl.BlockSpec((TM, D), lambda i, *_: (i, 0)),
            pl.BlockSpec((TM * SLAB, 128), lambda i, *_: (i, 0)),
            pl.BlockSpec((None, None, 8, D), lambda i, *_: (l - 1, _mod_row(i), 0, 0))]


def _stream_tile(l, refs, stage):
    if l == 0:
        xl_ref, xc_ref = refs
        return jnp.where(pl.program_id(0) < LAT_TILES, xl_ref[...], xc_ref[...])
    x1_ref, yt_ref, mod_prev_ref = refs
    return x1_ref[...] + mod_prev_ref[5:6, :] * _slabs_to_rows(yt_ref, stage, TM)


def _n_stream(l):
    return 2 if l == 0 else 3


def _layer_tiles(l):
    return LAT_TILES if l == DEPTH - 1 else N_TILES


CAST_STEPS = 32


def _cast_spec(rows, cols, l):
    return pl.BlockSpec((rows, cols), lambda i, *_: (l * CAST_STEPS + jnp.minimum(i, CAST_STEPS - 1), 0))


def _cast_out_spec(rows, cols):
    return pl.BlockSpec((rows, cols), lambda i, *_: (jnp.minimum(i, CAST_STEPS - 1), 0))


def _in_kernel(*refs, l):
    stream, refs = refs[:_n_stream(l)], refs[_n_stream(l):]
    (mod_ref, g_ref, wqkv_ref, wf_ref, wcv_ref, cos_ref, sin_ref, w1_ref, w3_ref,
     qkv_ref, f_ref, cv_ref, w1b_ref, w3b_ref) = refs[:14]
    stage = refs[14] if l > 0 else None
    w1b_ref[...] = w1_ref[...].astype(BF16)
    w3b_ref[...] = w3_ref[...].astype(BF16)
    h = _rmsnorm_mod(_stream_tile(l, stream, stage), g_ref[...], mod_ref[0:1, :], mod_ref[1:2, :])
    hb = h.astype(BF16)
    qkv = jnp.dot(hb, wqkv_ref[...], preferred_element_type=F32)
    cos = cos_ref[...]
    sin = sin_ref[...]
    lane = lax.broadcasted_iota(I32, (TM, 128), 1)
    first_half = (lane % 32) < 16

    def rope(t):
        rot = jnp.where(first_half, pltpu.roll(t, 112, 1), pltpu.roll(t, 16, 1))
        return t * cos + rot * sin

    scale = HEAD_DIM ** -0.5
    for c in range(4):
        qkv_ref[:, c * 128:(c + 1) * 128] = (rope(qkv[:, c * 128:(c + 1) * 128]) * scale).astype(BF16)
    qkv_ref[:, 512:640] = rope(qkv[:, 512:640]).astype(BF16)
    qkv_ref[:, 640:768] = qkv[:, 640:768].astype(BF16)
    qkv_ref[:, 768:896] = rope(qkv[:, 768:896]).astype(BF16)
    qkv_ref[:, 896:1024] = qkv[:, 896:1024].astype(BF16)

    f_ref[...] = jnp.dot(hb, wf_ref[...], preferred_element_type=F32).astype(BF16)

    cv = jnp.dot(hb, wcv_ref[...], preferred_element_type=F32)
    cv_ref[:, 0:GW] = cv[:, 0:GW]
    cv_ref[:, GW:2 * GW] = cv[:, GW:2 * GW] * cv[:, 2 * GW:3 * GW]
    cv_ref[:, 2 * GW:3 * GW] = cv[:, 3 * GW:4 * GW] * jax.nn.sigmoid(cv[:, 4 * GW:5 * GW])


W13_ROWS = N_EXPERTS * D // CAST_STEPS
W2_ROWS = N_EXPERTS * D_EXPERT // CAST_STEPS


def _in_call(l, stream, mod, g_norm1, wqkv, wf, wcv, cos_t, sin_t, w_gate_e, w_up_e):
    const = dict(pipeline_mode=pl.Buffered(1))
    flat = lambda w: w.reshape(DEPTH * N_EXPERTS * D, D_EXPERT)
    return pl.pallas_call(
        functools.partial(_in_kernel, l=l),
        out_shape=(jax.ShapeDtypeStruct((NT, QKV_W), BF16),
                   jax.ShapeDtypeStruct((NT, GW), BF16),
                   jax.ShapeDtypeStruct((NT, CV_OUT), F32),
                   jax.ShapeDtypeStruct((N_EXPERTS * D, D_EXPERT), BF16),
                   jax.ShapeDtypeStruct((N_EXPERTS * D, D_EXPERT), BF16)),
        grid=(N_TILES,),
        in_specs=_stream_specs(l) + [
            pl.BlockSpec((None, None, 8, D), lambda i: (l, _mod_row(i), 0, 0)),
            pl.BlockSpec((None, 1, D), lambda i: (l, 0, 0)),
            pl.BlockSpec((None, D, QKV_W), lambda i: (l, 0, 0), **const),
            pl.BlockSpec((None, D, GW), lambda i: (l, 0, 0), **const),
            pl.BlockSpec((None, D, CV_IN), lambda i: (l, 0, 0), **const),
            pl.BlockSpec((TM, 128), lambda i: (jnp.where(i < LAT_TILES, i % SEQ_TILES, SEQ_TILES), 0)),
            pl.BlockSpec((TM, 128), lambda i: (jnp.where(i < LAT_TILES, i % SEQ_TILES, SEQ_TILES), 0)),
            _cast_spec(W13_ROWS, D_EXPERT, l), _cast_spec(W13_ROWS, D_EXPERT, l),
        ],
        out_specs=(pl.BlockSpec((TM, QKV_W), lambda i: (i, 0)),
                   pl.BlockSpec((TM, GW), lambda i: (i, 0)),
                   pl.BlockSpec((TM, CV_OUT), lambda i: (i, 0)),
                   _cast_out_spec(W13_ROWS, D_EXPERT), _cast_out_spec(W13_ROWS, D_EXPERT)),
        scratch_shapes=[pltpu.VMEM((TM * SLAB, 128), F32)] if l > 0 else [],
        compiler_params=_cparams(1),
        name="norm_in_proj",
    )(*stream, mod, g_norm1.reshape(DEPTH, 1, D), wqkv, wf, wcv, cos_t, sin_t, flat(w_gate_e), flat(w_up_e))


N_QBLK_LAT = N_LAT // TQ
N_QBLK = NT // TQ
QBLK_PER_SEQ = SEQ // TQ
KWIN = 3 * TQ


def _attn_heads(sink_ref, l, q_ref, keys, vals, bias, o_ref, s_scr, e_scr):
    lane = lax.broadcasted_iota(I32, (TQ, 128), 1)
    row2 = lax.broadcasted_iota(I32, (2 * TQ, 1), 0)
    zero = jnp.zeros((TQ, 128), BF16)
    if bias is not None:
        bias = jnp.concatenate([bias, bias], axis=0)
    nk = keys[0].shape[0]
    combos = [(hk, p) for hk in range(2) for p in range(2)]
    for idx, (hk, p) in enumerate(combos):
        kx = keys[0] if hk == p else keys[1]
        in_half = (lane // HEAD_DIM) == p
        qs = jnp.concatenate([jnp.where(in_half, q_ref[:, c * 128:(c + 1) * 128], zero)
                              for c in (2 * hk, 2 * hk + 1)], axis=0)
        s = lax.dot_general(qs, kx, (((1,), (1,)), ((), ())), preferred_element_type=F32)
        s_scr[idx, :, 0:nk] = s if bias is None else s + bias
    dens = []
    for idx, (hk, p) in enumerate(combos):
        s = s_scr[idx, :, 0:nk]
        sink = jnp.where(row2 < TQ, sink_ref[l, 4 * hk + p], sink_ref[l, 4 * hk + 2 + p])
        m = jnp.maximum(jnp.max(s, axis=-1, keepdims=True), sink)
        e = jnp.exp(s - m)
        dens.append(jnp.sum(e, axis=-1, keepdims=True) + jnp.exp(sink - m))
        e_scr[idx, :, 0:nk] = e.astype(BF16)
    for idx, (hk, p) in enumerate(combos):
        vx = vals[0] if hk == p else vals[1]
        o = (jnp.dot(e_scr[idx, :, 0:nk], vx, preferred_element_type=F32) / dens[idx]).astype(BF16)
        for k, c in enumerate((2 * hk, 2 * hk + 1)):
            lo = c * 128 + p * HEAD_DIM
            o_ref[:, lo:lo + HEAD_DIM] = o[k * TQ:(k + 1) * TQ, p * HEAD_DIM:(p + 1) * HEAD_DIM]


def _attn_kernel(sink_ref, q_ref, k_ref, v_ref, ks_ref, vs_ref, kc_ref, vc_ref, kcs_ref, vcs_ref, o_ref, s_scr, e_scr,
                 *, l):
    i = pl.program_id(0)

    @pl.when(i < N_QBLK_LAT)
    def _latent():
        n = i % QBLK_PER_SEQ
        ws = pl.multiple_of(jnp.clip((n - 1) * TQ, 0, SEQ - KWIN), TQ)
        keys = [jnp.concatenate([r[pl.ds(ws, KWIN), :], c[...]], axis=0)
                for r, c in ((k_ref, kc_ref), (ks_ref, kcs_ref))]
        vals = [jnp.concatenate([r[pl.ds(ws, KWIN), :], c[...]], axis=0)
                for r, c in ((v_ref, vc_ref), (vs_ref, vcs_ref))]
        nk = KWIN + CTX
        qpos = n * TQ + lax.broadcasted_iota(I32, (TQ, nk), 0)
        col = lax.broadcasted_iota(I32, (TQ, nk), 1)
        valid = (jnp.abs(ws + col - qpos) <= WINDOW) | (col >= KWIN)
        _attn_heads(sink_ref, l, q_ref, keys, vals, jnp.where(valid, 0.0, NEG_INF), o_ref, s_scr, e_scr)

    @pl.when(i >= N_QBLK_LAT)
    def _context():
        _attn_heads(sink_ref, l, q_ref, [kc_ref[...], kcs_ref[...]], [vc_ref[...], vcs_ref[...]], None, o_ref,
                    s_scr, e_scr)


def _attn_call(l, attn_sink, qkv):
    def seq_blk(i):
        return jnp.minimum(i // QBLK_PER_SEQ, 1)

    def ctx_blk(i):
        b = jnp.where(i < N_QBLK_LAT, i // QBLK_PER_SEQ, (i - N_QBLK_LAT) // (CTX // TQ))
        return N_LAT // CTX + b

    seq_specs = [pl.BlockSpec((SEQ, 128), functools.partial(lambda i, col: (seq_blk(i), col), col=col))
                 for col in (4, 5, 6, 7)]
    ctx_specs = [pl.BlockSpec((CTX, 128), functools.partial(lambda i, col: (ctx_blk(i), col), col=col))
                 for col in (4, 5, 6, 7)]
    n_blk = _layer_tiles(l) * TM // TQ
    return pl.pallas_call(
        functools.partial(_attn_kernel, l=l),
        out_shape=jax.ShapeDtypeStruct((n_blk * TQ, GW), BF16),
        grid=(n_blk,),
        in_specs=[pl.BlockSpec(memory_space=pltpu.SMEM),
                  pl.BlockSpec((TQ, GW), lambda i: (i, 0))] + seq_specs + ctx_specs,
        out_specs=pl.BlockSpec((TQ, GW), lambda i: (i, 0)),
        scratch_shapes=[pltpu.VMEM((4, 2 * TQ, KWIN + CTX), F32), pltpu.VMEM((4, 2 * TQ, KWIN + CTX), BF16)],
        compiler_params=_cparams(1),
        name="window_attention",
    )(attn_sink, qkv, *([qkv] * 8))


FC = 256
FFT_UNROLL = 4


def _dft_tables(n):
    t = n * n
    j = np.arange(n)
    ang = 2.0 * np.pi * np.outer(j, j) / n
    cn, sn = np.cos(ang), np.sin(ang)
    m1 = np.block([[cn, sn], [-sn, cn]])
    m2 = np.concatenate([cn, sn], axis=1)
    angc = 2.0 * np.pi * np.outer(np.arange(128), np.arange(128)) / 128.0
    wc = np.concatenate([np.cos(angc), -np.sin(angc)], axis=1)
    angt = 2.0 * np.pi * np.outer(j, j) / t
    tw_re = np.broadcast_to(np.cos(angt)[:, :, None], (n, n, 128))
    tw_im = np.broadcast_to(-np.sin(angt)[:, :, None], (n, n, 128))
    def hi_lo(m):
        return jnp.stack(_split_bf16(jnp.asarray(m, F32)))

    return (hi_lo(m1), hi_lo(m2), hi_lo(wc),
            jnp.asarray(np.ascontiguousarray(tw_re), F32), jnp.asarray(np.ascontiguousarray(tw_im), F32))


def _fft_kernel(f_ref, m1_ref, m2_ref, wc_ref, twr_ref, twi_ref, o_ref, are, aim, hre, him, *, n):
    t = n * n
    ng = FC // 128
    for g in range(ng):
        fg = f_ref[:, g * 128:(g + 1) * 128]
        a = (jnp.dot(fg, wc_ref[0], preferred_element_type=F32)
             + jnp.dot(fg, wc_ref[1], preferred_element_type=F32))
        are[g] = a[:, :128]
        aim[g] = a[:, 128:]

    def table_dot(m_ref, v):
        return (jnp.dot(m_ref[0], v, preferred_element_type=F32)
                + jnp.dot(m_ref[1], v, preferred_element_type=F32))

    def gather_rows(re, im, first):
        rows = [pl.ds(first + u, n, stride=n) for u in range(FFT_UNROLL)]
        top = jnp.concatenate([re[g, r, :] for r in rows for g in range(ng)], axis=1)
        bot = jnp.concatenate([im[g, r, :] for r in rows for g in range(ng)], axis=1)
        return jnp.concatenate([top, bot], axis=0).astype(BF16)

    def stage1(it, carry):
        t1 = it * FFT_UNROLL
        h = table_dot(m1_ref, gather_rows(are, aim, t1))
        for u in range(FFT_UNROLL):
            dst = pl.ds(pl.multiple_of((t1 + u) * n, n), n)
            twr = twr_ref[t1 + u]
            twi = twi_ref[t1 + u]
            for g in range(ng):
                lanes = slice((u * ng + g) * 128, (u * ng + g + 1) * 128)
                h_re, h_im = h[:n, lanes], h[n:, lanes]
                hre[g, dst, :] = h_re * twr - h_im * twi
                him[g, dst, :] = h_re * twi + h_im * twr
        return carry

    lax.fori_loop(0, n // FFT_UNROLL, stage1, 0)

    def stage2(it, carry):
        k2 = it * FFT_UNROLL
        z = table_dot(m2_ref, gather_rows(hre, him, k2))
        z = z * ((t * 128.0) ** -0.5)
        for u in range(FFT_UNROLL):
            for g in range(ng):
                are[g, pl.ds(k2 + u, n, stride=n), :] = z[:, (u * ng + g) * 128:(u * ng + g + 1) * 128]
        return carry

    lax.fori_loop(0, n // FFT_UNROLL, stage2, 0)
    for g in range(ng):
        o_ref[:, g * 128:(g + 1) * 128] = are[g].astype(BF16)


def _fft_call(f, n, first_blk, n_seq):
    t = n * n
    m1, m2, wc, twr, twi = _dft_tables(n)
    full = lambda shape: pl.BlockSpec(shape, lambda b, h: (0,) * len(shape))
    return pl.pallas_call(
        functools.partial(_fft_kernel, n=n),
        out_shape=jax.ShapeDtypeStruct((n_seq * t, GW), BF16),
        grid=(n_seq, GW // FC),
        in_specs=[pl.BlockSpec((t, FC), lambda b, h: (first_blk + b, h)),
                  full((2, 2 * n, 2 * n)), full((2, n, 2 * n)), full((2, 128, 256)),
                  full((n, n, 128)), full((n, n, 128))],
        out_specs=pl.BlockSpec((t, FC), lambda b, h: (b, h)),
        scratch_shapes=[pltpu.VMEM((FC // 128, t, 128), F32)] * 4,
        compiler_params=_cparams(2),
        name="fourier_mix_%d" % t,
    )(f, m1, m2, wc, twr, twi)


CONV_ROWS = 32
GSH_ROWS = TM + 2 * HALO - 8


def _conv_kernel(sb_ref, p_ref, pp_ref, pn_ref, g_ref, gp_ref, gn_ref, sw_ref, cw_ref, cb_ref, ng_ref, nb_ref,
                 oc_ref, od_ref, pbuf, gbuf, gsh):
    i = pl.program_id(0)
    seq_start = jnp.logical_or(i % SEQ_TILES == 0, i >= LAT_TILES)
    seq_end = jnp.logical_or(i % SEQ_TILES == SEQ_TILES - 1, i >= LAT_TILES)
    for buf, main, prev, nxt in ((pbuf, p_ref, pp_ref, pn_ref), (gbuf, g_ref, gp_ref, gn_ref)):
        buf[0:HALO, :] = jnp.where(seq_start, 0.0, prev[...])
        buf[HALO:HALO + TM, :] = main[...]
        buf[HALO + TM:HALO + TM + HALO, :] = jnp.where(seq_end, 0.0, nxt[...])
    for s in range(8):
        gsh[s] = gbuf[s:s + GSH_ROWS, :]
    sw = sw_ref[...]
    cw = cw_ref[...]
    for r0 in range(0, TM, CONV_ROWS):
        base = HALO + r0
        acc = sw[0:1, :] * pbuf[base - 1:base - 1 + CONV_ROWS, :]
        acc = acc + sw[1:2, :] * pbuf[base:base + CONV_ROWS, :]
        acc = acc + sw[2:3, :] * pbuf[base + 1:base + 1 + CONV_ROWS, :]
        oc_ref[r0:r0 + CONV_ROWS, :] = (sb_ref[r0:r0 + CONV_ROWS, :] * acc).astype(BF16)

        z = None
        for j in range(CONF_W):
            o = base - CONF_PAD + j
            q = o - o % 8
            term = cw[j:j + 1, :] * gsh[o % 8, q:q + CONV_ROWS, :]
            z = term if z is None else z + term
        z = z + cb_ref[...]
        mu = jnp.mean(z, axis=-1, keepdims=True)
        zc = z - mu
        var = jnp.mean(zc * zc, axis=-1, keepdims=True)
        y = zc * lax.rsqrt(var + EPS) * ng_ref[...] + nb_ref[...]
        od_ref[r0:r0 + CONV_ROWS, :] = (y * jax.nn.sigmoid(y)).astype(BF16)


def _conv_call(l, cv, sconv_w, cconv_w, cconv_b, cnorm_g, cnorm_b):
    hb = TM // HALO
    last = NT // HALO - 1

    def halo_specs(col):
        return [pl.BlockSpec((TM, GW), lambda i: (i, col)),
                pl.BlockSpec((HALO, GW), lambda i: (jnp.maximum(i * hb - 1, 0), col)),
                pl.BlockSpec((HALO, GW), lambda i: (jnp.minimum((i + 1) * hb, last), col))]

    vec = lambda: pl.BlockSpec((None, 1, GW), lambda i: (l, 0, 0))
    cw = jnp.pad(cconv_w, ((0, 0), (0, 32 - CONF_W), (0, 0)))
    sw = jnp.pad(sconv_w, ((0, 0), (0, 8 - 3), (0, 0)))
    return pl.pallas_call(
        _conv_kernel,
        out_shape=(jax.ShapeDtypeStruct((_layer_tiles(l) * TM, GW), BF16),
                   jax.ShapeDtypeStruct((_layer_tiles(l) * TM, GW), BF16)),
        grid=(_layer_tiles(l),),
        in_specs=[pl.BlockSpec((TM, GW), lambda i: (i, 0))] + halo_specs(1) + halo_specs(2) + [
            pl.BlockSpec((None, 8, GW), lambda i: (l, 0, 0)),
            pl.BlockSpec((None, 32, GW), lambda i: (l, 0, 0)),
            vec(), vec(), vec()],
        out_specs=(pl.BlockSpec((TM, GW), lambda i: (i, 0)), pl.BlockSpec((TM, GW), lambda i: (i, 0))),
        scratch_shapes=[pltpu.VMEM((TM + 2 * HALO, GW), F32)] * 2 + [pltpu.VMEM((8, GSH_ROWS, GW), F32)],
        compiler_params=_cparams(1),
        name="conv_mixers",
    )(cv, cv, cv, cv, cv, cv, cv, sw, cw, cconv_b.reshape(DEPTH, 1, GW), cnorm_g.reshape(DEPTH, 1, GW),
      cnorm_b.reshape(DEPTH, 1, GW))


R_ROWS = 32
_SLOT_A = np.array([0, 2, 2, 0, 0, 1], np.int32)
_SLOT_B = np.array([1, 1, 3, 3, 2, 3], np.int32)


def _route(lg):
    gl = [lg[N_EXPERTS + g:N_EXPERTS + g + 1, :] for g in range(N_GROUPS)]
    gmax = functools.reduce(jnp.maximum, gl)
    pg = 1.0 / functools.reduce(jnp.add, [jnp.exp(v - gmax) for v in gl])
    gsel = jnp.full_like(gmax, N_GROUPS - 1).astype(I32)
    for g in range(N_GROUPS - 2, -1, -1):
        gsel = jnp.where(gl[g] == gmax, g, gsel)
    el = []
    for j in range(PER_GROUP):
        v = lg[j:j + 1, :]
        for g in range(1, N_GROUPS):
            v = jnp.where(gsel == g, lg[g * PER_GROUP + j:g * PER_GROUP + j + 1, :], v)
        el.append(v)
    v1 = functools.reduce(jnp.maximum, el)
    i1 = jnp.full_like(gsel, PER_GROUP - 1)
    for j in range(PER_GROUP - 2, -1, -1):
        i1 = jnp.where(el[j] == v1, j, i1)
    rest = [jnp.where(i1 == j, -jnp.inf, el[j]) for j in range(PER_GROUP)]
    v2 = functools.reduce(jnp.maximum, rest)
    i2 = jnp.full_like(gsel, PER_GROUP - 1)
    for j in range(PER_GROUP - 2, -1, -1):
        i2 = jnp.where(jnp.logical_and(rest[j] == v2, i1 != j), j, i2)
    e2 = jnp.exp(v2 - v1)
    w1 = pg / (1.0 + e2)
    w2 = pg * e2 / (1.0 + e2)
    lo = jnp.minimum(i1, i2)
    hi = jnp.maximum(i1, i2)
    w_lo = jnp.where(i1 < i2, w1, w2)
    w_hi = jnp.where(i1 < i2, w2, w1)
    pair = jnp.where(lo == 0, jnp.where(hi == 1, 0, jnp.where(hi == 2, 4, 3)),
                     jnp.where(lo == 1, jnp.where(hi == 2, 1, 5), 2))
    cls = gsel * 6 + pair
    a_is_hi = pair == 1
    return cls, jnp.where(a_is_hi, w_hi, w_lo), jnp.where(a_is_hi, w_lo, w_hi)


def _rows_to_slabs(v, stage_ref, out_ref):
    n = v.shape[0]
    for c in range(SLAB):
        stage_ref[pl.ds(c, n, stride=SLAB), :] = v[:, c * 128:(c + 1) * 128]
    out_ref[...] = stage_ref[...].astype(out_ref.dtype)


def _slabs_to_rows(src_ref, stage_ref, n):
    stage_ref[...] = src_ref[...].astype(F32)
    return jnp.concatenate([stage_ref[pl.ds(c, n, stride=SLAB), :] for c in range(SLAB)], axis=1)


def _out_kernel(*refs, l):
    oa_ref, obl_ref, obc_ref, oc_ref, od_ref, w_ref = refs[:6]
    stream, refs = refs[6:6 + _n_stream(l)], refs[6 + _n_stream(l):]
    (mod_ref, g_ref, wrh_ref, wrl_ref, br_ref, tri_ref, w2_ref,
     x1_ref, hp_ref, cls_ref, gw_ref, cnt_ref, w2b_ref, carry, stage) = refs
    i = pl.program_id(0)
    w2b_ref[...] = w2_ref[...].astype(BF16)

    @pl.when(i == 0)
    def _():
        carry[...] = jnp.zeros_like(carry)

    ob = jnp.where(i < LAT_TILES, obl_ref[...], obc_ref[...])
    y = jnp.dot(oa_ref[...], w_ref[0:GW, :], preferred_element_type=F32)
    y = y + jnp.dot(ob, w_ref[GW:2 * GW, :], preferred_element_type=F32)
    y = y + jnp.dot(oc_ref[...], w_ref[2 * GW:3 * GW, :], preferred_element_type=F32)
    y = y + jnp.dot(od_ref[...], w_ref[3 * GW:4 * GW, :], preferred_element_type=F32)
    x1 = _stream_tile(l, stream, stage) + mod_ref[2:3, :] * y
    x1_ref[...] = x1
    h2 = _rmsnorm_mod(x1, g_ref[...], mod_ref[3:4, :], mod_ref[4:5, :])
    _rows_to_slabs(h2, stage, hp_ref)

    h_hi, h_lo = _split_bf16(h2)
    lg = jnp.dot(h_hi, wrh_ref[...], preferred_element_type=F32)
    lg = lg + jnp.dot(h_lo, wrh_ref[...], preferred_element_type=F32)
    lg = lg + jnp.dot(h_hi, wrl_ref[...], preferred_element_type=F32)
    lg = jnp.transpose(lg + br_ref[...])[0:R_ROWS, :]
    cls, w_a, w_b = _route(lg)

    onehot = jnp.where(lax.broadcasted_iota(I32, (R_ROWS, TM), 0) == cls, 1.0, 0.0)
    prefix = jnp.dot(onehot.astype(BF16), tri_ref[...], preferred_element_type=F32)
    rank = jnp.sum(onehot * (prefix - 1.0 + carry[:, 0:1]), axis=0, keepdims=True)
    total = carry[...] + jnp.sum(onehot, axis=1, keepdims=True)
    carry[...] = total
    cnt_ref[...] = total.astype(I32)
    cls_ref[...] = jnp.concatenate([cls, rank.astype(I32), jnp.zeros((6, TM), I32)], axis=0)
    gw_ref[...] = jnp.concatenate([w_a, w_b, jnp.zeros((6, TM), F32)], axis=0)


def _out_call(l, o_a, o_b_lat, o_b_ctx, o_c, o_d, w_out, stream, mod, g_norm2, wr_hi, wr_lo, b_route, w_down_e):
    row = lambda: pl.BlockSpec((TM, GW), lambda i: (i, 0))
    tri = jnp.asarray(np.triu(np.ones((TM, TM), np.float32)), BF16)
    n_rows = _layer_tiles(l) * TM
    return pl.pallas_call(
        functools.partial(_out_kernel, l=l),
        out_shape=(jax.ShapeDtypeStruct((n_rows, D), F32), jax.ShapeDtypeStruct((n_rows * SLAB, 128), BF16),
                   jax.ShapeDtypeStruct((8, n_rows), I32), jax.ShapeDtypeStruct((8, n_rows), F32),
                   jax.ShapeDtypeStruct((R_ROWS, 128), I32),
                   jax.ShapeDtypeStruct((N_EXPERTS * D_EXPERT, D), BF16)),
        grid=(_layer_tiles(l),),
        in_specs=[row(),
                  pl.BlockSpec((TM, GW), lambda i: (jnp.minimum(i, LAT_TILES - 1), 0)),
                  pl.BlockSpec((TM, GW), lambda i: (jnp.maximum(i - LAT_TILES, 0), 0)),
                  row(), row(),
                  pl.BlockSpec((None, D, D), lambda i: (l, 0, 0), pipeline_mode=pl.Buffered(1))]
        + _stream_specs(l) + [
                  pl.BlockSpec((None, None, 8, D), lambda i: (l, _mod_row(i), 0, 0)),
                  pl.BlockSpec((None, 1, D), lambda i: (l, 0, 0)),
                  pl.BlockSpec((None, D, 128), lambda i: (l, 0, 0)),
                  pl.BlockSpec((None, D, 128), lambda i: (l, 0, 0)),
                  pl.BlockSpec((None, 1, 128), lambda i: (l, 0, 0)),
                  pl.BlockSpec((TM, TM), lambda i: (0, 0)),
                  _cast_spec(W2_ROWS, D, l)],
        out_specs=(pl.BlockSpec((TM, D), lambda i: (i, 0)),
                   pl.BlockSpec((TM * SLAB, 128), lambda i: (i, 0)),
                   pl.BlockSpec((8, TM), lambda i: (0, i)), pl.BlockSpec((8, TM), lambda i: (0, i)),
                   pl.BlockSpec((R_ROWS, 128), lambda i: (0, 0)),
                   _cast_out_spec(W2_ROWS, D)),
        scratch_shapes=[pltpu.VMEM((R_ROWS, 128), F32), pltpu.VMEM((TM * SLAB, 128), F32)],
        compiler_params=_cparams(1),
        name="out_proj_route",
    )(o_a, o_b_lat, o_b_ctx, o_c, o_d, w_out, *stream, mod, g_norm2.reshape(DEPTH, 1, D), wr_hi, wr_lo,
      b_route, tri, w_down_e.reshape(DEPTH * N_EXPERTS * D_EXPERT, D))


ROW_UNROLL = 8


def _gate_column(g_ref, src_ref, base):
    sub = lax.broadcasted_iota(I32, (8, 128), 0)
    pieces = []
    for g in range(TME // 8):
        v = jnp.zeros((8, 128), F32)
        for k in range(8):
            v = jnp.where(sub == k, g_ref[src_ref[base + 8 * g + k]], v)
        pieces.append(v)
    return jnp.concatenate(pieces, axis=0)


DMA_PIECES = 8


def _expert_kernel(ea_ref, eb_ref, nvalid_ref, zero_ref, starts_ref, counts_ref, cls_ref, rank_ref, gwa_ref, gwb_ref, hp_hbm, w1a_ref, w3a_ref, w2a_ref, w1b_ref,
                   w3b_ref, w2b_ref, yt_hbm, src_ref, dst_ref, hbuf0, hbuf1, ybuf0, ybuf1, stage, gsem, ssem, zsem,
                   *, n_tok):
    i = pl.program_id(0)
    nvalid = nvalid_ref[0]
    hbufs, ybufs = (hbuf0, hbuf1), (ybuf0, ybuf1)

    def slab(ref, row):
        return ref.at[pl.ds(pl.multiple_of(row * SLAB, SLAB), SLAB), :]

    def gather_copy(tile, sl, j, z=0):
        return pltpu.make_async_copy(slab(hp_hbm, src_ref[tile * TME + j] + z), slab(hbufs[sl], j), gsem.at[sl])

    def scatter_copy(tile, sl, j, z=0):
        return pltpu.make_async_copy(slab(ybufs[sl], j), slab(yt_hbm, dst_ref[(tile + 1) * TME + j] + z),
                                     ssem.at[sl])

    def for_rows(fn):
        def body(j, carry):
            fn(j)
            return carry

        lax.fori_loop(0, TME, body, 0, unroll=ROW_UNROLL)

    @pl.when(i == 0)
    def _():
        def unused(p, carry):
            src_ref[p] = n_tok - 1
            dst_ref[p + TME] = n_tok + ((p + TME) & (2 * TME - 1))
            return carry

        def dummy_tile(j, carry):
            dst_ref[j] = n_tok + j
            src_ref[nvalid * TME + j] = n_tok - 1
            return carry

        lax.fori_loop(0, TME, dummy_tile, 0, unroll=ROW_UNROLL)

        def class_tail(c, carry):
            lo = starts_ref[c] + counts_ref[c]
            hi = starts_ref[c] + ((counts_ref[c] + TME - 1) // TME) * TME
            lax.fori_loop(lo, hi, unused, 0)
            return carry

        lax.fori_loop(0, N_CLASSES, class_tail, 0)

        def place(t, carry):
            p = starts_ref[cls_ref[t]] + rank_ref[t]
            src_ref[p] = t
            dst_ref[p + TME] = t
            return carry

        lax.fori_loop(0, n_tok, place, 0, unroll=ROW_UNROLL)
        for sl in range(2):
            ybufs[sl][...] = jnp.zeros_like(ybufs[sl])
            spare = pltpu.make_async_copy(ybufs[sl], yt_hbm.at[pl.ds((n_tok + sl * TME) * SLAB, TME * SLAB), :],
                                          zsem)
            spare.start()
            spare.wait()
        for_rows(lambda j: gather_copy(0, 0, j).start())

    def tile_body(sl):
        for_rows(lambda j: gather_copy(0, sl, 0).wait())
        h = _slabs_to_rows(hbufs[sl], stage, TME).astype(BF16)
        pieces = iter(range(DMA_PIECES))

        def paced_dot(a, b):
            r = jnp.dot(a, b, preferred_element_type=F32)
            piece = next(pieces, None)
            if piece is None:
                return r
            z = pltpu.bitcast(r[0:8, 0:128], I32)[0, 0] * zero_ref[0]
            first = piece * (TME // DMA_PIECES)
            for j in range(first, first + TME // DMA_PIECES):
                gather_copy(i + 1, 1 - sl, j, z).start()
                scatter_copy(i - 1, 1 - sl, j, z).start(priority=1)
            return r

        y = [None] * (D // D_EXPERT)
        for g_ref, w1_ref, w3_ref, w2_ref in ((gwa_ref, w1a_ref, w3a_ref, w2a_ref),
                                              (gwb_ref, w1b_ref, w3b_ref, w2b_ref)):
            half = D_EXPERT // 2
            gate = jnp.concatenate([paced_dot(h, w1_ref[:, c * half:(c + 1) * half]) for c in range(2)], axis=1)
            up = jnp.concatenate([paced_dot(h, w3_ref[:, c * half:(c + 1) * half]) for c in range(2)], axis=1)
            gcol = _gate_column(g_ref, src_ref, i * TME)
            hid = (gate * jax.nn.sigmoid(gate) * up * jnp.concatenate([gcol] * (D_EXPERT // 128), axis=1)).astype(BF16)
            for c in range(len(y)):
                yc = paced_dot(hid, w2_ref[:, c * D_EXPERT:(c + 1) * D_EXPERT])
                y[c] = yc if y[c] is None else y[c] + yc
        y = jnp.concatenate(y, axis=1)

        @pl.when(i >= 1)
        def _():
            for_rows(lambda j: scatter_copy(0, sl, 0).wait())

        _rows_to_slabs(y, stage, ybufs[sl])

        @pl.when(i == nvalid - 1)
        def _():
            for_rows(lambda j: scatter_copy(i, sl, j).start())
            for_rows(lambda j: scatter_copy(0, 1 - sl, 0).wait())
            for_rows(lambda j: scatter_copy(0, sl, 0).wait())
            for_rows(lambda j: gather_copy(0, 1 - sl, 0).wait())

    for sl in range(2):
        @pl.when(jnp.logical_and(i < nvalid, i % 2 == sl))
        def _(sl=sl):
            tile_body(sl)


def _expert_call(tile_ea, tile_eb, nvalid, starts, counts, cls, rank, gwa, gwb, hp, w1b, w3b, w2b):
    n_tok = cls.shape[0]
    up_a = pl.BlockSpec((None, D, D_EXPERT), lambda i, ea, eb, *_: (ea[i], 0, 0))
    up_b = pl.BlockSpec((None, D, D_EXPERT), lambda i, ea, eb, *_: (eb[i], 0, 0))
    down_a = pl.BlockSpec((None, D_EXPERT, D), lambda i, ea, eb, *_: (ea[i], 0, 0))
    down_b = pl.BlockSpec((None, D_EXPERT, D), lambda i, ea, eb, *_: (eb[i], 0, 0))
    return pl.pallas_call(
        functools.partial(_expert_kernel, n_tok=n_tok),
        out_shape=jax.ShapeDtypeStruct(((n_tok + 2 * TME) * SLAB, 128), BF16),
        grid_spec=pltpu.PrefetchScalarGridSpec(
            num_scalar_prefetch=10,
            grid=(P_TILES,),
            in_specs=[pl.BlockSpec(memory_space=pl.ANY),
                      up_a, up_a, down_a, up_b, up_b, down_b],
            out_specs=pl.BlockSpec(memory_space=pl.ANY),
            scratch_shapes=[pltpu.SMEM((P_ROWS + TME,), I32), pltpu.SMEM((P_ROWS + TME,), I32)]
            + [pltpu.VMEM((TME * SLAB, 128), BF16)] * 4
            + [pltpu.VMEM((TME * SLAB, 128), F32),
                            pltpu.SemaphoreType.DMA((2,)), pltpu.SemaphoreType.DMA((2,)),
                            pltpu.SemaphoreType.DMA]),
        compiler_params=_cparams(1),
        name="moe_experts",
    )(tile_ea, tile_eb, nvalid, jnp.zeros((1,), I32), starts, counts, cls, rank, gwa, gwb, hp, w1b, w3b, w2b,
      w1b, w3b, w2b)


def _final_kernel(x1_ref, yt_ref, mod_ref, gf_ref, o_ref, stage):
    x2 = _stream_tile(DEPTH, (x1_ref, yt_ref, mod_ref), stage)
    ms = jnp.mean(x2 * x2, axis=-1, keepdims=True)
    o_ref[...] = x2 * lax.rsqrt(ms + EPS) * gf_ref[...]


def _final_call(x1, mod, g_final, yt):
    return pl.pallas_call(
        _final_kernel,
        out_shape=jax.ShapeDtypeStruct((N_LAT, D), F32),
        grid=(LAT_TILES,),
        in_specs=_stream_specs(DEPTH) + [pl.BlockSpec((1, D), lambda i: (0, 0))],
        out_specs=pl.BlockSpec((TM, D), lambda i: (i, 0)),
        scratch_shapes=[pltpu.VMEM((TM * SLAB, 128), F32)],
        compiler_params=_cparams(1),
        name="final_norm",
    )(x1, yt, mod, g_final.reshape(1, D))


def _slot_plan(counts):
    padded = ((counts + TME - 1) // TME) * TME
    ends = jnp.cumsum(padded)
    starts = ends - padded
    tile_start = jnp.arange(P_TILES, dtype=I32) * TME
    tile_cls = jnp.minimum(jnp.sum((tile_start[:, None] >= ends[None, :]).astype(I32), axis=1), N_CLASSES - 1)
    nvalid = (ends[-1] // TME).reshape(1)
    grp = tile_cls // 6
    tile_ea = grp * PER_GROUP + jnp.asarray(_SLOT_A)[tile_cls % 6]
    tile_eb = grp * PER_GROUP + jnp.asarray(_SLOT_B)[tile_cls % 6]
    return starts, tile_ea, tile_eb, nvalid


def _rope_tables():
    t = jnp.arange(SEQ, dtype=I32)
    row = (t // GRID_W).astype(F32)
    col = (t % GRID_W).astype(F32)
    half = HEAD_DIM // 4
    freqs = ROPE_BASE ** (-jnp.arange(0, 2 * half, 2, dtype=F32) / (2 * half))
    ang_r = row[:, None] * freqs
    ang_c = col[:, None] * freqs
    cos_h = jnp.concatenate([jnp.cos(ang_r), jnp.cos(ang_r), jnp.cos(ang_c), jnp.cos(ang_c)], axis=1)
    sin_h = jnp.concatenate([-jnp.sin(ang_r), jnp.sin(ang_r), -jnp.sin(ang_c), jnp.sin(ang_c)], axis=1)
    cos_t = jnp.concatenate([jnp.tile(cos_h, (1, 2)), jnp.ones((TM, 128), F32)], axis=0)
    sin_t = jnp.concatenate([jnp.tile(sin_h, (1, 2)), jnp.zeros((TM, 128), F32)], axis=0)
    return cos_t, sin_t


def kernel(x, c, ctx, c_ctx, w_mod, b_mod, g_norm1, g_norm2, w_in, w_out, attn_sink, sconv_w, cconv_w, cconv_b,
           cnorm_g, cnorm_b, w_route_group, b_route_group, w_route_expert, b_route_expert, w_gate_e, w_up_e,
           w_down_e, g_final):
    c8 = jnp.zeros((8, D), F32).at[0:BATCH].set(c).at[BATCH].set(c_ctx)
    w_route = jnp.concatenate([w_route_expert, w_route_group,
                               jnp.zeros((DEPTH, D, 128 - N_EXPERTS - N_GROUPS), F32)], axis=2)
    wr_hi = w_route.astype(BF16)
    wr_lo = (w_route - wr_hi.astype(F32)).astype(BF16)
    b_route = jnp.concatenate([b_route_expert, b_route_group,
                               jnp.zeros((DEPTH, 128 - N_EXPERTS - N_GROUPS), F32)], axis=1).reshape(DEPTH, 1, 128)
    cos_t, sin_t = _rope_tables()

    mod, wqkv, wf, wcv, w_out_b = _mod_call(c8, w_mod, b_mod, w_in, w_out)
    mod = mod.reshape(DEPTH, 8, N_MOD, D)[:, :3]
    mod = jnp.pad(mod, ((0, 0), (0, 0), (0, 8 - N_MOD), (0, 0)))

    stream = (x.reshape(N_LAT, D), ctx.reshape(N_CTX, D))
    for l in range(DEPTH):
        last = l == DEPTH - 1
        qkv, f, cv, w1b, w3b = _in_call(l, stream, mod, g_norm1, wqkv, wf, wcv, cos_t, sin_t, w_gate_e, w_up_e)
        o_a = _attn_call(l, attn_sink, qkv)
        o_b_lat = _fft_call(f, 64, 0, BATCH)
        o_b_ctx = o_b_lat if last else _fft_call(f, 16, N_LAT // CTX, BATCH)
        o_c, o_d = _conv_call(l, cv, sconv_w, cconv_w, cconv_b, cnorm_g, cnorm_b)
        x1, hp, cls8, gw8, cnt, w2b = _out_call(l, o_a, o_b_lat, o_b_ctx, o_c, o_d, w_out_b, stream, mod, g_norm2,
                                                wr_hi, wr_lo, b_route, w_down_e)
        counts = cnt[:N_CLASSES, 0]
        starts, tile_ea, tile_eb, nvalid = _slot_plan(counts)
        yt = _expert_call(tile_ea, tile_eb, nvalid, starts, counts, cls8[0], cls8[1], gw8[0], gw8[1], hp,
                          w1b.reshape(N_EXPERTS, D, D_EXPERT), w3b.reshape(N_EXPERTS, D, D_EXPERT),
                          w2b.reshape(N_EXPERTS, D_EXPERT, D))
        stream = (x1, yt, mod)
    return _final_call(x1, mod, g_final, yt).reshape(BATCH, SEQ, D)
```

```python
import functools

import numpy as np
import jax
import jax.numpy as jnp
from jax import lax
from jax.experimental import pallas as pl
from jax.experimental.pallas import tpu as pltpu

F32 = jnp.float32
BF16 = jnp.bfloat16
I32 = jnp.int32

D = 2048
BATCH = 2
SEQ = 4096
CTX = 256
DEPTH = 2
GRID_W = 64
N_LAT = BATCH * SEQ
N_CTX = BATCH * CTX
NT = N_LAT + N_CTX
GW = 512
HEAD_DIM = 64
N_HEADS = 8
WINDOW = 128
ROPE_BASE = 10000.0
N_MOD = 6
EPS = 1e-6
NEG_INF = -1e30
N_GROUPS = 4
PER_GROUP = 4
N_EXPERTS = 16
D_EXPERT = 512
N_CLASSES = N_GROUPS * 6
CONF_W = 31
CONF_PAD = 15

TM = 256
N_TILES = NT // TM
LAT_TILES = N_LAT // TM
SEQ_TILES = SEQ // TM
TQ = 128
HALO = 16
TME = 128
P_ROWS = NT + N_CLASSES * TME
P_TILES = P_ROWS // TME
SLAB = D // 128
VMEM_LIMIT = 56 * 1024 * 1024


def _cparams(n_axes=1, **kw):
    return pltpu.CompilerParams(dimension_semantics=("arbitrary",) * n_axes,
                                vmem_limit_bytes=VMEM_LIMIT, **kw)


def _split_bf16(v):
    hi = v.astype(BF16)
    lo = (v - hi.astype(F32)).astype(BF16)
    return hi, lo


def _mod_row(i):
    return jnp.minimum(i // SEQ_TILES, 2)


MOD_TN = 1536


MOD_STEPS = N_MOD * D // MOD_TN
W_ROWS = D // MOD_STEPS


def _mod_kernel(c_ref, w_ref, b_ref, win_ref, wout_ref, o_ref, wqkv_ref, wf_ref, wcv_ref, woutb_ref):
    w_in = win_ref[...]
    wqkv_ref[:, 0:768] = w_in[:, 0:768].astype(BF16)
    wqkv_ref[:, 768:896] = pltpu.roll(w_in[:, 512:640], HEAD_DIM, 1).astype(BF16)
    wqkv_ref[:, 896:1024] = pltpu.roll(w_in[:, 640:768], HEAD_DIM, 1).astype(BF16)
    wf_ref[...] = w_in[:, 768:768 + GW].astype(BF16)
    wcv_ref[...] = w_in[:, 768 + GW:].astype(BF16)
    woutb_ref[...] = wout_ref[...].astype(BF16)
    c = c_ref[...]
    s = c * jax.nn.sigmoid(c)
    s_hi, s_lo = _split_bf16(s)
    w_hi, w_lo = _split_bf16(w_ref[...])
    acc = jnp.dot(s_hi, w_hi, preferred_element_type=F32)
    acc = acc + jnp.dot(s_lo, w_hi, preferred_element_type=F32)
    acc = acc + jnp.dot(s_hi, w_lo, preferred_element_type=F32)
    o_ref[...] = acc + b_ref[...]


def _mod_call(c8, w_mod, b_mod, w_in, w_out):
    n = N_MOD * D
    rows = lambda cols: pl.BlockSpec((None, W_ROWS, cols), lambda l, j: (l, j, 0))
    return pl.pallas_call(
        _mod_kernel,
        out_shape=(jax.ShapeDtypeStruct((DEPTH, 8, n), F32),
                   jax.ShapeDtypeStruct((DEPTH, D, QKV_W), BF16), jax.ShapeDtypeStruct((DEPTH, D, GW), BF16),
                   jax.ShapeDtypeStruct((DEPTH, D, CV_IN), BF16), jax.ShapeDtypeStruct((DEPTH, D, D), BF16)),
        grid=(DEPTH, MOD_STEPS),
        in_specs=[
            pl.BlockSpec((8, D), lambda l, j: (0, 0)),
            pl.BlockSpec((None, D, MOD_TN), lambda l, j: (l, 0, j)),
            pl.BlockSpec((None, 1, MOD_TN), lambda l, j: (l, 0, j)),
            rows(w_in.shape[-1]), rows(D),
        ],
        out_specs=(pl.BlockSpec((None, 8, MOD_TN), lambda l, j: (l, 0, j)),
                   rows(QKV_W), rows(GW), rows(CV_IN), rows(D)),
        compiler_params=_cparams(2),
        name="adaln_mod",
    )(c8, w_mod, b_mod.reshape(DEPTH, 1, n), w_in, w_out)


QKV_W = 1024
CV_IN = 5 * GW
CV_OUT = 3 * GW


def _rmsnorm_mod(x, g, shift, scale):
    ms = jnp.mean(x * x, axis=-1, keepdims=True)
    y = x * lax.rsqrt(ms + EPS) * g
    return y * (1.0 + scale) + shift


def _stream_specs(l):
    if l == 0:
        return [pl.BlockSpec((TM, D), lambda i, *_: (jnp.minimum(i, LAT_TILES - 1), 0)),
                pl.BlockSpec((TM, D), lambda i, *_: (jnp.maximum(i - LAT_TILES, 0), 0))]
    return [pl.BlockSpec((TM, D), lambda i, *_: (i, 0)),
            pl.BlockSpec((TM * SLAB, 128), lambda i, *_: (i, 0)),
            pl.BlockSpec((None, None, 8, D), lambda i, *_: (l - 1, _mod_row(i), 0, 0))]


def _stream_tile(l, refs, stage):
    if l == 0:
        xl_ref, xc_ref = refs
        return jnp.where(pl.program_id(0) < LAT_TILES, xl_ref[...], xc_ref[...])
    x1_ref, yt_ref, mod_prev_ref = refs
    return x1_ref[...] + mod_prev_ref[5:6, :] * _slabs_to_rows(yt_ref, stage, TM)


def _n_stream(l):
    return 2 if l == 0 else 3


def _layer_tiles(l):
    return LAT_TILES if l == DEPTH - 1 else N_TILES


CAST_STEPS = 32


def _cast_spec(rows, cols, l):
    return pl.BlockSpec((rows, cols), lambda i, *_: (l * CAST_STEPS + jnp.minimum(i, CAST_STEPS - 1), 0))


def _cast_out_spec(rows, cols):
    return pl.BlockSpec((rows, cols), lambda i, *_: (jnp.minimum(i, CAST_STEPS - 1), 0))


def _in_kernel(*refs, l):
    stream, refs = refs[:_n_stream(l)], refs[_n_stream(l):]
    (mod_ref, g_ref, wqkv_ref, wf_ref, wcv_ref, cos_ref, sin_ref, w1_ref, w3_ref,
     qkv_ref, f_ref, cv_ref, w1b_ref, w3b_ref) = refs[:14]
    stage = refs[14] if l > 0 else None
    w1b_ref[...] = w1_ref[...].astype(BF16)
    w3b_ref[...] = w3_ref[...].astype(BF16)
    h = _rmsnorm_mod(_stream_tile(l, stream, stage), g_ref[...], mod_ref[0:1, :], mod_ref[1:2, :])
    hb = h.astype(BF16)
    qkv = jnp.dot(hb, wqkv_ref[...], preferred_element_type=F32)
    cos = cos_ref[...]
    sin = sin_ref[...]
    lane = lax.broadcasted_iota(I32, (TM, 128), 1)
    first_half = (lane % 32) < 16

    def rope(t):
        rot = jnp.where(first_half, pltpu.roll(t, 112, 1), pltpu.roll(t, 16, 1))
        return t * cos + rot * sin

    scale = HEAD_DIM ** -0.5
    for c in range(4):
        qkv_ref[:, c * 128:(c + 1) * 128] = (rope(qkv[:, c * 128:(c + 1) * 128]) * scale).astype(BF16)
    qkv_ref[:, 512:640] = rope(qkv[:, 512:640]).astype(BF16)
    qkv_ref[:, 640:768] = qkv[:, 640:768].astype(BF16)
    qkv_ref[:, 768:896] = rope(qkv[:, 768:896]).astype(BF16)
    qkv_ref[:, 896:1024] = qkv[:, 896:1024].astype(BF16)

    f_ref[...] = jnp.dot(hb, wf_ref[...], preferred_element_type=F32).astype(BF16)

    cv = jnp.dot(hb, wcv_ref[...], preferred_element_type=F32)
    cv_ref[:, 0:GW] = cv[:, 0:GW]
    cv_ref[:, GW:2 * GW] = cv[:, GW:2 * GW] * cv[:, 2 * GW:3 * GW]
    cv_ref[:, 2 * GW:3 * GW] = cv[:, 3 * GW:4 * GW] * jax.nn.sigmoid(cv[:, 4 * GW:5 * GW])


W13_ROWS = N_EXPERTS * D // CAST_STEPS
W2_ROWS = N_EXPERTS * D_EXPERT // CAST_STEPS


def _in_call(l, stream, mod, g_norm1, wqkv, wf, wcv, cos_t, sin_t, w_gate_e, w_up_e):
    const = dict(pipeline_mode=pl.Buffered(1))
    flat = lambda w: w.reshape(DEPTH * N_EXPERTS * D, D_EXPERT)
    return pl.pallas_call(
        functools.partial(_in_kernel, l=l),
        out_shape=(jax.ShapeDtypeStruct((NT, QKV_W), BF16),
                   jax.ShapeDtypeStruct((NT, GW), BF16),
                   jax.ShapeDtypeStruct((NT, CV_OUT), F32),
                   jax.ShapeDtypeStruct((N_EXPERTS * D, D_EXPERT), BF16),
                   jax.ShapeDtypeStruct((N_EXPERTS * D, D_EXPERT), BF16)),
        grid=(N_TILES,),
        in_specs=_stream_specs(l) + [
            pl.BlockSpec((None, None, 8, D), lambda i: (l, _mod_row(i), 0, 0)),
            pl.BlockSpec((None, 1, D), lambda i: (l, 0, 0)),
            pl.BlockSpec((None, D, QKV_W), lambda i: (l, 0, 0), **const),
            pl.BlockSpec((None, D, GW), lambda i: (l, 0, 0), **const),
            pl.BlockSpec((None, D, CV_IN), lambda i: (l, 0, 0), **const),
            pl.BlockSpec((TM, 128), lambda i: (jnp.where(i < LAT_TILES, i % SEQ_TILES, SEQ_TILES), 0)),
            pl.BlockSpec((TM, 128), lambda i: (jnp.where(i < LAT_TILES, i % SEQ_TILES, SEQ_TILES), 0)),
            _cast_spec(W13_ROWS, D_EXPERT, l), _cast_spec(W13_ROWS, D_EXPERT, l),
        ],
        out_specs=(pl.BlockSpec((TM, QKV_W), lambda i: (i, 0)),
                   pl.BlockSpec((TM, GW), lambda i: (i, 0)),
                   pl.BlockSpec((TM, CV_OUT), lambda i: (i, 0)),
                   _cast_out_spec(W13_ROWS, D_EXPERT), _cast_out_spec(W13_ROWS, D_EXPERT)),
        scratch_shapes=[pltpu.VMEM((TM * SLAB, 128), F32)] if l > 0 else [],
        compiler_params=_cparams(1),
        name="norm_in_proj",
    )(*stream, mod, g_norm1.reshape(DEPTH, 1, D), wqkv, wf, wcv, cos_t, sin_t, flat(w_gate_e), flat(w_up_e))


N_QBLK_LAT = N_LAT // TQ
N_QBLK = NT // TQ
QBLK_PER_SEQ = SEQ // TQ
KWIN = 3 * TQ


def _attn_heads(sink_ref, l, q_ref, keys, vals, bias, o_ref, s_scr, e_scr):
    lane = lax.broadcasted_iota(I32, (TQ, 128), 1)
    row2 = lax.broadcasted_iota(I32, (2 * TQ, 1), 0)
    zero = jnp.zeros((TQ, 128), BF16)
    if bias is not None:
        bias = jnp.concatenate([bias, bias], axis=0)
    nk = keys[0].shape[0]
    combos = [(hk, p) for hk in range(2) for p in range(2)]
    for idx, (hk, p) in enumerate(combos):
        kx = keys[0] if hk == p else keys[1]
        in_half = (lane // HEAD_DIM) == p
        qs = jnp.concatenate([jnp.where(in_half, q_ref[:, c * 128:(c + 1) * 128], zero)
                              for c in (2 * hk, 2 * hk + 1)], axis=0)
        s = lax.dot_general(qs, kx, (((1,), (1,)), ((), ())), preferred_element_type=F32)
        s_scr[idx, :, 0:nk] = s if bias is None else s + bias
    dens = []
    for idx, (hk, p) in enumerate(combos):
        s = s_scr[idx, :, 0:nk]
        sink = jnp.where(row2 < TQ, sink_ref[l, 4 * hk + p], sink_ref[l, 4 * hk + 2 + p])
        m = jnp.maximum(jnp.max(s, axis=-1, keepdims=True), sink)
        e = jnp.exp(s - m)
        dens.append(jnp.sum(e, axis=-1, keepdims=True) + jnp.exp(sink - m))
        e_scr[idx, :, 0:nk] = e.astype(BF16)
    for idx, (hk, p) in enumerate(combos):
        vx = vals[0] if hk == p else vals[1]
        o = (jnp.dot(e_scr[idx, :, 0:nk], vx, preferred_element_type=F32) / dens[idx]).astype(BF16)
        for k, c in enumerate((2 * hk, 2 * hk + 1)):
            lo = c * 128 + p * HEAD_DIM
            o_ref[:, lo:lo + HEAD_DIM] = o[k * TQ:(k + 1) * TQ, p * HEAD_DIM:(p + 1) * HEAD_DIM]


def _attn_kernel(sink_ref, q_ref, k_ref, v_ref, ks_ref, vs_ref, kc_ref, vc_ref, kcs_ref, vcs_ref, o_ref, s_scr, e_scr,
                 *, l):
    i = pl.program_id(0)

    @pl.when(i < N_QBLK_LAT)
    def _latent():
        n = i % QBLK_PER_SEQ
        ws = pl.multiple_of(jnp.clip((n - 1) * TQ, 0, SEQ - KWIN), TQ)
        keys = [jnp.concatenate([r[pl.ds(ws, KWIN), :], c[...]], axis=0)
                for r, c in ((k_ref, kc_ref), (ks_ref, kcs_ref))]
        vals = [jnp.concatenate([r[pl.ds(ws, KWIN), :], c[...]], axis=0)
                for r, c in ((v_ref, vc_ref), (vs_ref, vcs_ref))]
        nk = KWIN + CTX
        qpos = n * TQ + lax.broadcasted_iota(I32, (TQ, nk), 0)
        col = lax.broadcasted_iota(I32, (TQ, nk), 1)
        valid = (jnp.abs(ws + col - qpos) <= WINDOW) | (col >= KWIN)
        _attn_heads(sink_ref, l, q_ref, keys, vals, jnp.where(valid, 0.0, NEG_INF), o_ref, s_scr, e_scr)

    @pl.when(i >= N_QBLK_LAT)
    def _context():
        _attn_heads(sink_ref, l, q_ref, [kc_ref[...], kcs_ref[...]], [vc_ref[...], vcs_ref[...]], None, o_ref,
                    s_scr, e_scr)


def _attn_call(l, attn_sink, qkv):
    def seq_blk(i):
        return jnp.minimum(i // QBLK_PER_SEQ, 1)

    def ctx_blk(i):
        b = jnp.where(i < N_QBLK_LAT, i // QBLK_PER_SEQ, (i - N_QBLK_LAT) // (CTX // TQ))
        return N_LAT // CTX + b

    seq_specs = [pl.BlockSpec((SEQ, 128), functools.partial(lambda i, col: (seq_blk(i), col), col=col))
                 for col in (4, 5, 6, 7)]
    ctx_specs = [pl.BlockSpec((CTX, 128), functools.partial(lambda i, col: (ctx_blk(i), col), col=col))
                 for col in (4, 5, 6, 7)]
    n_blk = _layer_tiles(l) * TM // TQ
    return pl.pallas_call(
        functools.partial(_attn_kernel, l=l),
        out_shape=jax.ShapeDtypeStruct((n_blk * TQ, GW), BF16),
        grid=(n_blk,),
        in_specs=[pl.BlockSpec(memory_space=pltpu.SMEM),
                  pl.BlockSpec((TQ, GW), lambda i: (i, 0))] + seq_specs + ctx_specs,
        out_specs=pl.BlockSpec((TQ, GW), lambda i: (i, 0)),
        scratch_shapes=[pltpu.VMEM((4, 2 * TQ, KWIN + CTX), F32), pltpu.VMEM((4, 2 * TQ, KWIN + CTX), BF16)],
        compiler_params=_cparams(1),
        name="window_attention",
    )(attn_sink, qkv, *([qkv] * 8))


FC = 256
FFT_UNROLL = 8


def _dft_tables(n):
    t = n * n
    j = np.arange(n)
    ang = 2.0 * np.pi * np.outer(j, j) / n
    cn, sn = np.cos(ang), np.sin(ang)
    m1 = np.block([[cn, sn], [-sn, cn]])
    m2 = np.concatenate([cn, sn], axis=1)
    angc = 2.0 * np.pi * np.outer(np.arange(128), np.arange(128)) / 128.0
    wc = np.concatenate([np.cos(angc), -np.sin(angc)], axis=1)
    angt = 2.0 * np.pi * np.outer(j, j) / t
    tw_re = np.broadcast_to(np.cos(angt)[:, :, None], (n, n, 128))
    tw_im = np.broadcast_to(-np.sin(angt)[:, :, None], (n, n, 128))
    def hi_lo(m):
        return jnp.stack(_split_bf16(jnp.asarray(m, F32)))

    return (hi_lo(m1), hi_lo(m2), hi_lo(wc),
            jnp.asarray(np.ascontiguousarray(tw_re), F32), jnp.asarray(np.ascontiguousarray(tw_im), F32))


def _fft_kernel(f_ref, m1_ref, m2_ref, wc_ref, twr_ref, twi_ref, o_ref, are, aim, hre, him, *, n):
    t = n * n
    ng = FC // 128
    for g in range(ng):
        fg = f_ref[:, g * 128:(g + 1) * 128]
        a = (jnp.dot(fg, wc_ref[0], preferred_element_type=F32)
             + jnp.dot(fg, wc_ref[1], preferred_element_type=F32))
        are[g] = a[:, :128].reshape(n, n, 128)
        aim[g] = a[:, 128:].reshape(n, n, 128)

    def table_dot(m_ref, v):
        return (jnp.dot(m_ref[0], v, preferred_element_type=F32)
                + jnp.dot(m_ref[1], v, preferred_element_type=F32))

    def gather_rows(re, im, first):
        cols = pl.ds(pl.multiple_of(first, FFT_UNROLL), FFT_UNROLL)
        parts = []
        for src in (re, im):
            x = [pltpu.einshape("abc->bac", src[g, :, cols, :]) for g in range(ng)]
            parts.append(jnp.concatenate([x[g][u] for u in range(FFT_UNROLL) for g in range(ng)], axis=1))
        return jnp.concatenate(parts, axis=0).astype(BF16)

    def stage1(it, carry):
        t1 = it * FFT_UNROLL
        h = table_dot(m1_ref, gather_rows(are, aim, t1))
        for u in range(FFT_UNROLL):
            twr = twr_ref[t1 + u]
            twi = twi_ref[t1 + u]
            for g in range(ng):
                lanes = slice((u * ng + g) * 128, (u * ng + g + 1) * 128)
                h_re, h_im = h[:n, lanes], h[n:, lanes]
                hre[g, t1 + u] = h_re * twr - h_im * twi
                him[g, t1 + u] = h_re * twi + h_im * twr
        return carry

    lax.fori_loop(0, n // FFT_UNROLL, stage1, 0)

    def stage2(it, carry):
        k2 = it * FFT_UNROLL
        z = table_dot(m2_ref, gather_rows(hre, him, k2))
        z = z * ((t * 128.0) ** -0.5)
        cols = pl.ds(pl.multiple_of(k2, FFT_UNROLL), FFT_UNROLL)
        for g in range(ng):
            zg = jnp.stack([z[:, (u * ng + g) * 128:(u * ng + g + 1) * 128] for u in range(FFT_UNROLL)], axis=0)
            are[g, :, cols, :] = pltpu.einshape("abc->bac", zg)
        return carry

    lax.fori_loop(0, n // FFT_UNROLL, stage2, 0)
    for g in range(ng):
        o_ref[:, g * 128:(g + 1) * 128] = are[g].reshape(t, 128).astype(BF16)


def _fft_call(f, n, first_blk, n_seq):
    t = n * n
    m1, m2, wc, twr, twi = _dft_tables(n)
    full = lambda shape: pl.BlockSpec(shape, lambda b, h: (0,) * len(shape))
    return pl.pallas_call(
        functools.partial(_fft_kernel, n=n),
        out_shape=jax.ShapeDtypeStruct((n_seq * t, GW), BF16),
        grid=(n_seq, GW // FC),
        in_specs=[pl.BlockSpec((t, FC), lambda b, h: (first_blk + b, h)),
                  full((2, 2 * n, 2 * n)), full((2, n, 2 * n)), full((2, 128, 256)),
                  full((n, n, 128)), full((n, n, 128))],
        out_specs=pl.BlockSpec((t, FC), lambda b, h: (b, h)),
        scratch_shapes=[pltpu.VMEM((FC // 128, n, n, 128), F32)] * 4,
        compiler_params=_cparams(2),
        name="fourier_mix_%d" % t,
    )(f, m1, m2, wc, twr, twi)


CONV_ROWS = 32
GSH_ROWS = TM + 2 * HALO - 8


def _conv_kernel(sb_ref, p_ref, pp_ref, pn_ref, g_ref, gp_ref, gn_ref, sw_ref, cw_ref, cb_ref, ng_ref, nb_ref,
                 oc_ref, od_ref, pbuf, gbuf, gsh):
    i = pl.program_id(0)
    seq_start = jnp.logical_or(i % SEQ_TILES == 0, i >= LAT_TILES)
    seq_end = jnp.logical_or(i % SEQ_TILES == SEQ_TILES - 1, i >= LAT_TILES)
    for buf, main, prev, nxt in ((pbuf, p_ref, pp_ref, pn_ref), (gbuf, g_ref, gp_ref, gn_ref)):
        buf[0:HALO, :] = jnp.where(seq_start, 0.0, prev[...])
        buf[HALO:HALO + TM, :] = main[...]
        buf[HALO + TM:HALO + TM + HALO, :] = jnp.where(seq_end, 0.0, nxt[...])
    for s in range(8):
        gsh[s] = gbuf[s:s + GSH_ROWS, :]
    sw = sw_ref[...]
    cw = cw_ref[...]
    for r0 in range(0, TM, CONV_ROWS):
        base = HALO + r0
        acc = sw[0:1, :] * pbuf[base - 1:base - 1 + CONV_ROWS, :]
        acc = acc + sw[1:2, :] * pbuf[base:base + CONV_ROWS, :]
        acc = acc + sw[2:3, :] * pbuf[base + 1:base + 1 + CONV_ROWS, :]
        oc_ref[r0:r0 + CONV_ROWS, :] = (sb_ref[r0:r0 + CONV_ROWS, :] * acc).astype(BF16)

        z = None
        for j in range(CONF_W):
            o = base - CONF_PAD + j
            q = o - o % 8
            term = cw[j:j + 1, :] * gsh[o % 8, q:q + CONV_ROWS, :]
            z = term if z is None else z + term
        z = z + cb_ref[...]
        mu = jnp.mean(z, axis=-1, keepdims=True)
        zc = z - mu
        var = jnp.mean(zc * zc, axis=-1, keepdims=True)
        y = zc * lax.rsqrt(var + EPS) * ng_ref[...] + nb_ref[...]
        od_ref[r0:r0 + CONV_ROWS, :] = (y * jax.nn.sigmoid(y)).astype(BF16)


def _conv_call(l, cv, sconv_w, cconv_w, cconv_b, cnorm_g, cnorm_b):
    hb = TM // HALO
    last = NT // HALO - 1

    def halo_specs(col):
        return [pl.BlockSpec((TM, GW), lambda i: (i, col)),
                pl.BlockSpec((HALO, GW), lambda i: (jnp.maximum(i * hb - 1, 0), col)),
                pl.BlockSpec((HALO, GW), lambda i: (jnp.minimum((i + 1) * hb, last), col))]

    vec = lambda: pl.BlockSpec((None, 1, GW), lambda i: (l, 0, 0))
    cw = jnp.pad(cconv_w, ((0, 0), (0, 32 - CONF_W), (0, 0)))
    sw = jnp.pad(sconv_w, ((0, 0), (0, 8 - 3), (0, 0)))
    return pl.pallas_call(
        _conv_kernel,
        out_shape=(jax.ShapeDtypeStruct((_layer_tiles(l) * TM, GW), BF16),
                   jax.ShapeDtypeStruct((_layer_tiles(l) * TM, GW), BF16)),
        grid=(_layer_tiles(l),),
        in_specs=[pl.BlockSpec((TM, GW), lambda i: (i, 0))] + halo_specs(1) + halo_specs(2) + [
            pl.BlockSpec((None, 8, GW), lambda i: (l, 0, 0)),
            pl.BlockSpec((None, 32, GW), lambda i: (l, 0, 0)),
            vec(), vec(), vec()],
        out_specs=(pl.BlockSpec((TM, GW), lambda i: (i, 0)), pl.BlockSpec((TM, GW), lambda i: (i, 0))),
        scratch_shapes=[pltpu.VMEM((TM + 2 * HALO, GW), F32)] * 2 + [pltpu.VMEM((8, GSH_ROWS, GW), F32)],
        compiler_params=_cparams(1),
        name="conv_mixers",
    )(cv, cv, cv, cv, cv, cv, cv, sw, cw, cconv_b.reshape(DEPTH, 1, GW), cnorm_g.reshape(DEPTH, 1, GW),
      cnorm_b.reshape(DEPTH, 1, GW))


R_ROWS = 32
_SLOT_A = np.array([0, 2, 2, 0, 0, 1], np.int32)
_SLOT_B = np.array([1, 1, 3, 3, 2, 3], np.int32)


def _route(lg):
    gl = [lg[N_EXPERTS + g:N_EXPERTS + g + 1, :] for g in range(N_GROUPS)]
    gmax = functools.reduce(jnp.maximum, gl)
    pg = 1.0 / functools.reduce(jnp.add, [jnp.exp(v - gmax) for v in gl])
    gsel = jnp.full_like(gmax, N_GROUPS - 1).astype(I32)
    for g in range(N_GROUPS - 2, -1, -1):
        gsel = jnp.where(gl[g] == gmax, g, gsel)
    el = []
    for j in range(PER_GROUP):
        v = lg[j:j + 1, :]
        for g in range(1, N_GROUPS):
            v = jnp.where(gsel == g, lg[g * PER_GROUP + j:g * PER_GROUP + j + 1, :], v)
        el.append(v)
    v1 = functools.reduce(jnp.maximum, el)
    i1 = jnp.full_like(gsel, PER_GROUP - 1)
    for j in range(PER_GROUP - 2, -1, -1):
        i1 = jnp.where(el[j] == v1, j, i1)
    rest = [jnp.where(i1 == j, -jnp.inf, el[j]) for j in range(PER_GROUP)]
    v2 = functools.reduce(jnp.maximum, rest)
    i2 = jnp.full_like(gsel, PER_GROUP - 1)
    for j in range(PER_GROUP - 2, -1, -1):
        i2 = jnp.where(jnp.logical_and(rest[j] == v2, i1 != j), j, i2)
    e2 = jnp.exp(v2 - v1)
    w1 = pg / (1.0 + e2)
    w2 = pg * e2 / (1.0 + e2)
    lo = jnp.minimum(i1, i2)
    hi = jnp.maximum(i1, i2)
    w_lo = jnp.where(i1 < i2, w1, w2)
    w_hi = jnp.where(i1 < i2, w2, w1)
    pair = jnp.where(lo == 0, jnp.where(hi == 1, 0, jnp.where(hi == 2, 4, 3)),
                     jnp.where(lo == 1, jnp.where(hi == 2, 1, 5), 2))
    cls = gsel * 6 + pair
    a_is_hi = pair == 1
    return cls, jnp.where(a_is_hi, w_hi, w_lo), jnp.where(a_is_hi, w_lo, w_hi)


def _rows_to_slabs(v, stage_ref, out_ref):
    n = v.shape[0]
    for c in range(SLAB):
        stage_ref[pl.ds(c, n, stride=SLAB), :] = v[:, c * 128:(c + 1) * 128]
    out_ref[...] = stage_ref[...].astype(out_ref.dtype)


def _slabs_to_rows(src_ref, stage_ref, n):
    stage_ref[...] = src_ref[...].astype(F32)
    return jnp.concatenate([stage_ref[pl.ds(c, n, stride=SLAB), :] for c in range(SLAB)], axis=1)


def _out_kernel(*refs, l):
    oa_ref, obl_ref, obc_ref, oc_ref, od_ref, w_ref = refs[:6]
    stream, refs = refs[6:6 + _n_stream(l)], refs[6 + _n_stream(l):]
    (mod_ref, g_ref, wrh_ref, wrl_ref, br_ref, tri_ref, w2_ref,
     x1_ref, hp_ref, cls_ref, gw_ref, cnt_ref, w2b_ref, carry, stage) = refs
    i = pl.program_id(0)
    w2b_ref[...] = w2_ref[...].astype(BF16)

    @pl.when(i == 0)
    def _():
        carry[...] = jnp.zeros_like(carry)

    ob = jnp.where(i < LAT_TILES, obl_ref[...], obc_ref[...])
    y = jnp.dot(oa_ref[...], w_ref[0:GW, :], preferred_element_type=F32)
    y = y + jnp.dot(ob, w_ref[GW:2 * GW, :], preferred_element_type=F32)
    y = y + jnp.dot(oc_ref[...], w_ref[2 * GW:3 * GW, :], preferred_element_type=F32)
    y = y + jnp.dot(od_ref[...], w_ref[3 * GW:4 * GW, :], preferred_element_type=F32)
    x1 = _stream_tile(l, stream, stage) + mod_ref[2:3, :] * y
    x1_ref[...] = x1
    h2 = _rmsnorm_mod(x1, g_ref[...], mod_ref[3:4, :], mod_ref[4:5, :])
    _rows_to_slabs(h2, stage, hp_ref)

    h_hi, h_lo = _split_bf16(h2)
    lg = jnp.dot(h_hi, wrh_ref[...], preferred_element_type=F32)
    lg = lg + jnp.dot(h_lo, wrh_ref[...], preferred_element_type=F32)
    lg = lg + jnp.dot(h_hi, wrl_ref[...], preferred_element_type=F32)
    lg = jnp.transpose(lg + br_ref[...])[0:R_ROWS, :]
    cls, w_a, w_b = _route(lg)

    onehot = jnp.where(lax.broadcasted_iota(I32, (R_ROWS, TM), 0) == cls, 1.0, 0.0)
    prefix = jnp.dot(onehot.astype(BF16), tri_ref[...], preferred_element_type=F32)
    rank = jnp.sum(onehot * (prefix - 1.0 + carry[:, 0:1]), axis=0, keepdims=True)
    total = carry[...] + jnp.sum(onehot, axis=1, keepdims=True)
    carry[...] = total
    cnt_ref[...] = total.astype(I32)
    cls_ref[...] = jnp.concatenate([cls, rank.astype(I32), jnp.zeros((6, TM), I32)], axis=0)
    gw_ref[...] = jnp.concatenate([w_a, w_b, jnp.zeros((6, TM), F32)], axis=0)


def _out_call(l, o_a, o_b_lat, o_b_ctx, o_c, o_d, w_out, stream, mod, g_norm2, wr_hi, wr_lo, b_route, w_down_e):
    row = lambda: pl.BlockSpec((TM, GW), lambda i: (i, 0))
    tri = jnp.asarray(np.triu(np.ones((TM, TM), np.float32)), BF16)
    n_rows = _layer_tiles(l) * TM
    return pl.pallas_call(
        functools.partial(_out_kernel, l=l),
        out_shape=(jax.ShapeDtypeStruct((n_rows, D), F32), jax.ShapeDtypeStruct((n_rows * SLAB, 128), BF16),
                   jax.ShapeDtypeStruct((8, n_rows), I32), jax.ShapeDtypeStruct((8, n_rows), F32),
                   jax.ShapeDtypeStruct((R_ROWS, 128), I32),
                   jax.ShapeDtypeStruct((N_EXPERTS * D_EXPERT, D), BF16)),
        grid=(_layer_tiles(l),),
        in_specs=[row(),
                  pl.BlockSpec((TM, GW), lambda i: (jnp.minimum(i, LAT_TILES - 1), 0)),
                  pl.BlockSpec((TM, GW), lambda i: (jnp.maximum(i - LAT_TILES, 0), 0)),
                  row(), row(),
                  pl.BlockSpec((None, D, D), lambda i: (l, 0, 0), pipeline_mode=pl.Buffered(1))]
        + _stream_specs(l) + [
                  pl.BlockSpec((None, None, 8, D), lambda i: (l, _mod_row(i), 0, 0)),
                  pl.BlockSpec((None, 1, D), lambda i: (l, 0, 0)),
                  pl.BlockSpec((None, D, 128), lambda i: (l, 0, 0)),
                  pl.BlockSpec((None, D, 128), lambda i: (l, 0, 0)),
                  pl.BlockSpec((None, 1, 128), lambda i: (l, 0, 0)),
                  pl.BlockSpec((TM, TM), lambda i: (0, 0)),
                  _cast_spec(W2_ROWS, D, l)],
        out_specs=(pl.BlockSpec((TM, D), lambda i: (i, 0)),
                   pl.BlockSpec((TM * SLAB, 128), lambda i: (i, 0)),
                   pl.BlockSpec((8, TM), lambda i: (0, i)), pl.BlockSpec((8, TM), lambda i: (0, i)),
                   pl.BlockSpec((R_ROWS, 128), lambda i: (0, 0)),
                   _cast_out_spec(W2_ROWS, D)),
        scratch_shapes=[pltpu.VMEM((R_ROWS, 128), F32), pltpu.VMEM((TM * SLAB, 128), F32)],
        compiler_params=_cparams(1),
        name="out_proj_route",
    )(o_a, o_b_lat, o_b_ctx, o_c, o_d, w_out, *stream, mod, g_norm2.reshape(DEPTH, 1, D), wr_hi, wr_lo,
      b_route, tri, w_down_e.reshape(DEPTH * N_EXPERTS * D_EXPERT, D))


ROW_UNROLL = 8


def _gate_column(g_ref, src_ref, base):
    sub = lax.broadcasted_iota(I32, (8, 128), 0)
    pieces = []
    for g in range(TME // 8):
        v = jnp.zeros((8, 128), F32)
        for k in range(8):
            v = jnp.where(sub == k, g_ref[src_ref[base + 8 * g + k]], v)
        pieces.append(v)
    return jnp.concatenate(pieces, axis=0)


DMA_PIECES = 8


def _expert_kernel(ea_ref, eb_ref, nvalid_ref, zero_ref, starts_ref, counts_ref, cls_ref, rank_ref, gwa_ref, gwb_ref, hp_hbm, w1a_ref, w3a_ref, w2a_ref, w1b_ref,
                   w3b_ref, w2b_ref, yt_hbm, src_ref, dst_ref, hbuf0, hbuf1, ybuf0, ybuf1, stage, gsem, ssem, zsem,
                   *, n_tok):
    i = pl.program_id(0)
    nvalid = nvalid_ref[0]
    hbufs, ybufs = (hbuf0, hbuf1), (ybuf0, ybuf1)

    def slab(ref, row):
        return ref.at[pl.ds(pl.multiple_of(row * SLAB, SLAB), SLAB), :]

    def gather_copy(tile, sl, j, z=0):
        return pltpu.make_async_copy(slab(hp_hbm, src_ref[tile * TME + j] + z), slab(hbufs[sl], j), gsem.at[sl])

    def scatter_copy(tile, sl, j, z=0):
        return pltpu.make_async_copy(slab(ybufs[sl], j), slab(yt_hbm, dst_ref[(tile + 1) * TME + j] + z),
                                     ssem.at[sl])

    def for_rows(fn):
        def body(j, carry):
            fn(j)
            return carry

        lax.fori_loop(0, TME, body, 0, unroll=ROW_UNROLL)

    @pl.when(i == 0)
    def _():
        def unused(p, carry):
            src_ref[p] = n_tok - 1
            dst_ref[p + TME] = n_tok + ((p + TME) & (2 * TME - 1))
            return carry

        def dummy_tile(j, carry):
            dst_ref[j] = n_tok + j
            src_ref[nvalid * TME + j] = n_tok - 1
            return carry

        lax.fori_loop(0, TME, dummy_tile, 0, unroll=ROW_UNROLL)

        def class_tail(c, carry):
            lo = starts_ref[c] + counts_ref[c]
            hi = starts_ref[c] + ((counts_ref[c] + TME - 1) // TME) * TME
            lax.fori_loop(lo, hi, unused, 0)
            return carry

        lax.fori_loop(0, N_CLASSES, class_tail, 0)

        def place(t, carry):
            p = starts_ref[cls_ref[t]] + rank_ref[t]
            src_ref[p] = t
            dst_ref[p + TME] = t
            return carry

        lax.fori_loop(0, n_tok, place, 0, unroll=ROW_UNROLL)
        for sl in range(2):
            ybufs[sl][...] = jnp.zeros_like(ybufs[sl])
            spare = pltpu.make_async_copy(ybufs[sl], yt_hbm.at[pl.ds((n_tok + sl * TME) * SLAB, TME * SLAB), :],
                                          zsem)
            spare.start()
            spare.wait()
        for_rows(lambda j: gather_copy(0, 0, j).start())

    def tile_body(sl):
        for_rows(lambda j: gather_copy(0, sl, 0).wait())
        h = _slabs_to_rows(hbufs[sl], stage, TME).astype(BF16)
        pieces = iter(range(DMA_PIECES))

        def paced_dot(a, b):
            r = jnp.dot(a, b, preferred_element_type=F32)
            piece = next(pieces, None)
            if piece is None:
                return r
            z = pltpu.bitcast(r[0:8, 0:128], I32)[0, 0] * zero_ref[0]
            first = piece * (TME // DMA_PIECES)
            for j in range(first, first + TME // DMA_PIECES):
                gather_copy(i + 1, 1 - sl, j, z).start()
                scatter_copy(i - 1, 1 - sl, j, z).start(priority=1)
            return r

        y = [None] * (D // D_EXPERT)
        for g_ref, w1_ref, w3_ref, w2_ref in ((gwa_ref, w1a_ref, w3a_ref, w2a_ref),
                                              (gwb_ref, w1b_ref, w3b_ref, w2b_ref)):
            half = D_EXPERT // 2
            gate = jnp.concatenate([paced_dot(h, w1_ref[:, c * half:(c + 1) * half]) for c in range(2)], axis=1)
            up = jnp.concatenate([paced_dot(h, w3_ref[:, c * half:(c + 1) * half]) for c in range(2)], axis=1)
            gcol = _gate_column(g_ref, src_ref, i * TME)
            hid = (gate * jax.nn.sigmoid(gate) * up * jnp.concatenate([gcol] * (D_EXPERT // 128), axis=1)).astype(BF16)
            for c in range(len(y)):
                yc = paced_dot(hid, w2_ref[:, c * D_EXPERT:(c + 1) * D_EXPERT])
                y[c] = yc if y[c] is None else y[c] + yc
        y = jnp.concatenate(y, axis=1)

        @pl.when(i >= 1)
        def _():
            for_rows(lambda j: scatter_copy(0, sl, 0).wait())

        _rows_to_slabs(y, stage, ybufs[sl])

        @pl.when(i == nvalid - 1)
        def _():
            for_rows(lambda j: scatter_copy(i, sl, j).start())
            for_rows(lambda j: scatter_copy(0, 1 - sl, 0).wait())
            for_rows(lambda j: scatter_copy(0, sl, 0).wait())
            for_rows(lambda j: gather_copy(0, 1 - sl, 0).wait())

    for sl in range(2):
        @pl.when(jnp.logical_and(i < nvalid, i % 2 == sl))
        def _(sl=sl):
            tile_body(sl)


def _expert_call(tile_ea, tile_eb, nvalid, starts, counts, cls, rank, gwa, gwb, hp, w1b, w3b, w2b):
    n_tok = cls.shape[0]
    up_a = pl.BlockSpec((None, D, D_EXPERT), lambda i, ea, eb, *_: (ea[i], 0, 0))
    up_b = pl.BlockSpec((None, D, D_EXPERT), lambda i, ea, eb, *_: (eb[i], 0, 0))
    down_a = pl.BlockSpec((None, D_EXPERT, D), lambda i, ea, eb, *_: (ea[i], 0, 0))
    down_b = pl.BlockSpec((None, D_EXPERT, D), lambda i, ea, eb, *_: (eb[i], 0, 0))
    return pl.pallas_call(
        functools.partial(_expert_kernel, n_tok=n_tok),
        out_shape=jax.ShapeDtypeStruct(((n_tok + 2 * TME) * SLAB, 128), BF16),
        grid_spec=pltpu.PrefetchScalarGridSpec(
            num_scalar_prefetch=10,
            grid=(P_TILES,),
            in_specs=[pl.BlockSpec(memory_space=pl.ANY),
                      up_a, up_a, down_a, up_b, up_b, down_b],
            out_specs=pl.BlockSpec(memory_space=pl.ANY),
            scratch_shapes=[pltpu.SMEM((P_ROWS + TME,), I32), pltpu.SMEM((P_ROWS + TME,), I32)]
            + [pltpu.VMEM((TME * SLAB, 128), BF16)] * 4
            + [pltpu.VMEM((TME * SLAB, 128), F32),
                            pltpu.SemaphoreType.DMA((2,)), pltpu.SemaphoreType.DMA((2,)),
                            pltpu.SemaphoreType.DMA]),
        compiler_params=_cparams(1),
        name="moe_experts",
    )(tile_ea, tile_eb, nvalid, jnp.zeros((1,), I32), starts, counts, cls, rank, gwa, gwb, hp, w1b, w3b, w2b,
      w1b, w3b, w2b)


def _final_kernel(x1_ref, yt_ref, mod_ref, gf_ref, o_ref, stage):
    x2 = _stream_tile(DEPTH, (x1_ref, yt_ref, mod_ref), stage)
    ms = jnp.mean(x2 * x2, axis=-1, keepdims=True)
    o_ref[...] = x2 * lax.rsqrt(ms + EPS) * gf_ref[...]


def _final_call(x1, mod, g_final, yt):
    return pl.pallas_call(
        _final_kernel,
        out_shape=jax.ShapeDtypeStruct((N_LAT, D), F32),
        grid=(LAT_TILES,),
        in_specs=_stream_specs(DEPTH) + [pl.BlockSpec((1, D), lambda i: (0, 0))],
        out_specs=pl.BlockSpec((TM, D), lambda i: (i, 0)),
        scratch_shapes=[pltpu.VMEM((TM * SLAB, 128), F32)],
        compiler_params=_cparams(1),
        name="final_norm",
    )(x1, yt, mod, g_final.reshape(1, D))


def _slot_plan(counts):
    padded = ((counts + TME - 1) // TME) * TME
    ends = jnp.cumsum(padded)
    starts = ends - padded
    tile_start = jnp.arange(P_TILES, dtype=I32) * TME
    tile_cls = jnp.minimum(jnp.sum((tile_start[:, None] >= ends[None, :]).astype(I32), axis=1), N_CLASSES - 1)
    nvalid = (ends[-1] // TME).reshape(1)
    grp = tile_cls // 6
    tile_ea = grp * PER_GROUP + jnp.asarray(_SLOT_A)[tile_cls % 6]
    tile_eb = grp * PER_GROUP + jnp.asarray(_SLOT_B)[tile_cls % 6]
    return starts, tile_ea, tile_eb, nvalid


def _rope_tables():
    t = jnp.arange(SEQ, dtype=I32)
    row = (t // GRID_W).astype(F32)
    col = (t % GRID_W).astype(F32)
    half = HEAD_DIM // 4
    freqs = ROPE_BASE ** (-jnp.arange(0, 2 * half, 2, dtype=F32) / (2 * half))
    ang_r = row[:, None] * freqs
    ang_c = col[:, None] * freqs
    cos_h = jnp.concatenate([jnp.cos(ang_r), jnp.cos(ang_r), jnp.cos(ang_c), jnp.cos(ang_c)], axis=1)
    sin_h = jnp.concatenate([-jnp.sin(ang_r), jnp.sin(ang_r), -jnp.sin(ang_c), jnp.sin(ang_c)], axis=1)
    cos_t = jnp.concatenate([jnp.tile(cos_h, (1, 2)), jnp.ones((TM, 128), F32)], axis=0)
    sin_t = jnp.concatenate([jnp.tile(sin_h, (1, 2)), jnp.zeros((TM, 128), F32)], axis=0)
    return cos_t, sin_t


def kernel(x, c, ctx, c_ctx, w_mod, b_mod, g_norm1, g_norm2, w_in, w_out, attn_sink, sconv_w, cconv_w, cconv_b,
           cnorm_g, cnorm_b, w_route_group, b_route_group, w_route_expert, b_route_expert, w_gate_e, w_up_e,
           w_down_e, g_final):
    c8 = jnp.zeros((8, D), F32).at[0:BATCH].set(c).at[BATCH].set(c_ctx)
    w_route = jnp.concatenate([w_route_expert, w_route_group,
                               jnp.zeros((DEPTH, D, 128 - N_EXPERTS - N_GROUPS), F32)], axis=2)
    wr_hi = w_route.astype(BF16)
    wr_lo = (w_route - wr_hi.astype(F32)).astype(BF16)
    b_route = jnp.concatenate([b_route_expert, b_route_group,
                               jnp.zeros((DEPTH, 128 - N_EXPERTS - N_GROUPS), F32)], axis=1).reshape(DEPTH, 1, 128)
    cos_t, sin_t = _rope_tables()

    mod, wqkv, wf, wcv, w_out_b = _mod_call(c8, w_mod, b_mod, w_in, w_out)
    mod = mod.reshape(DEPTH, 8, N_MOD, D)[:, :3]
    mod = jnp.pad(mod, ((0, 0), (0, 0), (0, 8 - N_MOD), (0, 0)))

    stream = (x.reshape(N_LAT, D), ctx.reshape(N_CTX, D))
    for l in range(DEPTH):
        last = l == DEPTH - 1
        qkv, f, cv, w1b, w3b = _in_call(l, stream, mod, g_norm1, wqkv, wf, wcv, cos_t, sin_t, w_gate_e, w_up_e)
        o_a = _attn_call(l, attn_sink, qkv)
        o_b_lat = _fft_call(f, 64, 0, BATCH)
        o_b_ctx = o_b_lat if last else _fft_call(f, 16, N_LAT // CTX, BATCH)
        o_c, o_d = _conv_call(l, cv, sconv_w, cconv_w, cconv_b, cnorm_g, cnorm_b)
        x1, hp, cls8, gw8, cnt, w2b = _out_call(l, o_a, o_b_lat, o_b_ctx, o_c, o_d, w_out_b, stream, mod, g_norm2,
                                                wr_hi, wr_lo, b_route, w_down_e)
        counts = cnt[:N_CLASSES, 0]
        starts, tile_ea, tile_eb, nvalid = _slot_plan(counts)
        yt = _expert_call(tile_ea, tile_eb, nvalid, starts, counts, cls8[0], cls8[1], gw8[0], gw8[1], hp,
                          w1b.reshape(N_EXPERTS, D, D_EXPERT), w3b.reshape(N_EXPERTS, D, D_EXPERT),
                          w2b.reshape(N_EXPERTS, D_EXPERT, D))
        stream = (x1, yt, mod)
    return _final_call(x1, mod, g_final, yt).reshape(BATCH, SEQ, D)
```

```python
import functools

import numpy as np
import jax
import jax.numpy as jnp
from jax import lax
from jax.experimental import pallas as pl
from jax.experimental.pallas import tpu as pltpu

F32 = jnp.float32
BF16 = jnp.bfloat16
I32 = jnp.int32

D = 2048
BATCH = 2
SEQ = 4096
CTX = 256
DEPTH = 2
GRID_W = 64
N_LAT = BATCH * SEQ
N_CTX = BATCH * CTX
NT = N_LAT + N_CTX
GW = 512
HEAD_DIM = 64
N_HEADS = 8
WINDOW = 128
ROPE_BASE = 10000.0
N_MOD = 6
EPS = 1e-6
NEG_INF = -1e30
N_GROUPS = 4
PER_GROUP = 4
N_EXPERTS = 16
D_EXPERT = 512
N_CLASSES = N_GROUPS * 6
CONF_W = 31
CONF_PAD = 15

TM = 256
N_TILES = NT // TM
LAT_TILES = N_LAT // TM
SEQ_TILES = SEQ // TM
TQ = 128
HALO = 16
TME = 128
P_ROWS = NT + N_CLASSES * TME
P_TILES = P_ROWS // TME
SLAB = D // 128
VMEM_LIMIT = 56 * 1024 * 1024


def _cparams(n_axes=1, **kw):
    return pltpu.CompilerParams(dimension_semantics=("arbitrary",) * n_axes,
                                vmem_limit_bytes=VMEM_LIMIT, **kw)


def _split_bf16(v):
    hi = v.astype(BF16)
    lo = (v - hi.astype(F32)).astype(BF16)
    return hi, lo


def _mod_row(i):
    return jnp.minimum(i // SEQ_TILES, 2)


MOD_TN = 1536


MOD_STEPS = N_MOD * D // MOD_TN
W_ROWS = D // MOD_STEPS


def _mod_kernel(c_ref, w_ref, b_ref, win_ref, wout_ref, o_ref, wqkv_ref, wf_ref, wcv_ref, woutb_ref):
    w_in = win_ref[...]
    wqkv_ref[...] = w_in[:, 0:768].astype(BF16)
    wf_ref[...] = w_in[:, 768:768 + GW].astype(BF16)
    wcv_ref[...] = w_in[:, 768 + GW:].astype(BF16)
    woutb_ref[...] = wout_ref[...].astype(BF16)
    c = c_ref[...]
    s = c * jax.nn.sigmoid(c)
    s_hi, s_lo = _split_bf16(s)
    w_hi, w_lo = _split_bf16(w_ref[...])
    acc = jnp.dot(s_hi, w_hi, preferred_element_type=F32)
    acc = acc + jnp.dot(s_lo, w_hi, preferred_element_type=F32)
    acc = acc + jnp.dot(s_hi, w_lo, preferred_element_type=F32)
    o_ref[...] = acc + b_ref[...]


def _mod_call(c8, w_mod, b_mod, w_in, w_out):
    n = N_MOD * D
    rows = lambda cols: pl.BlockSpec((None, W_ROWS, cols), lambda l, j: (l, j, 0))
    return pl.pallas_call(
        _mod_kernel,
        out_shape=(jax.ShapeDtypeStruct((DEPTH, 8, n), F32),
                   jax.ShapeDtypeStruct((DEPTH, D, QKV_IN), BF16), jax.ShapeDtypeStruct((DEPTH, D, GW), BF16),
                   jax.ShapeDtypeStruct((DEPTH, D, CV_IN), BF16), jax.ShapeDtypeStruct((DEPTH, D, D), BF16)),
        grid=(DEPTH, MOD_STEPS),
        in_specs=[
            pl.BlockSpec((8, D), lambda l, j: (0, 0)),
            pl.BlockSpec((None, D, MOD_TN), lambda l, j: (l, 0, j)),
            pl.BlockSpec((None, 1, MOD_TN), lambda l, j: (l, 0, j)),
            rows(w_in.shape[-1]), rows(D),
        ],
        out_specs=(pl.BlockSpec((None, 8, MOD_TN), lambda l, j: (l, 0, j)),
                   rows(QKV_IN), rows(GW), rows(CV_IN), rows(D)),
        compiler_params=_cparams(2),
        name="adaln_mod",
    )(c8, w_mod, b_mod.reshape(DEPTH, 1, n), w_in, w_out)


QKV_W = 1024
QKV_IN = 768
CV_IN = 5 * GW
CV_OUT = 3 * GW


def _rmsnorm_mod(x, g, shift, scale):
    ms = jnp.mean(x * x, axis=-1, keepdims=True)
    y = x * lax.rsqrt(ms + EPS) * g
    return y * (1.0 + scale) + shift


def _stream_specs(l):
    if l == 0:
        return [pl.BlockSpec((TM, D), lambda i, *_: (jnp.minimum(i, LAT_TILES - 1), 0)),
                pl.BlockSpec((TM, D), lambda i, *_: (jnp.maximum(i - LAT_TILES, 0), 0))]
    return [pl.BlockSpec((TM, D), lambda i, *_: (i, 0)),
            pl.BlockSpec((TM * SLAB, 128), lambda i, *_: (i, 0)),
            pl.BlockSpec((None, None, 8, D), lambda i, *_: (l - 1, _mod_row(i), 0, 0))]


def _stream_tile(l, refs, stage):
    if l == 0:
        xl_ref, xc_ref = refs
        return jnp.where(pl.program_id(0) < LAT_TILES, xl_ref[...], xc_ref[...])
    x1_ref, yt_ref, mod_prev_ref = refs
    return x1_ref[...] + mod_prev_ref[5:6, :] * _slabs_to_rows(yt_ref, stage, TM)


def _n_stream(l):
    return 2 if l == 0 else 3


def _layer_tiles(l):
    return LAT_TILES if l == DEPTH - 1 else N_TILES


CAST_STEPS = 32


def _cast_spec(rows, cols, l):
    return pl.BlockSpec((rows, cols), lambda i, *_: (l * CAST_STEPS + jnp.minimum(i, CAST_STEPS - 1), 0))


def _cast_out_spec(rows, cols):
    return pl.BlockSpec((rows, cols), lambda i, *_: (jnp.minimum(i, CAST_STEPS - 1), 0))


def _in_kernel(*refs, l):
    stream, refs = refs[:_n_stream(l)], refs[_n_stream(l):]
    (mod_ref, g_ref, wqkv_ref, wf_ref, wcv_ref, cos_ref, sin_ref, w1_ref, w3_ref,
     qkv_ref, f_ref, cv_ref, w1b_ref, w3b_ref) = refs[:14]
    stage = refs[14] if l > 0 else None
    w1b_ref[...] = w1_ref[...].astype(BF16)
    w3b_ref[...] = w3_ref[...].astype(BF16)
    h = _rmsnorm_mod(_stream_tile(l, stream, stage), g_ref[...], mod_ref[0:1, :], mod_ref[1:2, :])
    hb = h.astype(BF16)
    qkv = jnp.dot(hb, wqkv_ref[...], preferred_element_type=F32)
    cos = cos_ref[...]
    sin = sin_ref[...]
    lane = lax.broadcasted_iota(I32, (TM, 128), 1)
    first_half = (lane % 32) < 16

    def rope(t):
        rot = jnp.where(first_half, pltpu.roll(t, 112, 1), pltpu.roll(t, 16, 1))
        return t * cos + rot * sin

    scale = HEAD_DIM ** -0.5
    for c in range(4):
        qkv_ref[:, c * 128:(c + 1) * 128] = (rope(qkv[:, c * 128:(c + 1) * 128]) * scale).astype(BF16)
    k = rope(qkv[:, 512:640])
    v = qkv[:, 640:768]
    qkv_ref[:, 512:640] = k.astype(BF16)
    qkv_ref[:, 640:768] = v.astype(BF16)
    qkv_ref[:, 768:896] = pltpu.roll(k, HEAD_DIM, 1).astype(BF16)
    qkv_ref[:, 896:1024] = pltpu.roll(v, HEAD_DIM, 1).astype(BF16)

    f_ref[...] = jnp.dot(hb, wf_ref[...], preferred_element_type=F32).astype(BF16)

    cv = jnp.dot(hb, wcv_ref[...], preferred_element_type=F32)
    cv_ref[:, 0:GW] = cv[:, 0:GW]
    cv_ref[:, GW:2 * GW] = cv[:, GW:2 * GW] * cv[:, 2 * GW:3 * GW]
    cv_ref[:, 2 * GW:3 * GW] = cv[:, 3 * GW:4 * GW] * jax.nn.sigmoid(cv[:, 4 * GW:5 * GW])


W13_ROWS = N_EXPERTS * D // CAST_STEPS
W2_ROWS = N_EXPERTS * D_EXPERT // CAST_STEPS


def _in_call(l, stream, mod, g_norm1, wqkv, wf, wcv, cos_t, sin_t, w_gate_e, w_up_e):
    const = dict(pipeline_mode=pl.Buffered(1))
    flat = lambda w: w.reshape(DEPTH * N_EXPERTS * D, D_EXPERT)
    return pl.pallas_call(
        functools.partial(_in_kernel, l=l),
        out_shape=(jax.ShapeDtypeStruct((NT, QKV_W), BF16),
                   jax.ShapeDtypeStruct((NT, GW), BF16),
                   jax.ShapeDtypeStruct((NT, CV_OUT), F32),
                   jax.ShapeDtypeStruct((N_EXPERTS * D, D_EXPERT), BF16),
                   jax.ShapeDtypeStruct((N_EXPERTS * D, D_EXPERT), BF16)),
        grid=(N_TILES,),
        in_specs=_stream_specs(l) + [
            pl.BlockSpec((None, None, 8, D), lambda i: (l, _mod_row(i), 0, 0)),
            pl.BlockSpec((None, 1, D), lambda i: (l, 0, 0)),
            pl.BlockSpec((None, D, QKV_IN), lambda i: (l, 0, 0), **const),
            pl.BlockSpec((None, D, GW), lambda i: (l, 0, 0), **const),
            pl.BlockSpec((None, D, CV_IN), lambda i: (l, 0, 0), **const),
            pl.BlockSpec((TM, 128), lambda i: (jnp.where(i < LAT_TILES, i % SEQ_TILES, SEQ_TILES), 0)),
            pl.BlockSpec((TM, 128), lambda i: (jnp.where(i < LAT_TILES, i % SEQ_TILES, SEQ_TILES), 0)),
            _cast_spec(W13_ROWS, D_EXPERT, l), _cast_spec(W13_ROWS, D_EXPERT, l),
        ],
        out_specs=(pl.BlockSpec((TM, QKV_W), lambda i: (i, 0)),
                   pl.BlockSpec((TM, GW), lambda i: (i, 0)),
                   pl.BlockSpec((TM, CV_OUT), lambda i: (i, 0)),
                   _cast_out_spec(W13_ROWS, D_EXPERT), _cast_out_spec(W13_ROWS, D_EXPERT)),
        scratch_shapes=[pltpu.VMEM((TM * SLAB, 128), F32)] if l > 0 else [],
        compiler_params=_cparams(1),
        name="norm_in_proj",
    )(*stream, mod, g_norm1.reshape(DEPTH, 1, D), wqkv, wf, wcv, cos_t, sin_t, flat(w_gate_e), flat(w_up_e))


N_QBLK_LAT = N_LAT // TQ
N_QBLK = NT // TQ
QBLK_PER_SEQ = SEQ // TQ
KWIN = 3 * TQ


def _attn_heads(sink_ref, l, q_ref, keys, vals, bias, o_ref, s_scr, e_scr):
    lane = lax.broadcasted_iota(I32, (TQ, 128), 1)
    row2 = lax.broadcasted_iota(I32, (2 * TQ, 1), 0)
    zero = jnp.zeros((TQ, 128), BF16)
    if bias is not None:
        bias = jnp.concatenate([bias, bias], axis=0)
    nk = keys[0].shape[0]
    combos = [(hk, p) for hk in range(2) for p in range(2)]
    for idx, (hk, p) in enumerate(combos):
        kx = keys[0] if hk == p else keys[1]
        in_half = (lane // HEAD_DIM) == p
        qs = jnp.concatenate([jnp.where(in_half, q_ref[:, c * 128:(c + 1) * 128], zero)
                              for c in (2 * hk, 2 * hk + 1)], axis=0)
        s = lax.dot_general(qs, kx, (((1,), (1,)), ((), ())), preferred_element_type=F32)
        s_scr[idx, :, 0:nk] = s if bias is None else s + bias
    dens = []
    for idx, (hk, p) in enumerate(combos):
        s = s_scr[idx, :, 0:nk]
        sink = jnp.where(row2 < TQ, sink_ref[l, 4 * hk + p], sink_ref[l, 4 * hk + 2 + p])
        m = jnp.maximum(jnp.max(s, axis=-1, keepdims=True), sink)
        e = jnp.exp(s - m)
        dens.append(jnp.sum(e, axis=-1, keepdims=True) + jnp.exp(sink - m))
        e_scr[idx, :, 0:nk] = e.astype(BF16)
    for idx, (hk, p) in enumerate(combos):
        vx = vals[0] if hk == p else vals[1]
        o = (jnp.dot(e_scr[idx, :, 0:nk], vx, preferred_element_type=F32) / dens[idx]).astype(BF16)
        for k, c in enumerate((2 * hk, 2 * hk + 1)):
            lo = c * 128 + p * HEAD_DIM
            o_ref[:, lo:lo + HEAD_DIM] = o[k * TQ:(k + 1) * TQ, p * HEAD_DIM:(p + 1) * HEAD_DIM]


def _attn_kernel(sink_ref, q_ref, k_ref, v_ref, ks_ref, vs_ref, kc_ref, vc_ref, kcs_ref, vcs_ref, o_ref, s_scr, e_scr,
                 *, l):
    i = pl.program_id(0)

    @pl.when(i < N_QBLK_LAT)
    def _latent():
        n = i % QBLK_PER_SEQ
        ws = pl.multiple_of(jnp.clip((n - 1) * TQ, 0, SEQ - KWIN), TQ)
        keys = [jnp.concatenate([r[pl.ds(ws, KWIN), :], c[...]], axis=0)
                for r, c in ((k_ref, kc_ref), (ks_ref, kcs_ref))]
        vals = [jnp.concatenate([r[pl.ds(ws, KWIN), :], c[...]], axis=0)
                for r, c in ((v_ref, vc_ref), (vs_ref, vcs_ref))]
        nk = KWIN + CTX
        qpos = n * TQ + lax.broadcasted_iota(I32, (TQ, nk), 0)
        col = lax.broadcasted_iota(I32, (TQ, nk), 1)
        valid = (jnp.abs(ws + col - qpos) <= WINDOW) | (col >= KWIN)
        _attn_heads(sink_ref, l, q_ref, keys, vals, jnp.where(valid, 0.0, NEG_INF), o_ref, s_scr, e_scr)

    @pl.when(i >= N_QBLK_LAT)
    def _context():
        _attn_heads(sink_ref, l, q_ref, [kc_ref[...], kcs_ref[...]], [vc_ref[...], vcs_ref[...]], None, o_ref,
                    s_scr, e_scr)


def _attn_call(l, attn_sink, qkv):
    def seq_blk(i):
        return jnp.minimum(i // QBLK_PER_SEQ, 1)

    def ctx_blk(i):
        b = jnp.where(i < N_QBLK_LAT, i // QBLK_PER_SEQ, (i - N_QBLK_LAT) // (CTX // TQ))
        return N_LAT // CTX + b

    seq_specs = [pl.BlockSpec((SEQ, 128), functools.partial(lambda i, col: (seq_blk(i), col), col=col))
                 for col in (4, 5, 6, 7)]
    ctx_specs = [pl.BlockSpec((CTX, 128), functools.partial(lambda i, col: (ctx_blk(i), col), col=col))
                 for col in (4, 5, 6, 7)]
    n_blk = _layer_tiles(l) * TM // TQ
    return pl.pallas_call(
        functools.partial(_attn_kernel, l=l),
        out_shape=jax.ShapeDtypeStruct((n_blk * TQ, GW), BF16),
        grid=(n_blk,),
        in_specs=[pl.BlockSpec(memory_space=pltpu.SMEM),
                  pl.BlockSpec((TQ, GW), lambda i: (i, 0))] + seq_specs + ctx_specs,
        out_specs=pl.BlockSpec((TQ, GW), lambda i: (i, 0)),
        scratch_shapes=[pltpu.VMEM((4, 2 * TQ, KWIN + CTX), F32), pltpu.VMEM((4, 2 * TQ, KWIN + CTX), BF16)],
        compiler_params=_cparams(1),
        name="window_attention",
    )(attn_sink, qkv, *([qkv] * 8))


FC = 256
FFT_UNROLL = 8


def _dft_tables(n):
    t = n * n
    j = np.arange(n)
    ang = 2.0 * np.pi * np.outer(j, j) / n
    cn, sn = np.cos(ang), np.sin(ang)
    m1 = np.block([[cn, sn], [-sn, cn]])
    m2 = np.concatenate([cn, sn], axis=1)
    angc = 2.0 * np.pi * np.outer(np.arange(128), np.arange(128)) / 128.0
    wc = np.concatenate([np.cos(angc), -np.sin(angc)], axis=1)
    angt = 2.0 * np.pi * np.outer(j, j) / t
    tw_re = np.broadcast_to(np.cos(angt)[:, :, None], (n, n, 128))
    tw_im = np.broadcast_to(-np.sin(angt)[:, :, None], (n, n, 128))
    def hi_lo(m):
        return jnp.stack(_split_bf16(jnp.asarray(m, F32)))

    return (hi_lo(m1), hi_lo(m2), hi_lo(wc),
            jnp.asarray(np.ascontiguousarray(tw_re), F32), jnp.asarray(np.ascontiguousarray(tw_im), F32))


def _fft_kernel(f_ref, m1_ref, m2_ref, wc_ref, twr_ref, twi_ref, o_ref, are, aim, hre, him, *, n):
    t = n * n
    ng = FC // 128
    for g in range(ng):
        fg = f_ref[:, g * 128:(g + 1) * 128]
        a = (jnp.dot(fg, wc_ref[0], preferred_element_type=F32)
             + jnp.dot(fg, wc_ref[1], preferred_element_type=F32))
        are[g] = a[:, :128].reshape(n, n, 128)
        aim[g] = a[:, 128:].reshape(n, n, 128)

    def table_dot(m_ref, v):
        return (jnp.dot(m_ref[0], v, preferred_element_type=F32)
                + jnp.dot(m_ref[1], v, preferred_element_type=F32))

    def gather_rows(re, im, first):
        cols = pl.ds(pl.multiple_of(first, FFT_UNROLL), FFT_UNROLL)
        parts = []
        for src in (re, im):
            x = [pltpu.einshape("abc->bac", src[g, :, cols, :]) for g in range(ng)]
            parts.append(jnp.concatenate([x[g][u] for u in range(FFT_UNROLL) for g in range(ng)], axis=1))
        return jnp.concatenate(parts, axis=0).astype(BF16)

    def stage1(it, carry):
        t1 = it * FFT_UNROLL
        h = table_dot(m1_ref, gather_rows(are, aim, t1))
        for u in range(FFT_UNROLL):
            twr = twr_ref[t1 + u]
            twi = twi_ref[t1 + u]
            for g in range(ng):
                lanes = slice((u * ng + g) * 128, (u * ng + g + 1) * 128)
                h_re, h_im = h[:n, lanes], h[n:, lanes]
                hre[g, t1 + u] = h_re * twr - h_im * twi
                him[g, t1 + u] = h_re * twi + h_im * twr
        return carry

    lax.fori_loop(0, n // FFT_UNROLL, stage1, 0)

    def stage2(it, carry):
        k2 = it * FFT_UNROLL
        z = table_dot(m2_ref, gather_rows(hre, him, k2))
        z = z * ((t * 128.0) ** -0.5)
        cols = pl.ds(pl.multiple_of(k2, FFT_UNROLL), FFT_UNROLL)
        for g in range(ng):
            zg = jnp.stack([z[:, (u * ng + g) * 128:(u * ng + g + 1) * 128] for u in range(FFT_UNROLL)], axis=0)
            are[g, :, cols, :] = pltpu.einshape("abc->bac", zg)
        return carry

    lax.fori_loop(0, n // FFT_UNROLL, stage2, 0)
    for g in range(ng):
        o_ref[:, g * 128:(g + 1) * 128] = are[g].reshape(t, 128).astype(BF16)


def _fft_call(f, n, first_blk, n_seq):
    t = n * n
    m1, m2, wc, twr, twi = _dft_tables(n)
    full = lambda shape: pl.BlockSpec(shape, lambda b, h: (0,) * len(shape))
    return pl.pallas_call(
        functools.partial(_fft_kernel, n=n),
        out_shape=jax.ShapeDtypeStruct((n_seq * t, GW), BF16),
        grid=(n_seq, GW // FC),
        in_specs=[pl.BlockSpec((t, FC), lambda b, h: (first_blk + b, h)),
                  full((2, 2 * n, 2 * n)), full((2, n, 2 * n)), full((2, 128, 256)),
                  full((n, n, 128)), full((n, n, 128))],
        out_specs=pl.BlockSpec((t, FC), lambda b, h: (b, h)),
        scratch_shapes=[pltpu.VMEM((FC // 128, n, n, 128), F32)] * 4,
        compiler_params=_cparams(2),
        name="fourier_mix_%d" % t,
    )(f, m1, m2, wc, twr, twi)


CONV_ROWS = 32
GSH_ROWS = TM + 2 * HALO - 8


def _conv_kernel(sb_ref, p_ref, pp_ref, pn_ref, g_ref, gp_ref, gn_ref, sw_ref, cw_ref, cb_ref, ng_ref, nb_ref,
                 oc_ref, od_ref, pbuf, gbuf, gsh):
    i = pl.program_id(0)
    seq_start = jnp.logical_or(i % SEQ_TILES == 0, i >= LAT_TILES)
    seq_end = jnp.logical_or(i % SEQ_TILES == SEQ_TILES - 1, i >= LAT_TILES)
    for buf, main, prev, nxt in ((pbuf, p_ref, pp_ref, pn_ref), (gbuf, g_ref, gp_ref, gn_ref)):
        buf[0:HALO, :] = jnp.where(seq_start, 0.0, prev[...])
        buf[HALO:HALO + TM, :] = main[...]
        buf[HALO + TM:HALO + TM + HALO, :] = jnp.where(seq_end, 0.0, nxt[...])
    for s in range(8):
        gsh[s] = gbuf[s:s + GSH_ROWS, :]
    sw = sw_ref[...]
    cw = cw_ref[...]
    for r0 in range(0, TM, CONV_ROWS):
        base = HALO + r0
        acc = sw[0:1, :] * pbuf[base - 1:base - 1 + CONV_ROWS, :]
        acc = acc + sw[1:2, :] * pbuf[base:base + CONV_ROWS, :]
        acc = acc + sw[2:3, :] * pbuf[base + 1:base + 1 + CONV_ROWS, :]
        oc_ref[r0:r0 + CONV_ROWS, :] = (sb_ref[r0:r0 + CONV_ROWS, :] * acc).astype(BF16)

        z = None
        for j in range(CONF_W):
            o = base - CONF_PAD + j
            q = o - o % 8
            term = cw[j:j + 1, :] * gsh[o % 8, q:q + CONV_ROWS, :]
            z = term if z is None else z + term
        z = z + cb_ref[...]
        mu = jnp.mean(z, axis=-1, keepdims=True)
        zc = z - mu
        var = jnp.mean(zc * zc, axis=-1, keepdims=True)
        y = zc * lax.rsqrt(var + EPS) * ng_ref[...] + nb_ref[...]
        od_ref[r0:r0 + CONV_ROWS, :] = (y * jax.nn.sigmoid(y)).astype(BF16)


def _conv_call(l, cv, sconv_w, cconv_w, cconv_b, cnorm_g, cnorm_b):
    hb = TM // HALO
    last = NT // HALO - 1

    def halo_specs(col):
        return [pl.BlockSpec((TM, GW), lambda i: (i, col)),
                pl.BlockSpec((HALO, GW), lambda i: (jnp.maximum(i * hb - 1, 0), col)),
                pl.BlockSpec((HALO, GW), lambda i: (jnp.minimum((i + 1) * hb, last), col))]

    vec = lambda: pl.BlockSpec((None, 1, GW), lambda i: (l, 0, 0))
    cw = jnp.pad(cconv_w, ((0, 0), (0, 32 - CONF_W), (0, 0)))
    sw = jnp.pad(sconv_w, ((0, 0), (0, 8 - 3), (0, 0)))
    return pl.pallas_call(
        _conv_kernel,
        out_shape=(jax.ShapeDtypeStruct((_layer_tiles(l) * TM, GW), BF16),
                   jax.ShapeDtypeStruct((_layer_tiles(l) * TM, GW), BF16)),
        grid=(_layer_tiles(l),),
        in_specs=[pl.BlockSpec((TM, GW), lambda i: (i, 0))] + halo_specs(1) + halo_specs(2) + [
            pl.BlockSpec((None, 8, GW), lambda i: (l, 0, 0)),
            pl.BlockSpec((None, 32, GW), lambda i: (l, 0, 0)),
            vec(), vec(), vec()],
        out_specs=(pl.BlockSpec((TM, GW), lambda i: (i, 0)), pl.BlockSpec((TM, GW), lambda i: (i, 0))),
        scratch_shapes=[pltpu.VMEM((TM + 2 * HALO, GW), F32)] * 2 + [pltpu.VMEM((8, GSH_ROWS, GW), F32)],
        compiler_params=_cparams(1),
        name="conv_mixers",
    )(cv, cv, cv, cv, cv, cv, cv, sw, cw, cconv_b.reshape(DEPTH, 1, GW), cnorm_g.reshape(DEPTH, 1, GW),
      cnorm_b.reshape(DEPTH, 1, GW))


R_ROWS = 32
_SLOT_A = np.array([0, 2, 2, 0, 0, 1], np.int32)
_SLOT_B = np.array([1, 1, 3, 3, 2, 3], np.int32)


def _route(lg):
    gl = [lg[N_EXPERTS + g:N_EXPERTS + g + 1, :] for g in range(N_GROUPS)]
    gmax = functools.reduce(jnp.maximum, gl)
    pg = 1.0 / functools.reduce(jnp.add, [jnp.exp(v - gmax) for v in gl])
    gsel = jnp.full_like(gmax, N_GROUPS - 1).astype(I32)
    for g in range(N_GROUPS - 2, -1, -1):
        gsel = jnp.where(gl[g] == gmax, g, gsel)
    el = []
    for j in range(PER_GROUP):
        v = lg[j:j + 1, :]
        for g in range(1, N_GROUPS):
            v = jnp.where(gsel == g, lg[g * PER_GROUP + j:g * PER_GROUP + j + 1, :], v)
        el.append(v)
    v1 = functools.reduce(jnp.maximum, el)
    i1 = jnp.full_like(gsel, PER_GROUP - 1)
    for j in range(PER_GROUP - 2, -1, -1):
        i1 = jnp.where(el[j] == v1, j, i1)
    rest = [jnp.where(i1 == j, -jnp.inf, el[j]) for j in range(PER_GROUP)]
    v2 = functools.reduce(jnp.maximum, rest)
    i2 = jnp.full_like(gsel, PER_GROUP - 1)
    for j in range(PER_GROUP - 2, -1, -1):
        i2 = jnp.where(jnp.logical_and(rest[j] == v2, i1 != j), j, i2)
    e2 = jnp.exp(v2 - v1)
    w1 = pg / (1.0 + e2)
    w2 = pg * e2 / (1.0 + e2)
    lo = jnp.minimum(i1, i2)
    hi = jnp.maximum(i1, i2)
    w_lo = jnp.where(i1 < i2, w1, w2)
    w_hi = jnp.where(i1 < i2, w2, w1)
    pair = jnp.where(lo == 0, jnp.where(hi == 1, 0, jnp.where(hi == 2, 4, 3)),
                     jnp.where(lo == 1, jnp.where(hi == 2, 1, 5), 2))
    cls = gsel * 6 + pair
    a_is_hi = pair == 1
    return cls, jnp.where(a_is_hi, w_hi, w_lo), jnp.where(a_is_hi, w_lo, w_hi)


def _rows_to_slabs(v, stage_ref, out_ref):
    n = v.shape[0]
    for c in range(SLAB):
        stage_ref[pl.ds(c, n, stride=SLAB), :] = v[:, c * 128:(c + 1) * 128]
    out_ref[...] = stage_ref[...].astype(out_ref.dtype)


def _slabs_to_rows(src_ref, stage_ref, n):
    stage_ref[...] = src_ref[...].astype(F32)
    return jnp.concatenate([stage_ref[pl.ds(c, n, stride=SLAB), :] for c in range(SLAB)], axis=1)


def _out_kernel(*refs, l):
    oa_ref, obl_ref, obc_ref, oc_ref, od_ref, w_ref = refs[:6]
    stream, refs = refs[6:6 + _n_stream(l)], refs[6 + _n_stream(l):]
    (mod_ref, g_ref, wrh_ref, wrl_ref, br_ref, tri_ref, w2_ref,
     x1_ref, hp_ref, cls_ref, gw_ref, cnt_ref, w2b_ref, carry, stage) = refs
    i = pl.program_id(0)
    w2b_ref[...] = w2_ref[...].astype(BF16)

    @pl.when(i == 0)
    def _():
        carry[...] = jnp.zeros_like(carry)

    ob = jnp.where(i < LAT_TILES, obl_ref[...], obc_ref[...])
    y = jnp.dot(oa_ref[...], w_ref[0:GW, :], preferred_element_type=F32)
    y = y + jnp.dot(ob, w_ref[GW:2 * GW, :], preferred_element_type=F32)
    y = y + jnp.dot(oc_ref[...], w_ref[2 * GW:3 * GW, :], preferred_element_type=F32)
    y = y + jnp.dot(od_ref[...], w_ref[3 * GW:4 * GW, :], preferred_element_type=F32)
    x1 = _stream_tile(l, stream, stage) + mod_ref[2:3, :] * y
    x1_ref[...] = x1
    h2 = _rmsnorm_mod(x1, g_ref[...], mod_ref[3:4, :], mod_ref[4:5, :])
    _rows_to_slabs(h2, stage, hp_ref)

    h_hi, h_lo = _split_bf16(h2)
    lg = jnp.dot(h_hi, wrh_ref[...], preferred_element_type=F32)
    lg = lg + jnp.dot(h_lo, wrh_ref[...], preferred_element_type=F32)
    lg = lg + jnp.dot(h_hi, wrl_ref[...], preferred_element_type=F32)
    lg = jnp.transpose(lg + br_ref[...])[0:R_ROWS, :]
    cls, w_a, w_b = _route(lg)

    onehot = jnp.where(lax.broadcasted_iota(I32, (R_ROWS, TM), 0) == cls, 1.0, 0.0)
    prefix = jnp.dot(onehot.astype(BF16), tri_ref[...], preferred_element_type=F32)
    rank = jnp.sum(onehot * (prefix - 1.0 + carry[:, 0:1]), axis=0, keepdims=True)
    total = carry[...] + jnp.sum(onehot, axis=1, keepdims=True)
    carry[...] = total
    cnt_ref[...] = total.astype(I32)
    cls_ref[...] = jnp.concatenate([cls, rank.astype(I32), jnp.zeros((6, TM), I32)], axis=0)
    gw_ref[...] = jnp.concatenate([w_a, w_b, jnp.zeros((6, TM), F32)], axis=0)


def _out_call(l, o_a, o_b_lat, o_b_ctx, o_c, o_d, w_out, stream, mod, g_norm2, wr_hi, wr_lo, b_route, w_down_e):
    row = lambda: pl.BlockSpec((TM, GW), lambda i: (i, 0))
    tri = jnp.asarray(np.triu(np.ones((TM, TM), np.float32)), BF16)
    n_rows = _layer_tiles(l) * TM
    return pl.pallas_call(
        functools.partial(_out_kernel, l=l),
        out_shape=(jax.ShapeDtypeStruct((n_rows, D), F32), jax.ShapeDtypeStruct((n_rows * SLAB, 128), BF16),
                   jax.ShapeDtypeStruct((8, n_rows), I32), jax.ShapeDtypeStruct((8, n_rows), F32),
                   jax.ShapeDtypeStruct((R_ROWS, 128), I32),
                   jax.ShapeDtypeStruct((N_EXPERTS * D_EXPERT, D), BF16)),
        grid=(_layer_tiles(l),),
        in_specs=[row(),
                  pl.BlockSpec((TM, GW), lambda i: (jnp.minimum(i, LAT_TILES - 1), 0)),
                  pl.BlockSpec((TM, GW), lambda i: (jnp.maximum(i - LAT_TILES, 0), 0)),
                  row(), row(),
                  pl.BlockSpec((None, D, D), lambda i: (l, 0, 0), pipeline_mode=pl.Buffered(1))]
        + _stream_specs(l) + [
                  pl.BlockSpec((None, None, 8, D), lambda i: (l, _mod_row(i), 0, 0)),
                  pl.BlockSpec((None, 1, D), lambda i: (l, 0, 0)),
                  pl.BlockSpec((None, D, 128), lambda i: (l, 0, 0)),
                  pl.BlockSpec((None, D, 128), lambda i: (l, 0, 0)),
                  pl.BlockSpec((None, 1, 128), lambda i: (l, 0, 0)),
                  pl.BlockSpec((TM, TM), lambda i: (0, 0)),
                  _cast_spec(W2_ROWS, D, l)],
        out_specs=(pl.BlockSpec((TM, D), lambda i: (i, 0)),
                   pl.BlockSpec((TM * SLAB, 128), lambda i: (i, 0)),
                   pl.BlockSpec((8, TM), lambda i: (0, i)), pl.BlockSpec((8, TM), lambda i: (0, i)),
                   pl.BlockSpec((R_ROWS, 128), lambda i: (0, 0)),
                   _cast_out_spec(W2_ROWS, D)),
        scratch_shapes=[pltpu.VMEM((R_ROWS, 128), F32), pltpu.VMEM((TM * SLAB, 128), F32)],
        compiler_params=_cparams(1),
        name="out_proj_route",
    )(o_a, o_b_lat, o_b_ctx, o_c, o_d, w_out, *stream, mod, g_norm2.reshape(DEPTH, 1, D), wr_hi, wr_lo,
      b_route, tri, w_down_e.reshape(DEPTH * N_EXPERTS * D_EXPERT, D))


ROW_UNROLL = 8


def _gate_column(g_ref, src_ref, base):
    sub = lax.broadcasted_iota(I32, (8, 128), 0)
    pieces = []
    for g in range(TME // 8):
        v = jnp.zeros((8, 128), F32)
        for k in range(8):
            v = jnp.where(sub == k, g_ref[src_ref[base + 8 * g + k]], v)
        pieces.append(v)
    return jnp.concatenate(pieces, axis=0)


DMA_PIECES = 8


def _expert_kernel(ea_ref, eb_ref, nvalid_ref, zero_ref, starts_ref, counts_ref, cls_ref, rank_ref, gwa_ref, gwb_ref, hp_hbm, w1a_ref, w3a_ref, w2a_ref, w1b_ref,
                   w3b_ref, w2b_ref, yt_hbm, src_ref, dst_ref, hbuf0, hbuf1, ybuf0, ybuf1, stage, gsem, ssem, zsem,
                   *, n_tok):
    i = pl.program_id(0)
    nvalid = nvalid_ref[0]
    hbufs, ybufs = (hbuf0, hbuf1), (ybuf0, ybuf1)

    def slab(ref, row):
        return ref.at[pl.ds(pl.multiple_of(row * SLAB, SLAB), SLAB), :]

    def gather_copy(tile, sl, j, z=0):
        return pltpu.make_async_copy(slab(hp_hbm, src_ref[tile * TME + j] + z), slab(hbufs[sl], j), gsem.at[sl])

    def scatter_copy(tile, sl, j, z=0):
        return pltpu.make_async_copy(slab(ybufs[sl], j), slab(yt_hbm, dst_ref[(tile + 1) * TME + j] + z),
                                     ssem.at[sl])

    def for_rows(fn):
        def body(j, carry):
            fn(j)
            return carry

        lax.fori_loop(0, TME, body, 0, unroll=ROW_UNROLL)

    @pl.when(i == 0)
    def _():
        def unused(p, carry):
            src_ref[p] = n_tok - 1
            dst_ref[p + TME] = n_tok + ((p + TME) & (2 * TME - 1))
            return carry

        def dummy_tile(j, carry):
            dst_ref[j] = n_tok + j
            src_ref[nvalid * TME + j] = n_tok - 1
            return carry

        lax.fori_loop(0, TME, dummy_tile, 0, unroll=ROW_UNROLL)

        def class_tail(c, carry):
            lo = starts_ref[c] + counts_ref[c]
            hi = starts_ref[c] + ((counts_ref[c] + TME - 1) // TME) * TME
            lax.fori_loop(lo, hi, unused, 0)
            return carry

        lax.fori_loop(0, N_CLASSES, class_tail, 0)

        def place(t, carry):
            p = starts_ref[cls_ref[t]] + rank_ref[t]
            src_ref[p] = t
            dst_ref[p + TME] = t
            return carry

        lax.fori_loop(0, n_tok, place, 0, unroll=ROW_UNROLL)
        for sl in range(2):
            ybufs[sl][...] = jnp.zeros_like(ybufs[sl])
            spare = pltpu.make_async_copy(ybufs[sl], yt_hbm.at[pl.ds((n_tok + sl * TME) * SLAB, TME * SLAB), :],
                                          zsem)
            spare.start()
            spare.wait()
        for_rows(lambda j: gather_copy(0, 0, j).start())

    def tile_body(sl):
        for_rows(lambda j: gather_copy(0, sl, 0).wait())
        h = _slabs_to_rows(hbufs[sl], stage, TME).astype(BF16)
        pieces = iter(range(DMA_PIECES))

        def paced_dot(a, b):
            r = jnp.dot(a, b, preferred_element_type=F32)
            piece = next(pieces, None)
            if piece is None:
                return r
            z = pltpu.bitcast(r[0:8, 0:128], I32)[0, 0] * zero_ref[0]
            first = piece * (TME // DMA_PIECES)
            for j in range(first, first + TME // DMA_PIECES):
                gather_copy(i + 1, 1 - sl, j, z).start()
                scatter_copy(i - 1, 1 - sl, j, z).start(priority=1)
            return r

        y = [None] * (D // D_EXPERT)
        for g_ref, w1_ref, w3_ref, w2_ref in ((gwa_ref, w1a_ref, w3a_ref, w2a_ref),
                                              (gwb_ref, w1b_ref, w3b_ref, w2b_ref)):
            half = D_EXPERT // 2
            gate = jnp.concatenate([paced_dot(h, w1_ref[:, c * half:(c + 1) * half]) for c in range(2)], axis=1)
            up = jnp.concatenate([paced_dot(h, w3_ref[:, c * half:(c + 1) * half]) for c in range(2)], axis=1)
            gcol = _gate_column(g_ref, src_ref, i * TME)
            hid = (gate * jax.nn.sigmoid(gate) * up * jnp.concatenate([gcol] * (D_EXPERT // 128), axis=1)).astype(BF16)
            for c in range(len(y)):
                yc = paced_dot(hid, w2_ref[:, c * D_EXPERT:(c + 1) * D_EXPERT])
                y[c] = yc if y[c] is None else y[c] + yc
        y = jnp.concatenate(y, axis=1)

        @pl.when(i >= 1)
        def _():
            for_rows(lambda j: scatter_copy(0, sl, 0).wait())

        _rows_to_slabs(y, stage, ybufs[sl])

        @pl.when(i == nvalid - 1)
        def _():
            for_rows(lambda j: scatter_copy(i, sl, j).start())
            for_rows(lambda j: scatter_copy(0, 1 - sl, 0).wait())
            for_rows(lambda j: scatter_copy(0, sl, 0).wait())
            for_rows(lambda j: gather_copy(0, 1 - sl, 0).wait())

    for sl in range(2):
        @pl.when(jnp.logical_and(i < nvalid, i % 2 == sl))
        def _(sl=sl):
            tile_body(sl)


def _expert_call(tile_ea, tile_eb, nvalid, starts, counts, cls, rank, gwa, gwb, hp, w1b, w3b, w2b):
    n_tok = cls.shape[0]
    up_a = pl.BlockSpec((None, D, D_EXPERT), lambda i, ea, eb, *_: (ea[i], 0, 0))
    up_b = pl.BlockSpec((None, D, D_EXPERT), lambda i, ea, eb, *_: (eb[i], 0, 0))
    down_a = pl.BlockSpec((None, D_EXPERT, D), lambda i, ea, eb, *_: (ea[i], 0, 0))
    down_b = pl.BlockSpec((None, D_EXPERT, D), lambda i, ea, eb, *_: (eb[i], 0, 0))
    return pl.pallas_call(
        functools.partial(_expert_kernel, n_tok=n_tok),
        out_shape=jax.ShapeDtypeStruct(((n_tok + 2 * TME) * SLAB, 128), BF16),
        grid_spec=pltpu.PrefetchScalarGridSpec(
            num_scalar_prefetch=10,
            grid=(P_TILES,),
            in_specs=[pl.BlockSpec(memory_space=pl.ANY),
                      up_a, up_a, down_a, up_b, up_b, down_b],
            out_specs=pl.BlockSpec(memory_space=pl.ANY),
            scratch_shapes=[pltpu.SMEM((P_ROWS + TME,), I32), pltpu.SMEM((P_ROWS + TME,), I32)]
            + [pltpu.VMEM((TME * SLAB, 128), BF16)] * 4
            + [pltpu.VMEM((TME * SLAB, 128), F32),
                            pltpu.SemaphoreType.DMA((2,)), pltpu.SemaphoreType.DMA((2,)),
                            pltpu.SemaphoreType.DMA]),
        compiler_params=_cparams(1),
        name="moe_experts",
    )(tile_ea, tile_eb, nvalid, jnp.zeros((1,), I32), starts, counts, cls, rank, gwa, gwb, hp, w1b, w3b, w2b,
      w1b, w3b, w2b)


def _final_kernel(x1_ref, yt_ref, mod_ref, gf_ref, o_ref, stage):
    x2 = _stream_tile(DEPTH, (x1_ref, yt_ref, mod_ref), stage)
    ms = jnp.mean(x2 * x2, axis=-1, keepdims=True)
    o_ref[...] = x2 * lax.rsqrt(ms + EPS) * gf_ref[...]


def _final_call(x1, mod, g_final, yt):
    return pl.pallas_call(
        _final_kernel,
        out_shape=jax.ShapeDtypeStruct((N_LAT, D), F32),
        grid=(LAT_TILES,),
        in_specs=_stream_specs(DEPTH) + [pl.BlockSpec((1, D), lambda i: (0, 0))],
        out_specs=pl.BlockSpec((TM, D), lambda i: (i, 0)),
        scratch_shapes=[pltpu.VMEM((TM * SLAB, 128), F32)],
        compiler_params=_cparams(1),
        name="final_norm",
    )(x1, yt, mod, g_final.reshape(1, D))


def _slot_plan(counts):
    padded = ((counts + TME - 1) // TME) * TME
    ends = jnp.cumsum(padded)
    starts = ends - padded
    tile_start = jnp.arange(P_TILES, dtype=I32) * TME
    tile_cls = jnp.minimum(jnp.sum((tile_start[:, None] >= ends[None, :]).astype(I32), axis=1), N_CLASSES - 1)
    nvalid = (ends[-1] // TME).reshape(1)
    grp = tile_cls // 6
    tile_ea = grp * PER_GROUP + jnp.asarray(_SLOT_A)[tile_cls % 6]
    tile_eb = grp * PER_GROUP + jnp.asarray(_SLOT_B)[tile_cls % 6]
    return starts, tile_ea, tile_eb, nvalid


def _rope_tables():
    t = jnp.arange(SEQ, dtype=I32)
    row = (t // GRID_W).astype(F32)
    col = (t % GRID_W).astype(F32)
    half = HEAD_DIM // 4
    freqs = ROPE_BASE ** (-jnp.arange(0, 2 * half, 2, dtype=F32) / (2 * half))
    ang_r = row[:, None] * freqs
    ang_c = col[:, None] * freqs
    cos_h = jnp.concatenate([jnp.cos(ang_r), jnp.cos(ang_r), jnp.cos(ang_c), jnp.cos(ang_c)], axis=1)
    sin_h = jnp.concatenate([-jnp.sin(ang_r), jnp.sin(ang_r), -jnp.sin(ang_c), jnp.sin(ang_c)], axis=1)
    cos_t = jnp.concatenate([jnp.tile(cos_h, (1, 2)), jnp.ones((TM, 128), F32)], axis=0)
    sin_t = jnp.concatenate([jnp.tile(sin_h, (1, 2)), jnp.zeros((TM, 128), F32)], axis=0)
    return cos_t, sin_t


def kernel(x, c, ctx, c_ctx, w_mod, b_mod, g_norm1, g_norm2, w_in, w_out, attn_sink, sconv_w, cconv_w, cconv_b,
           cnorm_g, cnorm_b, w_route_group, b_route_group, w_route_expert, b_route_expert, w_gate_e, w_up_e,
           w_down_e, g_final):
    c8 = jnp.zeros((8, D), F32).at[0:BATCH].set(c).at[BATCH].set(c_ctx)
    w_route = jnp.concatenate([w_route_expert, w_route_group,
                               jnp.zeros((DEPTH, D, 128 - N_EXPERTS - N_GROUPS), F32)], axis=2)
    wr_hi = w_route.astype(BF16)
    wr_lo = (w_route - wr_hi.astype(F32)).astype(BF16)
    b_route = jnp.concatenate([b_route_expert, b_route_group,
                               jnp.zeros((DEPTH, 128 - N_EXPERTS - N_GROUPS), F32)], axis=1).reshape(DEPTH, 1, 128)
    cos_t, sin_t = _rope_tables()

    mod, wqkv, wf, wcv, w_out_b = _mod_call(c8, w_mod, b_mod, w_in, w_out)
    mod = mod.reshape(DEPTH, 8, N_MOD, D)[:, :3]
    mod = jnp.pad(mod, ((0, 0), (0, 0), (0, 8 - N_MOD), (0, 0)))

    stream = (x.reshape(N_LAT, D), ctx.reshape(N_CTX, D))
    for l in range(DEPTH):
        last = l == DEPTH - 1
        qkv, f, cv, w1b, w3b = _in_call(l, stream, mod, g_norm1, wqkv, wf, wcv, cos_t, sin_t, w_gate_e, w_up_e)
        o_a = _attn_call(l, attn_sink, qkv)
        o_b_lat = _fft_call(f, 64, 0, BATCH)
        o_b_ctx = o_b_lat if last else _fft_call(f, 16, N_LAT // CTX, BATCH)
        o_c, o_d = _conv_call(l, cv, sconv_w, cconv_w, cconv_b, cnorm_g, cnorm_b)
        x1, hp, cls8, gw8, cnt, w2b = _out_call(l, o_a, o_b_lat, o_b_ctx, o_c, o_d, w_out_b, stream, mod, g_norm2,
                                                wr_hi, wr_lo, b_route, w_down_e)
        counts = cnt[:N_CLASSES, 0]
        starts, tile_ea, tile_eb, nvalid = _slot_plan(counts)
        yt = _expert_call(tile_ea, tile_eb, nvalid, starts, counts, cls8[0], cls8[1], gw8[0], gw8[1], hp,
                          w1b.reshape(N_EXPERTS, D, D_EXPERT), w3b.reshape(N_EXPERTS, D, D_EXPERT),
                          w2b.reshape(N_EXPERTS, D_EXPERT, D))
        stream = (x1, yt, mod)
    return _final_call(x1, mod, g_final, yt).reshape(BATCH, SEQ, D)
```

```python
import functools

import numpy as np
import jax
import jax.numpy as jnp
from jax import lax
from jax.experimental import pallas as pl
from jax.experimental.pallas import tpu as pltpu

F32 = jnp.float32
BF16 = jnp.bfloat16
I32 = jnp.int32

D = 2048
BATCH = 2
SEQ = 4096
CTX = 256
DEPTH = 2
GRID_W = 64
N_LAT = BATCH * SEQ
N_CTX = BATCH * CTX
NT = N_LAT + N_CTX
GW = 512
HEAD_DIM = 64
N_HEADS = 8
WINDOW = 128
ROPE_BASE = 10000.0
N_MOD = 6
EPS = 1e-6
NEG_INF = -1e30
N_GROUPS = 4
PER_GROUP = 4
N_EXPERTS = 16
D_EXPERT = 512
N_CLASSES = N_GROUPS * 6
CONF_W = 31
CONF_PAD = 15

TM = 256
N_TILES = NT // TM
LAT_TILES = N_LAT // TM
SEQ_TILES = SEQ // TM
TQ = 128
HALO = 16
TME = 128
P_ROWS = NT + N_CLASSES * TME
P_TILES = P_ROWS // TME
SLAB = D // 128
VMEM_LIMIT = 56 * 1024 * 1024


def _cparams(n_axes=1, **kw):
    return pltpu.CompilerParams(dimension_semantics=("arbitrary",) * n_axes,
                                vmem_limit_bytes=VMEM_LIMIT, **kw)


def _split_bf16(v):
    hi = v.astype(BF16)
    lo = (v - hi.astype(F32)).astype(BF16)
    return hi, lo


def _mod_row(i):
    return jnp.minimum(i // SEQ_TILES, 2)


MOD_TN = 1536


MOD_STEPS = N_MOD * D // MOD_TN
W_ROWS = D // MOD_STEPS


def _mod_kernel(c_ref, w_ref, b_ref, win_ref, wout_ref, o_ref, wqkv_ref, wf_ref, wcv_ref, woutb_ref):
    w_in = win_ref[...]
    wqkv_ref[...] = w_in[:, 0:768].astype(BF16)
    wf_ref[...] = w_in[:, 768:768 + GW].astype(BF16)
    wcv_ref[...] = w_in[:, 768 + GW:].astype(BF16)
    woutb_ref[...] = wout_ref[...].astype(BF16)
    c = c_ref[...]
    s = c * jax.nn.sigmoid(c)
    s_hi, s_lo = _split_bf16(s)
    w_hi, w_lo = _split_bf16(w_ref[...])
    acc = jnp.dot(s_hi, w_hi, preferred_element_type=F32)
    acc = acc + jnp.dot(s_lo, w_hi, preferred_element_type=F32)
    acc = acc + jnp.dot(s_hi, w_lo, preferred_element_type=F32)
    o_ref[...] = acc + b_ref[...]


def _mod_call(c8, w_mod, b_mod, w_in, w_out):
    n = N_MOD * D
    rows = lambda cols: pl.BlockSpec((None, W_ROWS, cols), lambda l, j: (l, j, 0))
    return pl.pallas_call(
        _mod_kernel,
        out_shape=(jax.ShapeDtypeStruct((DEPTH, 8, n), F32),
                   jax.ShapeDtypeStruct((DEPTH, D, QKV_IN), BF16), jax.ShapeDtypeStruct((DEPTH, D, GW), BF16),
                   jax.ShapeDtypeStruct((DEPTH, D, CV_IN), BF16), jax.ShapeDtypeStruct((DEPTH, D, D), BF16)),
        grid=(DEPTH, MOD_STEPS),
        in_specs=[
            pl.BlockSpec((8, D), lambda l, j: (0, 0)),
            pl.BlockSpec((None, D, MOD_TN), lambda l, j: (l, 0, j)),
            pl.BlockSpec((None, 1, MOD_TN), lambda l, j: (l, 0, j)),
            rows(w_in.shape[-1]), rows(D),
        ],
        out_specs=(pl.BlockSpec((None, 8, MOD_TN), lambda l, j: (l, 0, j)),
                   rows(QKV_IN), rows(GW), rows(CV_IN), rows(D)),
        compiler_params=_cparams(2),
        name="adaln_mod",
    )(c8, w_mod, b_mod.reshape(DEPTH, 1, n), w_in, w_out)


QKV_W = 1024
QKV_IN = 768
CV_IN = 5 * GW
CV_OUT = 3 * GW


def _rmsnorm_mod(x, g, shift, scale):
    ms = jnp.mean(x * x, axis=-1, keepdims=True)
    y = x * lax.rsqrt(ms + EPS) * g
    return y * (1.0 + scale) + shift


def _stream_specs(l):
    if l == 0:
        return [pl.BlockSpec((TM, D), lambda i, *_: (jnp.minimum(i, LAT_TILES - 1), 0)),
                pl.BlockSpec((TM, D), lambda i, *_: (jnp.maximum(i - LAT_TILES, 0), 0))]
    return [pl.BlockSpec((TM, D), lambda i, *_: (i, 0)),
            pl.BlockSpec((TM * SLAB, 128), lambda i, *_: (i, 0)),
            pl.BlockSpec((None, None, 8, D), lambda i, *_: (l - 1, _mod_row(i), 0, 0))]


def _stream_tile(l, refs, stage):
    if l == 0:
        xl_ref, xc_ref = refs
        return jnp.where(pl.program_id(0) < LAT_TILES, xl_ref[...], xc_ref[...])
    x1_ref, yt_ref, mod_prev_ref = refs
    return x1_ref[...] + mod_prev_ref[5:6, :] * _slabs_to_rows(yt_ref, stage, TM)


def _n_stream(l):
    return 2 if l == 0 else 3


def _layer_tiles(l):
    return LAT_TILES if l == DEPTH - 1 else N_TILES


CAST_STEPS = 32


def _cast_spec(rows, cols, l):
    return pl.BlockSpec((rows, cols), lambda i, *_: (l * CAST_STEPS + jnp.minimum(i, CAST_STEPS - 1), 0))


def _cast_out_spec(rows, cols):
    return pl.BlockSpec((rows, cols), lambda i, *_: (jnp.minimum(i, CAST_STEPS - 1), 0))


def _in_kernel(*refs, l):
    stream, refs = refs[:_n_stream(l)], refs[_n_stream(l):]
    (mod_ref, g_ref, wqkv_ref, wf_ref, wcv_ref, cos_ref, sin_ref, w1_ref, w3_ref,
     qkv_ref, f_ref, cv_ref, w1b_ref, w3b_ref) = refs[:14]
    stage = refs[14] if l > 0 else None
    w1b_ref[...] = w1_ref[...].astype(BF16)
    w3b_ref[...] = w3_ref[...].astype(BF16)
    h = _rmsnorm_mod(_stream_tile(l, stream, stage), g_ref[...], mod_ref[0:1, :], mod_ref[1:2, :])
    hb = h.astype(BF16)
    qkv = jnp.dot(hb, wqkv_ref[...], preferred_element_type=F32)
    cos = cos_ref[...]
    sin = sin_ref[...]
    lane = lax.broadcasted_iota(I32, (TM, 128), 1)
    first_half = (lane % 32) < 16

    def rope(t):
        rot = jnp.where(first_half, pltpu.roll(t, 112, 1), pltpu.roll(t, 16, 1))
        return t * cos + rot * sin

    scale = HEAD_DIM ** -0.5
    for c in range(4):
        qkv_ref[:, c * 128:(c + 1) * 128] = (rope(qkv[:, c * 128:(c + 1) * 128]) * scale).astype(BF16)
    k = rope(qkv[:, 512:640])
    v = qkv[:, 640:768]
    qkv_ref[:, 512:640] = k.astype(BF16)
    qkv_ref[:, 640:768] = v.astype(BF16)
    qkv_ref[:, 768:896] = pltpu.roll(k, HEAD_DIM, 1).astype(BF16)
    qkv_ref[:, 896:1024] = pltpu.roll(v, HEAD_DIM, 1).astype(BF16)

    f_ref[...] = jnp.dot(hb, wf_ref[...], preferred_element_type=F32).astype(BF16)

    cv = jnp.dot(hb, wcv_ref[...], preferred_element_type=F32)
    cv_ref[:, 0:GW] = cv[:, 0:GW]
    cv_ref[:, GW:2 * GW] = cv[:, GW:2 * GW] * cv[:, 2 * GW:3 * GW]
    cv_ref[:, 2 * GW:3 * GW] = cv[:, 3 * GW:4 * GW] * jax.nn.sigmoid(cv[:, 4 * GW:5 * GW])


W13_ROWS = N_EXPERTS * D // CAST_STEPS
W2_ROWS = N_EXPERTS * D_EXPERT // CAST_STEPS


def _in_call(l, stream, mod, g_norm1, wqkv, wf, wcv, cos_t, sin_t, w_gate_e, w_up_e):
    const = dict(pipeline_mode=pl.Buffered(1))
    flat = lambda w: w.reshape(DEPTH * N_EXPERTS * D, D_EXPERT)
    return pl.pallas_call(
        functools.partial(_in_kernel, l=l),
        out_shape=(jax.ShapeDtypeStruct((NT, QKV_W), BF16),
                   jax.ShapeDtypeStruct((NT, GW), BF16),
                   jax.ShapeDtypeStruct((NT, CV_OUT), F32),
                   jax.ShapeDtypeStruct((N_EXPERTS * D, D_EXPERT), BF16),
                   jax.ShapeDtypeStruct((N_EXPERTS * D, D_EXPERT), BF16)),
        grid=(N_TILES,),
        in_specs=_stream_specs(l) + [
            pl.BlockSpec((None, None, 8, D), lambda i: (l, _mod_row(i), 0, 0)),
            pl.BlockSpec((None, 1, D), lambda i: (l, 0, 0)),
            pl.BlockSpec((None, D, QKV_IN), lambda i: (l, 0, 0), **const),
            pl.BlockSpec((None, D, GW), lambda i: (l, 0, 0), **const),
            pl.BlockSpec((None, D, CV_IN), lambda i: (l, 0, 0), **const),
            pl.BlockSpec((TM, 128), lambda i: (jnp.where(i < LAT_TILES, i % SEQ_TILES, SEQ_TILES), 0)),
            pl.BlockSpec((TM, 128), lambda i: (jnp.where(i < LAT_TILES, i % SEQ_TILES, SEQ_TILES), 0)),
            _cast_spec(W13_ROWS, D_EXPERT, l), _cast_spec(W13_ROWS, D_EXPERT, l),
        ],
        out_specs=(pl.BlockSpec((TM, QKV_W), lambda i: (i, 0)),
                   pl.BlockSpec((TM, GW), lambda i: (i, 0)),
                   pl.BlockSpec((TM, CV_OUT), lambda i: (i, 0)),
                   _cast_out_spec(W13_ROWS, D_EXPERT), _cast_out_spec(W13_ROWS, D_EXPERT)),
        scratch_shapes=[pltpu.VMEM((TM * SLAB, 128), F32)] if l > 0 else [],
        compiler_params=_cparams(1),
        name="norm_in_proj",
    )(*stream, mod, g_norm1.reshape(DEPTH, 1, D), wqkv, wf, wcv, cos_t, sin_t, flat(w_gate_e), flat(w_up_e))


N_QBLK_LAT = N_LAT // TQ
N_QBLK = NT // TQ
QBLK_PER_SEQ = SEQ // TQ
KWIN = 3 * TQ


def _attn_heads(sink_ref, l, q_ref, keys, vals, bias, o_ref, s_scr, e_scr):
    lane = lax.broadcasted_iota(I32, (TQ, 128), 1)
    row2 = lax.broadcasted_iota(I32, (2 * TQ, 1), 0)
    zero = jnp.zeros((TQ, 128), BF16)
    if bias is not None:
        bias = jnp.concatenate([bias, bias], axis=0)
    nk = keys[0].shape[0]
    combos = [(hk, p) for hk in range(2) for p in range(2)]
    for idx, (hk, p) in enumerate(combos):
        kx = keys[0] if hk == p else keys[1]
        in_half = (lane // HEAD_DIM) == p
        qs = jnp.concatenate([jnp.where(in_half, q_ref[:, c * 128:(c + 1) * 128], zero)
                              for c in (2 * hk, 2 * hk + 1)], axis=0)
        s = lax.dot_general(qs, kx, (((1,), (1,)), ((), ())), preferred_element_type=F32)
        s_scr[idx, :, 0:nk] = s if bias is None else s + bias
    dens = []
    for idx, (hk, p) in enumerate(combos):
        s = s_scr[idx, :, 0:nk]
        sink = jnp.where(row2 < TQ, sink_ref[l, 4 * hk + p], sink_ref[l, 4 * hk + 2 + p])
        m = jnp.maximum(jnp.max(s, axis=-1, keepdims=True), sink)
        e = jnp.exp(s - m)
        dens.append(jnp.sum(e, axis=-1, keepdims=True) + jnp.exp(sink - m))
        e_scr[idx, :, 0:nk] = e.astype(BF16)
    for idx, (hk, p) in enumerate(combos):
        vx = vals[0] if hk == p else vals[1]
        o = (jnp.dot(e_scr[idx, :, 0:nk], vx, preferred_element_type=F32) / dens[idx]).astype(BF16)
        for k, c in enumerate((2 * hk, 2 * hk + 1)):
            lo = c * 128 + p * HEAD_DIM
            o_ref[:, lo:lo + HEAD_DIM] = o[k * TQ:(k + 1) * TQ, p * HEAD_DIM:(p + 1) * HEAD_DIM]


def _attn_kernel(sink_ref, q_ref, k_ref, v_ref, ks_ref, vs_ref, kc_ref, vc_ref, kcs_ref, vcs_ref, o_ref, s_scr, e_scr,
                 *, l):
    i = pl.program_id(0)

    @pl.when(i < N_QBLK_LAT)
    def _latent():
        n = i % QBLK_PER_SEQ
        ws = pl.multiple_of(jnp.clip((n - 1) * TQ, 0, SEQ - KWIN), TQ)
        keys = [jnp.concatenate([r[pl.ds(ws, KWIN), :], c[...]], axis=0)
                for r, c in ((k_ref, kc_ref), (ks_ref, kcs_ref))]
        vals = [jnp.concatenate([r[pl.ds(ws, KWIN), :], c[...]], axis=0)
                for r, c in ((v_ref, vc_ref), (vs_ref, vcs_ref))]
        nk = KWIN + CTX
        qpos = n * TQ + lax.broadcasted_iota(I32, (TQ, nk), 0)
        col = lax.broadcasted_iota(I32, (TQ, nk), 1)
        valid = (jnp.abs(ws + col - qpos) <= WINDOW) | (col >= KWIN)
        _attn_heads(sink_ref, l, q_ref, keys, vals, jnp.where(valid, 0.0, NEG_INF), o_ref, s_scr, e_scr)

    @pl.when(i >= N_QBLK_LAT)
    def _context():
        _attn_heads(sink_ref, l, q_ref, [kc_ref[...], kcs_ref[...]], [vc_ref[...], vcs_ref[...]], None, o_ref,
                    s_scr, e_scr)


def _attn_call(l, attn_sink, qkv):
    def seq_blk(i):
        return jnp.minimum(i // QBLK_PER_SEQ, 1)

    def ctx_blk(i):
        b = jnp.where(i < N_QBLK_LAT, i // QBLK_PER_SEQ, (i - N_QBLK_LAT) // (CTX // TQ))
        return N_LAT // CTX + b

    seq_specs = [pl.BlockSpec((SEQ, 128), functools.partial(lambda i, col: (seq_blk(i), col), col=col))
                 for col in (4, 5, 6, 7)]
    ctx_specs = [pl.BlockSpec((CTX, 128), functools.partial(lambda i, col: (ctx_blk(i), col), col=col))
                 for col in (4, 5, 6, 7)]
    n_blk = _layer_tiles(l) * TM // TQ
    return pl.pallas_call(
        functools.partial(_attn_kernel, l=l),
        out_shape=jax.ShapeDtypeStruct((n_blk * TQ, GW), BF16),
        grid=(n_blk,),
        in_specs=[pl.BlockSpec(memory_space=pltpu.SMEM),
                  pl.BlockSpec((TQ, GW), lambda i: (i, 0))] + seq_specs + ctx_specs,
        out_specs=pl.BlockSpec((TQ, GW), lambda i: (i, 0)),
        scratch_shapes=[pltpu.VMEM((4, 2 * TQ, KWIN + CTX), F32), pltpu.VMEM((4, 2 * TQ, KWIN + CTX), BF16)],
        compiler_params=_cparams(1),
        name="window_attention",
    )(attn_sink, qkv, *([qkv] * 8))


FC = 256
FFT_UNROLL = 8


def _dft_tables(n):
    t = n * n
    j = np.arange(n)
    ang = 2.0 * np.pi * np.outer(j, j) / n
    cn, sn = np.cos(ang), np.sin(ang)
    m1 = np.block([[cn, sn], [-sn, cn]])
    m2 = np.concatenate([cn, sn], axis=1)
    angc = 2.0 * np.pi * np.outer(np.arange(128), np.arange(128)) / 128.0
    wc = np.concatenate([np.cos(angc), -np.sin(angc)], axis=1)
    angt = 2.0 * np.pi * np.outer(j, j) / t
    tw_re = np.broadcast_to(np.cos(angt)[:, :, None], (n, n, 128))
    tw_im = np.broadcast_to(-np.sin(angt)[:, :, None], (n, n, 128))
    def hi_lo(m):
        return jnp.stack(_split_bf16(jnp.asarray(m, F32)))

    return (hi_lo(m1), hi_lo(m2), hi_lo(wc),
            jnp.asarray(np.ascontiguousarray(tw_re), F32), jnp.asarray(np.ascontiguousarray(tw_im), F32))


def _fft_kernel(f_ref, m1_ref, m2_ref, wc_ref, twr_ref, twi_ref, o_ref, are, aim, hre, him, *, n):
    t = n * n
    ng = FC // 128
    for g in range(ng):
        fg = f_ref[:, g * 128:(g + 1) * 128]
        a = (jnp.dot(fg, wc_ref[0], preferred_element_type=F32)
             + jnp.dot(fg, wc_ref[1], preferred_element_type=F32))
        are[g] = a[:, :128].reshape(n, n, 128)
        aim[g] = a[:, 128:].reshape(n, n, 128)

    def table_dot(m_ref, v):
        return (jnp.dot(m_ref[0], v, preferred_element_type=F32)
                + jnp.dot(m_ref[1], v, preferred_element_type=F32))

    def gather_rows(re, im, first):
        cols = pl.ds(pl.multiple_of(first, FFT_UNROLL), FFT_UNROLL)
        parts = []
        for src in (re, im):
            x = [pltpu.einshape("abc->bac", src[g, :, cols, :]) for g in range(ng)]
            parts.append(jnp.concatenate([x[g][u] for u in range(FFT_UNROLL) for g in range(ng)], axis=1))
        return jnp.concatenate(parts, axis=0).astype(BF16)

    def stage1(it, carry):
        t1 = it * FFT_UNROLL
        h = table_dot(m1_ref, gather_rows(are, aim, t1))
        for u in range(FFT_UNROLL):
            twr = twr_ref[t1 + u]
            twi = twi_ref[t1 + u]
            for g in range(ng):
                lanes = slice((u * ng + g) * 128, (u * ng + g + 1) * 128)
                h_re, h_im = h[:n, lanes], h[n:, lanes]
                hre[g, t1 + u] = h_re * twr - h_im * twi
                him[g, t1 + u] = h_re * twi + h_im * twr
        return carry

    lax.fori_loop(0, n // FFT_UNROLL, stage1, 0)

    def stage2(it, carry):
        k2 = it * FFT_UNROLL
        z = table_dot(m2_ref, gather_rows(hre, him, k2))
        z = z * ((t * 128.0) ** -0.5)
        cols = pl.ds(pl.multiple_of(k2, FFT_UNROLL), FFT_UNROLL)
        for g in range(ng):
            zg = jnp.stack([z[:, (u * ng + g) * 128:(u * ng + g + 1) * 128] for u in range(FFT_UNROLL)], axis=0)
            are[g, :, cols, :] = pltpu.einshape("abc->bac", zg)
        return carry

    lax.fori_loop(0, n // FFT_UNROLL, stage2, 0)
    for g in range(ng):
        o_ref[:, g * 128:(g + 1) * 128] = are[g].reshape(t, 128).astype(BF16)


def _fft_call(f, n, first_blk, n_seq):
    t = n * n
    m1, m2, wc, twr, twi = _dft_tables(n)
    full = lambda shape: pl.BlockSpec(shape, lambda b, h: (0,) * len(shape))
    return pl.pallas_call(
        functools.partial(_fft_kernel, n=n),
        out_shape=jax.ShapeDtypeStruct((n_seq * t, GW), BF16),
        grid=(n_seq, GW // FC),
        in_specs=[pl.BlockSpec((t, FC), lambda b, h: (first_blk + b, h)),
                  full((2, 2 * n, 2 * n)), full((2, n, 2 * n)), full((2, 128, 256)),
                  full((n, n, 128)), full((n, n, 128))],
        out_specs=pl.BlockSpec((t, FC), lambda b, h: (b, h)),
        scratch_shapes=[pltpu.VMEM((FC // 128, n, n, 128), F32)] * 4,
        compiler_params=_cparams(2),
        name="fourier_mix_%d" % t,
    )(f, m1, m2, wc, twr, twi)


CONV_ROWS = 32
GSH_ROWS = TM + 2 * HALO - 8


def _conv_kernel(sb_ref, p_ref, pp_ref, pn_ref, g_ref, gp_ref, gn_ref, sw_ref, cw_ref, cb_ref, ng_ref, nb_ref,
                 oc_ref, od_ref, pbuf, gbuf, gsh):
    i = pl.program_id(0)
    seq_start = jnp.logical_or(i % SEQ_TILES == 0, i >= LAT_TILES)
    seq_end = jnp.logical_or(i % SEQ_TILES == SEQ_TILES - 1, i >= LAT_TILES)
    for buf, main, prev, nxt in ((pbuf, p_ref, pp_ref, pn_ref), (gbuf, g_ref, gp_ref, gn_ref)):
        buf[0:HALO, :] = jnp.where(seq_start, 0.0, prev[...])
        buf[HALO:HALO + TM, :] = main[...]
        buf[HALO + TM:HALO + TM + HALO, :] = jnp.where(seq_end, 0.0, nxt[...])
    for s in range(8):
        gsh[s] = gbuf[s:s + GSH_ROWS, :]
    sw = sw_ref[...]
    cw = cw_ref[...]
    for r0 in range(0, TM, CONV_ROWS):
        base = HALO + r0
        acc = sw[0:1, :] * pbuf[base - 1:base - 1 + CONV_ROWS, :]
        acc = acc + sw[1:2, :] * pbuf[base:base + CONV_ROWS, :]
        acc = acc + sw[2:3, :] * pbuf[base + 1:base + 1 + CONV_ROWS, :]
        oc_ref[r0:r0 + CONV_ROWS, :] = (sb_ref[r0:r0 + CONV_ROWS, :] * acc).astype(BF16)

        z = None
        for j in range(CONF_W):
            o = base - CONF_PAD + j
            q = o - o % 8
            term = cw[j:j + 1, :] * gsh[o % 8, q:q + CONV_ROWS, :]
            z = term if z is None else z + term
        z = z + cb_ref[...]
        mu = jnp.mean(z, axis=-1, keepdims=True)
        zc = z - mu
        var = jnp.mean(zc * zc, axis=-1, keepdims=True)
        y = zc * lax.rsqrt(var + EPS) * ng_ref[...] + nb_ref[...]
        od_ref[r0:r0 + CONV_ROWS, :] = (y * jax.nn.sigmoid(y)).astype(BF16)


def _conv_call(l, cv, sconv_w, cconv_w, cconv_b, cnorm_g, cnorm_b):
    hb = TM // HALO
    last = NT // HALO - 1

    def halo_specs(col):
        return [pl.BlockSpec((TM, GW), lambda i: (i, col)),
                pl.BlockSpec((HALO, GW), lambda i: (jnp.maximum(i * hb - 1, 0), col)),
                pl.BlockSpec((HALO, GW), lambda i: (jnp.minimum((i + 1) * hb, last), col))]

    vec = lambda: pl.BlockSpec((None, 1, GW), lambda i: (l, 0, 0))
    cw = jnp.pad(cconv_w, ((0, 0), (0, 32 - CONF_W), (0, 0)))
    sw = jnp.pad(sconv_w, ((0, 0), (0, 8 - 3), (0, 0)))
    return pl.pallas_call(
        _conv_kernel,
        out_shape=(jax.ShapeDtypeStruct((_layer_tiles(l) * TM, GW), BF16),
                   jax.ShapeDtypeStruct((_layer_tiles(l) * TM, GW), BF16)),
        grid=(_layer_tiles(l),),
        in_specs=[pl.BlockSpec((TM, GW), lambda i: (i, 0))] + halo_specs(1) + halo_specs(2) + [
            pl.BlockSpec((None, 8, GW), lambda i: (l, 0, 0)),
            pl.BlockSpec((None, 32, GW), lambda i: (l, 0, 0)),
            vec(), vec(), vec()],
        out_specs=(pl.BlockSpec((TM, GW), lambda i: (i, 0)), pl.BlockSpec((TM, GW), lambda i: (i, 0))),
        scratch_shapes=[pltpu.VMEM((TM + 2 * HALO, GW), F32)] * 2 + [pltpu.VMEM((8, GSH_ROWS, GW), F32)],
        compiler_params=_cparams(1),
        name="conv_mixers",
    )(cv, cv, cv, cv, cv, cv, cv, sw, cw, cconv_b.reshape(DEPTH, 1, GW), cnorm_g.reshape(DEPTH, 1, GW),
      cnorm_b.reshape(DEPTH, 1, GW))


R_ROWS = 32
_SLOT_A = np.array([0, 2, 2, 0, 0, 1], np.int32)
_SLOT_B = np.array([1, 1, 3, 3, 2, 3], np.int32)


def _route(lg):
    gl = [lg[N_EXPERTS + g:N_EXPERTS + g + 1, :] for g in range(N_GROUPS)]
    gmax = functools.reduce(jnp.maximum, gl)
    pg = 1.0 / functools.reduce(jnp.add, [jnp.exp(v - gmax) for v in gl])
    gsel = jnp.full_like(gmax, N_GROUPS - 1).astype(I32)
    for g in range(N_GROUPS - 2, -1, -1):
        gsel = jnp.where(gl[g] == gmax, g, gsel)
    el = []
    for j in range(PER_GROUP):
        v = lg[j:j + 1, :]
        for g in range(1, N_GROUPS):
            v = jnp.where(gsel == g, lg[g * PER_GROUP + j:g * PER_GROUP + j + 1, :], v)
        el.append(v)
    v1 = functools.reduce(jnp.maximum, el)
    i1 = jnp.full_like(gsel, PER_GROUP - 1)
    for j in range(PER_GROUP - 2, -1, -1):
        i1 = jnp.where(el[j] == v1, j, i1)
    rest = [jnp.where(i1 == j, -jnp.inf, el[j]) for j in range(PER_GROUP)]
    v2 = functools.reduce(jnp.maximum, rest)
    i2 = jnp.full_like(gsel, PER_GROUP - 1)
    for j in range(PER_GROUP - 2, -1, -1):
        i2 = jnp.where(jnp.logical_and(rest[j] == v2, i1 != j), j, i2)
    e2 = jnp.exp(v2 - v1)
    w1 = pg / (1.0 + e2)
    w2 = pg * e2 / (1.0 + e2)
    lo = jnp.minimum(i1, i2)
    hi = jnp.maximum(i1, i2)
    w_lo = jnp.where(i1 < i2, w1, w2)
    w_hi = jnp.where(i1 < i2, w2, w1)
    pair = jnp.where(lo == 0, jnp.where(hi == 1, 0, jnp.where(hi == 2, 4, 3)),
                     jnp.where(lo == 1, jnp.where(hi == 2, 1, 5), 2))
    cls = gsel * 6 + pair
    a_is_hi = pair == 1
    return cls, jnp.where(a_is_hi, w_hi, w_lo), jnp.where(a_is_hi, w_lo, w_hi)


def _rows_to_slabs(v, stage_ref, out_ref):
    n = v.shape[0]
    for c in range(SLAB):
        stage_ref[pl.ds(c, n, stride=SLAB), :] = v[:, c * 128:(c + 1) * 128]
    out_ref[...] = stage_ref[...].astype(out_ref.dtype)


def _slabs_to_rows(src_ref, stage_ref, n):
    stage_ref[...] = src_ref[...].astype(F32)
    return jnp.concatenate([stage_ref[pl.ds(c, n, stride=SLAB), :] for c in range(SLAB)], axis=1)


def _out_kernel(*refs, l):
    oa_ref, obl_ref, obc_ref, oc_ref, od_ref, w_ref = refs[:6]
    stream, refs = refs[6:6 + _n_stream(l)], refs[6 + _n_stream(l):]
    (mod_ref, g_ref, wrc_ref, wrh_ref, br_ref, tri_ref, w2_ref,
     x1_ref, hp_ref, cls_ref, gw_ref, cnt_ref, w2b_ref, carry, stage) = refs
    i = pl.program_id(0)
    w2b_ref[...] = w2_ref[...].astype(BF16)

    @pl.when(i == 0)
    def _():
        carry[...] = jnp.zeros_like(carry)

    ob = jnp.where(i < LAT_TILES, obl_ref[...], obc_ref[...])
    y = jnp.dot(oa_ref[...], w_ref[0:GW, :], preferred_element_type=F32)
    y = y + jnp.dot(ob, w_ref[GW:2 * GW, :], preferred_element_type=F32)
    y = y + jnp.dot(oc_ref[...], w_ref[2 * GW:3 * GW, :], preferred_element_type=F32)
    y = y + jnp.dot(od_ref[...], w_ref[3 * GW:4 * GW, :], preferred_element_type=F32)
    x1 = _stream_tile(l, stream, stage) + mod_ref[2:3, :] * y
    x1_ref[...] = x1
    h2 = _rmsnorm_mod(x1, g_ref[...], mod_ref[3:4, :], mod_ref[4:5, :])
    _rows_to_slabs(h2, stage, hp_ref)

    h_hi, h_lo = _split_bf16(h2)
    hh = jnp.dot(h_hi, wrc_ref[...], preferred_element_type=F32)
    lg = hh[:, :128] + hh[:, 128:] + jnp.dot(h_lo, wrh_ref[...], preferred_element_type=F32)
    lg = jnp.transpose(lg + br_ref[...])[0:R_ROWS, :]
    cls, w_a, w_b = _route(lg)

    onehot = jnp.where(lax.broadcasted_iota(I32, (R_ROWS, TM), 0) == cls, 1.0, 0.0)
    prefix = jnp.dot(onehot.astype(BF16), tri_ref[...], preferred_element_type=F32)
    rank = jnp.sum(onehot * (prefix - 1.0 + carry[:, 0:1]), axis=0, keepdims=True)
    total = carry[...] + jnp.sum(onehot, axis=1, keepdims=True)
    carry[...] = total
    cnt_ref[...] = total.astype(I32)
    cls_ref[...] = jnp.concatenate([cls, rank.astype(I32), jnp.zeros((6, TM), I32)], axis=0)
    gw_ref[...] = jnp.concatenate([w_a, w_b, jnp.zeros((6, TM), F32)], axis=0)


def _out_call(l, o_a, o_b_lat, o_b_ctx, o_c, o_d, w_out, stream, mod, g_norm2, wr_cat, wr_hi, b_route, w_down_e):
    row = lambda: pl.BlockSpec((TM, GW), lambda i: (i, 0))
    tri = jnp.asarray(np.triu(np.ones((TM, TM), np.float32)), BF16)
    n_rows = _layer_tiles(l) * TM
    return pl.pallas_call(
        functools.partial(_out_kernel, l=l),
        out_shape=(jax.ShapeDtypeStruct((n_rows, D), F32), jax.ShapeDtypeStruct((n_rows * SLAB, 128), BF16),
                   jax.ShapeDtypeStruct((8, n_rows), I32), jax.ShapeDtypeStruct((8, n_rows), F32),
                   jax.ShapeDtypeStruct((R_ROWS, 128), I32),
                   jax.ShapeDtypeStruct((N_EXPERTS * D_EXPERT, D), BF16)),
        grid=(_layer_tiles(l),),
        in_specs=[row(),
                  pl.BlockSpec((TM, GW), lambda i: (jnp.minimum(i, LAT_TILES - 1), 0)),
                  pl.BlockSpec((TM, GW), lambda i: (jnp.maximum(i - LAT_TILES, 0), 0)),
                  row(), row(),
                  pl.BlockSpec((None, D, D), lambda i: (l, 0, 0), pipeline_mode=pl.Buffered(1))]
        + _stream_specs(l) + [
                  pl.BlockSpec((None, None, 8, D), lambda i: (l, _mod_row(i), 0, 0)),
                  pl.BlockSpec((None, 1, D), lambda i: (l, 0, 0)),
                  pl.BlockSpec((None, D, 256), lambda i: (l, 0, 0)),
                  pl.BlockSpec((None, D, 128), lambda i: (l, 0, 0)),
                  pl.BlockSpec((None, 1, 128), lambda i: (l, 0, 0)),
                  pl.BlockSpec((TM, TM), lambda i: (0, 0)),
                  _cast_spec(W2_ROWS, D, l)],
        out_specs=(pl.BlockSpec((TM, D), lambda i: (i, 0)),
                   pl.BlockSpec((TM * SLAB, 128), lambda i: (i, 0)),
                   pl.BlockSpec((8, TM), lambda i: (0, i)), pl.BlockSpec((8, TM), lambda i: (0, i)),
                   pl.BlockSpec((R_ROWS, 128), lambda i: (0, 0)),
                   _cast_out_spec(W2_ROWS, D)),
        scratch_shapes=[pltpu.VMEM((R_ROWS, 128), F32), pltpu.VMEM((TM * SLAB, 128), F32)],
        compiler_params=_cparams(1),
        name="out_proj_route",
    )(o_a, o_b_lat, o_b_ctx, o_c, o_d, w_out, *stream, mod, g_norm2.reshape(DEPTH, 1, D), wr_cat, wr_hi,
      b_route, tri, w_down_e.reshape(DEPTH * N_EXPERTS * D_EXPERT, D))


ROW_UNROLL = 8


def _gate_column(g_ref, src_ref, base):
    sub = lax.broadcasted_iota(I32, (8, 128), 0)
    pieces = []
    for g in range(TME // 8):
        v = jnp.zeros((8, 128), F32)
        for k in range(8):
            v = jnp.where(sub == k, g_ref[src_ref[base + 8 * g + k]], v)
        pieces.append(v)
    return jnp.concatenate(pieces, axis=0)


DMA_PIECES = 8


def _expert_kernel(ea_ref, eb_ref, nvalid_ref, zero_ref, starts_ref, counts_ref, cls_ref, rank_ref, gwa_ref, gwb_ref, hp_hbm, w1a_ref, w3a_ref, w2a_ref, w1b_ref,
                   w3b_ref, w2b_ref, yt_hbm, src_ref, dst_ref, hbuf0, hbuf1, ybuf0, ybuf1, stage, gsem, ssem, zsem,
                   *, n_tok):
    i = pl.program_id(0)
    nvalid = nvalid_ref[0]
    hbufs, ybufs = (hbuf0, hbuf1), (ybuf0, ybuf1)

    def slab(ref, row):
        return ref.at[pl.ds(pl.multiple_of(row * SLAB, SLAB), SLAB), :]

    def gather_copy(tile, sl, j, z=0):
        return pltpu.make_async_copy(slab(hp_hbm, src_ref[tile * TME + j] + z), slab(hbufs[sl], j), gsem.at[sl])

    def scatter_copy(tile, sl, j, z=0):
        return pltpu.make_async_copy(slab(ybufs[sl], j), slab(yt_hbm, dst_ref[(tile + 1) * TME + j] + z),
                                     ssem.at[sl])

    def for_rows(fn):
        def body(j, carry):
            fn(j)
            return carry

        lax.fori_loop(0, TME, body, 0, unroll=ROW_UNROLL)

    @pl.when(i == 0)
    def _():
        def unused(p, carry):
            src_ref[p] = n_tok - 1
            dst_ref[p + TME] = n_tok + ((p + TME) & (2 * TME - 1))
            return carry

        def dummy_tile(j, carry):
            dst_ref[j] = n_tok + j
            src_ref[nvalid * TME + j] = n_tok - 1
            return carry

        lax.fori_loop(0, TME, dummy_tile, 0, unroll=ROW_UNROLL)

        def class_tail(c, carry):
            lo = starts_ref[c] + counts_ref[c]
            hi = starts_ref[c] + ((counts_ref[c] + TME - 1) // TME) * TME
            lax.fori_loop(lo, hi, unused, 0)
            return carry

        lax.fori_loop(0, N_CLASSES, class_tail, 0)

        def place(t, carry):
            p = starts_ref[cls_ref[t]] + rank_ref[t]
            src_ref[p] = t
            dst_ref[p + TME] = t
            return carry

        lax.fori_loop(0, n_tok, place, 0, unroll=ROW_UNROLL)
        for sl in range(2):
            ybufs[sl][...] = jnp.zeros_like(ybufs[sl])
            spare = pltpu.make_async_copy(ybufs[sl], yt_hbm.at[pl.ds((n_tok + sl * TME) * SLAB, TME * SLAB), :],
                                          zsem)
            spare.start()
            spare.wait()
        for_rows(lambda j: gather_copy(0, 0, j).start())

    def tile_body(sl):
        for_rows(lambda j: gather_copy(0, sl, 0).wait())
        h = _slabs_to_rows(hbufs[sl], stage, TME).astype(BF16)
        pieces = iter(range(DMA_PIECES))

        def paced_dot(a, b):
            r = jnp.dot(a, b, preferred_element_type=F32)
            piece = next(pieces, None)
            if piece is None:
                return r
            z = pltpu.bitcast(r[0:8, 0:128], I32)[0, 0] * zero_ref[0]
            first = piece * (TME // DMA_PIECES)
            for j in range(first, first + TME // DMA_PIECES):
                gather_copy(i + 1, 1 - sl, j, z).start()
                scatter_copy(i - 1, 1 - sl, j, z).start(priority=1)
            return r

        y = [None] * (D // D_EXPERT)
        for g_ref, w1_ref, w3_ref, w2_ref in ((gwa_ref, w1a_ref, w3a_ref, w2a_ref),
                                              (gwb_ref, w1b_ref, w3b_ref, w2b_ref)):
            half = D_EXPERT // 2
            gate = jnp.concatenate([paced_dot(h, w1_ref[:, c * half:(c + 1) * half]) for c in range(2)], axis=1)
            up = jnp.concatenate([paced_dot(h, w3_ref[:, c * half:(c + 1) * half]) for c in range(2)], axis=1)
            gcol = _gate_column(g_ref, src_ref, i * TME)
            hid = (gate * jax.nn.sigmoid(gate) * up * jnp.concatenate([gcol] * (D_EXPERT // 128), axis=1)).astype(BF16)
            for c in range(len(y)):
                yc = paced_dot(hid, w2_ref[:, c * D_EXPERT:(c + 1) * D_EXPERT])
                y[c] = yc if y[c] is None else y[c] + yc
        y = jnp.concatenate(y, axis=1)

        @pl.when(i >= 1)
        def _():
            for_rows(lambda j: scatter_copy(0, sl, 0).wait())

        _rows_to_slabs(y, stage, ybufs[sl])

        @pl.when(i == nvalid - 1)
        def _():
            for_rows(lambda j: scatter_copy(i, sl, j).start())
            for_rows(lambda j: scatter_copy(0, 1 - sl, 0).wait())
            for_rows(lambda j: scatter_copy(0, sl, 0).wait())
            for_rows(lambda j: gather_copy(0, 1 - sl, 0).wait())

    for sl in range(2):
        @pl.when(jnp.logical_and(i < nvalid, i % 2 == sl))
        def _(sl=sl):
            tile_body(sl)


def _expert_call(tile_ea, tile_eb, nvalid, starts, counts, cls, rank, gwa, gwb, hp, w1b, w3b, w2b):
    n_tok = cls.shape[0]
    up_a = pl.BlockSpec((None, D, D_EXPERT), lambda i, ea, eb, *_: (ea[i], 0, 0))
    up_b = pl.BlockSpec((None, D, D_EXPERT), lambda i, ea, eb, *_: (eb[i], 0, 0))
    down_a = pl.BlockSpec((None, D_EXPERT, D), lambda i, ea, eb, *_: (ea[i], 0, 0))
    down_b = pl.BlockSpec((None, D_EXPERT, D), lambda i, ea, eb, *_: (eb[i], 0, 0))
    return pl.pallas_call(
        functools.partial(_expert_kernel, n_tok=n_tok),
        out_shape=jax.ShapeDtypeStruct(((n_tok + 2 * TME) * SLAB, 128), BF16),
        grid_spec=pltpu.PrefetchScalarGridSpec(
            num_scalar_prefetch=10,
            grid=(P_TILES,),
            in_specs=[pl.BlockSpec(memory_space=pl.ANY),
                      up_a, up_a, down_a, up_b, up_b, down_b],
            out_specs=pl.BlockSpec(memory_space=pl.ANY),
            scratch_shapes=[pltpu.SMEM((P_ROWS + TME,), I32), pltpu.SMEM((P_ROWS + TME,), I32)]
            + [pltpu.VMEM((TME * SLAB, 128), BF16)] * 4
            + [pltpu.VMEM((TME * SLAB, 128), F32),
                            pltpu.SemaphoreType.DMA((2,)), pltpu.SemaphoreType.DMA((2,)),
                            pltpu.SemaphoreType.DMA]),
        compiler_params=_cparams(1),
        name="moe_experts",
    )(tile_ea, tile_eb, nvalid, jnp.zeros((1,), I32), starts, counts, cls, rank, gwa, gwb, hp, w1b, w3b, w2b,
      w1b, w3b, w2b)


def _final_kernel(x1_ref, yt_ref, mod_ref, gf_ref, o_ref, stage):
    x2 = _stream_tile(DEPTH, (x1_ref, yt_ref, mod_ref), stage)
    ms = jnp.mean(x2 * x2, axis=-1, keepdims=True)
    o_ref[...] = x2 * lax.rsqrt(ms + EPS) * gf_ref[...]


def _final_call(x1, mod, g_final, yt):
    return pl.pallas_call(
        _final_kernel,
        out_shape=jax.ShapeDtypeStruct((N_LAT, D), F32),
        grid=(LAT_TILES,),
        in_specs=_stream_specs(DEPTH) + [pl.BlockSpec((1, D), lambda i: (0, 0))],
        out_specs=pl.BlockSpec((TM, D), lambda i: (i, 0)),
        scratch_shapes=[pltpu.VMEM((TM * SLAB, 128), F32)],
        compiler_params=_cparams(1),
        name="final_norm",
    )(x1, yt, mod, g_final.reshape(1, D))


def _slot_plan(counts):
    padded = ((counts + TME - 1) // TME) * TME
    ends = jnp.cumsum(padded)
    starts = ends - padded
    tile_start = jnp.arange(P_TILES, dtype=I32) * TME
    tile_cls = jnp.minimum(jnp.sum((tile_start[:, None] >= ends[None, :]).astype(I32), axis=1), N_CLASSES - 1)
    nvalid = (ends[-1] // TME).reshape(1)
    grp = tile_cls // 6
    tile_ea = grp * PER_GROUP + jnp.asarray(_SLOT_A)[tile_cls % 6]
    tile_eb = grp * PER_GROUP + jnp.asarray(_SLOT_B)[tile_cls % 6]
    return starts, tile_ea, tile_eb, nvalid


def _rope_tables():
    t = jnp.arange(SEQ, dtype=I32)
    row = (t // GRID_W).astype(F32)
    col = (t % GRID_W).astype(F32)
    half = HEAD_DIM // 4
    freqs = ROPE_BASE ** (-jnp.arange(0, 2 * half, 2, dtype=F32) / (2 * half))
    ang_r = row[:, None] * freqs
    ang_c = col[:, None] * freqs
    cos_h = jnp.concatenate([jnp.cos(ang_r), jnp.cos(ang_r), jnp.cos(ang_c), jnp.cos(ang_c)], axis=1)
    sin_h = jnp.concatenate([-jnp.sin(ang_r), jnp.sin(ang_r), -jnp.sin(ang_c), jnp.sin(ang_c)], axis=1)
    cos_t = jnp.concatenate([jnp.tile(cos_h, (1, 2)), jnp.ones((TM, 128), F32)], axis=0)
    sin_t = jnp.concatenate([jnp.tile(sin_h, (1, 2)), jnp.zeros((TM, 128), F32)], axis=0)
    return cos_t, sin_t


def kernel(x, c, ctx, c_ctx, w_mod, b_mod, g_norm1, g_norm2, w_in, w_out, attn_sink, sconv_w, cconv_w, cconv_b,
           cnorm_g, cnorm_b, w_route_group, b_route_group, w_route_expert, b_route_expert, w_gate_e, w_up_e,
           w_down_e, g_final):
    c8 = jnp.zeros((8, D), F32).at[0:BATCH].set(c).at[BATCH].set(c_ctx)
    w_route = jnp.concatenate([w_route_expert, w_route_group,
                               jnp.zeros((DEPTH, D, 128 - N_EXPERTS - N_GROUPS), F32)], axis=2)
    wr_hi = w_route.astype(BF16)
    wr_cat = jnp.concatenate([wr_hi, (w_route - wr_hi.astype(F32)).astype(BF16)], axis=2)
    b_route = jnp.concatenate([b_route_expert, b_route_group,
                               jnp.zeros((DEPTH, 128 - N_EXPERTS - N_GROUPS), F32)], axis=1).reshape(DEPTH, 1, 128)
    cos_t, sin_t = _rope_tables()

    mod, wqkv, wf, wcv, w_out_b = _mod_call(c8, w_mod, b_mod, w_in, w_out)
    mod = mod.reshape(DEPTH, 8, N_MOD, D)[:, :3]
    mod = jnp.pad(mod, ((0, 0), (0, 0), (0, 8 - N_MOD), (0, 0)))

    stream = (x.reshape(N_LAT, D), ctx.reshape(N_CTX, D))
    for l in range(DEPTH):
        last = l == DEPTH - 1
        qkv, f, cv, w1b, w3b = _in_call(l, stream, mod, g_norm1, wqkv, wf, wcv, cos_t, sin_t, w_gate_e, w_up_e)
        o_a = _attn_call(l, attn_sink, qkv)
        o_b_lat = _fft_call(f, 64, 0, BATCH)
        o_b_ctx = o_b_lat if last else _fft_call(f, 16, N_LAT // CTX, BATCH)
        o_c, o_d = _conv_call(l, cv, sconv_w, cconv_w, cconv_b, cnorm_g, cnorm_b)
        x1, hp, cls8, gw8, cnt, w2b = _out_call(l, o_a, o_b_lat, o_b_ctx, o_c, o_d, w_out_b, stream, mod, g_norm2,
                                                wr_cat, wr_hi, b_route, w_down_e)
        counts = cnt[:N_CLASSES, 0]
        starts, tile_ea, tile_eb, nvalid = _slot_plan(counts)
        yt = _expert_call(tile_ea, tile_eb, nvalid, starts, counts, cls8[0], cls8[1], gw8[0], gw8[1], hp,
                          w1b.reshape(N_EXPERTS, D, D_EXPERT), w3b.reshape(N_EXPERTS, D, D_EXPERT),
                          w2b.reshape(N_EXPERTS, D_EXPERT, D))
        stream = (x1, yt, mod)
    return _final_call(x1, mod, g_final, yt).reshape(BATCH, SEQ, D)
```
